```python
import math
import jax
import jax.numpy as jnp
from jax import lax
import numpy as np

D_MODEL = 1024
BATCH = 2
SEQ = 8192
DEPTH = 2

HEAD_DIM = 64
N_HEADS_SB = 8
N_HEADS_NSA = 8
NSA_KV_GROUPS = 2
NSA_HPG = N_HEADS_NSA // NSA_KV_GROUPS
W_SB = N_HEADS_SB * HEAD_DIM
W_NSA = N_HEADS_NSA * HEAD_DIM
CMP_BLOCK = 32
CMP_STRIDE = 16
CMP_HIDDEN = 256
SLC_BLOCK = 64
N_SELECT = 16
WINDOW = 512
Q_BLOCK = 128
REL_BUCKETS = 32
REL_MAX_DIST = 128
D_FF_DENSE = 2816
N_EXPERTS = 8
TOP_K = 2
D_FF_EXPERT = 3584
MOE_ROW_BLOCK = 256
NORM_EPS = 1e-6
FORCED_BLOCK_SCORE = 1e4
N_DENSE_LAYERS = (DEPTH + 1) // 2
N_MOE_LAYERS = DEPTH // 2

OFF_SB_Q = 0
OFF_SB_K = OFF_SB_Q + W_SB
OFF_SB_V = OFF_SB_K + W_SB
OFF_NSA_Q = OFF_SB_V + W_SB
OFF_NSA_KV = OFF_NSA_Q + W_NSA
OFF_NSA_GATE = OFF_NSA_KV + 3 * 2 * NSA_KV_GROUPS * HEAD_DIM
IN_COLS = OFF_NSA_GATE + 3 * N_HEADS_NSA

kernel_name = "hybrid_sb_nsa_moe_block"


def rms_norm(x, gain):
    xf = x.astype(jnp.float32)
    y = xf * lax.rsqrt(jnp.mean(xf * xf, axis=-1, keepdims=True) + NORM_EPS)
    return (y * gain.astype(jnp.float32)).astype(x.dtype)


def t5_bucket(dist):
    n = jnp.maximum(dist, 0)
    max_exact = REL_BUCKETS // 2
    large = max_exact + (jnp.log(jnp.maximum(n, 1).astype(jnp.float32) / max_exact)
                         / math.log(REL_MAX_DIST / max_exact)
                         * (REL_BUCKETS - max_exact)).astype(jnp.int32)
    large = jnp.minimum(large, REL_BUCKETS - 1)
    return jnp.where(n < max_exact, n, large)


def masked_softmax(logits, mask):
    logits = jnp.where(mask, logits, -jnp.inf)
    m = jnp.max(logits, axis=-1, keepdims=True)
    m = jnp.where(jnp.isfinite(m), m, 0.0)
    p = jnp.where(mask, jnp.exp(logits - m), 0.0)
    return p / jnp.maximum(jnp.sum(p, axis=-1, keepdims=True), 1e-30)


def stick_breaking_attention(q, k, v):
    B, H, S, dh = q.shape
    nq = S // Q_BLOCK
    scale = dh ** -0.5
    q_blocks = jnp.moveaxis(q.reshape(B, H, nq, Q_BLOCK, dh), 2, 0)
    key_pos = jnp.arange(S, dtype=jnp.int32)

    def one_block(args):
        qb, blk = args
        t = blk * Q_BLOCK + jnp.arange(Q_BLOCK, dtype=jnp.int32)
        mask = key_pos[None, :] < t[:, None]
        z = jnp.einsum('bhqd,bhkd->bhqk', qb, k) * scale
        log_keep = jnp.where(mask, jax.nn.log_sigmoid(-z), 0.0)
        later = lax.cumsum(log_keep, axis=3, reverse=True) - log_keep
        w = jnp.where(mask, jnp.exp(jax.nn.log_sigmoid(z) + later), 0.0)
        return jnp.einsum('bhqk,bhkd->bhqd', w, v)

    out = lax.map(one_block, (q_blocks, jnp.arange(nq, dtype=jnp.int32)))
    return jnp.moveaxis(out, 0, 2).reshape(B, H, S, dh)


def compress_tokens(k, pos_emb, w1, w2):
    B, G, S, dh = k.shape
    n_cmp = (S - CMP_BLOCK) // CMP_STRIDE + 1
    idx = jnp.arange(n_cmp)[:, None] * CMP_STRIDE + jnp.arange(CMP_BLOCK)[None, :]
    blocks = k[:, :, idx] + pos_emb
    flat = blocks.reshape(B, G, n_cmp, CMP_BLOCK * dh)
    return (jax.nn.gelu(flat @ w1) @ w2).astype(jnp.float32)


def native_sparse_attention(q, k_cmp_raw, v_cmp_raw, k_slc, v_slc, k_win, v_win, gates,
                            rel_table, cmp_pos_k, cmp_w1_k, cmp_w2_k,
                            cmp_pos_v, cmp_w1_v, cmp_w2_v):
    B, G, Z, S, dh = q.shape
    scale = dh ** -0.5
    nq = S // Q_BLOCK
    n_slc = S // SLC_BLOCK
    n_sel = min(N_SELECT, n_slc)

    k_c = compress_tokens(k_cmp_raw, cmp_pos_k, cmp_w1_k, cmp_w2_k)
    v_c = compress_tokens(v_cmp_raw, cmp_pos_v, cmp_w1_v, cmp_w2_v)
    n_cmp = k_c.shape[2]
    cmp_start = jnp.arange(n_cmp, dtype=jnp.int32) * CMP_STRIDE
    cmp_end = cmp_start + CMP_BLOCK - 1
    slc_start = jnp.arange(n_slc, dtype=jnp.int32) * SLC_BLOCK
    overlap = ((cmp_start[:, None] < slc_start[None, :] + SLC_BLOCK)
               & (cmp_start[:, None] + CMP_BLOCK > slc_start[None, :])).astype(jnp.float32)

    k_blocks = k_slc.reshape(B, G, n_slc, SLC_BLOCK, dh)
    v_blocks = v_slc.reshape(B, G, n_slc, SLC_BLOCK, dh)
    k_win_pad = jnp.pad(k_win, ((0, 0), (0, 0), (WINDOW, 0), (0, 0)))
    v_win_pad = jnp.pad(v_win, ((0, 0), (0, 0), (WINDOW, 0), (0, 0)))

    table_hd = rel_table.reshape(REL_BUCKETS, G, Z)
    table_g = jnp.transpose(table_hd, (1, 0, 2))
    bi = jnp.arange(B)[:, None, None, None]
    gi = jnp.arange(G)[None, :, None, None]
    in_blk = jnp.arange(SLC_BLOCK, dtype=jnp.int32)
    win_off = jnp.arange(Q_BLOCK + WINDOW, dtype=jnp.int32) - WINDOW
    blk_ids = jnp.arange(n_slc, dtype=jnp.int32)[None, :]

    q_blocks = jnp.moveaxis(q.reshape(B, G, Z, nq, Q_BLOCK, dh), 3, 0)
    g_blocks = jnp.moveaxis(gates.reshape(B, G, Z, nq, Q_BLOCK, 3), 3, 0)

    def head_bias(dist):
        return jnp.transpose(table_hd[t5_bucket(dist)], (2, 3, 0, 1))

    def one_block(args):
        qb, gb, blk = args
        c0 = blk * Q_BLOCK
        t = c0 + jnp.arange(Q_BLOCK, dtype=jnp.int32)

        dist_c = t[:, None] - cmp_end[None, :]
        logit_c = jnp.einsum('bgzqd,bgnd->bgzqn', qb, k_c) * scale + head_bias(dist_c)
        p_c = masked_softmax(logit_c, dist_c >= 0)
        o_c = jnp.einsum('bgzqn,bgnd->bgzqd', p_c, v_c)

        imp = jnp.einsum('bgzqn,nj->bgqj', p_c, overlap)
        valid = slc_start[None, :] <= t[:, None]
        cur = (t // SLC_BLOCK)[:, None]
        forced = valid & ((blk_ids == 0) | (blk_ids == cur) | (blk_ids == cur - 1))
        score = jnp.where(valid, imp + jnp.where(forced, FORCED_BLOCK_SCORE, 0.0), -1.0)
        top_score, top_idx = lax.top_k(score, n_sel)
        k_sel = k_blocks[bi, gi, top_idx]
        v_sel = v_blocks[bi, gi, top_idx]
        tok = top_idx[..., None] * SLC_BLOCK + in_blk
        dist_s = t[None, None, :, None, None] - tok
        mask_s = (top_score >= 0.0)[..., None] & (dist_s >= 0)
        bias_s = jnp.moveaxis(table_g[gi[..., None], t5_bucket(dist_s)], -1, 2)
        logit_s = jnp.einsum('bgzqd,bgqnld->bgzqnl', qb, k_sel) * scale + bias_s
        n_tok = n_sel * SLC_BLOCK
        p_s = masked_softmax(logit_s.reshape(B, G, Z, Q_BLOCK, n_tok),
                             mask_s.reshape(B, G, 1, Q_BLOCK, n_tok))
        o_s = jnp.einsum('bgzqm,bgqmd->bgzqd', p_s, v_sel.reshape(B, G, Q_BLOCK, n_tok, dh))

        k_w = lax.dynamic_slice_in_dim(k_win_pad, c0, Q_BLOCK + WINDOW, axis=2)
        v_w = lax.dynamic_slice_in_dim(v_win_pad, c0, Q_BLOCK + WINDOW, axis=2)
        pos_w = c0 + win_off
        dist_w = t[:, None] - pos_w[None, :]
        mask_w = (dist_w >= 0) & (dist_w < WINDOW) & (pos_w[None, :] >= 0)
        logit_w = jnp.einsum('bgzqd,bgkd->bgzqk', qb, k_w) * scale + head_bias(dist_w)
        p_w = masked_softmax(logit_w, mask_w)
        o_w = jnp.einsum('bgzqk,bgkd->bgzqd', p_w, v_w)

        return gb[..., 0:1] * o_c + gb[..., 1:2] * o_s + gb[..., 2:3] * o_w

    out = lax.map(one_block, (q_blocks, g_blocks, jnp.arange(nq, dtype=jnp.int32)))
    return jnp.transpose(out, (1, 0, 4, 2, 3, 5)).reshape(B, S, G * Z * dh)


def hybrid_mixer(h, w_in, w_out, g_sb, g_nsa, rel_table, cmp_pos_k, cmp_w1_k, cmp_w2_k,
                 cmp_pos_v, cmp_w1_v, cmp_w2_v):
    B, S, _ = h.shape
    G, Z, dh = NSA_KV_GROUPS, NSA_HPG, HEAD_DIM
    proj = (h @ w_in).astype(jnp.float32)

    def heads(lo, n):
        return proj[..., lo:lo + n * dh].reshape(B, S, n, dh).transpose(0, 2, 1, 3)

    o_sb = stick_breaking_attention(heads(OFF_SB_Q, N_HEADS_SB), heads(OFF_SB_K, N_HEADS_SB),
                                    heads(OFF_SB_V, N_HEADS_SB))
    o_sb = o_sb.transpose(0, 2, 1, 3).reshape(B, S, W_SB)

    q_nsa = proj[..., OFF_NSA_Q:OFF_NSA_KV].reshape(B, S, G, Z, dh).transpose(0, 2, 3, 1, 4)
    kv = proj[..., OFF_NSA_KV:OFF_NSA_GATE].reshape(B, S, 3, 2, G, dh).transpose(2, 3, 0, 4, 1, 5)
    gates = jax.nn.sigmoid(proj[..., OFF_NSA_GATE:IN_COLS]).reshape(B, S, G, Z, 3)
    gates = gates.transpose(0, 2, 3, 1, 4)
    o_nsa = native_sparse_attention(q_nsa, kv[0, 0], kv[0, 1], kv[1, 0], kv[1, 1],
                                    kv[2, 0], kv[2, 1], gates, rel_table,
                                    cmp_pos_k, cmp_w1_k, cmp_w2_k, cmp_pos_v, cmp_w1_v, cmp_w2_v)

    merged = jnp.concatenate([rms_norm(o_sb, g_sb), rms_norm(o_nsa, g_nsa)], axis=-1)
    return merged.astype(h.dtype) @ w_out


def swiglu(h, w_gate, w_up, w_down):
    return (jax.nn.silu(h @ w_gate) * (h @ w_up)) @ w_down


def moe_swiglu(h, w_router, b_router, w_gate, w_up, w_down):
    T, D = h.shape
    logits = h.astype(jnp.float32) @ w_router.astype(jnp.float32) + b_router.astype(jnp.float32)
    probs = jax.nn.softmax(logits, axis=-1)
    top_p, top_e = lax.top_k(probs, TOP_K)
    top_p = top_p / jnp.sum(top_p, axis=-1, keepdims=True)

    n_assign = T * TOP_K
    flat_e = top_e.reshape(-1)
    flat_tok = jnp.arange(n_assign, dtype=jnp.int32) // TOP_K
    order = jnp.argsort(flat_e)
    sorted_e = flat_e[order]
    sorted_tok = flat_tok[order]
    sorted_w = top_p.reshape(-1)[order]

    counts = jnp.bincount(flat_e, length=N_EXPERTS)
    padded = (counts + MOE_ROW_BLOCK - 1) // MOE_ROW_BLOCK * MOE_ROW_BLOCK
    start = jnp.cumsum(counts) - counts
    pad_end = jnp.cumsum(padded)
    pad_start = pad_end - padded
    dest = pad_start[sorted_e] + jnp.arange(n_assign, dtype=jnp.int32) - start[sorted_e]
    n_blocks = -(-n_assign // MOE_ROW_BLOCK) + N_EXPERTS
    row_tok = jnp.zeros((n_blocks * MOE_ROW_BLOCK,), jnp.int32).at[dest].set(sorted_tok)
    block_expert = jnp.minimum(
        jnp.searchsorted(pad_end, jnp.arange(n_blocks, dtype=jnp.int32) * MOE_ROW_BLOCK, side='right'),
        N_EXPERTS - 1)
    rows = h[row_tok].reshape(n_blocks, MOE_ROW_BLOCK, D)

    def expert_block(args):
        xb, e = args
        return swiglu(xb, w_gate[e], w_up[e], w_down[e])

    y = lax.map(expert_block, (rows, block_expert)).reshape(-1, D)
    contrib = y[dest].astype(jnp.float32) * sorted_w[:, None]
    return jax.ops.segment_sum(contrib, sorted_tok, num_segments=T).astype(h.dtype)


def setup_inputs(seed: int = 0) -> dict:
    key = jax.random.key(seed)
    ks = iter(jax.random.split(key, 32))
    D, dh = D_MODEL, HEAD_DIM

    def nrm(shape, scale):
        return jax.random.normal(next(ks), shape, jnp.float32) * scale

    def gain(shape):
        return 1.0 + nrm(shape, 0.02)

    return {
        "x": nrm((BATCH, SEQ, D), 1.0),
        "c": nrm((BATCH, D), 1.0),
        "rel_table": nrm((REL_BUCKETS, N_HEADS_NSA), 0.2),
        "w_ada": nrm((DEPTH, D, 6 * D), 0.5 * D ** -0.5),
        "b_ada": nrm((DEPTH, 6 * D), 0.01),
        "g_pre_mix": gain((DEPTH, D)),
        "g_post_mix": gain((DEPTH, D)),
        "g_pre_ffn": gain((DEPTH, D)),
        "g_post_ffn": gain((DEPTH, D)),
        "w_in": nrm((DEPTH, D, IN_COLS), D ** -0.5),
        "w_out": nrm((DEPTH, W_SB + W_NSA, D), (W_SB + W_NSA) ** -0.5),
        "g_sb": gain((DEPTH, W_SB)),
        "g_nsa": gain((DEPTH, W_NSA)),
        "cmp_pos_k": nrm((DEPTH, CMP_BLOCK, dh), 0.5),
        "cmp_w1_k": nrm((DEPTH, CMP_BLOCK * dh, CMP_HIDDEN), (CMP_BLOCK * dh) ** -0.5),
        "cmp_w2_k": nrm((DEPTH, CMP_HIDDEN, dh), CMP_HIDDEN ** -0.5),
        "cmp_pos_v": nrm((DEPTH, CMP_BLOCK, dh), 0.5),
        "cmp_w1_v": nrm((DEPTH, CMP_BLOCK * dh, CMP_HIDDEN), (CMP_BLOCK * dh) ** -0.5),
        "cmp_w2_v": nrm((DEPTH, CMP_HIDDEN, dh), CMP_HIDDEN ** -0.5),
        "ffn_w_gate": nrm((N_DENSE_LAYERS, D, D_FF_DENSE), D ** -0.5),
        "ffn_w_up": nrm((N_DENSE_LAYERS, D, D_FF_DENSE), D ** -0.5),
        "ffn_w_down": nrm((N_DENSE_LAYERS, D_FF_DENSE, D), D_FF_DENSE ** -0.5),
        "moe_w_router": nrm((N_MOE_LAYERS, D, N_EXPERTS), D ** -0.5),
        "moe_b_router": nrm((N_MOE_LAYERS, N_EXPERTS), 0.01),
        "moe_w_gate": nrm((N_MOE_LAYERS, N_EXPERTS, D, D_FF_EXPERT), D ** -0.5),
        "moe_w_up": nrm((N_MOE_LAYERS, N_EXPERTS, D, D_FF_EXPERT), D ** -0.5),
        "moe_w_down": nrm((N_MOE_LAYERS, N_EXPERTS, D_FF_EXPERT, D), D_FF_EXPERT ** -0.5),
    }


def reference(x, c, rel_table, w_ada, b_ada, g_pre_mix, g_post_mix, g_pre_ffn, g_post_ffn,
              w_in, w_out, g_sb, g_nsa, cmp_pos_k, cmp_w1_k, cmp_w2_k, cmp_pos_v, cmp_w1_v,
              cmp_w2_v, ffn_w_gate, ffn_w_up, ffn_w_down, moe_w_router, moe_b_router,
              moe_w_gate, moe_w_up, moe_w_down):
    B, S, D = x.shape
    c_act = jax.nn.silu(c)
    for layer in range(DEPTH):
        mod = (c_act @ w_ada[layer] + b_ada[layer])[:, None, :]
        shift_m, scale_m, gate_m, shift_f, scale_f, gate_f = jnp.split(mod, 6, axis=-1)

        h = rms_norm(x, g_pre_mix[layer]) * (1.0 + scale_m) + shift_m
        m = hybrid_mixer(h, w_in[layer], w_out[layer], g_sb[layer], g_nsa[layer], rel_table,
                         cmp_pos_k[layer], cmp_w1_k[layer], cmp_w2_k[layer],
                         cmp_pos_v[layer], cmp_w1_v[layer], cmp_w2_v[layer])
        x = x + gate_m * rms_norm(m, g_post_mix[layer])

        h = rms_norm(x, g_pre_ffn[layer]) * (1.0 + scale_f) + shift_f
        i = layer // 2
        if layer % 2 == 0:
            f = swiglu(h, ffn_w_gate[i], ffn_w_up[i], ffn_w_down[i])
        else:
            f = moe_swiglu(h.reshape(B * S, D), moe_w_router[i], moe_b_router[i],
                           moe_w_gate[i], moe_w_up[i], moe_w_down[i]).reshape(B, S, D)
        x = x + gate_f * rms_norm(f, g_post_ffn[layer])
    return x
```

```python
import functools
import math

import numpy as np
import jax
import jax.numpy as jnp
from jax import lax
from jax.experimental import pallas as pl
from jax.experimental.pallas import tpu as pltpu

F32 = jnp.float32
BF16 = jnp.bfloat16
I32 = jnp.int32

LANES = 128
DH = 64
N_SB = 8
N_NSA = 8
G_NSA = 2
Z_NSA = 4
CMP_BLOCK = 32
CMP_STRIDE = 16
SLC_BLOCK = 64
N_SELECT = 16
WINDOW = 512
REL_BUCKETS = 32
REL_MAX_DIST = 128
N_EXPERTS = 8
EPS = 1e-6
FORCED = 1e4
NEG = -1e30
M_INIT = -1e29
SB_EXIT = -104.5
VMEM_LIMIT = 56 * 1024 * 1024

TQ = 128
KC_FRONT = 16
SEL_FAR_TILE = 512


def _cparams(*sem):
    return pltpu.CompilerParams(dimension_semantics=sem, vmem_limit_bytes=VMEM_LIMIT)


def _nt(a, b):
    return lax.dot_general(a, b, (((1,), (1,)), ((), ())), preferred_element_type=F32)


def _dot(a, b):
    return jnp.dot(a, b, preferred_element_type=F32)


def _split3(a):
    hi = a.astype(BF16)
    r = a - hi.astype(F32)
    mid = r.astype(BF16)
    lo = (r - mid.astype(F32)).astype(BF16)
    return hi, mid, lo


def _dot_hl(a, b):
    hi = a.astype(BF16)
    lo = (a - hi.astype(F32)).astype(BF16)
    return _dot(hi, b) + _dot(lo, b)


def _dot_f32(a, b):
    ah, am, al = _split3(a)
    bh, bm, bl = _split3(b)
    return (_dot(ah, bh) + (_dot(ah, bm) + _dot(am, bh))
            + (_dot(ah, bl) + _dot(am, bm) + _dot(al, bh)))


def _rms(x, g):
    return x * lax.rsqrt(jnp.mean(x * x, axis=-1, keepdims=True) + EPS) * g


def _ada_kernel(c_ref, w_ref, b_ref, o_ref):
    c = c_ref[...]
    ca = c * (1.0 / (1.0 + jnp.exp(-c)))
    o_ref[0] = _dot_f32(ca, w_ref[0]) + b_ref[0]


def _ada(c, w_ada, b_ada):
    depth, d, n = w_ada.shape
    bsz = c.shape[0]
    rows = 8
    tn = 1536
    cp = jnp.zeros((rows, d), F32).at[:bsz].set(c)
    out = pl.pallas_call(
        _ada_kernel,
        grid=(depth, n // tn),
        in_specs=[pl.BlockSpec((rows, d), lambda l, j: (0, 0)),
                  pl.BlockSpec((1, d, tn), lambda l, j: (l, 0, j)),
                  pl.BlockSpec((1, 1, tn), lambda l, j: (l, 0, j))],
        out_specs=pl.BlockSpec((1, rows, tn), lambda l, j: (l, 0, j)),
        out_shape=jax.ShapeDtypeStruct((depth, rows, n), F32),
        compiler_params=_cparams("arbitrary", "arbitrary"),
        name="ada_mod",
    )(cp, w_ada, b_ada.reshape(depth, 1, n))
    return out[:, :bsz]


def _in_kernel(x_ref, g_ref, sc_ref, sh_ref, w_ref, o_ref, *, cn):
    h = _rms(x_ref[0], g_ref[...]) * (1.0 + sc_ref[0]) + sh_ref[0]
    hb = h.astype(BF16)
    for j in range(w_ref.shape[1] // cn):
        o_ref[0, :, j * cn:(j + 1) * cn] = _dot(hb, w_ref[:, j * cn:(j + 1) * cn]).astype(BF16)


def _in_proj(x, g, scale, shift, w):
    bsz, s, d = x.shape
    nc = w.shape[1]
    tm = min(512, s)
    cn = nc // 3 if (nc // 3) % LANES == 0 else nc
    return pl.pallas_call(
        functools.partial(_in_kernel, cn=cn),
        grid=(bsz, s // tm),
        in_specs=[pl.BlockSpec((1, tm, d), lambda b, i: (b, i, 0)),
                  pl.BlockSpec((1, d), lambda b, i: (0, 0)),
                  pl.BlockSpec((1, 1, d), lambda b, i: (b, 0, 0)),
                  pl.BlockSpec((1, 1, d), lambda b, i: (b, 0, 0)),
                  pl.BlockSpec((d, nc), lambda b, i: (0, 0))],
        out_specs=pl.BlockSpec((1, tm, nc), lambda b, i: (b, i, 0)),
        out_shape=jax.ShapeDtypeStruct((bsz, s, nc), BF16),
        compiler_params=_cparams("arbitrary", "arbitrary"),
        name="in_proj",
    )(x, g.reshape(1, d), scale.reshape(bsz, 1, d), shift.reshape(bsz, 1, d), w)


CB_SBQ, CB_SBK, CB_SBV, CB_NSAQ = 0, 4, 8, 12
CB_KCMP, CB_VCMP = 16, 17
CB_KSLC, CB_VSLC, CB_KWIN, CB_VWIN = 18, 20, 22, 24
CB_GATE = 26
N_CB = 27


def _arrange_w_in(w_in):
    d = w_in.shape[0]
    w_sb = N_SB * DH
    off_nsa_q = 3 * w_sb
    off_kv = off_nsa_q + N_NSA * DH
    off_gate = off_kv + 3 * 2 * G_NSA * DH
    scale = DH ** -0.5

    def kv(br, kvi, g):
        lo = off_kv + ((br * 2 + kvi) * G_NSA + g) * DH
        return w_in[:, lo:lo + DH]

    cols = [w_in[:, 0:w_sb] * scale, w_in[:, w_sb:2 * w_sb], w_in[:, 2 * w_sb:3 * w_sb],
            w_in[:, off_nsa_q:off_kv] * scale,
            kv(0, 0, 0), kv(0, 0, 1), kv(0, 1, 0), kv(0, 1, 1)]
    for br in (1, 2):
        for kvi in (0, 1):
            for g in range(G_NSA):
                cols += [kv(br, kvi, g), kv(br, kvi, g)]
    n_gate = 3 * N_NSA
    cols += [w_in[:, off_gate:off_gate + n_gate], jnp.zeros((d, LANES - n_gate), w_in.dtype)]
    out = jnp.concatenate(cols, axis=1).astype(BF16)
    assert out.shape[1] == N_CB * LANES
    return out


def _sb_kernel(q_ref, k_ref, v_ref, o_ref, *, t):
    qi = pl.program_id(2)
    q = q_ref[0]
    lane = lax.broadcasted_iota(I32, (t, LANES), 1)
    row = lax.broadcasted_iota(I32, (t, t), 0)
    col = lax.broadcasted_iota(I32, (t, t), 1)
    upper = jnp.where(row > col, 1.0, 0.0).astype(BF16)
    ones = jnp.ones((t, LANES), BF16)
    causal = col < row
    rep = t // LANES

    def tile(qh, kt, carry, acc, diag):
        k = k_ref[0, pl.ds(kt * t, t), :]
        v = v_ref[0, pl.ds(kt * t, t), :]
        z = _nt(qh, k)
        lk = -(jnp.maximum(z, 0.0) + jnp.log(1.0 + jnp.exp(-jnp.abs(z))))
        if diag:
            lk = jnp.where(causal, lk, 0.0)
        hi = lk.astype(BF16)
        lo = (lk - hi.astype(F32)).astype(BF16)
        later = _dot(hi, upper) + _dot(lo, upper)
        tot = _dot(hi, ones) + _dot(lo, ones)
        cb = carry if rep == 1 else jnp.concatenate([carry] * rep, axis=1)
        w = jnp.exp(z + lk + later + cb)
        if diag:
            w = jnp.where(causal, w, 0.0)
        return carry + tot, acc + _dot(w.astype(BF16), v)

    outs = []
    for h in range(2):
        qh = jnp.where((lane >= DH) == (h == 1), q, jnp.zeros_like(q))
        zero = jnp.zeros((t, LANES), F32)
        carry, acc = tile(qh, qi, zero, zero, True)

        def cond(st):
            kt, mx, _, _ = st
            return jnp.logical_and(kt >= 0, mx > SB_EXIT)

        def body(st, qh=qh):
            kt, _, carry, acc = st
            carry, acc = tile(qh, kt, carry, acc, False)
            return kt - 1, jnp.max(carry), carry, acc

        _, _, _, acc = lax.while_loop(cond, body, (qi - 1, jnp.max(carry), carry, acc))
        outs.append(acc)
    o_ref[0] = jnp.where(lane < DH, outs[0], outs[1])


def _sb_attention(proj, t=256):
    bsz, s, _ = proj.shape
    t = min(t, s)
    npair = N_SB // 2
    return pl.pallas_call(
        functools.partial(_sb_kernel, t=t),
        grid=(bsz, npair, s // t),
        in_specs=[pl.BlockSpec((1, t, LANES), lambda b, j, i: (b, i, CB_SBQ + j)),
                  pl.BlockSpec((1, s, LANES), lambda b, j, i: (b, 0, CB_SBK + j)),
                  pl.BlockSpec((1, s, LANES), lambda b, j, i: (b, 0, CB_SBV + j))],
        out_specs=pl.BlockSpec((1, t, LANES), lambda b, j, i: (b, i, j)),
        out_shape=jax.ShapeDtypeStruct((bsz, s, N_SB * DH), F32),
        compiler_params=_cparams("arbitrary", "arbitrary", "arbitrary"),
        name="sb_attn",
    )(proj, proj, proj)


def _gelu_tanh(x):
    return 0.5 * x * (1.0 + jnp.tanh(math.sqrt(2.0 / math.pi) * (x + 0.044715 * (x * x * x))))


def _cmp_kernel(xa_ref, xb_ref, pos_ref, w1_ref, w2_ref, o_ref):
    half = w1_ref.shape[1] // 2
    w1a = w1_ref[0, :half, :]
    w1b = w1_ref[0, half:, :]
    pos = pos_ref[0]
    bias = _dot(pos[:, :half], w1a) + _dot(pos[:, half:], w1b)
    hid = _dot(xa_ref[0, 0], w1a) + _dot(xb_ref[0, 0], w1b) + bias[0:1, :]
    o_ref[0, 0] = _dot(_gelu_tanh(hid).astype(BF16), w2_ref[0])


def _compress(proj, pos_k, w1_k, w2_k, pos_v, w1_v, w2_v):
    bsz, s, _ = proj.shape
    n16 = s // CMP_STRIDE
    raw = proj[:, :, CB_KCMP * LANES:(CB_VCMP + 1) * LANES]
    x16 = raw.reshape(bsz, s, 4, DH).transpose(0, 2, 1, 3).reshape(bsz, 4, n16, CMP_STRIDE * DH)
    x16b = jnp.concatenate([x16[:, :, 1:], jnp.zeros_like(x16[:, :, :1])], axis=2)
    hidden = w1_k.shape[1]
    w1 = jnp.stack([w1_k, w1_v]).astype(BF16)
    w2 = jnp.stack([jnp.concatenate([w2_k, w2_k], 1), jnp.concatenate([w2_v, w2_v], 1)]).astype(BF16)
    pos = jnp.stack([pos_k.reshape(1, -1), pos_v.reshape(1, -1)])
    pos = jnp.concatenate([pos, jnp.zeros((2, 7, pos.shape[2]), pos.dtype)], axis=1).astype(BF16)
    k16 = CMP_STRIDE * DH
    return pl.pallas_call(
        _cmp_kernel,
        grid=(bsz, 4),
        in_specs=[pl.BlockSpec((1, 1, n16, k16), lambda b, j: (b, j, 0, 0)),
                  pl.BlockSpec((1, 1, n16, k16), lambda b, j: (b, j, 0, 0)),
                  pl.BlockSpec((1, 8, 2 * k16), lambda b, j: (j // 2, 0, 0)),
                  pl.BlockSpec((1, 2 * k16, hidden), lambda b, j: (j // 2, 0, 0)),
                  pl.BlockSpec((1, hidden, LANES), lambda b, j: (j // 2, 0, 0))],
        out_specs=pl.BlockSpec((1, 1, n16, LANES), lambda b, j: (b, j, 0, 0)),
        out_shape=jax.ShapeDtypeStruct((bsz, 4, n16, LANES), F32),
        compiler_params=_cparams("arbitrary", "arbitrary"),
        name="nsa_compress",
    )(x16, x16b, pos, w1, w2)


def _bucket_table():
    n = np.arange(REL_MAX_DIST + 1)
    exact = REL_BUCKETS // 2
    val = (np.log(np.maximum(n, 1).astype(np.float32) / np.float32(exact)).astype(np.float32)
           / np.float32(math.log(REL_MAX_DIST / exact)) * np.float32(REL_BUCKETS - exact))
    large = np.minimum(exact + val.astype(np.int32), REL_BUCKETS - 1)
    return np.where(n < exact, n, large).astype(np.int32)


def _bias_tiles(rel_table):
    tab = rel_table[_bucket_table()]
    r = np.arange(TQ)[:, None]

    def tile(dist, valid):
        idx = np.clip(dist, 0, REL_MAX_DIST)
        t = jnp.where(valid[:, :, None], tab[idx], NEG)
        t = jnp.transpose(t, (2, 0, 1))
        return t.reshape(G_NSA, Z_NSA * TQ, dist.shape[1]).astype(F32)

    w = np.arange(LANES)[None, :]
    d_c = r - CMP_STRIDE * (w - KC_FRONT) - (CMP_BLOCK - 1)
    w2 = np.arange(2 * TQ)[None, :]
    d_s = r + TQ - w2
    w5 = np.arange(WINDOW + TQ)[None, :]
    d_w = r + WINDOW - w5
    far = jnp.broadcast_to(tab[REL_MAX_DIST][:, None, None], (N_NSA, TQ, LANES))
    far = far.reshape(G_NSA, Z_NSA * TQ, LANES).astype(F32)
    return tile(d_c, d_c >= 0), tile(d_s, d_s >= 0), tile(d_w, (d_w >= 0) & (d_w < WINDOW)), far


def _gate_expand():
    e = np.zeros((G_NSA, LANES, 3 * Z_NSA * DH), np.float32)
    for g in range(G_NSA):
        for z in range(Z_NSA):
            for br in range(3):
                e[g, (g * Z_NSA + z) * 3 + br, br * Z_NSA * DH + z * DH: br * Z_NSA * DH + (z + 1) * DH] = 1.0
    return e


def _overlap_padded(n16, kcp, n_slc):
    i = np.arange(kcp)[:, None] - KC_FRONT
    j = np.arange(LANES)[None, :]
    n_cmp = n16 - 1
    ok = (i >= 0) & (i < n_cmp) & (j < n_slc)
    ov = (i * CMP_STRIDE < j * SLC_BLOCK + SLC_BLOCK) & (i * CMP_STRIDE + CMP_BLOCK > j * SLC_BLOCK)
    return (ok & ov).astype(np.float32)


def _softmax_rows(s, mask):
    m = jnp.max(s, axis=1, keepdims=True)
    p = jnp.where(mask, jnp.exp(s - m), 0.0)
    return p, jnp.sum(p, axis=1, keepdims=True)


def _nsa_kernel(q_ref, kcb_ref, vcb_ref, kcf_ref, vcf_ref, ks_ref, vs_ref, kw_ref, vw_ref,
                gate_ref, ovb_ref, ovf_ref, bcn_ref, bsn_ref, bw_ref, cf_ref, eg_ref, o_ref,
                *, n_slc, n_sel):
    blk = pl.program_id(2)
    c0 = blk * TQ
    r4 = Z_NSA * TQ
    q = q_ref[0]
    lane = lax.broadcasted_iota(I32, (TQ, LANES), 1)
    lo_half = lane < DH
    zq = jnp.zeros((TQ, LANES), BF16)
    qa, qb = q[:, :LANES], q[:, LANES:]
    qs = jnp.concatenate([jnp.where(lo_half, qa, zq), jnp.where(lo_half, zq, qa),
                          jnp.where(lo_half, qb, zq), jnp.where(lo_half, zq, qb)], axis=0)
    cfar = cf_ref[0][:, 0:1]

    kcp = kcb_ref.shape[2]
    prow = lax.broadcasted_iota(I32, (r4, kcp), 1)
    far_mask = (prow >= KC_FRONT) & (prow < 8 * blk)
    s_far = jnp.where(far_mask, _nt(qs, kcb_ref[0, 0]) + cfar, NEG)
    near0 = pl.multiple_of(8 * blk, 8)
    kcn = kcf_ref[0, 0, pl.ds(near0, LANES), :].astype(BF16)
    vcn = vcf_ref[0, 0, pl.ds(near0, LANES), :].astype(BF16)
    bcn = bcn_ref[0]
    lane4 = lax.broadcasted_iota(I32, (r4, LANES), 1)
    near_mask = (bcn > 0.5 * NEG) & (lane4 + 8 * blk >= KC_FRONT)
    s_near = jnp.where(near_mask, _nt(qs, kcn) + bcn, NEG)
    m_c = jnp.maximum(jnp.max(s_far, axis=1, keepdims=True), jnp.max(s_near, axis=1, keepdims=True))
    p_far = jnp.where(far_mask, jnp.exp(s_far - m_c), 0.0)
    p_near = jnp.where(near_mask, jnp.exp(s_near - m_c), 0.0)
    den = jnp.sum(p_far, axis=1, keepdims=True) + jnp.sum(p_near, axis=1, keepdims=True)
    inv_c = 1.0 / jnp.maximum(den, 1e-30)
    o_c = (_dot(p_far.astype(BF16), vcb_ref[0, 0]) + _dot(p_near.astype(BF16), vcn)) * inv_c

    pn_far = p_far * inv_c
    pn_near = p_near * inv_c
    pz_far = pn_far[0:TQ] + pn_far[TQ:2 * TQ] + pn_far[2 * TQ:3 * TQ] + pn_far[3 * TQ:]
    pz_near = pn_near[0:TQ] + pn_near[TQ:2 * TQ] + pn_near[2 * TQ:3 * TQ] + pn_near[3 * TQ:]
    ovn = ovf_ref[pl.ds(near0, LANES), :].astype(BF16)
    imp = _dot_hl(pz_far, ovb_ref[...]) + _dot_hl(pz_near, ovn)
    tpos = c0 + lax.broadcasted_iota(I32, (TQ, LANES), 0)
    cur = lax.shift_right_logical(tpos, 6)
    valid = (lane * SLC_BLOCK <= tpos) & (lane < n_slc)
    forced = valid & ((lane == 0) | (lane == cur) | (lane == cur - 1))
    score = jnp.where(valid, imp + jnp.where(forced, FORCED, 0.0), -1.0)
    score = jnp.where(lane < n_slc, score, -2.0)
    lanef = lane.astype(F32)

    def pick(_, st):
        score, sel = st
        m = jnp.max(score, axis=1, keepdims=True)
        idx = jnp.min(jnp.where(score == m, lanef, 1e9), axis=1, keepdims=True)
        hit = lanef == idx
        return jnp.where(hit, -jnp.inf, score), jnp.where(hit, 1.0, sel)

    _, sel = lax.fori_loop(0, n_sel, pick, (score, jnp.zeros((TQ, LANES), F32)))
    sel = (sel > 0.5) & valid

    sel_all = jnp.where(sel, 0.0, NEG).astype(BF16)
    sel_far = jnp.where(sel & (lane < 2 * blk - 2), 0.0, NEG).astype(BF16)
    q_far = jnp.concatenate([qs, jnp.concatenate([sel_far] * Z_NSA, axis=0)], axis=1)
    q_near = jnp.concatenate([qs, jnp.concatenate([sel_all] * Z_NSA, axis=0)], axis=1)
    tk = SEL_FAR_TILE
    krow = lax.broadcasted_iota(I32, (tk, LANES), 0)
    klane = lax.broadcasted_iota(I32, (tk, LANES), 1)
    kblk = lax.shift_right_logical(krow, 6)
    n_far = lax.shift_right_logical(jnp.maximum(blk - 1, 0) + 3, 2)

    def far_step(kt, st):
        m, l, acc = st
        k0 = pl.multiple_of(kt * tk, tk)
        k = ks_ref[0, pl.ds(k0, tk), :]
        v = vs_ref[0, pl.ds(k0, tk), :]
        e_t = jnp.where(kblk + kt * (tk // SLC_BLOCK) == klane, 1.0, 0.0).astype(BF16)
        s = _nt(q_far, jnp.concatenate([k, e_t], axis=1))
        mn = jnp.maximum(m, jnp.max(s, axis=1, keepdims=True))
        alpha = jnp.exp(m - mn)
        p = jnp.exp(s - mn)
        return (mn, alpha * l + jnp.sum(p, axis=1, keepdims=True),
                alpha * acc + _dot(p.astype(BF16), v))

    m0 = jnp.full((r4, 1), M_INIT, F32)
    m_s, l_s, acc_s = lax.fori_loop(0, n_far, far_step,
                                    (m0, jnp.zeros((r4, 1), F32), jnp.zeros((r4, LANES), F32)))
    m_s = m_s + cfar
    p0 = pl.multiple_of(jnp.maximum(c0 - TQ, 0), TQ)
    d0 = pl.multiple_of(c0, TQ)
    kn = jnp.concatenate([ks_ref[0, pl.ds(p0, TQ), :], ks_ref[0, pl.ds(d0, TQ), :]], axis=0)
    vn = jnp.concatenate([vs_ref[0, pl.ds(p0, TQ), :], vs_ref[0, pl.ds(d0, TQ), :]], axis=0)
    nrow = lax.broadcasted_iota(I32, (2 * TQ, LANES), 0)
    nlane = lax.broadcasted_iota(I32, (2 * TQ, LANES), 1)
    e_n = jnp.where(lax.shift_right_logical(nrow, 6) + 2 * blk - 2 == nlane, 1.0, 0.0).astype(BF16)
    s = _nt(q_near, jnp.concatenate([kn, e_n], axis=1)) + bsn_ref[0]
    lane8 = lax.broadcasted_iota(I32, (r4, 2 * TQ), 1)
    s = jnp.where((lane8 < TQ) & (blk == 0), NEG, s)
    mn = jnp.maximum(m_s, jnp.max(s, axis=1, keepdims=True))
    alpha = jnp.exp(m_s - mn)
    p = jnp.exp(s - mn)
    l_s = alpha * l_s + jnp.sum(p, axis=1, keepdims=True)
    o_s = (alpha * acc_s + _dot(p.astype(BF16), vn)) / l_s

    n_w = WINDOW // TQ + 1
    kws, vws = [], []
    for w in range(n_w):
        st = pl.multiple_of(jnp.maximum(c0 - WINDOW + w * TQ, 0), TQ)
        kws.append(kw_ref[0, pl.ds(st, TQ), :])
        vws.append(vw_ref[0, pl.ds(st, TQ), :])
    s = _nt(qs, jnp.concatenate(kws, axis=0)) + bw_ref[0]
    wl = lax.broadcasted_iota(I32, (r4, WINDOW + TQ), 1)
    s = jnp.where(c0 - WINDOW + wl >= 0, s, NEG)
    p = jnp.exp(s - jnp.max(s, axis=1, keepdims=True))
    o_w = _dot(p.astype(BF16), jnp.concatenate(vws, axis=0)) / jnp.sum(p, axis=1, keepdims=True)

    gl = _dot(gate_ref[0], eg_ref[0])
    sig = 1.0 / (1.0 + jnp.exp(-gl))
    wide = Z_NSA * DH

    def heads(o):
        return jnp.concatenate([jnp.where(lo_half, o[0:TQ], o[TQ:2 * TQ]),
                                jnp.where(lo_half, o[2 * TQ:3 * TQ], o[3 * TQ:])], axis=1)

    o_ref[0] = (sig[:, 0:wide] * heads(o_c) + sig[:, wide:2 * wide] * heads(o_s)
                + sig[:, 2 * wide:] * heads(o_w))


def _nsa_attention(proj, cmp_out, rel_table):
    bsz, s, _ = proj.shape
    assert s % SEL_FAR_TILE == 0 and s >= WINDOW + TQ
    n16 = s // CMP_STRIDE
    n_slc = s // SLC_BLOCK
    assert n_slc <= LANES
    n_sel = min(N_SELECT, n_slc)
    nq = s // TQ
    kcp = -(-(max(n16 + KC_FRONT, 8 * (nq - 1) + LANES)) // LANES) * LANES
    pad = ((0, 0), (0, 0), (KC_FRONT, kcp - n16 - KC_FRONT), (0, 0))
    cf = jnp.pad(cmp_out, pad)
    cb = cf.astype(BF16)
    ov = _overlap_padded(n16, kcp, n_slc)
    bcn, bsn, bw, far = _bias_tiles(rel_table)
    eg = jnp.asarray(_gate_expand(), BF16)
    r4 = Z_NSA * TQ
    kv_spec = lambda cb0: pl.BlockSpec((1, s, LANES), lambda b, g, i, cb0=cb0: (b, 0, cb0 + g))
    cmp_spec = lambda j0: pl.BlockSpec((1, 1, kcp, LANES), lambda b, g, i, j0=j0: (b, j0 + g, 0, 0))
    tile_spec = lambda w: pl.BlockSpec((1, r4, w), lambda b, g, i: (g, 0, 0))
    return pl.pallas_call(
        functools.partial(_nsa_kernel, n_slc=n_slc, n_sel=n_sel),
        grid=(bsz, G_NSA, nq),
        in_specs=[pl.BlockSpec((1, TQ, 2 * LANES), lambda b, g, i: (b, i, CB_NSAQ // 2 + g)),
                  cmp_spec(0), cmp_spec(2), cmp_spec(0), cmp_spec(2),
                  kv_spec(CB_KSLC), kv_spec(CB_VSLC), kv_spec(CB_KWIN), kv_spec(CB_VWIN),
                  pl.BlockSpec((1, TQ, LANES), lambda b, g, i: (b, i, CB_GATE)),
                  pl.BlockSpec((kcp, LANES), lambda b, g, i: (0, 0)),
                  pl.BlockSpec((kcp, LANES), lambda b, g, i: (0, 0)),
                  tile_spec(LANES), tile_spec(2 * TQ), tile_spec(WINDOW + TQ), tile_spec(LANES),
                  pl.BlockSpec((1, LANES, 3 * Z_NSA * DH), lambda b, g, i: (g, 0, 0))],
        out_specs=pl.BlockSpec((1, TQ, Z_NSA * DH), lambda b, g, i: (b, i, g)),
        out_shape=jax.ShapeDtypeStruct((bsz, s, N_NSA * DH), F32),
        compiler_params=_cparams("arbitrary", "arbitrary", "arbitrary"),
        name="nsa_attn",
    )(proj, cb, cb, cf, cf, proj, proj, proj, proj, proj,
      jnp.asarray(ov, BF16), jnp.asarray(ov, F32), bcn, bsn, bw, far, eg)


def _out_kernel(osb_ref, onsa_ref, x_ref, gsb_ref, gnsa_ref, w_ref, gpost_ref, gate_ref,
                gpre_ref, sc_ref, sh_ref, xo_ref, h_ref):
    half = osb_ref.shape[2]
    a = _rms(osb_ref[0], gsb_ref[...]).astype(BF16)
    b = _rms(onsa_ref[0], gnsa_ref[...]).astype(BF16)
    m = _dot(a, w_ref[:half, :]) + _dot(b, w_ref[half:, :])
    x = x_ref[0] + gate_ref[0] * _rms(m, gpost_ref[...])
    xo_ref[0] = x
    h_ref[0] = (_rms(x, gpre_ref[...]) * (1.0 + sc_ref[0]) + sh_ref[0]).astype(BF16)


def _out_proj(o_sb, o_nsa, x, g_sb, g_nsa, w_out, g_post, gate_m, g_pre_ffn, scale_f, shift_f):
    bsz, s, d = x.shape
    half = o_sb.shape[2]
    tm = min(512, s)
    row = lambda n: pl.BlockSpec((1, n), lambda b, i: (0, 0))
    mod = pl.BlockSpec((1, 1, d), lambda b, i: (b, 0, 0))
    act = lambda n: pl.BlockSpec((1, tm, n), lambda b, i: (b, i, 0))
    return pl.pallas_call(
        _out_kernel,
        grid=(bsz, s // tm),
        in_specs=[act(half), act(half), act(d), row(half), row(half),
                  pl.BlockSpec((2 * half, d), lambda b, i: (0, 0)), row(d), mod, row(d), mod, mod],
        out_specs=[act(d), act(d)],
        out_shape=[jax.ShapeDtypeStruct((bsz, s, d), F32), jax.ShapeDtypeStruct((bsz, s, d), BF16)],
        compiler_params=_cparams("arbitrary", "arbitrary"),
        name="out_proj",
    )(o_sb, o_nsa, x, g_sb.reshape(1, half), g_nsa.reshape(1, half), w_out.astype(BF16),
      g_post.reshape(1, d), gate_m.reshape(bsz, 1, d), g_pre_ffn.reshape(1, d),
      scale_f.reshape(bsz, 1, d), shift_f.reshape(bsz, 1, d))


def _silu(x):
    return x * (1.0 / (1.0 + jnp.exp(-x)))


def _ffn_kernel(h_ref, x_ref, wg_ref, wu_ref, wd_ref, gpost_ref, gate_ref, o_ref, acc_ref):
    j = pl.program_id(2)
    h = h_ref[0]
    a = (_silu(_dot(h, wg_ref[...])) * _dot(h, wu_ref[...])).astype(BF16)
    part = _dot(a, wd_ref[...])

    @pl.when(j == 0)
    def _():
        acc_ref[...] = part

    @pl.when(j > 0)
    def _():
        acc_ref[...] += part

    @pl.when(j == pl.num_programs(2) - 1)
    def _():
        o_ref[0] = x_ref[0] + gate_ref[0] * _rms(acc_ref[...], gpost_ref[...])


def _dense_ffn(h, x, w_gate, w_up, w_down, g_post, gate_f):
    bsz, s, d = x.shape
    ff = w_gate.shape[1]
    tm = min(1024, s)
    tf = 256
    assert ff % tf == 0
    act = pl.BlockSpec((1, tm, d), lambda b, i, j: (b, i, 0))
    return pl.pallas_call(
        _ffn_kernel,
        grid=(bsz, s // tm, ff // tf),
        in_specs=[act, act,
                  pl.BlockSpec((d, tf), lambda b, i, j: (0, j)),
                  pl.BlockSpec((d, tf), lambda b, i, j: (0, j)),
                  pl.BlockSpec((tf, d), lambda b, i, j: (j, 0)),
                  pl.BlockSpec((1, d), lambda b, i, j: (0, 0)),
                  pl.BlockSpec((1, 1, d), lambda b, i, j: (b, 0, 0))],
        out_specs=act,
        out_shape=jax.ShapeDtypeStruct((bsz, s, d), F32),
        scratch_shapes=[pltpu.VMEM((tm, d), F32)],
        compiler_params=_cparams("arbitrary", "arbitrary", "arbitrary"),
        name="dense_ffn",
    )(h, x, w_gate.astype(BF16), w_up.astype(BF16), w_down.astype(BF16),
      g_post.reshape(1, d), gate_f.reshape(bsz, 1, d))


def _router_kernel(x_ref, gpre_ref, sc_ref, sh_ref, wr_ref, br_ref, pos_ref, w_ref, post_ref, cnt_ref):
    tile = x_ref.shape[1]
    h = _rms(x_ref[0], gpre_ref[...]) * (1.0 + sc_ref[0]) + sh_ref[0]
    logits = _dot_f32(h, wr_ref[...]) + br_ref[...]
    lane = lax.broadcasted_iota(I32, (tile, LANES), 1)
    lanef = lane.astype(F32)
    e = jnp.exp(logits - jnp.max(logits, axis=1, keepdims=True))
    probs = e / jnp.sum(e, axis=1, keepdims=True)
    probs = jnp.where(lane < N_EXPERTS, probs, -1.0)
    m1 = jnp.max(probs, axis=1, keepdims=True)
    i1 = jnp.min(jnp.where(probs == m1, lanef, 1e9), axis=1, keepdims=True)
    rest = jnp.where(lanef == i1, -1.0, probs)
    m2 = jnp.max(rest, axis=1, keepdims=True)
    i2 = jnp.min(jnp.where(rest == m2, lanef, 1e9), axis=1, keepdims=True)
    tot = m1 + m2
    wgt = jnp.where(lanef == i1, m1 / tot, jnp.where(lanef == i2, m2 / tot, 0.0))
    mask = (lanef == i1) | (lanef == i2)
    ch = 256
    r = lax.broadcasted_iota(I32, (ch, ch), 0)
    c = lax.broadcasted_iota(I32, (ch, ch), 1)
    lower = jnp.where(c < r, 1.0, 0.0).astype(BF16)
    ones = jnp.ones((8, ch), BF16)
    carry = jnp.zeros((1, LANES), F32)
    ranks = []
    for k in range(tile // ch):
        mk = jnp.where(mask[k * ch:(k + 1) * ch], 1.0, 0.0).astype(BF16)
        ranks.append(_dot(lower, mk) + carry)
        carry = carry + _dot(ones, mk)[0:1]
    rank = jnp.concatenate(ranks, axis=0)
    pos = jnp.where(mask, rank, -1.0)
    pos_ref[0] = pos.astype(I32)
    w_ref[0] = wgt
    post_ref[0] = jnp.transpose(pos)[0:N_EXPERTS].astype(I32)
    cnt_ref[0] = jnp.broadcast_to(carry, (8, LANES)).astype(I32)


def _router(x, g_pre, scale_f, shift_f, w_router, b_router, tile):
    bsz, s, d = x.shape
    nt = s // tile
    wr = jnp.zeros((d, LANES), F32).at[:, :N_EXPERTS].set(w_router.astype(F32))
    br = jnp.full((1, LANES), NEG, F32).at[0, :N_EXPERTS].set(b_router.astype(F32))
    mod = pl.BlockSpec((1, 1, d), lambda b, i: (b, 0, 0))
    tok = lambda n: pl.BlockSpec((1, tile, n), lambda b, i: (b * nt + i, 0, 0))
    pos, wgt, post, cnt = pl.pallas_call(
        _router_kernel,
        grid=(bsz, nt),
        in_specs=[pl.BlockSpec((1, tile, d), lambda b, i: (b, i, 0)),
                  pl.BlockSpec((1, d), lambda b, i: (0, 0)), mod, mod,
                  pl.BlockSpec((d, LANES), lambda b, i: (0, 0)),
                  pl.BlockSpec((1, LANES), lambda b, i: (0, 0))],
        out_specs=[tok(LANES), tok(LANES),
                   pl.BlockSpec((1, N_EXPERTS, tile), lambda b, i: (b * nt + i, 0, 0)),
                   pl.BlockSpec((1, 8, LANES), lambda b, i: (b * nt + i, 0, 0))],
        out_shape=[jax.ShapeDtypeStruct((bsz * nt, tile, LANES), I32),
                   jax.ShapeDtypeStruct((bsz * nt, tile, LANES), F32),
                   jax.ShapeDtypeStruct((bsz * nt, N_EXPERTS, tile), I32),
                   jax.ShapeDtypeStruct((bsz * nt, 8, LANES), I32)],
        compiler_params=_cparams("arbitrary", "arbitrary"),
        name="moe_router",
    )(x, g_pre.reshape(1, d), scale_f.reshape(bsz, 1, d), shift_f.reshape(bsz, 1, d), wr, br)
    return pos, wgt, post, cnt[:, 0, :N_EXPERTS]


def _moe_kernel(cnt_ref, h_ref, pos_ref, w_ref, post_ref, wg_ref, wu_ref, wd_ref, o_ref,
                xg_ref, y_ref, *, rc):
    i, e, j = pl.program_id(0), pl.program_id(1), pl.program_id(2)
    tile = h_ref.shape[1]
    n_chunk = lax.shift_right_logical(cnt_ref[i * N_EXPERTS + e] + (rc - 1), int(math.log2(rc)))
    lane = lax.broadcasted_iota(I32, (tile, LANES), 1)

    @pl.when(jnp.logical_and(e == 0, j == 0))
    def _():
        o_ref[0] = jnp.zeros(o_ref.shape[1:], F32)

    @pl.when(j == 0)
    def _():
        prow = post_ref[0]
        rid = lax.broadcasted_iota(I32, (rc, tile), 0)

        def gather(c, _):
            onehot = jnp.where(prow == rid + c * rc, 1.0, 0.0).astype(BF16)
            r0 = pl.multiple_of(c * rc, rc)
            xg_ref[pl.ds(r0, rc), :] = _dot(onehot, h_ref[0]).astype(BF16)
            return 0

        lax.fori_loop(0, n_chunk, gather, 0)

    def expert(c, _):
        r0 = pl.multiple_of(c * rc, rc)
        xg = xg_ref[pl.ds(r0, rc), :]
        a = (_silu(_dot(xg, wg_ref[0])) * _dot(xg, wu_ref[0])).astype(BF16)
        part = _dot(a, wd_ref[0])

        @pl.when(j == 0)
        def _():
            y_ref[pl.ds(r0, rc), :] = part

        @pl.when(j > 0)
        def _():
            y_ref[pl.ds(r0, rc), :] += part

        return 0

    lax.fori_loop(0, n_chunk, expert, 0)

    @pl.when(j == pl.num_programs(2) - 1)
    def _():
        pcol = jnp.sum(jnp.where(lane == e, pos_ref[0], 0), axis=1, keepdims=True)
        wcol = jnp.sum(jnp.where(lane == e, w_ref[0], 0.0), axis=1, keepdims=True)
        cid = lax.broadcasted_iota(I32, (tile, rc), 1)

        def scatter(c, z):
            r0 = pl.multiple_of(c * rc, rc)
            onehot_t = jnp.where(pcol == cid + c * rc, 1.0, 0.0).astype(BF16)
            return z + _dot(onehot_t, y_ref[pl.ds(r0, rc), :].astype(BF16))

        z = lax.fori_loop(0, n_chunk, scatter, jnp.zeros((tile, o_ref.shape[2]), F32))
        o_ref[0] += wcol * z


def _moe_ffn(h, pos, wgt, post, cnt, w_gate, w_up, w_down, tile, rc=128, tf=512):
    t_tokens, d = h.shape
    nt = t_tokens // tile
    ff = w_gate.shape[2]
    assert ff % tf == 0
    hb = h.reshape(nt, tile, d)
    post3 = post.reshape(nt * N_EXPERTS, 1, tile)
    grid_spec = pltpu.PrefetchScalarGridSpec(
        num_scalar_prefetch=1,
        grid=(nt, N_EXPERTS, ff // tf),
        in_specs=[pl.BlockSpec((1, tile, d), lambda i, e, j, c: (i, 0, 0)),
                  pl.BlockSpec((1, tile, LANES), lambda i, e, j, c: (i, 0, 0)),
                  pl.BlockSpec((1, tile, LANES), lambda i, e, j, c: (i, 0, 0)),
                  pl.BlockSpec((1, 1, tile), lambda i, e, j, c: (i * N_EXPERTS + e, 0, 0)),
                  pl.BlockSpec((1, d, tf), lambda i, e, j, c: (e, 0, j)),
                  pl.BlockSpec((1, d, tf), lambda i, e, j, c: (e, 0, j)),
                  pl.BlockSpec((1, tf, d), lambda i, e, j, c: (e, j, 0))],
        out_specs=pl.BlockSpec((1, tile, d), lambda i, e, j, c: (i, 0, 0)),
        scratch_shapes=[pltpu.VMEM((tile, d), BF16), pltpu.VMEM((tile, d), F32)])
    out = pl.pallas_call(
        functools.partial(_moe_kernel, rc=rc),
        grid_spec=grid_spec,
        out_shape=jax.ShapeDtypeStruct((nt, tile, d), F32),
        compiler_params=_cparams("arbitrary", "arbitrary", "arbitrary"),
        name="moe_ffn",
    )(cnt.reshape(-1), hb, pos, wgt, post3, w_gate.astype(BF16), w_up.astype(BF16), w_down.astype(BF16))
    return out.reshape(t_tokens, d)


def _post_kernel(x_ref, f_ref, gpost_ref, gate_ref, o_ref):
    o_ref[0] = x_ref[0] + gate_ref[0] * _rms(f_ref[0], gpost_ref[...])


def _post_residual(x, f, g_post, gate_f):
    bsz, s, d = x.shape
    tm = min(1024, s)
    act = pl.BlockSpec((1, tm, d), lambda b, i: (b, i, 0))
    return pl.pallas_call(
        _post_kernel,
        grid=(bsz, s // tm),
        in_specs=[act, act, pl.BlockSpec((1, d), lambda b, i: (0, 0)),
                  pl.BlockSpec((1, 1, d), lambda b, i: (b, 0, 0))],
        out_specs=act,
        out_shape=jax.ShapeDtypeStruct((bsz, s, d), F32),
        compiler_params=_cparams("arbitrary", "arbitrary"),
        name="post_residual",
    )(x, f, g_post.reshape(1, d), gate_f.reshape(bsz, 1, d))


def _mixer(x, mod, layer, rel_table, g_pre_mix, w_in, cmp_params):
    shift_m, scale_m = mod[:, 0], mod[:, 1]
    proj = _in_proj(x, g_pre_mix, scale_m, shift_m, _arrange_w_in(w_in))
    o_sb = _sb_attention(proj)
    o_nsa = _nsa_attention(proj, _compress(proj, *cmp_params), rel_table)
    return o_sb, o_nsa


def kernel(x, c, rel_table, w_ada, b_ada, g_pre_mix, g_post_mix, g_pre_ffn, g_post_ffn, w_in, w_out, g_sb, g_nsa, cmp_pos_k, cmp_w1_k, cmp_w2_k, cmp_pos_v, cmp_w1_v, cmp_w2_v, ffn_w_gate, ffn_w_up, ffn_w_down, moe_w_router, moe_b_router, moe_w_gate, moe_w_up, moe_w_down):
    bsz, s, d = x.shape
    depth = w_in.shape[0]
    mods = _ada(c, w_ada, b_ada).reshape(depth, bsz, 6, d)
    moe_tile = min(1024, s)
    for layer in range(depth):
        mod = mods[layer]
        cmp_params = (cmp_pos_k[layer], cmp_w1_k[layer], cmp_w2_k[layer],
                      cmp_pos_v[layer], cmp_w1_v[layer], cmp_w2_v[layer])
        o_sb, o_nsa = _mixer(x, mod, layer, rel_table, g_pre_mix[layer], w_in[layer], cmp_params)
        x, h = _out_proj(o_sb, o_nsa, x, g_sb[layer], g_nsa[layer], w_out[layer], g_post_mix[layer],
                         mod[:, 2], g_pre_ffn[layer], mod[:, 4], mod[:, 3])
        i = layer // 2
        if layer % 2 == 0:
            x = _dense_ffn(h, x, ffn_w_gate[i], ffn_w_up[i], ffn_w_down[i], g_post_ffn[layer], mod[:, 5])
        else:
            pos, wgt, post, cnt = _router(x, g_pre_ffn[layer], mod[:, 4], mod[:, 3],
                                          moe_w_router[i], moe_b_router[i], moe_tile)
            f = _moe_ffn(h.reshape(bsz * s, d), pos, wgt, post, cnt,
                         moe_w_gate[i], moe_w_up[i], moe_w_down[i], moe_tile)
            x = _post_residual(x, f.reshape(bsz, s, d), g_post_ffn[layer], mod[:, 5])
    return x
```

```python
import functools
import math

import numpy as np
import jax
import jax.numpy as jnp
from jax import lax
from jax.experimental import pallas as pl
from jax.experimental.pallas import tpu as pltpu

F32 = jnp.float32
BF16 = jnp.bfloat16
I32 = jnp.int32

LANES = 128
DH = 64
N_SB = 8
N_NSA = 8
G_NSA = 2
Z_NSA = 4
CMP_BLOCK = 32
CMP_STRIDE = 16
SLC_BLOCK = 64
N_SELECT = 16
WINDOW = 512
REL_BUCKETS = 32
REL_MAX_DIST = 128
N_EXPERTS = 8
EPS = 1e-6
FORCED = 1e4
NEG = -1e30
M_INIT = -1e29
SB_EXIT = -104.5
VMEM_LIMIT = 56 * 1024 * 1024

TQ = 128
KC_FRONT = 16
SEL_FAR_TILE = 512


def _cparams(*sem):
    return pltpu.CompilerParams(dimension_semantics=sem, vmem_limit_bytes=VMEM_LIMIT)


def _nt(a, b):
    return lax.dot_general(a, b, (((1,), (1,)), ((), ())), preferred_element_type=F32)


def _dot(a, b):
    return jnp.dot(a, b, preferred_element_type=F32)


def _split3(a):
    hi = a.astype(BF16)
    r = a - hi.astype(F32)
    mid = r.astype(BF16)
    lo = (r - mid.astype(F32)).astype(BF16)
    return hi, mid, lo


def _dot_hl(a, b):
    hi = a.astype(BF16)
    lo = (a - hi.astype(F32)).astype(BF16)
    return _dot(hi, b) + _dot(lo, b)


def _dot_f32(a, b):
    ah, am, al = _split3(a)
    bh, bm, bl = _split3(b)
    return (_dot(ah, bh) + (_dot(ah, bm) + _dot(am, bh))
            + (_dot(ah, bl) + _dot(am, bm) + _dot(al, bh)))


def _rms(x, g):
    return x * lax.rsqrt(jnp.mean(x * x, axis=-1, keepdims=True) + EPS) * g


def _ada_kernel(c_ref, w_ref, b_ref, o_ref):
    c = c_ref[...]
    ca = c * (1.0 / (1.0 + jnp.exp(-c)))
    o_ref[0] = _dot_f32(ca, w_ref[0]) + b_ref[0]


def _ada(c, w_ada, b_ada):
    depth, d, n = w_ada.shape
    bsz = c.shape[0]
    rows = 8
    tn = 1536
    cp = jnp.zeros((rows, d), F32).at[:bsz].set(c)
    out = pl.pallas_call(
        _ada_kernel,
        grid=(depth, n // tn),
        in_specs=[pl.BlockSpec((rows, d), lambda l, j: (0, 0)),
                  pl.BlockSpec((1, d, tn), lambda l, j: (l, 0, j)),
                  pl.BlockSpec((1, 1, tn), lambda l, j: (l, 0, j))],
        out_specs=pl.BlockSpec((1, rows, tn), lambda l, j: (l, 0, j)),
        out_shape=jax.ShapeDtypeStruct((depth, rows, n), F32),
        compiler_params=_cparams("arbitrary", "arbitrary"),
        name="ada_mod",
    )(cp, w_ada, b_ada.reshape(depth, 1, n))
    return out[:, :bsz]


def _in_kernel(x_ref, g_ref, sc_ref, sh_ref, w_ref, o_ref, *, cn):
    h = _rms(x_ref[0], g_ref[...]) * (1.0 + sc_ref[0]) + sh_ref[0]
    hb = h.astype(BF16)
    for j in range(w_ref.shape[1] // cn):
        o_ref[0, :, j * cn:(j + 1) * cn] = _dot(hb, w_ref[:, j * cn:(j + 1) * cn]).astype(BF16)


def _in_proj(x, g, scale, shift, w):
    bsz, s, d = x.shape
    nc = w.shape[1]
    tm = min(512, s)
    cn = nc // 3 if (nc // 3) % LANES == 0 else nc
    return pl.pallas_call(
        functools.partial(_in_kernel, cn=cn),
        grid=(bsz, s // tm),
        in_specs=[pl.BlockSpec((1, tm, d), lambda b, i: (b, i, 0)),
                  pl.BlockSpec((1, d), lambda b, i: (0, 0)),
                  pl.BlockSpec((1, 1, d), lambda b, i: (b, 0, 0)),
                  pl.BlockSpec((1, 1, d), lambda b, i: (b, 0, 0)),
                  pl.BlockSpec((d, nc), lambda b, i: (0, 0))],
        out_specs=pl.BlockSpec((1, tm, nc), lambda b, i: (b, i, 0)),
        out_shape=jax.ShapeDtypeStruct((bsz, s, nc), BF16),
        compiler_params=_cparams("arbitrary", "arbitrary"),
        name="in_proj",
    )(x, g.reshape(1, d), scale.reshape(bsz, 1, d), shift.reshape(bsz, 1, d), w)


CB_SBQ, CB_SBK, CB_SBV, CB_NSAQ = 0, 4, 8, 12
CB_KCMP, CB_VCMP = 16, 17
CB_KSLC, CB_VSLC, CB_KWIN, CB_VWIN = 18, 20, 22, 24
CB_GATE = 26
N_CB = 27


def _arrange_w_in(w_in):
    d = w_in.shape[0]
    w_sb = N_SB * DH
    off_nsa_q = 3 * w_sb
    off_kv = off_nsa_q + N_NSA * DH
    off_gate = off_kv + 3 * 2 * G_NSA * DH
    scale = DH ** -0.5

    def kv(br, kvi, g):
        lo = off_kv + ((br * 2 + kvi) * G_NSA + g) * DH
        return w_in[:, lo:lo + DH]

    cols = [w_in[:, 0:w_sb] * scale, w_in[:, w_sb:2 * w_sb], w_in[:, 2 * w_sb:3 * w_sb],
            w_in[:, off_nsa_q:off_kv] * scale,
            kv(0, 0, 0), kv(0, 0, 1), kv(0, 1, 0), kv(0, 1, 1)]
    for br in (1, 2):
        for kvi in (0, 1):
            for g in range(G_NSA):
                cols += [kv(br, kvi, g), kv(br, kvi, g)]
    n_gate = 3 * N_NSA
    cols += [w_in[:, off_gate:off_gate + n_gate], jnp.zeros((d, LANES - n_gate), w_in.dtype)]
    out = jnp.concatenate(cols, axis=1).astype(BF16)
    assert out.shape[1] == N_CB * LANES
    return out


def _sb_kernel(q_ref, k_ref, v_ref, o_ref, *, t):
    qi = pl.program_id(2)
    q = q_ref[0]
    lane = lax.broadcasted_iota(I32, (t, LANES), 1)
    row = lax.broadcasted_iota(I32, (t, t), 0)
    col = lax.broadcasted_iota(I32, (t, t), 1)
    upper = jnp.where(row > col, 1.0, 0.0).astype(BF16)
    ones = jnp.ones((t, LANES), BF16)
    causal = col < row
    rep = t // LANES

    def tile(qh, kt, carry, acc, diag):
        k = k_ref[0, pl.ds(kt * t, t), :]
        v = v_ref[0, pl.ds(kt * t, t), :]
        z = _nt(qh, k)
        lk = -(jnp.maximum(z, 0.0) + jnp.log(1.0 + jnp.exp(-jnp.abs(z))))
        if diag:
            lk = jnp.where(causal, lk, 0.0)
        hi = lk.astype(BF16)
        lo = (lk - hi.astype(F32)).astype(BF16)
        later = _dot(hi, upper) + _dot(lo, upper)
        tot = _dot(hi, ones) + _dot(lo, ones)
        cb = carry if rep == 1 else jnp.concatenate([carry] * rep, axis=1)
        w = jnp.exp(z + lk + later + cb)
        if diag:
            w = jnp.where(causal, w, 0.0)
        return carry + tot, acc + _dot(w.astype(BF16), v)

    outs = []
    for h in range(2):
        qh = jnp.where((lane >= DH) == (h == 1), q, jnp.zeros_like(q))
        zero = jnp.zeros((t, LANES), F32)
        carry, acc = tile(qh, qi, zero, zero, True)

        def cond(st):
            kt, mx, _, _ = st
            return jnp.logical_and(kt >= 0, mx > SB_EXIT)

        def body(st, qh=qh):
            kt, _, carry, acc = st
            carry, acc = tile(qh, kt, carry, acc, False)
            return kt - 1, jnp.max(carry), carry, acc

        _, _, _, acc = lax.while_loop(cond, body, (qi - 1, jnp.max(carry), carry, acc))
        outs.append(acc)
    o_ref[0] = jnp.where(lane < DH, outs[0], outs[1])


def _sb_attention(proj, t=256):
    bsz, s, _ = proj.shape
    t = min(t, s)
    npair = N_SB // 2
    return pl.pallas_call(
        functools.partial(_sb_kernel, t=t),
        grid=(bsz, npair, s // t),
        in_specs=[pl.BlockSpec((1, t, LANES), lambda b, j, i: (b, i, CB_SBQ + j)),
                  pl.BlockSpec((1, s, LANES), lambda b, j, i: (b, 0, CB_SBK + j)),
                  pl.BlockSpec((1, s, LANES), lambda b, j, i: (b, 0, CB_SBV + j))],
        out_specs=pl.BlockSpec((1, t, LANES), lambda b, j, i: (b, i, j)),
        out_shape=jax.ShapeDtypeStruct((bsz, s, N_SB * DH), F32),
        compiler_params=_cparams("arbitrary", "arbitrary", "arbitrary"),
        name="sb_attn",
    )(proj, proj, proj)


def _gelu_tanh(x):
    return 0.5 * x * (1.0 + jnp.tanh(math.sqrt(2.0 / math.pi) * (x + 0.044715 * (x * x * x))))


def _cmp_kernel(xa_ref, xb_ref, pos_ref, w1_ref, w2_ref, o_ref):
    half = w1_ref.shape[1] // 2
    w1a = w1_ref[0, :half, :]
    w1b = w1_ref[0, half:, :]
    pos = pos_ref[0]
    bias = _dot(pos[:, :half], w1a) + _dot(pos[:, half:], w1b)
    hid = _dot(xa_ref[0, 0], w1a) + _dot(xb_ref[0, 0], w1b) + bias[0:1, :]
    o_ref[0, 0] = _dot(_gelu_tanh(hid).astype(BF16), w2_ref[0])


def _compress(proj, pos_k, w1_k, w2_k, pos_v, w1_v, w2_v):
    bsz, s, _ = proj.shape
    n16 = s // CMP_STRIDE
    raw = proj[:, :, CB_KCMP * LANES:(CB_VCMP + 1) * LANES]
    x16 = raw.reshape(bsz, s, 4, DH).transpose(0, 2, 1, 3).reshape(bsz, 4, n16, CMP_STRIDE * DH)
    x16b = jnp.concatenate([x16[:, :, 1:], jnp.zeros_like(x16[:, :, :1])], axis=2)
    hidden = w1_k.shape[1]
    w1 = jnp.stack([w1_k, w1_v]).astype(BF16)
    w2 = jnp.stack([jnp.concatenate([w2_k, w2_k], 1), jnp.concatenate([w2_v, w2_v], 1)]).astype(BF16)
    pos = jnp.stack([pos_k.reshape(1, -1), pos_v.reshape(1, -1)])
    pos = jnp.concatenate([pos, jnp.zeros((2, 7, pos.shape[2]), pos.dtype)], axis=1).astype(BF16)
    k16 = CMP_STRIDE * DH
    return pl.pallas_call(
        _cmp_kernel,
        grid=(bsz, 4),
        in_specs=[pl.BlockSpec((1, 1, n16, k16), lambda b, j: (b, j, 0, 0)),
                  pl.BlockSpec((1, 1, n16, k16), lambda b, j: (b, j, 0, 0)),
                  pl.BlockSpec((1, 8, 2 * k16), lambda b, j: (j // 2, 0, 0)),
                  pl.BlockSpec((1, 2 * k16, hidden), lambda b, j: (j // 2, 0, 0)),
                  pl.BlockSpec((1, hidden, LANES), lambda b, j: (j // 2, 0, 0))],
        out_specs=pl.BlockSpec((1, 1, n16, LANES), lambda b, j: (b, j, 0, 0)),
        out_shape=jax.ShapeDtypeStruct((bsz, 4, n16, LANES), F32),
        compiler_params=_cparams("arbitrary", "arbitrary"),
        name="nsa_compress",
    )(x16, x16b, pos, w1, w2)


def _bucket_table():
    n = np.arange(REL_MAX_DIST + 1)
    exact = REL_BUCKETS // 2
    val = (np.log(np.maximum(n, 1).astype(np.float32) / np.float32(exact)).astype(np.float32)
           / np.float32(math.log(REL_MAX_DIST / exact)) * np.float32(REL_BUCKETS - exact))
    large = np.minimum(exact + val.astype(np.int32), REL_BUCKETS - 1)
    return np.where(n < exact, n, large).astype(np.int32)


def _bias_tiles(rel_table):
    tab = rel_table[_bucket_table()]
    r = np.arange(TQ)[:, None]

    def tile(dist, valid, base):
        idx = np.clip(dist, 0, REL_MAX_DIST)
        t = jnp.where(valid[:, :, None], tab[idx] - base, NEG)
        t = jnp.transpose(t, (2, 0, 1))
        return t.reshape(G_NSA, Z_NSA * TQ, dist.shape[1]).astype(F32)

    w = np.arange(LANES)[None, :]
    d_c = r - CMP_STRIDE * (w - KC_FRONT) - (CMP_BLOCK - 1)
    w2 = np.arange(2 * TQ)[None, :]
    d_s = r + TQ - w2
    w5 = np.arange(WINDOW + TQ)[None, :]
    d_w = r + WINDOW - w5
    far = jnp.broadcast_to(tab[REL_MAX_DIST][:, None, None], (N_NSA, TQ, LANES))
    far = far.reshape(G_NSA, Z_NSA * TQ, LANES).astype(F32)
    return (tile(d_c, d_c >= 0, tab[REL_MAX_DIST]), tile(d_s, d_s >= 0, 0.0),
            tile(d_w, (d_w >= 0) & (d_w < WINDOW), 0.0), far)


def _gate_expand():
    e = np.zeros((G_NSA, LANES, 3 * Z_NSA * DH), np.float32)
    for g in range(G_NSA):
        for z in range(Z_NSA):
            for br in range(3):
                e[g, (g * Z_NSA + z) * 3 + br, br * Z_NSA * DH + z * DH: br * Z_NSA * DH + (z + 1) * DH] = 1.0
    return e


def _overlap_padded(n16, kcp, n_slc):
    i = np.arange(kcp)[:, None] - KC_FRONT
    j = np.arange(LANES)[None, :]
    n_cmp = n16 - 1
    ok = (i >= 0) & (i < n_cmp) & (j < n_slc)
    ov = (i * CMP_STRIDE < j * SLC_BLOCK + SLC_BLOCK) & (i * CMP_STRIDE + CMP_BLOCK > j * SLC_BLOCK)
    return (ok & ov).astype(np.float32)


def _block_onehot(s):
    return (np.arange(s)[:, None] // SLC_BLOCK == np.arange(LANES)[None, :]).astype(np.float32)


def _nsa_kernel(q_ref, kcb_ref, vcb_ref, kcf_ref, vcf_ref, ks_ref, vs_ref, kw_ref, vw_ref, et_ref,
                gate_ref, ovb_ref, ovf_ref, bcn_ref, bsn_ref, bw_ref, cf_ref, eg_ref, o_ref,
                sb0_ref, sb1_ref, *, n_slc, n_round):
    blk = pl.program_id(2)
    c0 = blk * TQ
    r4 = Z_NSA * TQ
    q = q_ref[0]
    lane = lax.broadcasted_iota(I32, (TQ, LANES), 1)
    lo_half = lane < DH
    zq = jnp.zeros((TQ, LANES), BF16)
    qa, qb = q[:, :LANES], q[:, LANES:]
    qs = jnp.concatenate([jnp.where(lo_half, qa, zq), jnp.where(lo_half, zq, qa),
                          jnp.where(lo_half, qb, zq), jnp.where(lo_half, zq, qb)], axis=0)
    ones_k = jnp.ones((SEL_FAR_TILE, LANES), BF16)
    ones_w = jnp.ones((WINDOW + TQ, LANES), BF16)

    kcp = kcb_ref.shape[2]
    prow = lax.broadcasted_iota(I32, (1, kcp), 1)
    far_row = jnp.where((prow >= KC_FRONT) & (prow < 8 * blk), 0.0, NEG)
    lane1 = lax.broadcasted_iota(I32, (1, LANES), 1)
    near_row = jnp.where(lane1 + 8 * blk >= KC_FRONT, 0.0, NEG)
    near0 = pl.multiple_of(8 * blk, 8)
    kcn = kcf_ref[0, 0, pl.ds(near0, LANES), :].astype(BF16)
    vcn = vcf_ref[0, 0, pl.ds(near0, LANES), :].astype(BF16)
    s_far = _nt(qs, kcb_ref[0, 0]) + far_row
    s_near = _nt(qs, kcn) + bcn_ref[0] + near_row
    m_c = jnp.maximum(jnp.max(s_far, axis=1, keepdims=True), jnp.max(s_near, axis=1, keepdims=True))
    m_c = jnp.maximum(m_c, M_INIT)
    p_far = jnp.exp(s_far - m_c)
    p_near = jnp.exp(s_near - m_c)
    den = jnp.sum(p_far, axis=1, keepdims=True) + jnp.sum(p_near, axis=1, keepdims=True)
    inv_c = 1.0 / jnp.maximum(den, 1e-30)
    o_c = (_dot(p_far.astype(BF16), vcb_ref[0, 0]) + _dot(p_near.astype(BF16), vcn)) * inv_c

    pn_far = p_far * inv_c
    pn_near = p_near * inv_c
    pz_far = pn_far[0:TQ] + pn_far[TQ:2 * TQ] + pn_far[2 * TQ:3 * TQ] + pn_far[3 * TQ:]
    pz_near = pn_near[0:TQ] + pn_near[TQ:2 * TQ] + pn_near[2 * TQ:3 * TQ] + pn_near[3 * TQ:]
    ovn = ovf_ref[pl.ds(near0, LANES), :].astype(BF16)
    imp = _dot_hl(pz_far, ovb_ref[...]) + _dot_hl(pz_near, ovn)
    tpos = c0 + lax.broadcasted_iota(I32, (TQ, LANES), 0)
    cur = lax.shift_right_logical(tpos, 6)
    valid = (lane * SLC_BLOCK <= tpos) & (lane < n_slc)
    forced = valid & ((lane == 0) | (lane == cur) | (lane == cur - 1))
    score = jnp.where(valid & jnp.logical_not(forced), imp, -1.0)
    score = jnp.where(lane < n_slc, score, -2.0)

    rest, taken, thr = score, jnp.zeros((TQ, 1), F32), jnp.zeros((TQ, 1), F32)
    for _ in range(n_round):
        m = jnp.max(rest, axis=1, keepdims=True)
        eq = rest == m
        thr = jnp.where(taken < n_round, m, thr)
        taken = taken + jnp.sum(jnp.where(eq, 1.0, 0.0), axis=1, keepdims=True)
        rest = jnp.where(eq, -jnp.inf, rest)
    above = score > thr
    at_thr = score == thr
    need = n_round - jnp.sum(jnp.where(above, 1.0, 0.0), axis=1, keepdims=True)
    brow = lax.broadcasted_iota(I32, (LANES, LANES), 0)
    bcol = lax.broadcasted_iota(I32, (LANES, LANES), 1)
    before = _dot(jnp.where(at_thr, 1.0, 0.0).astype(BF16),
                  jnp.where(brow < bcol, 1.0, 0.0).astype(BF16))
    sel = ((above | (at_thr & (before < need))) & valid) | forced

    sel_all = jnp.where(sel, 0.0, NEG).astype(BF16)
    sel_far = jnp.where(sel & (lane < 2 * blk - 2), 0.0, NEG).astype(BF16)
    q_far = jnp.concatenate([qs, jnp.concatenate([sel_far] * Z_NSA, axis=0)], axis=1)
    q_near = jnp.concatenate([qs, jnp.concatenate([sel_all] * Z_NSA, axis=0)], axis=1)
    tk = SEL_FAR_TILE
    last_tile = ks_ref.shape[1] // tk - 1

    def qk(kt):
        k0 = pl.multiple_of(kt * tk, tk)
        return _nt(q_far, jnp.concatenate([ks_ref[0, pl.ds(k0, tk), :], et_ref[pl.ds(k0, tk), :]], axis=1))

    def fold(s, v, st):
        m, acc = st
        mn = jnp.maximum(m, jnp.max(s, axis=1, keepdims=True))
        p = jnp.exp(s - mn).astype(BF16)
        return mn, jnp.exp(m - mn) * acc + _dot(p, v)

    def v_tile(kt):
        k0 = pl.multiple_of(kt * tk, tk)
        return jnp.concatenate([vs_ref[0, pl.ds(k0, tk), :], ones_k], axis=1)

    n_pair = lax.shift_right_logical(jnp.maximum(blk - 1, 0) + 7, 3)
    sb0_ref[...] = qk(0)

    def far_step(i, st):
        sb1_ref[...] = qk(2 * i + 1)
        st = fold(sb0_ref[...], v_tile(2 * i), st)
        sb0_ref[...] = qk(jnp.minimum(2 * i + 2, last_tile))
        return fold(sb1_ref[...], v_tile(2 * i + 1), st)

    m0 = jnp.full((r4, 1), M_INIT, F32)
    m_s, acc_s = lax.fori_loop(0, n_pair, far_step, (m0, jnp.zeros((r4, 2 * LANES), F32)))
    m_s = m_s + cf_ref[0][:, 0:1]
    p0 = pl.multiple_of(jnp.maximum(c0 - TQ, 0), TQ)
    d0 = pl.multiple_of(c0, TQ)
    kn = jnp.concatenate([ks_ref[0, pl.ds(p0, TQ), :], ks_ref[0, pl.ds(d0, TQ), :]], axis=0)
    vn = jnp.concatenate([vs_ref[0, pl.ds(p0, TQ), :], vs_ref[0, pl.ds(d0, TQ), :]], axis=0)
    nrow = lax.broadcasted_iota(I32, (2 * TQ, LANES), 0)
    nlane = lax.broadcasted_iota(I32, (2 * TQ, LANES), 1)
    e_n = jnp.where(lax.shift_right_logical(nrow, 6) + 2 * blk - 2 == nlane, 1.0, 0.0).astype(BF16)
    lane2 = lax.broadcasted_iota(I32, (1, 2 * TQ), 1)
    prev_row = jnp.where((lane2 < TQ) & (blk == 0), NEG, 0.0)
    s = _nt(q_near, jnp.concatenate([kn, e_n], axis=1)) + bsn_ref[0] + prev_row
    _, acc_s = fold(s, jnp.concatenate([vn, ones_k[:2 * TQ]], axis=1), (m_s, acc_s))
    o_s = acc_s[:, :LANES] / acc_s[:, LANES:]

    n_w = WINDOW // TQ + 1
    kws, vws = [], []
    for w in range(n_w):
        st = pl.multiple_of(jnp.maximum(c0 - WINDOW + w * TQ, 0), TQ)
        kws.append(kw_ref[0, pl.ds(st, TQ), :])
        vws.append(vw_ref[0, pl.ds(st, TQ), :])
    wl = lax.broadcasted_iota(I32, (1, WINDOW + TQ), 1)
    pos_row = jnp.where(c0 - WINDOW + wl >= 0, 0.0, NEG)
    s = _nt(qs, jnp.concatenate(kws, axis=0)) + bw_ref[0] + pos_row
    p = jnp.exp(s - jnp.max(s, axis=1, keepdims=True)).astype(BF16)
    r_w = _dot(p, jnp.concatenate([jnp.concatenate(vws, axis=0), ones_w], axis=1))
    o_w = r_w[:, :LANES] / r_w[:, LANES:]

    gl = _dot(gate_ref[0], eg_ref[0])
    sig = 1.0 / (1.0 + jnp.exp(-gl))
    wide = Z_NSA * DH

    def heads(o):
        return jnp.concatenate([jnp.where(lo_half, o[0:TQ], o[TQ:2 * TQ]),
                                jnp.where(lo_half, o[2 * TQ:3 * TQ], o[3 * TQ:])], axis=1)

    o_ref[0] = (sig[:, 0:wide] * heads(o_c) + sig[:, wide:2 * wide] * heads(o_s)
                + sig[:, 2 * wide:] * heads(o_w))


def _nsa_attention(proj, cmp_out, rel_table):
    bsz, s, _ = proj.shape
    assert s % (2 * SEL_FAR_TILE) == 0 and s >= WINDOW + TQ
    n16 = s // CMP_STRIDE
    n_slc = s // SLC_BLOCK
    assert N_SELECT <= n_slc <= LANES
    nq = s // TQ
    kcp = -(-(max(n16 + KC_FRONT, 8 * (nq - 1) + LANES)) // LANES) * LANES
    pad = ((0, 0), (0, 0), (KC_FRONT, kcp - n16 - KC_FRONT), (0, 0))
    cf = jnp.pad(cmp_out, pad)
    cb = cf.astype(BF16)
    ov = _overlap_padded(n16, kcp, n_slc)
    bcn, bsn, bw, far = _bias_tiles(rel_table)
    eg = jnp.asarray(_gate_expand(), BF16)
    r4 = Z_NSA * TQ
    kv_spec = lambda cb0: pl.BlockSpec((1, s, LANES), lambda b, g, i, cb0=cb0: (b, 0, cb0 + g))
    cmp_spec = lambda j0: pl.BlockSpec((1, 1, kcp, LANES), lambda b, g, i, j0=j0: (b, j0 + g, 0, 0))
    tile_spec = lambda w: pl.BlockSpec((1, r4, w), lambda b, g, i: (g, 0, 0))
    const2 = lambda n: pl.BlockSpec((n, LANES), lambda b, g, i: (0, 0))
    return pl.pallas_call(
        functools.partial(_nsa_kernel, n_slc=n_slc, n_round=N_SELECT - 3),
        grid=(bsz, G_NSA, nq),
        in_specs=[pl.BlockSpec((1, TQ, 2 * LANES), lambda b, g, i: (b, i, CB_NSAQ // 2 + g)),
                  cmp_spec(0), cmp_spec(2), cmp_spec(0), cmp_spec(2),
                  kv_spec(CB_KSLC), kv_spec(CB_VSLC), kv_spec(CB_KWIN), kv_spec(CB_VWIN),
                  const2(s),
                  pl.BlockSpec((1, TQ, LANES), lambda b, g, i: (b, i, CB_GATE)),
                  const2(kcp), const2(kcp),
                  tile_spec(LANES), tile_spec(2 * TQ), tile_spec(WINDOW + TQ), tile_spec(LANES),
                  pl.BlockSpec((1, LANES, 3 * Z_NSA * DH), lambda b, g, i: (g, 0, 0))],
        out_specs=pl.BlockSpec((1, TQ, Z_NSA * DH), lambda b, g, i: (b, i, g)),
        out_shape=jax.ShapeDtypeStruct((bsz, s, N_NSA * DH), F32),
        scratch_shapes=[pltpu.VMEM((r4, SEL_FAR_TILE), F32), pltpu.VMEM((r4, SEL_FAR_TILE), F32)],
        compiler_params=_cparams("arbitrary", "arbitrary", "arbitrary"),
        name="nsa_attn",
    )(proj, cb, cb, cf, cf, proj, proj, proj, proj, jnp.asarray(_block_onehot(s), BF16), proj,
      jnp.asarray(ov, BF16), jnp.asarray(ov, F32), bcn, bsn, bw, far, eg)


def _out_kernel(osb_ref, onsa_ref, x_ref, gsb_ref, gnsa_ref, w_ref, gpost_ref, gate_ref,
                gpre_ref, sc_ref, sh_ref, xo_ref, h_ref):
    half = osb_ref.shape[2]
    a = _rms(osb_ref[0], gsb_ref[...]).astype(BF16)
    b = _rms(onsa_ref[0], gnsa_ref[...]).astype(BF16)
    m = _dot(a, w_ref[:half, :]) + _dot(b, w_ref[half:, :])
    x = x_ref[0] + gate_ref[0] * _rms(m, gpost_ref[...])
    xo_ref[0] = x
    h_ref[0] = (_rms(x, gpre_ref[...]) * (1.0 + sc_ref[0]) + sh_ref[0]).astype(BF16)


def _out_proj(o_sb, o_nsa, x, g_sb, g_nsa, w_out, g_post, gate_m, g_pre_ffn, scale_f, shift_f):
    bsz, s, d = x.shape
    half = o_sb.shape[2]
    tm = min(512, s)
    row = lambda n: pl.BlockSpec((1, n), lambda b, i: (0, 0))
    mod = pl.BlockSpec((1, 1, d), lambda b, i: (b, 0, 0))
    act = lambda n: pl.BlockSpec((1, tm, n), lambda b, i: (b, i, 0))
    return pl.pallas_call(
        _out_kernel,
        grid=(bsz, s // tm),
        in_specs=[act(half), act(half), act(d), row(half), row(half),
                  pl.BlockSpec((2 * half, d), lambda b, i: (0, 0)), row(d), mod, row(d), mod, mod],
        out_specs=[act(d), act(d)],
        out_shape=[jax.ShapeDtypeStruct((bsz, s, d), F32), jax.ShapeDtypeStruct((bsz, s, d), BF16)],
        compiler_params=_cparams("arbitrary", "arbitrary"),
        name="out_proj",
    )(o_sb, o_nsa, x, g_sb.reshape(1, half), g_nsa.reshape(1, half), w_out.astype(BF16),
      g_post.reshape(1, d), gate_m.reshape(bsz, 1, d), g_pre_ffn.reshape(1, d),
      scale_f.reshape(bsz, 1, d), shift_f.reshape(bsz, 1, d))


def _silu(x):
    return x * (1.0 / (1.0 + jnp.exp(-x)))


def _ffn_kernel(h_ref, x_ref, wg_ref, wu_ref, wd_ref, gpost_ref, gate_ref, o_ref, acc_ref):
    j = pl.program_id(2)
    h = h_ref[0]
    a = (_silu(_dot(h, wg_ref[...])) * _dot(h, wu_ref[...])).astype(BF16)
    part = _dot(a, wd_ref[...])

    @pl.when(j == 0)
    def _():
        acc_ref[...] = part

    @pl.when(j > 0)
    def _():
        acc_ref[...] += part

    @pl.when(j == pl.num_programs(2) - 1)
    def _():
        o_ref[0] = x_ref[0] + gate_ref[0] * _rms(acc_ref[...], gpost_ref[...])


def _dense_ffn(h, x, w_gate, w_up, w_down, g_post, gate_f):
    bsz, s, d = x.shape
    ff = w_gate.shape[1]
    tm = min(1024, s)
    tf = 256
    assert ff % tf == 0
    act = pl.BlockSpec((1, tm, d), lambda b, i, j: (b, i, 0))
    return pl.pallas_call(
        _ffn_kernel,
        grid=(bsz, s // tm, ff // tf),
        in_specs=[act, act,
                  pl.BlockSpec((d, tf), lambda b, i, j: (0, j)),
                  pl.BlockSpec((d, tf), lambda b, i, j: (0, j)),
                  pl.BlockSpec((tf, d), lambda b, i, j: (j, 0)),
                  pl.BlockSpec((1, d), lambda b, i, j: (0, 0)),
                  pl.BlockSpec((1, 1, d), lambda b, i, j: (b, 0, 0))],
        out_specs=act,
        out_shape=jax.ShapeDtypeStruct((bsz, s, d), F32),
        scratch_shapes=[pltpu.VMEM((tm, d), F32)],
        compiler_params=_cparams("arbitrary", "arbitrary", "arbitrary"),
        name="dense_ffn",
    )(h, x, w_gate.astype(BF16), w_up.astype(BF16), w_down.astype(BF16),
      g_post.reshape(1, d), gate_f.reshape(bsz, 1, d))


def _router_kernel(x_ref, gpre_ref, sc_ref, sh_ref, wr_ref, br_ref, pos_ref, w_ref, post_ref, cnt_ref):
    tile = x_ref.shape[1]
    h = _rms(x_ref[0], gpre_ref[...]) * (1.0 + sc_ref[0]) + sh_ref[0]
    logits = _dot_f32(h, wr_ref[...]) + br_ref[...]
    lane = lax.broadcasted_iota(I32, (tile, LANES), 1)
    lanef = lane.astype(F32)
    e = jnp.exp(logits - jnp.max(logits, axis=1, keepdims=True))
    probs = e / jnp.sum(e, axis=1, keepdims=True)
    probs = jnp.where(lane < N_EXPERTS, probs, -1.0)
    m1 = jnp.max(probs, axis=1, keepdims=True)
    i1 = jnp.min(jnp.where(probs == m1, lanef, 1e9), axis=1, keepdims=True)
    rest = jnp.where(lanef == i1, -1.0, probs)
    m2 = jnp.max(rest, axis=1, keepdims=True)
    i2 = jnp.min(jnp.where(rest == m2, lanef, 1e9), axis=1, keepdims=True)
    tot = m1 + m2
    wgt = jnp.where(lanef == i1, m1 / tot, jnp.where(lanef == i2, m2 / tot, 0.0))
    mask = (lanef == i1) | (lanef == i2)
    ch = 256
    r = lax.broadcasted_iota(I32, (ch, ch), 0)
    c = lax.broadcasted_iota(I32, (ch, ch), 1)
    lower = jnp.where(c < r, 1.0, 0.0).astype(BF16)
    ones = jnp.ones((8, ch), BF16)
    carry = jnp.zeros((1, LANES), F32)
    ranks = []
    for k in range(tile // ch):
        mk = jnp.where(mask[k * ch:(k + 1) * ch], 1.0, 0.0).astype(BF16)
        ranks.append(_dot(lower, mk) + carry)
        carry = carry + _dot(ones, mk)[0:1]
    rank = jnp.concatenate(ranks, axis=0)
    pos = jnp.where(mask, rank, -1.0)
    pos_ref[0] = pos.astype(I32)
    w_ref[0] = wgt
    post_ref[0] = jnp.transpose(pos)[0:N_EXPERTS].astype(I32)
    cnt_ref[0] = jnp.broadcast_to(carry, (8, LANES)).astype(I32)


def _router(x, g_pre, scale_f, shift_f, w_router, b_router, tile):
    bsz, s, d = x.shape
    nt = s // tile
    wr = jnp.zeros((d, LANES), F32).at[:, :N_EXPERTS].set(w_router.astype(F32))
    br = jnp.full((1, LANES), NEG, F32).at[0, :N_EXPERTS].set(b_router.astype(F32))
    mod = pl.BlockSpec((1, 1, d), lambda b, i: (b, 0, 0))
    tok = lambda n: pl.BlockSpec((1, tile, n), lambda b, i: (b * nt + i, 0, 0))
    pos, wgt, post, cnt = pl.pallas_call(
        _router_kernel,
        grid=(bsz, nt),
        in_specs=[pl.BlockSpec((1, tile, d), lambda b, i: (b, i, 0)),
                  pl.BlockSpec((1, d), lambda b, i: (0, 0)), mod, mod,
                  pl.BlockSpec((d, LANES), lambda b, i: (0, 0)),
                  pl.BlockSpec((1, LANES), lambda b, i: (0, 0))],
        out_specs=[tok(LANES), tok(LANES),
                   pl.BlockSpec((1, N_EXPERTS, tile), lambda b, i: (b * nt + i, 0, 0)),
                   pl.BlockSpec((1, 8, LANES), lambda b, i: (b * nt + i, 0, 0))],
        out_shape=[jax.ShapeDtypeStruct((bsz * nt, tile, LANES), I32),
                   jax.ShapeDtypeStruct((bsz * nt, tile, LANES), F32),
                   jax.ShapeDtypeStruct((bsz * nt, N_EXPERTS, tile), I32),
                   jax.ShapeDtypeStruct((bsz * nt, 8, LANES), I32)],
        compiler_params=_cparams("arbitrary", "arbitrary"),
        name="moe_router",
    )(x, g_pre.reshape(1, d), scale_f.reshape(bsz, 1, d), shift_f.reshape(bsz, 1, d), wr, br)
    return pos, wgt, post, cnt[:, 0, :N_EXPERTS]


def _moe_kernel(cnt_ref, h_ref, pos_ref, w_ref, post_ref, wg_ref, wu_ref, wd_ref, o_ref,
                xg_ref, y_ref, *, rc):
    i, e, j = pl.program_id(0), pl.program_id(1), pl.program_id(2)
    tile = h_ref.shape[1]
    n_chunk = lax.shift_right_logical(cnt_ref[i * N_EXPERTS + e] + (rc - 1), int(math.log2(rc)))
    lane = lax.broadcasted_iota(I32, (tile, LANES), 1)

    @pl.when(jnp.logical_and(e == 0, j == 0))
    def _():
        o_ref[0] = jnp.zeros(o_ref.shape[1:], F32)

    @pl.when(j == 0)
    def _():
        prow = post_ref[0]
        rid = lax.broadcasted_iota(I32, (rc, tile), 0)

        def gather(c, _):
            onehot = jnp.where(prow == rid + c * rc, 1.0, 0.0).astype(BF16)
            r0 = pl.multiple_of(c * rc, rc)
            xg_ref[pl.ds(r0, rc), :] = _dot(onehot, h_ref[0]).astype(BF16)
            return 0

        lax.fori_loop(0, n_chunk, gather, 0)

    def expert(c, _):
        r0 = pl.multiple_of(c * rc, rc)
        xg = xg_ref[pl.ds(r0, rc), :]
        a = (_silu(_dot(xg, wg_ref[0])) * _dot(xg, wu_ref[0])).astype(BF16)
        part = _dot(a, wd_ref[0])

        @pl.when(j == 0)
        def _():
            y_ref[pl.ds(r0, rc), :] = part

        @pl.when(j > 0)
        def _():
            y_ref[pl.ds(r0, rc), :] += part

        return 0

    lax.fori_loop(0, n_chunk, expert, 0)

    @pl.when(j == pl.num_programs(2) - 1)
    def _():
        pcol = jnp.sum(jnp.where(lane == e, pos_ref[0], 0), axis=1, keepdims=True)
        wcol = jnp.sum(jnp.where(lane == e, w_ref[0], 0.0), axis=1, keepdims=True)
        cid = lax.broadcasted_iota(I32, (tile, rc), 1)

        def scatter(c, z):
            r0 = pl.multiple_of(c * rc, rc)
            onehot_t = jnp.where(pcol == cid + c * rc, 1.0, 0.0).astype(BF16)
            return z + _dot(onehot_t, y_ref[pl.ds(r0, rc), :].astype(BF16))

        z = lax.fori_loop(0, n_chunk, scatter, jnp.zeros((tile, o_ref.shape[2]), F32))
        o_ref[0] += wcol * z


def _moe_ffn(h, pos, wgt, post, cnt, w_gate, w_up, w_down, tile, rc=128, tf=512):
    t_tokens, d = h.shape
    nt = t_tokens // tile
    ff = w_gate.shape[2]
    assert ff % tf == 0
    hb = h.reshape(nt, tile, d)
    post3 = post.reshape(nt * N_EXPERTS, 1, tile)
    grid_spec = pltpu.PrefetchScalarGridSpec(
        num_scalar_prefetch=1,
        grid=(nt, N_EXPERTS, ff // tf),
        in_specs=[pl.BlockSpec((1, tile, d), lambda i, e, j, c: (i, 0, 0)),
                  pl.BlockSpec((1, tile, LANES), lambda i, e, j, c: (i, 0, 0)),
                  pl.BlockSpec((1, tile, LANES), lambda i, e, j, c: (i, 0, 0)),
                  pl.BlockSpec((1, 1, tile), lambda i, e, j, c: (i * N_EXPERTS + e, 0, 0)),
                  pl.BlockSpec((1, d, tf), lambda i, e, j, c: (e, 0, j)),
                  pl.BlockSpec((1, d, tf), lambda i, e, j, c: (e, 0, j)),
                  pl.BlockSpec((1, tf, d), lambda i, e, j, c: (e, j, 0))],
        out_specs=pl.BlockSpec((1, tile, d), lambda i, e, j, c: (i, 0, 0)),
        scratch_shapes=[pltpu.VMEM((tile, d), BF16), pltpu.VMEM((tile, d), F32)])
    out = pl.pallas_call(
        functools.partial(_moe_kernel, rc=rc),
        grid_spec=grid_spec,
        out_shape=jax.ShapeDtypeStruct((nt, tile, d), F32),
        compiler_params=_cparams("arbitrary", "arbitrary", "arbitrary"),
        name="moe_ffn",
    )(cnt.reshape(-1), hb, pos, wgt, post3, w_gate.astype(BF16), w_up.astype(BF16), w_down.astype(BF16))
    return out.reshape(t_tokens, d)


def _post_kernel(x_ref, f_ref, gpost_ref, gate_ref, o_ref):
    o_ref[0] = x_ref[0] + gate_ref[0] * _rms(f_ref[0], gpost_ref[...])


def _post_residual(x, f, g_post, gate_f):
    bsz, s, d = x.shape
    tm = min(1024, s)
    act = pl.BlockSpec((1, tm, d), lambda b, i: (b, i, 0))
    return pl.pallas_call(
        _post_kernel,
        grid=(bsz, s // tm),
        in_specs=[act, act, pl.BlockSpec((1, d), lambda b, i: (0, 0)),
                  pl.BlockSpec((1, 1, d), lambda b, i: (b, 0, 0))],
        out_specs=act,
        out_shape=jax.ShapeDtypeStruct((bsz, s, d), F32),
        compiler_params=_cparams("arbitrary", "arbitrary"),
        name="post_residual",
    )(x, f, g_post.reshape(1, d), gate_f.reshape(bsz, 1, d))


def _mixer(x, mod, layer, rel_table, g_pre_mix, w_in, cmp_params):
    shift_m, scale_m = mod[:, 0], mod[:, 1]
    proj = _in_proj(x, g_pre_mix, scale_m, shift_m, _arrange_w_in(w_in))
    o_sb = _sb_attention(proj)
    o_nsa = _nsa_attention(proj, _compress(proj, *cmp_params), rel_table)
    return o_sb, o_nsa


def kernel(x, c, rel_table, w_ada, b_ada, g_pre_mix, g_post_mix, g_pre_ffn, g_post_ffn, w_in, w_out, g_sb, g_nsa, cmp_pos_k, cmp_w1_k, cmp_w2_k, cmp_pos_v, cmp_w1_v, cmp_w2_v, ffn_w_gate, ffn_w_up, ffn_w_down, moe_w_router, moe_b_router, moe_w_gate, moe_w_up, moe_w_down):
    bsz, s, d = x.shape
    depth = w_in.shape[0]
    mods = _ada(c, w_ada, b_ada).reshape(depth, bsz, 6, d)
    moe_tile = min(1024, s)
    for layer in range(depth):
        mod = mods[layer]
        cmp_params = (cmp_pos_k[layer], cmp_w1_k[layer], cmp_w2_k[layer],
                      cmp_pos_v[layer], cmp_w1_v[layer], cmp_w2_v[layer])
        o_sb, o_nsa = _mixer(x, mod, layer, rel_table, g_pre_mix[layer], w_in[layer], cmp_params)
        x, h = _out_proj(o_sb, o_nsa, x, g_sb[layer], g_nsa[layer], w_out[layer], g_post_mix[layer],
                         mod[:, 2], g_pre_ffn[layer], mod[:, 4], mod[:, 3])
        i = layer // 2
        if layer % 2 == 0:
            x = _dense_ffn(h, x, ffn_w_gate[i], ffn_w_up[i], ffn_w_down[i], g_post_ffn[layer], mod[:, 5])
        else:
            pos, wgt, post, cnt = _router(x, g_pre_ffn[layer], mod[:, 4], mod[:, 3],
                                          moe_w_router[i], moe_b_router[i], moe_tile)
            f = _moe_ffn(h.reshape(bsz * s, d), pos, wgt, post, cnt,
                         moe_w_gate[i], moe_w_up[i], moe_w_down[i], moe_tile)
            x = _post_residual(x, f.reshape(bsz, s, d), g_post_ffn[layer], mod[:, 5])
    return x
```

```python
import functools
import math

import numpy as np
import jax
import jax.numpy as jnp
from jax import lax
from jax.experimental import pallas as pl
from jax.experimental.pallas import tpu as pltpu

F32 = jnp.float32
BF16 = jnp.bfloat16
I32 = jnp.int32

LANES = 128
DH = 64
N_SB = 8
N_NSA = 8
G_NSA = 2
Z_NSA = 4
CMP_BLOCK = 32
CMP_STRIDE = 16
SLC_BLOCK = 64
N_SELECT = 16
WINDOW = 512
REL_BUCKETS = 32
REL_MAX_DIST = 128
N_EXPERTS = 8
EPS = 1e-6
FORCED = 1e4
NEG = -1e30
M_INIT = -1e29
SB_EXIT = -104.5
VMEM_LIMIT = 56 * 1024 * 1024

TQ = 128
KC_FRONT = 16
SEL_FAR_TILE = 512


def _cparams(*sem):
    return pltpu.CompilerParams(dimension_semantics=sem, vmem_limit_bytes=VMEM_LIMIT)


def _nt(a, b):
    return lax.dot_general(a, b, (((1,), (1,)), ((), ())), preferred_element_type=F32)


def _dot(a, b):
    return jnp.dot(a, b, preferred_element_type=F32)


def _split3(a):
    hi = a.astype(BF16)
    r = a - hi.astype(F32)
    mid = r.astype(BF16)
    lo = (r - mid.astype(F32)).astype(BF16)
    return hi, mid, lo


def _dot_hl(a, b):
    hi = a.astype(BF16)
    lo = (a - hi.astype(F32)).astype(BF16)
    return _dot(hi, b) + _dot(lo, b)


def _dot_f32(a, b):
    ah, am, al = _split3(a)
    bh, bm, bl = _split3(b)
    return (_dot(ah, bh) + (_dot(ah, bm) + _dot(am, bh))
            + (_dot(ah, bl) + _dot(am, bm) + _dot(al, bh)))


def _rms(x, g):
    return x * lax.rsqrt(jnp.mean(x * x, axis=-1, keepdims=True) + EPS) * g


def _ada_kernel(c_ref, w_ref, b_ref, o_ref):
    c = c_ref[...]
    ca = c * (1.0 / (1.0 + jnp.exp(-c)))
    o_ref[0] = _dot_f32(ca, w_ref[0]) + b_ref[0]


def _ada(c, w_ada, b_ada):
    depth, d, n = w_ada.shape
    bsz = c.shape[0]
    rows = 8
    tn = 1536
    cp = jnp.zeros((rows, d), F32).at[:bsz].set(c)
    out = pl.pallas_call(
        _ada_kernel,
        grid=(depth, n // tn),
        in_specs=[pl.BlockSpec((rows, d), lambda l, j: (0, 0)),
                  pl.BlockSpec((1, d, tn), lambda l, j: (l, 0, j)),
                  pl.BlockSpec((1, 1, tn), lambda l, j: (l, 0, j))],
        out_specs=pl.BlockSpec((1, rows, tn), lambda l, j: (l, 0, j)),
        out_shape=jax.ShapeDtypeStruct((depth, rows, n), F32),
        compiler_params=_cparams("arbitrary", "arbitrary"),
        name="ada_mod",
    )(cp, w_ada, b_ada.reshape(depth, 1, n))
    return out[:, :bsz]


def _in_kernel(x_ref, g_ref, sc_ref, sh_ref, w_ref, o_ref, *, cn):
    h = _rms(x_ref[0], g_ref[...]) * (1.0 + sc_ref[0]) + sh_ref[0]
    hb = h.astype(BF16)
    for j in range(w_ref.shape[1] // cn):
        o_ref[0, :, j * cn:(j + 1) * cn] = _dot(hb, w_ref[:, j * cn:(j + 1) * cn]).astype(BF16)


def _in_proj(x, g, scale, shift, w):
    bsz, s, d = x.shape
    nc = w.shape[1]
    tm = min(512, s)
    cn = nc // 3 if (nc // 3) % LANES == 0 else nc
    return pl.pallas_call(
        functools.partial(_in_kernel, cn=cn),
        grid=(bsz, s // tm),
        in_specs=[pl.BlockSpec((1, tm, d), lambda b, i: (b, i, 0)),
                  pl.BlockSpec((1, d), lambda b, i: (0, 0)),
                  pl.BlockSpec((1, 1, d), lambda b, i: (b, 0, 0)),
                  pl.BlockSpec((1, 1, d), lambda b, i: (b, 0, 0)),
                  pl.BlockSpec((d, nc), lambda b, i: (0, 0))],
        out_specs=pl.BlockSpec((1, tm, nc), lambda b, i: (b, i, 0)),
        out_shape=jax.ShapeDtypeStruct((bsz, s, nc), BF16),
        compiler_params=_cparams("arbitrary", "arbitrary"),
        name="in_proj",
    )(x, g.reshape(1, d), scale.reshape(bsz, 1, d), shift.reshape(bsz, 1, d), w)


CB_SBQ, CB_SBK, CB_SBV, CB_NSAQ = 0, 4, 8, 12
CB_KCMP, CB_VCMP = 16, 17
CB_KSLC, CB_VSLC, CB_KWIN, CB_VWIN = 18, 20, 22, 24
CB_GATE = 26
N_CB = 27


def _arrange_w_in(w_in):
    d = w_in.shape[0]
    w_sb = N_SB * DH
    off_nsa_q = 3 * w_sb
    off_kv = off_nsa_q + N_NSA * DH
    off_gate = off_kv + 3 * 2 * G_NSA * DH
    scale = DH ** -0.5

    def kv(br, kvi, g):
        lo = off_kv + ((br * 2 + kvi) * G_NSA + g) * DH
        return w_in[:, lo:lo + DH]

    cols = [w_in[:, 0:w_sb] * scale, w_in[:, w_sb:2 * w_sb], w_in[:, 2 * w_sb:3 * w_sb],
            w_in[:, off_nsa_q:off_kv] * scale,
            kv(0, 0, 0), kv(0, 0, 1), kv(0, 1, 0), kv(0, 1, 1)]
    for br in (1, 2):
        for kvi in (0, 1):
            for g in range(G_NSA):
                cols += [kv(br, kvi, g), kv(br, kvi, g)]
    n_gate = 3 * N_NSA
    cols += [w_in[:, off_gate:off_gate + n_gate], jnp.zeros((d, LANES - n_gate), w_in.dtype)]
    out = jnp.concatenate(cols, axis=1).astype(BF16)
    assert out.shape[1] == N_CB * LANES
    return out


def _sb_kernel(q_ref, k_ref, v_ref, o_ref, *, t):
    qi = pl.program_id(2)
    q = q_ref[0]
    lane = lax.broadcasted_iota(I32, (t, LANES), 1)
    row = lax.broadcasted_iota(I32, (t, t), 0)
    col = lax.broadcasted_iota(I32, (t, t), 1)
    upper = jnp.where(row > col, 1.0, 0.0).astype(BF16)
    ones = jnp.ones((t, LANES), BF16)
    causal = col < row
    rep = t // LANES

    def tile(qh, kt, carry, acc, diag):
        k = k_ref[0, pl.ds(kt * t, t), :]
        v = v_ref[0, pl.ds(kt * t, t), :]
        z = _nt(qh, k)
        lk = -(jnp.maximum(z, 0.0) + jnp.log(1.0 + jnp.exp(-jnp.abs(z))))
        if diag:
            lk = jnp.where(causal, lk, 0.0)
        hi = lk.astype(BF16)
        lo = (lk - hi.astype(F32)).astype(BF16)
        later = _dot(hi, upper) + _dot(lo, upper)
        tot = _dot(hi, ones) + _dot(lo, ones)
        cb = carry if rep == 1 else jnp.concatenate([carry] * rep, axis=1)
        w = jnp.exp(z + lk + later + cb)
        if diag:
            w = jnp.where(causal, w, 0.0)
        return carry + tot, acc + _dot(w.astype(BF16), v)

    outs = []
    for h in range(2):
        qh = jnp.where((lane >= DH) == (h == 1), q, jnp.zeros_like(q))
        zero = jnp.zeros((t, LANES), F32)
        carry, acc = tile(qh, qi, zero, zero, True)

        def cond(st):
            kt, mx, _, _ = st
            return jnp.logical_and(kt >= 0, mx > SB_EXIT)

        def body(st, qh=qh):
            kt, _, carry, acc = st
            carry, acc = tile(qh, kt, carry, acc, False)
            return kt - 1, jnp.max(carry), carry, acc

        _, _, _, acc = lax.while_loop(cond, body, (qi - 1, jnp.max(carry), carry, acc))
        outs.append(acc)
    o_ref[0] = jnp.where(lane < DH, outs[0], outs[1])


def _sb_attention(proj, t=256):
    bsz, s, _ = proj.shape
    t = min(t, s)
    npair = N_SB // 2
    return pl.pallas_call(
        functools.partial(_sb_kernel, t=t),
        grid=(bsz, npair, s // t),
        in_specs=[pl.BlockSpec((1, t, LANES), lambda b, j, i: (b, i, CB_SBQ + j)),
                  pl.BlockSpec((1, s, LANES), lambda b, j, i: (b, 0, CB_SBK + j)),
                  pl.BlockSpec((1, s, LANES), lambda b, j, i: (b, 0, CB_SBV + j))],
        out_specs=pl.BlockSpec((1, t, LANES), lambda b, j, i: (b, i, j)),
        out_shape=jax.ShapeDtypeStruct((bsz, s, N_SB * DH), F32),
        compiler_params=_cparams("arbitrary", "arbitrary", "arbitrary"),
        name="sb_attn",
    )(proj, proj, proj)


def _gelu_tanh(x):
    return 0.5 * x * (1.0 + jnp.tanh(math.sqrt(2.0 / math.pi) * (x + 0.044715 * (x * x * x))))


def _cmp_kernel(xa_ref, xb_ref, pos_ref, w1_ref, w2_ref, o_ref):
    half = w1_ref.shape[1] // 2
    w1a = w1_ref[0, :half, :]
    w1b = w1_ref[0, half:, :]
    pos = pos_ref[0]
    bias = _dot(pos[:, :half], w1a) + _dot(pos[:, half:], w1b)
    hid = _dot(xa_ref[0, 0], w1a) + _dot(xb_ref[0, 0], w1b) + bias[0:1, :]
    o_ref[0, 0] = _dot(_gelu_tanh(hid).astype(BF16), w2_ref[0])


def _compress(proj, pos_k, w1_k, w2_k, pos_v, w1_v, w2_v):
    bsz, s, _ = proj.shape
    n16 = s // CMP_STRIDE
    raw = proj[:, :, CB_KCMP * LANES:(CB_VCMP + 1) * LANES]
    x16 = raw.reshape(bsz, s, 4, DH).transpose(0, 2, 1, 3).reshape(bsz, 4, n16, CMP_STRIDE * DH)
    x16b = jnp.concatenate([x16[:, :, 1:], jnp.zeros_like(x16[:, :, :1])], axis=2)
    hidden = w1_k.shape[1]
    w1 = jnp.stack([w1_k, w1_v]).astype(BF16)
    w2 = jnp.stack([jnp.concatenate([w2_k, w2_k], 1), jnp.concatenate([w2_v, w2_v], 1)]).astype(BF16)
    pos = jnp.stack([pos_k.reshape(1, -1), pos_v.reshape(1, -1)])
    pos = jnp.concatenate([pos, jnp.zeros((2, 7, pos.shape[2]), pos.dtype)], axis=1).astype(BF16)
    k16 = CMP_STRIDE * DH
    return pl.pallas_call(
        _cmp_kernel,
        grid=(bsz, 4),
        in_specs=[pl.BlockSpec((1, 1, n16, k16), lambda b, j: (b, j, 0, 0)),
                  pl.BlockSpec((1, 1, n16, k16), lambda b, j: (b, j, 0, 0)),
                  pl.BlockSpec((1, 8, 2 * k16), lambda b, j: (j // 2, 0, 0)),
                  pl.BlockSpec((1, 2 * k16, hidden), lambda b, j: (j // 2, 0, 0)),
                  pl.BlockSpec((1, hidden, LANES), lambda b, j: (j // 2, 0, 0))],
        out_specs=pl.BlockSpec((1, 1, n16, LANES), lambda b, j: (b, j, 0, 0)),
        out_shape=jax.ShapeDtypeStruct((bsz, 4, n16, LANES), F32),
        compiler_params=_cparams("arbitrary", "arbitrary"),
        name="nsa_compress",
    )(x16, x16b, pos, w1, w2)


def _bucket_table():
    n = np.arange(REL_MAX_DIST + 1)
    exact = REL_BUCKETS // 2
    val = (np.log(np.maximum(n, 1).astype(np.float32) / np.float32(exact)).astype(np.float32)
           / np.float32(math.log(REL_MAX_DIST / exact)) * np.float32(REL_BUCKETS - exact))
    large = np.minimum(exact + val.astype(np.int32), REL_BUCKETS - 1)
    return np.where(n < exact, n, large).astype(np.int32)


def _bias_tiles(rel_table):
    tab = rel_table[_bucket_table()]
    r = np.arange(TQ)[:, None]

    def tile(dist, valid, base):
        idx = np.clip(dist, 0, REL_MAX_DIST)
        t = jnp.where(valid[:, :, None], tab[idx] - base, NEG)
        t = jnp.transpose(t, (2, 0, 1))
        return t.reshape(G_NSA, Z_NSA * TQ, dist.shape[1]).astype(F32)

    w = np.arange(LANES)[None, :]
    d_c = r - CMP_STRIDE * (w - KC_FRONT) - (CMP_BLOCK - 1)
    w2 = np.arange(2 * TQ)[None, :]
    d_s = r + TQ - w2
    w5 = np.arange(WINDOW + TQ)[None, :]
    d_w = r + WINDOW - w5
    far = jnp.broadcast_to(tab[REL_MAX_DIST][:, None, None], (N_NSA, TQ, LANES))
    far = far.reshape(G_NSA, Z_NSA * TQ, LANES).astype(F32)
    return (tile(d_c, d_c >= 0, tab[REL_MAX_DIST]), tile(d_s, d_s >= 0, 0.0),
            tile(d_w, (d_w >= 0) & (d_w < WINDOW), 0.0), far)


def _gate_expand():
    e = np.zeros((G_NSA, LANES, 3 * Z_NSA * DH), np.float32)
    for g in range(G_NSA):
        for z in range(Z_NSA):
            for br in range(3):
                e[g, (g * Z_NSA + z) * 3 + br, br * Z_NSA * DH + z * DH: br * Z_NSA * DH + (z + 1) * DH] = 1.0
    return e


def _overlap_padded(n16, kcp, n_slc):
    i = np.arange(kcp)[:, None] - KC_FRONT
    j = np.arange(LANES)[None, :]
    n_cmp = n16 - 1
    ok = (i >= 0) & (i < n_cmp) & (j < n_slc)
    ov = (i * CMP_STRIDE < j * SLC_BLOCK + SLC_BLOCK) & (i * CMP_STRIDE + CMP_BLOCK > j * SLC_BLOCK)
    return (ok & ov).astype(np.float32)


def _block_onehot(s):
    return (np.arange(s)[:, None] // SLC_BLOCK == np.arange(LANES)[None, :]).astype(np.float32)


def _nsa_kernel(q_ref, kcb_ref, vcb_ref, kcf_ref, vcf_ref, ks_ref, vs_ref, kw_ref, vw_ref, et_ref,
                gate_ref, ovb_ref, ovf_ref, bcn_ref, bsn_ref, bw_ref, cf_ref, eg_ref, o_ref,
                sb0_ref, sb1_ref, *, n_slc, n_round):
    blk = pl.program_id(2)
    c0 = blk * TQ
    r4 = Z_NSA * TQ
    q = q_ref[0]
    lane = lax.broadcasted_iota(I32, (TQ, LANES), 1)
    lo_half = lane < DH
    zq = jnp.zeros((TQ, LANES), BF16)
    qa, qb = q[:, :LANES], q[:, LANES:]
    qs = jnp.concatenate([jnp.where(lo_half, qa, zq), jnp.where(lo_half, zq, qa),
                          jnp.where(lo_half, qb, zq), jnp.where(lo_half, zq, qb)], axis=0)
    ones_k = jnp.ones((SEL_FAR_TILE, LANES), BF16)
    ones_w = jnp.ones((WINDOW + TQ, LANES), BF16)

    kcp = kcb_ref.shape[2]
    prow = lax.broadcasted_iota(I32, (1, kcp), 1)
    far_row = jnp.where((prow >= KC_FRONT) & (prow < 8 * blk), 0.0, NEG)
    lane1 = lax.broadcasted_iota(I32, (1, LANES), 1)
    near_row = jnp.where(lane1 + 8 * blk >= KC_FRONT, 0.0, NEG)
    near0 = pl.multiple_of(8 * blk, 8)
    kcn = kcf_ref[0, 0, pl.ds(near0, LANES), :].astype(BF16)
    vcn = vcf_ref[0, 0, pl.ds(near0, LANES), :].astype(BF16)
    s_far = _nt(qs, kcb_ref[0, 0]) + far_row
    s_near = _nt(qs, kcn) + bcn_ref[0] + near_row
    m_c = jnp.maximum(jnp.max(s_far, axis=1, keepdims=True), jnp.max(s_near, axis=1, keepdims=True))
    m_c = jnp.maximum(m_c, M_INIT)
    p_far = jnp.exp(s_far - m_c)
    p_near = jnp.exp(s_near - m_c)
    den = jnp.sum(p_far, axis=1, keepdims=True) + jnp.sum(p_near, axis=1, keepdims=True)
    inv_c = 1.0 / jnp.maximum(den, 1e-30)
    o_c = (_dot(p_far.astype(BF16), vcb_ref[0, 0]) + _dot(p_near.astype(BF16), vcn)) * inv_c

    pn_far = p_far * inv_c
    pn_near = p_near * inv_c
    pz_far = pn_far[0:TQ] + pn_far[TQ:2 * TQ] + pn_far[2 * TQ:3 * TQ] + pn_far[3 * TQ:]
    pz_near = pn_near[0:TQ] + pn_near[TQ:2 * TQ] + pn_near[2 * TQ:3 * TQ] + pn_near[3 * TQ:]
    ovn = ovf_ref[pl.ds(near0, LANES), :].astype(BF16)
    imp = _dot_hl(pz_far, ovb_ref[...]) + _dot_hl(pz_near, ovn)
    tpos = c0 + lax.broadcasted_iota(I32, (TQ, LANES), 0)
    cur = lax.shift_right_logical(tpos, 6)
    valid = (lane * SLC_BLOCK <= tpos) & (lane < n_slc)
    forced = valid & ((lane == 0) | (lane == cur) | (lane == cur - 1))
    score = jnp.where(valid & jnp.logical_not(forced), imp, -1.0)
    score = jnp.where(lane < n_slc, score, -2.0)

    rest, taken, thr = score, jnp.zeros((TQ, 1), F32), jnp.zeros((TQ, 1), F32)
    for _ in range(n_round):
        m = jnp.max(rest, axis=1, keepdims=True)
        eq = rest == m
        thr = jnp.where(taken < n_round, m, thr)
        taken = taken + jnp.sum(jnp.where(eq, 1.0, 0.0), axis=1, keepdims=True)
        rest = jnp.where(eq, -jnp.inf, rest)
    above = score > thr
    at_thr = score == thr
    need = n_round - jnp.sum(jnp.where(above, 1.0, 0.0), axis=1, keepdims=True)
    brow = lax.broadcasted_iota(I32, (LANES, LANES), 0)
    bcol = lax.broadcasted_iota(I32, (LANES, LANES), 1)
    before = _dot(jnp.where(at_thr, 1.0, 0.0).astype(BF16),
                  jnp.where(brow < bcol, 1.0, 0.0).astype(BF16))
    sel = ((above | (at_thr & (before < need))) & valid) | forced

    sel_all = jnp.where(sel, 0.0, NEG).astype(BF16)
    sel_far = jnp.where(sel & (lane < 2 * blk - 2), 0.0, NEG).astype(BF16)
    q_far = jnp.concatenate([qs, jnp.concatenate([sel_far] * Z_NSA, axis=0)], axis=1)
    q_near = jnp.concatenate([qs, jnp.concatenate([sel_all] * Z_NSA, axis=0)], axis=1)
    tk = SEL_FAR_TILE
    last_tile = ks_ref.shape[1] // tk - 1

    def qk(kt):
        k0 = pl.multiple_of(kt * tk, tk)
        return _nt(q_far, jnp.concatenate([ks_ref[0, pl.ds(k0, tk), :], et_ref[pl.ds(k0, tk), :]], axis=1))

    def fold(s, v, st):
        m, acc = st
        mn = jnp.maximum(m, jnp.max(s, axis=1, keepdims=True))
        p = jnp.exp(s - mn).astype(BF16)
        return mn, jnp.exp(m - mn) * acc + _dot(p, v)

    def v_tile(kt):
        k0 = pl.multiple_of(kt * tk, tk)
        return jnp.concatenate([vs_ref[0, pl.ds(k0, tk), :], ones_k], axis=1)

    n_pair = lax.shift_right_logical(jnp.maximum(blk - 1, 0) + 7, 3)
    sb0_ref[...] = qk(0)

    def far_step(i, st):
        sb1_ref[...] = qk(2 * i + 1)
        st = fold(sb0_ref[...], v_tile(2 * i), st)
        sb0_ref[...] = qk(jnp.minimum(2 * i + 2, last_tile))
        return fold(sb1_ref[...], v_tile(2 * i + 1), st)

    m0 = jnp.full((r4, 1), M_INIT, F32)
    m_s, acc_s = lax.fori_loop(0, n_pair, far_step, (m0, jnp.zeros((r4, 2 * LANES), F32)))
    m_s = m_s + cf_ref[0][:, 0:1]
    p0 = pl.multiple_of(jnp.maximum(c0 - TQ, 0), TQ)
    d0 = pl.multiple_of(c0, TQ)
    kn = jnp.concatenate([ks_ref[0, pl.ds(p0, TQ), :], ks_ref[0, pl.ds(d0, TQ), :]], axis=0)
    vn = jnp.concatenate([vs_ref[0, pl.ds(p0, TQ), :], vs_ref[0, pl.ds(d0, TQ), :]], axis=0)
    nrow = lax.broadcasted_iota(I32, (2 * TQ, LANES), 0)
    nlane = lax.broadcasted_iota(I32, (2 * TQ, LANES), 1)
    e_n = jnp.where(lax.shift_right_logical(nrow, 6) + 2 * blk - 2 == nlane, 1.0, 0.0).astype(BF16)
    lane2 = lax.broadcasted_iota(I32, (1, 2 * TQ), 1)
    prev_row = jnp.where((lane2 < TQ) & (blk == 0), NEG, 0.0)
    s = _nt(q_near, jnp.concatenate([kn, e_n], axis=1)) + bsn_ref[0] + prev_row
    _, acc_s = fold(s, jnp.concatenate([vn, ones_k[:2 * TQ]], axis=1), (m_s, acc_s))
    o_s = acc_s[:, :LANES] / acc_s[:, LANES:]

    n_w = WINDOW // TQ + 1
    kws, vws = [], []
    for w in range(n_w):
        st = pl.multiple_of(jnp.maximum(c0 - WINDOW + w * TQ, 0), TQ)
        kws.append(kw_ref[0, pl.ds(st, TQ), :])
        vws.append(vw_ref[0, pl.ds(st, TQ), :])
    wl = lax.broadcasted_iota(I32, (1, WINDOW + TQ), 1)
    pos_row = jnp.where(c0 - WINDOW + wl >= 0, 0.0, NEG)
    s = _nt(qs, jnp.concatenate(kws, axis=0)) + bw_ref[0] + pos_row
    p = jnp.exp(s - jnp.max(s, axis=1, keepdims=True)).astype(BF16)
    r_w = _dot(p, jnp.concatenate([jnp.concatenate(vws, axis=0), ones_w], axis=1))
    o_w = r_w[:, :LANES] / r_w[:, LANES:]

    gl = _dot(gate_ref[0], eg_ref[0])
    sig = 1.0 / (1.0 + jnp.exp(-gl))
    wide = Z_NSA * DH

    def heads(o):
        return jnp.concatenate([jnp.where(lo_half, o[0:TQ], o[TQ:2 * TQ]),
                                jnp.where(lo_half, o[2 * TQ:3 * TQ], o[3 * TQ:])], axis=1)

    o_ref[0] = (sig[:, 0:wide] * heads(o_c) + sig[:, wide:2 * wide] * heads(o_s)
                + sig[:, 2 * wide:] * heads(o_w))


def _nsa_attention(proj, cmp_out, rel_table):
    bsz, s, _ = proj.shape
    assert s % (2 * SEL_FAR_TILE) == 0 and s >= WINDOW + TQ
    n16 = s // CMP_STRIDE
    n_slc = s // SLC_BLOCK
    assert N_SELECT <= n_slc <= LANES
    nq = s // TQ
    kcp = -(-(max(n16 + KC_FRONT, 8 * (nq - 1) + LANES)) // LANES) * LANES
    pad = ((0, 0), (0, 0), (KC_FRONT, kcp - n16 - KC_FRONT), (0, 0))
    cf = jnp.pad(cmp_out, pad)
    cb = cf.astype(BF16)
    ov = _overlap_padded(n16, kcp, n_slc)
    bcn, bsn, bw, far = _bias_tiles(rel_table)
    eg = jnp.asarray(_gate_expand(), BF16)
    r4 = Z_NSA * TQ
    kv_spec = lambda cb0: pl.BlockSpec((1, s, LANES), lambda b, g, i, cb0=cb0: (b, 0, cb0 + g))
    cmp_spec = lambda j0: pl.BlockSpec((1, 1, kcp, LANES), lambda b, g, i, j0=j0: (b, j0 + g, 0, 0))
    tile_spec = lambda w: pl.BlockSpec((1, r4, w), lambda b, g, i: (g, 0, 0))
    const2 = lambda n: pl.BlockSpec((n, LANES), lambda b, g, i: (0, 0))
    return pl.pallas_call(
        functools.partial(_nsa_kernel, n_slc=n_slc, n_round=N_SELECT - 3),
        grid=(bsz, G_NSA, nq),
        in_specs=[pl.BlockSpec((1, TQ, 2 * LANES), lambda b, g, i: (b, i, CB_NSAQ // 2 + g)),
                  cmp_spec(0), cmp_spec(2), cmp_spec(0), cmp_spec(2),
                  kv_spec(CB_KSLC), kv_spec(CB_VSLC), kv_spec(CB_KWIN), kv_spec(CB_VWIN),
                  const2(s),
                  pl.BlockSpec((1, TQ, LANES), lambda b, g, i: (b, i, CB_GATE)),
                  const2(kcp), const2(kcp),
                  tile_spec(LANES), tile_spec(2 * TQ), tile_spec(WINDOW + TQ), tile_spec(LANES),
                  pl.BlockSpec((1, LANES, 3 * Z_NSA * DH), lambda b, g, i: (g, 0, 0))],
        out_specs=pl.BlockSpec((1, TQ, Z_NSA * DH), lambda b, g, i: (b, i, g)),
        out_shape=jax.ShapeDtypeStruct((bsz, s, N_NSA * DH), F32),
        scratch_shapes=[pltpu.VMEM((r4, SEL_FAR_TILE), F32), pltpu.VMEM((r4, SEL_FAR_TILE), F32)],
        compiler_params=_cparams("arbitrary", "arbitrary", "arbitrary"),
        name="nsa_attn",
    )(proj, cb, cb, cf, cf, proj, proj, proj, proj, jnp.asarray(_block_onehot(s), BF16), proj,
      jnp.asarray(ov, BF16), jnp.asarray(ov, F32), bcn, bsn, bw, far, eg)


def _out_kernel(osb_ref, onsa_ref, x_ref, gsb_ref, gnsa_ref, w_ref, gpost_ref, gate_ref,
                gpre_ref, sc_ref, sh_ref, xo_ref, h_ref):
    half = osb_ref.shape[2]
    a = _rms(osb_ref[0], gsb_ref[...]).astype(BF16)
    b = _rms(onsa_ref[0], gnsa_ref[...]).astype(BF16)
    m = _dot(a, w_ref[:half, :]) + _dot(b, w_ref[half:, :])
    x = x_ref[0] + gate_ref[0] * _rms(m, gpost_ref[...])
    xo_ref[0] = x
    h_ref[0] = (_rms(x, gpre_ref[...]) * (1.0 + sc_ref[0]) + sh_ref[0]).astype(BF16)


def _out_proj(o_sb, o_nsa, x, g_sb, g_nsa, w_out, g_post, gate_m, g_pre_ffn, scale_f, shift_f):
    bsz, s, d = x.shape
    half = o_sb.shape[2]
    tm = min(512, s)
    row = lambda n: pl.BlockSpec((1, n), lambda b, i: (0, 0))
    mod = pl.BlockSpec((1, 1, d), lambda b, i: (b, 0, 0))
    act = lambda n: pl.BlockSpec((1, tm, n), lambda b, i: (b, i, 0))
    return pl.pallas_call(
        _out_kernel,
        grid=(bsz, s // tm),
        in_specs=[act(half), act(half), act(d), row(half), row(half),
                  pl.BlockSpec((2 * half, d), lambda b, i: (0, 0)), row(d), mod, row(d), mod, mod],
        out_specs=[act(d), act(d)],
        out_shape=[jax.ShapeDtypeStruct((bsz, s, d), F32), jax.ShapeDtypeStruct((bsz, s, d), BF16)],
        compiler_params=_cparams("arbitrary", "arbitrary"),
        name="out_proj",
    )(o_sb, o_nsa, x, g_sb.reshape(1, half), g_nsa.reshape(1, half), w_out.astype(BF16),
      g_post.reshape(1, d), gate_m.reshape(bsz, 1, d), g_pre_ffn.reshape(1, d),
      scale_f.reshape(bsz, 1, d), shift_f.reshape(bsz, 1, d))


def _silu(x):
    return x * (1.0 / (1.0 + jnp.exp(-x)))


def _ffn_kernel(h_ref, x_ref, wg_ref, wu_ref, wd_ref, gpost_ref, gate_ref, o_ref, acc_ref):
    j = pl.program_id(2)
    h = h_ref[0]
    a = (_silu(_dot(h, wg_ref[...])) * _dot(h, wu_ref[...])).astype(BF16)
    part = _dot(a, wd_ref[...])

    @pl.when(j == 0)
    def _():
        acc_ref[...] = part

    @pl.when(j > 0)
    def _():
        acc_ref[...] += part

    @pl.when(j == pl.num_programs(2) - 1)
    def _():
        o_ref[0] = x_ref[0] + gate_ref[0] * _rms(acc_ref[...], gpost_ref[...])


def _dense_ffn(h, x, w_gate, w_up, w_down, g_post, gate_f):
    bsz, s, d = x.shape
    ff = w_gate.shape[1]
    tm = min(1024, s)
    tf = 256
    assert ff % tf == 0
    act = pl.BlockSpec((1, tm, d), lambda b, i, j: (b, i, 0))
    return pl.pallas_call(
        _ffn_kernel,
        grid=(bsz, s // tm, ff // tf),
        in_specs=[act, act,
                  pl.BlockSpec((d, tf), lambda b, i, j: (0, j)),
                  pl.BlockSpec((d, tf), lambda b, i, j: (0, j)),
                  pl.BlockSpec((tf, d), lambda b, i, j: (j, 0)),
                  pl.BlockSpec((1, d), lambda b, i, j: (0, 0)),
                  pl.BlockSpec((1, 1, d), lambda b, i, j: (b, 0, 0))],
        out_specs=act,
        out_shape=jax.ShapeDtypeStruct((bsz, s, d), F32),
        scratch_shapes=[pltpu.VMEM((tm, d), F32)],
        compiler_params=_cparams("arbitrary", "arbitrary", "arbitrary"),
        name="dense_ffn",
    )(h, x, w_gate.astype(BF16), w_up.astype(BF16), w_down.astype(BF16),
      g_post.reshape(1, d), gate_f.reshape(bsz, 1, d))


def _router_kernel(x_ref, gpre_ref, sc_ref, sh_ref, wr_ref, br_ref, pos_ref, w_ref, post_ref, cnt_ref):
    tile = x_ref.shape[1]
    h = _rms(x_ref[0], gpre_ref[...]) * (1.0 + sc_ref[0]) + sh_ref[0]
    logits = _dot_f32(h, wr_ref[...]) + br_ref[...]
    lane = lax.broadcasted_iota(I32, (tile, LANES), 1)
    lanef = lane.astype(F32)
    e = jnp.exp(logits - jnp.max(logits, axis=1, keepdims=True))
    probs = e / jnp.sum(e, axis=1, keepdims=True)
    probs = jnp.where(lane < N_EXPERTS, probs, -1.0)
    m1 = jnp.max(probs, axis=1, keepdims=True)
    i1 = jnp.min(jnp.where(probs == m1, lanef, 1e9), axis=1, keepdims=True)
    rest = jnp.where(lanef == i1, -1.0, probs)
    m2 = jnp.max(rest, axis=1, keepdims=True)
    i2 = jnp.min(jnp.where(rest == m2, lanef, 1e9), axis=1, keepdims=True)
    tot = m1 + m2
    wgt = jnp.where(lanef == i1, m1 / tot, jnp.where(lanef == i2, m2 / tot, 0.0))
    mask = (lanef == i1) | (lanef == i2)
    ch = 256
    r = lax.broadcasted_iota(I32, (ch, ch), 0)
    c = lax.broadcasted_iota(I32, (ch, ch), 1)
    lower = jnp.where(c < r, 1.0, 0.0).astype(BF16)
    ones = jnp.ones((8, ch), BF16)
    carry = jnp.zeros((1, LANES), F32)
    ranks = []
    for k in range(tile // ch):
        mk = jnp.where(mask[k * ch:(k + 1) * ch], 1.0, 0.0).astype(BF16)
        ranks.append(_dot(lower, mk) + carry)
        carry = carry + _dot(ones, mk)[0:1]
    rank = jnp.concatenate(ranks, axis=0)
    pos = jnp.where(mask, rank, -1.0)
    pos_ref[0] = pos.astype(I32)
    w_ref[0] = wgt
    post_ref[0] = jnp.transpose(pos)[0:N_EXPERTS].astype(I32)
    cnt_ref[0] = jnp.broadcast_to(carry, (8, LANES)).astype(I32)


def _router(x, g_pre, scale_f, shift_f, w_router, b_router, tile):
    bsz, s, d = x.shape
    nt = s // tile
    wr = jnp.zeros((d, LANES), F32).at[:, :N_EXPERTS].set(w_router.astype(F32))
    br = jnp.full((1, LANES), NEG, F32).at[0, :N_EXPERTS].set(b_router.astype(F32))
    mod = pl.BlockSpec((1, 1, d), lambda b, i: (b, 0, 0))
    tok = lambda n: pl.BlockSpec((1, tile, n), lambda b, i: (b * nt + i, 0, 0))
    pos, wgt, post, cnt = pl.pallas_call(
        _router_kernel,
        grid=(bsz, nt),
        in_specs=[pl.BlockSpec((1, tile, d), lambda b, i: (b, i, 0)),
                  pl.BlockSpec((1, d), lambda b, i: (0, 0)), mod, mod,
                  pl.BlockSpec((d, LANES), lambda b, i: (0, 0)),
                  pl.BlockSpec((1, LANES), lambda b, i: (0, 0))],
        out_specs=[tok(LANES), tok(LANES),
                   pl.BlockSpec((1, N_EXPERTS, tile), lambda b, i: (b * nt + i, 0, 0)),
                   pl.BlockSpec((1, 8, LANES), lambda b, i: (b * nt + i, 0, 0))],
        out_shape=[jax.ShapeDtypeStruct((bsz * nt, tile, LANES), I32),
                   jax.ShapeDtypeStruct((bsz * nt, tile, LANES), F32),
                   jax.ShapeDtypeStruct((bsz * nt, N_EXPERTS, tile), I32),
                   jax.ShapeDtypeStruct((bsz * nt, 8, LANES), I32)],
        compiler_params=_cparams("arbitrary", "arbitrary"),
        name="moe_router",
    )(x, g_pre.reshape(1, d), scale_f.reshape(bsz, 1, d), shift_f.reshape(bsz, 1, d), wr, br)
    return pos, wgt, post, cnt[:, 0, :N_EXPERTS]


def _moe_kernel(cnt_ref, h_ref, pos_ref, w_ref, post_ref, wg_ref, wu_ref, wd_ref, o_ref,
                xg_ref, y_ref, *, rc):
    i, e, j = pl.program_id(0), pl.program_id(1), pl.program_id(2)
    tile = h_ref.shape[1]
    n_chunk = lax.shift_right_logical(cnt_ref[i * N_EXPERTS + e] + (rc - 1), int(math.log2(rc)))
    lane = lax.broadcasted_iota(I32, (tile, LANES), 1)

    @pl.when(jnp.logical_and(e == 0, j == 0))
    def _():
        o_ref[0] = jnp.zeros(o_ref.shape[1:], F32)

    @pl.when(j == 0)
    def _():
        prow = post_ref[0]
        rid = lax.broadcasted_iota(I32, (rc, tile), 0)

        def gather(c, _):
            onehot = jnp.where(prow == rid + c * rc, 1.0, 0.0).astype(BF16)
            r0 = pl.multiple_of(c * rc, rc)
            xg_ref[pl.ds(r0, rc), :] = _dot(onehot, h_ref[0]).astype(BF16)
            return 0

        lax.fori_loop(0, n_chunk, gather, 0)

    def expert(c, _):
        r0 = pl.multiple_of(c * rc, rc)
        xg = xg_ref[pl.ds(r0, rc), :]
        a = (_silu(_dot(xg, wg_ref[0])) * _dot(xg, wu_ref[0])).astype(BF16)
        part = _dot(a, wd_ref[0])

        @pl.when(j == 0)
        def _():
            y_ref[pl.ds(r0, rc), :] = part

        @pl.when(j > 0)
        def _():
            y_ref[pl.ds(r0, rc), :] += part

        return 0

    lax.fori_loop(0, n_chunk, expert, 0)

    @pl.when(j == pl.num_programs(2) - 1)
    def _():
        pcol = jnp.sum(jnp.where(lane == e, pos_ref[0], 0), axis=1, keepdims=True)
        wcol = jnp.sum(jnp.where(lane == e, w_ref[0], 0.0), axis=1, keepdims=True)
        cid = lax.broadcasted_iota(I32, (tile, rc), 1)

        def scatter(c, z):
            r0 = pl.multiple_of(c * rc, rc)
            onehot_t = jnp.where(pcol == cid + c * rc, 1.0, 0.0).astype(BF16)
            return z + _dot(onehot_t, y_ref[pl.ds(r0, rc), :].astype(BF16))

        z = lax.fori_loop(0, n_chunk, scatter, jnp.zeros((tile, o_ref.shape[2]), F32))
        o_ref[0] += wcol * z


def _moe_ffn(h, pos, wgt, post, cnt, w_gate, w_up, w_down, tile, rc=128, tf=512):
    t_tokens, d = h.shape
    nt = t_tokens // tile
    ff = w_gate.shape[2]
    assert ff % tf == 0
    hb = h.reshape(nt, tile, d)
    post3 = post.reshape(nt * N_EXPERTS, 1, tile)
    grid_spec = pltpu.PrefetchScalarGridSpec(
        num_scalar_prefetch=1,
        grid=(nt, N_EXPERTS, ff // tf),
        in_specs=[pl.BlockSpec((1, tile, d), lambda i, e, j, c: (i, 0, 0)),
                  pl.BlockSpec((1, tile, LANES), lambda i, e, j, c: (i, 0, 0)),
                  pl.BlockSpec((1, tile, LANES), lambda i, e, j, c: (i, 0, 0)),
                  pl.BlockSpec((1, 1, tile), lambda i, e, j, c: (i * N_EXPERTS + e, 0, 0)),
                  pl.BlockSpec((1, d, tf), lambda i, e, j, c: (e, 0, j)),
                  pl.BlockSpec((1, d, tf), lambda i, e, j, c: (e, 0, j)),
                  pl.BlockSpec((1, tf, d), lambda i, e, j, c: (e, j, 0))],
        out_specs=pl.BlockSpec((1, tile, d), lambda i, e, j, c: (i, 0, 0)),
        scratch_shapes=[pltpu.VMEM((tile, d), BF16), pltpu.VMEM((tile, d), F32)])
    out = pl.pallas_call(
        functools.partial(_moe_kernel, rc=rc),
        grid_spec=grid_spec,
        out_shape=jax.ShapeDtypeStruct((nt, tile, d), F32),
        compiler_params=_cparams("arbitrary", "arbitrary", "arbitrary"),
        name="moe_ffn",
    )(cnt.reshape(-1), hb, pos, wgt, post3, w_gate.astype(BF16), w_up.astype(BF16), w_down.astype(BF16))
    return out.reshape(t_tokens, d)


MOE_RC = 128
MOE_GC = 256
MOE_SC = 1024


def _moe_rows(tile):
    return -(-(2 * tile + N_EXPERTS * MOE_RC) // MOE_SC) * MOE_SC


def _route_kernel(x_ref, gpre_ref, sc_ref, sh_ref, wr_ref, br_ref, d_ref, w_ref, dt_ref, off_ref, nch_ref):
    tile = x_ref.shape[1]
    h = _rms(x_ref[0], gpre_ref[...]) * (1.0 + sc_ref[0]) + sh_ref[0]
    logits = _dot_f32(h, wr_ref[...]) + br_ref[...]
    lane = lax.broadcasted_iota(I32, (tile, LANES), 1)
    lanef = lane.astype(F32)
    e = jnp.exp(logits - jnp.max(logits, axis=1, keepdims=True))
    probs = e / jnp.sum(e, axis=1, keepdims=True)
    probs = jnp.where(lane < N_EXPERTS, probs, -1.0)
    m1 = jnp.max(probs, axis=1, keepdims=True)
    i1 = jnp.min(jnp.where(probs == m1, lanef, 1e9), axis=1, keepdims=True)
    rest = jnp.where(lanef == i1, -1.0, probs)
    m2 = jnp.max(rest, axis=1, keepdims=True)
    i2 = jnp.min(jnp.where(rest == m2, lanef, 1e9), axis=1, keepdims=True)
    tot = m1 + m2
    hit1 = lanef == i1
    hit2 = lanef == i2
    mask = hit1 | hit2
    ch = 256
    r = lax.broadcasted_iota(I32, (ch, ch), 0)
    c = lax.broadcasted_iota(I32, (ch, ch), 1)
    lower = jnp.where(c < r, 1.0, 0.0).astype(BF16)
    ones = jnp.ones((8, ch), BF16)
    count = jnp.zeros((1, LANES), F32)
    ranks = []
    for k in range(tile // ch):
        mk = jnp.where(mask[k * ch:(k + 1) * ch], 1.0, 0.0).astype(BF16)
        ranks.append(_dot(lower, mk) + count)
        count = count + _dot(ones, mk)[0:1]
    rank = jnp.concatenate(ranks, axis=0)
    shift = int(math.log2(MOE_RC))
    nch = lax.shift_right_logical(count.astype(I32) + (MOE_RC - 1), shift)
    nch8 = jnp.broadcast_to(nch.astype(F32), (8, LANES))
    lr = lax.broadcasted_iota(I32, (LANES, LANES), 0)
    lc = lax.broadcasted_iota(I32, (LANES, LANES), 1)
    offc = _dot(nch8.astype(BF16), jnp.where(lr < lc, 1.0, 0.0).astype(BF16))
    row = offc[0:1] * float(MOE_RC) + rank
    d1 = jnp.sum(jnp.where(hit1, row, 0.0), axis=1, keepdims=True)
    d2 = jnp.sum(jnp.where(hit2, row, 0.0), axis=1, keepdims=True)
    dd = jnp.where(lane == 0, d1, jnp.where(lane == 1, d2, -1.0))
    d_ref[0] = dd.astype(I32)
    w_ref[0] = jnp.where(lane == 0, m1 / tot, jnp.where(lane == 1, m2 / tot, 0.0))
    dt_ref[0] = jnp.transpose(dd)[0:8].astype(I32)
    off_ref[0] = offc.astype(I32)
    nch_ref[0] = jnp.broadcast_to(nch, (8, LANES))


def _route(x, g_pre, scale_f, shift_f, w_router, b_router, tile):
    bsz, s, d = x.shape
    nt = s // tile
    wr = jnp.zeros((d, LANES), F32).at[:, :N_EXPERTS].set(w_router.astype(F32))
    br = jnp.full((1, LANES), NEG, F32).at[0, :N_EXPERTS].set(b_router.astype(F32))
    mod = pl.BlockSpec((1, 1, d), lambda b, i: (b, 0, 0))
    tok = pl.BlockSpec((1, tile, LANES), lambda b, i: (b * nt + i, 0, 0))
    small = pl.BlockSpec((1, 8, LANES), lambda b, i: (b * nt + i, 0, 0))
    dest, wgt, dest_t, offc, nch = pl.pallas_call(
        _route_kernel,
        grid=(bsz, nt),
        in_specs=[pl.BlockSpec((1, tile, d), lambda b, i: (b, i, 0)),
                  pl.BlockSpec((1, d), lambda b, i: (0, 0)), mod, mod,
                  pl.BlockSpec((d, LANES), lambda b, i: (0, 0)),
                  pl.BlockSpec((1, LANES), lambda b, i: (0, 0))],
        out_specs=[tok, tok, pl.BlockSpec((1, 8, tile), lambda b, i: (b * nt + i, 0, 0)), small, small],
        out_shape=[jax.ShapeDtypeStruct((bsz * nt, tile, LANES), I32),
                   jax.ShapeDtypeStruct((bsz * nt, tile, LANES), F32),
                   jax.ShapeDtypeStruct((bsz * nt, 8, tile), I32),
                   jax.ShapeDtypeStruct((bsz * nt, 8, LANES), I32),
                   jax.ShapeDtypeStruct((bsz * nt, 8, LANES), I32)],
        compiler_params=_cparams("arbitrary", "arbitrary"),
        name="moe_router",
    )(x, g_pre.reshape(1, d), scale_f.reshape(bsz, 1, d), shift_f.reshape(bsz, 1, d), wr, br)
    return dest, wgt, dest_t, offc[:, 0, :N_EXPERTS].reshape(-1), nch[:, 0, :N_EXPERTS].reshape(-1)


def _experts_kernel(off_ref, nch_ref, h_ref, d_ref, w_ref, dt_ref, wg_ref, wu_ref, wd_ref, o_ref,
                    xg_ref, y_ref, wr_ref):
    i, e, j = pl.program_id(0), pl.program_id(1), pl.program_id(2)
    tile = h_ref.shape[1]
    rows = xg_ref.shape[0]

    @pl.when(jnp.logical_and(e == 0, j == 0))
    def _():
        y_ref[...] = jnp.zeros(y_ref.shape, F32)
        d1t = dt_ref[0, 0:1, :]
        d2t = dt_ref[0, 1:2, :]
        w = w_ref[0]
        lane = lax.broadcasted_iota(I32, (tile, LANES), 1)

        def pieces(col):
            hi, mid, lo = _split3(jnp.broadcast_to(col, (tile, LANES)))
            z = jnp.zeros((tile, LANES), F32)
            return jnp.where(lane == 0, hi.astype(F32), jnp.where(lane == 1, mid.astype(F32),
                             jnp.where(lane == 2, lo.astype(F32), z))).astype(BF16)

        wp1 = pieces(w[:, 0:1])
        wp2 = pieces(w[:, 1:2])
        hb = h_ref[0]
        rid = lax.broadcasted_iota(I32, (MOE_GC, tile), 0)
        for c in range(rows // MOE_GC):
            p1 = jnp.where(rid + c * MOE_GC == d1t, 1.0, 0.0).astype(BF16)
            p2 = jnp.where(rid + c * MOE_GC == d2t, 1.0, 0.0).astype(BF16)
            xg_ref[c * MOE_GC:(c + 1) * MOE_GC, :] = _dot(p1 + p2, hb).astype(BF16)
            wrow = jnp.sum(_dot(p1, wp1) + _dot(p2, wp2), axis=1, keepdims=True)
            wr_ref[c * MOE_GC:(c + 1) * MOE_GC, :] = jnp.broadcast_to(wrow, (MOE_GC, LANES))

    base = off_ref[i * N_EXPERTS + e] * MOE_RC
    n = nch_ref[i * N_EXPERTS + e]

    def ffn(r0, m):
        xg = xg_ref[pl.ds(r0, m), :]
        a = (_silu(_dot(xg, wg_ref[0])) * _dot(xg, wu_ref[0])).astype(BF16)
        y_ref[pl.ds(r0, m), :] += _dot(a, wd_ref[0])

    def pair(c, _):
        ffn(pl.multiple_of(base + c * (2 * MOE_RC), MOE_RC), 2 * MOE_RC)
        return 0

    lax.fori_loop(0, lax.shift_right_logical(n, 1), pair, 0)

    @pl.when((n & 1) == 1)
    def _():
        ffn(pl.multiple_of(base + (n - 1) * MOE_RC, MOE_RC), MOE_RC)

    @pl.when(jnp.logical_and(e == pl.num_programs(1) - 1, j == pl.num_programs(2) - 1))
    def _():
        dd = d_ref[0]
        d1c = dd[:, 0:1]
        d2c = dd[:, 1:2]
        cid = lax.broadcasted_iota(I32, (tile, MOE_SC), 1)
        z = None
        for k in range(rows // MOE_SC):
            pt = jnp.where((cid + k * MOE_SC == d1c) | (cid + k * MOE_SC == d2c), 1.0, 0.0).astype(BF16)
            yb = (y_ref[k * MOE_SC:(k + 1) * MOE_SC, :] * wr_ref[k * MOE_SC:(k + 1) * MOE_SC, 0:1]).astype(BF16)
            part = _dot(pt, yb)
            z = part if z is None else z + part
        o_ref[0] = z


def _experts(h, dest, wgt, dest_t, offc, nch, w_gate, w_up, w_down, tile, tf=512):
    t_tokens, d = h.shape
    nt = t_tokens // tile
    ff = w_gate.shape[2]
    assert ff % tf == 0 and tf % LANES == 0
    rows = _moe_rows(tile)
    tok = lambda n: pl.BlockSpec((1, tile, n), lambda i, e, j, o, c: (i, 0, 0))
    grid_spec = pltpu.PrefetchScalarGridSpec(
        num_scalar_prefetch=2,
        grid=(nt, N_EXPERTS, ff // tf),
        in_specs=[tok(d), tok(LANES), tok(LANES),
                  pl.BlockSpec((1, 8, tile), lambda i, e, j, o, c: (i, 0, 0)),
                  pl.BlockSpec((1, d, tf), lambda i, e, j, o, c: (e, 0, j)),
                  pl.BlockSpec((1, d, tf), lambda i, e, j, o, c: (e, 0, j)),
                  pl.BlockSpec((1, tf, d), lambda i, e, j, o, c: (e, j, 0))],
        out_specs=tok(d),
        scratch_shapes=[pltpu.VMEM((rows, d), BF16), pltpu.VMEM((rows, d), F32),
                        pltpu.VMEM((rows, LANES), F32)])
    out = pl.pallas_call(
        _experts_kernel,
        grid_spec=grid_spec,
        out_shape=jax.ShapeDtypeStruct((nt, tile, d), F32),
        compiler_params=_cparams("arbitrary", "arbitrary", "arbitrary"),
        name="moe_ffn",
    )(offc, nch, h.reshape(nt, tile, d), dest, wgt, dest_t,
      w_gate.astype(BF16), w_up.astype(BF16), w_down.astype(BF16))
    return out.reshape(t_tokens, d)


def _post_kernel(x_ref, f_ref, gpost_ref, gate_ref, o_ref):
    o_ref[0] = x_ref[0] + gate_ref[0] * _rms(f_ref[0], gpost_ref[...])


def _post_residual(x, f, g_post, gate_f):
    bsz, s, d = x.shape
    tm = min(1024, s)
    act = pl.BlockSpec((1, tm, d), lambda b, i: (b, i, 0))
    return pl.pallas_call(
        _post_kernel,
        grid=(bsz, s // tm),
        in_specs=[act, act, pl.BlockSpec((1, d), lambda b, i: (0, 0)),
                  pl.BlockSpec((1, 1, d), lambda b, i: (b, 0, 0))],
        out_specs=act,
        out_shape=jax.ShapeDtypeStruct((bsz, s, d), F32),
        compiler_params=_cparams("arbitrary", "arbitrary"),
        name="post_residual",
    )(x, f, g_post.reshape(1, d), gate_f.reshape(bsz, 1, d))


def _mixer(x, mod, layer, rel_table, g_pre_mix, w_in, cmp_params):
    shift_m, scale_m = mod[:, 0], mod[:, 1]
    proj = _in_proj(x, g_pre_mix, scale_m, shift_m, _arrange_w_in(w_in))
    o_sb = _sb_attention(proj)
    o_nsa = _nsa_attention(proj, _compress(proj, *cmp_params), rel_table)
    return o_sb, o_nsa


def kernel(x, c, rel_table, w_ada, b_ada, g_pre_mix, g_post_mix, g_pre_ffn, g_post_ffn, w_in, w_out, g_sb, g_nsa, cmp_pos_k, cmp_w1_k, cmp_w2_k, cmp_pos_v, cmp_w1_v, cmp_w2_v, ffn_w_gate, ffn_w_up, ffn_w_down, moe_w_router, moe_b_router, moe_w_gate, moe_w_up, moe_w_down):
    bsz, s, d = x.shape
    depth = w_in.shape[0]
    mods = _ada(c, w_ada, b_ada).reshape(depth, bsz, 6, d)
    moe_tile = min(1024, s)
    for layer in range(depth):
        mod = mods[layer]
        cmp_params = (cmp_pos_k[layer], cmp_w1_k[layer], cmp_w2_k[layer],
                      cmp_pos_v[layer], cmp_w1_v[layer], cmp_w2_v[layer])
        o_sb, o_nsa = _mixer(x, mod, layer, rel_table, g_pre_mix[layer], w_in[layer], cmp_params)
        x, h = _out_proj(o_sb, o_nsa, x, g_sb[layer], g_nsa[layer], w_out[layer], g_post_mix[layer],
                         mod[:, 2], g_pre_ffn[layer], mod[:, 4], mod[:, 3])
        i = layer // 2
        if layer % 2 == 0:
            x = _dense_ffn(h, x, ffn_w_gate[i], ffn_w_up[i], ffn_w_down[i], g_post_ffn[layer], mod[:, 5])
        else:
            routing = _route(x, g_pre_ffn[layer], mod[:, 4], mod[:, 3],
                             moe_w_router[i], moe_b_router[i], moe_tile)
            f = _experts(h.reshape(bsz * s, d), *routing,
                         moe_w_gate[i], moe_w_up[i], moe_w_down[i], moe_tile)
            x = _post_residual(x, f.reshape(bsz, s, d), g_post_ffn[layer], mod[:, 5])
    return x
```

```python
import functools
import math

import numpy as np
import jax
import jax.numpy as jnp
from jax import lax
from jax.experimental import pallas as pl
from jax.experimental.pallas import tpu as pltpu

F32 = jnp.float32
BF16 = jnp.bfloat16
I32 = jnp.int32

LANES = 128
DH = 64
N_SB = 8
N_NSA = 8
G_NSA = 2
Z_NSA = 4
CMP_BLOCK = 32
CMP_STRIDE = 16
SLC_BLOCK = 64
N_SELECT = 16
WINDOW = 512
REL_BUCKETS = 32
REL_MAX_DIST = 128
N_EXPERTS = 8
EPS = 1e-6
FORCED = 1e4
NEG = -1e30
M_INIT = -1e29
SB_EXIT = -104.5
VMEM_LIMIT = 56 * 1024 * 1024

TQ = 128
KC_FRONT = 16
SEL_FAR_TILE = 512


def _cparams(*sem):
    return pltpu.CompilerParams(dimension_semantics=sem, vmem_limit_bytes=VMEM_LIMIT)


def _nt(a, b):
    return lax.dot_general(a, b, (((1,), (1,)), ((), ())), preferred_element_type=F32)


def _dot(a, b):
    return jnp.dot(a, b, preferred_element_type=F32)


def _split3(a):
    hi = a.astype(BF16)
    r = a - hi.astype(F32)
    mid = r.astype(BF16)
    lo = (r - mid.astype(F32)).astype(BF16)
    return hi, mid, lo


def _dot_hl(a, b):
    hi = a.astype(BF16)
    lo = (a - hi.astype(F32)).astype(BF16)
    return _dot(hi, b) + _dot(lo, b)


def _dot_f32(a, b):
    ah, am, al = _split3(a)
    bh, bm, bl = _split3(b)
    return (_dot(ah, bh) + (_dot(ah, bm) + _dot(am, bh))
            + (_dot(ah, bl) + _dot(am, bm) + _dot(al, bh)))


def _rms(x, g):
    return x * lax.rsqrt(jnp.mean(x * x, axis=-1, keepdims=True) + EPS) * g


def _ada_kernel(c_ref, w_ref, b_ref, o_ref):
    c = c_ref[...]
    ca = c * (1.0 / (1.0 + jnp.exp(-c)))
    o_ref[0] = _dot_f32(ca, w_ref[0]) + b_ref[0]


def _ada(c, w_ada, b_ada):
    depth, d, n = w_ada.shape
    bsz = c.shape[0]
    rows = 8
    tn = 1536
    cp = jnp.zeros((rows, d), F32).at[:bsz].set(c)
    out = pl.pallas_call(
        _ada_kernel,
        grid=(depth, n // tn),
        in_specs=[pl.BlockSpec((rows, d), lambda l, j: (0, 0)),
                  pl.BlockSpec((1, d, tn), lambda l, j: (l, 0, j)),
                  pl.BlockSpec((1, 1, tn), lambda l, j: (l, 0, j))],
        out_specs=pl.BlockSpec((1, rows, tn), lambda l, j: (l, 0, j)),
        out_shape=jax.ShapeDtypeStruct((depth, rows, n), F32),
        compiler_params=_cparams("arbitrary", "arbitrary"),
        name="ada_mod",
    )(cp, w_ada, b_ada.reshape(depth, 1, n))
    return out[:, :bsz]


def _in_kernel(x_ref, g_ref, sc_ref, sh_ref, w_ref, o_ref, *, cn):
    h = _rms(x_ref[0], g_ref[...]) * (1.0 + sc_ref[0]) + sh_ref[0]
    hb = h.astype(BF16)
    for j in range(w_ref.shape[1] // cn):
        o_ref[0, :, j * cn:(j + 1) * cn] = _dot(hb, w_ref[:, j * cn:(j + 1) * cn]).astype(BF16)


def _in_proj(x, g, scale, shift, w):
    bsz, s, d = x.shape
    nc = w.shape[1]
    tm = min(512, s)
    cn = nc // 3 if (nc // 3) % LANES == 0 else nc
    return pl.pallas_call(
        functools.partial(_in_kernel, cn=cn),
        grid=(bsz, s // tm),
        in_specs=[pl.BlockSpec((1, tm, d), lambda b, i: (b, i, 0)),
                  pl.BlockSpec((1, d), lambda b, i: (0, 0)),
                  pl.BlockSpec((1, 1, d), lambda b, i: (b, 0, 0)),
                  pl.BlockSpec((1, 1, d), lambda b, i: (b, 0, 0)),
                  pl.BlockSpec((d, nc), lambda b, i: (0, 0))],
        out_specs=pl.BlockSpec((1, tm, nc), lambda b, i: (b, i, 0)),
        out_shape=jax.ShapeDtypeStruct((bsz, s, nc), BF16),
        compiler_params=_cparams("arbitrary", "arbitrary"),
        name="in_proj",
    )(x, g.reshape(1, d), scale.reshape(bsz, 1, d), shift.reshape(bsz, 1, d), w)


CB_SBQ, CB_SBK, CB_SBV, CB_NSAQ = 0, 4, 8, 12
CB_KCMP, CB_VCMP = 16, 17
CB_KSLC, CB_VSLC, CB_KWIN, CB_VWIN = 18, 20, 22, 24
CB_GATE = 26
N_CB = 27


def _arrange_w_in(w_in):
    d = w_in.shape[0]
    w_sb = N_SB * DH
    off_nsa_q = 3 * w_sb
    off_kv = off_nsa_q + N_NSA * DH
    off_gate = off_kv + 3 * 2 * G_NSA * DH
    scale = DH ** -0.5

    def kv(br, kvi, g):
        lo = off_kv + ((br * 2 + kvi) * G_NSA + g) * DH
        return w_in[:, lo:lo + DH]

    cols = [w_in[:, 0:w_sb] * scale, w_in[:, w_sb:2 * w_sb], w_in[:, 2 * w_sb:3 * w_sb],
            w_in[:, off_nsa_q:off_kv] * scale,
            kv(0, 0, 0), kv(0, 0, 1), kv(0, 1, 0), kv(0, 1, 1)]
    for br in (1, 2):
        for kvi in (0, 1):
            for g in range(G_NSA):
                cols += [kv(br, kvi, g), kv(br, kvi, g)]
    n_gate = 3 * N_NSA
    cols += [w_in[:, off_gate:off_gate + n_gate], jnp.zeros((d, LANES - n_gate), w_in.dtype)]
    out = jnp.concatenate(cols, axis=1).astype(BF16)
    assert out.shape[1] == N_CB * LANES
    return out


def _sb_kernel(q_ref, k_ref, v_ref, o_ref, *, t):
    qi = pl.program_id(2)
    q = q_ref[0]
    lane = lax.broadcasted_iota(I32, (t, LANES), 1)
    row = lax.broadcasted_iota(I32, (t, t), 0)
    col = lax.broadcasted_iota(I32, (t, t), 1)
    upper = jnp.where(row > col, 1.0, 0.0).astype(BF16)
    ones = jnp.ones((t, LANES), BF16)
    causal = col < row
    rep = t // LANES

    def tile(qh, k, v, carry, acc, diag):
        z = _nt(qh, k)
        lk = -(jnp.maximum(z, 0.0) + jnp.log(1.0 + jnp.exp(-jnp.abs(z))))
        if diag:
            lk = jnp.where(causal, lk, 0.0)
        hi = lk.astype(BF16)
        lo = (lk - hi.astype(F32)).astype(BF16)
        later = _dot(hi, upper) + _dot(lo, upper)
        tot = _dot(hi, ones) + _dot(lo, ones)
        cb = carry if rep == 1 else jnp.concatenate([carry] * rep, axis=1)
        w = jnp.exp(z + lk + later + cb)
        if diag:
            w = jnp.where(causal, w, 0.0)
        return carry + tot, acc + _dot(w.astype(BF16), v)

    zq = jnp.zeros_like(q)
    qhs = (jnp.where(lane < DH, q, zq), jnp.where(lane < DH, zq, q))
    zero = jnp.zeros((t, LANES), F32)

    def both(kt, st, diag):
        k = k_ref[0, pl.ds(kt * t, t), :]
        v = v_ref[0, pl.ds(kt * t, t), :]
        c0, a0 = tile(qhs[0], k, v, st[0], st[1], diag)
        c1, a1 = tile(qhs[1], k, v, st[2], st[3], diag)
        return jnp.maximum(jnp.max(c0), jnp.max(c1)), (c0, a0, c1, a1)

    mx, st = both(qi, (zero, zero, zero, zero), True)

    def cond(s):
        return jnp.logical_and(s[0] >= 0, s[1] > SB_EXIT)

    def body(s):
        mx, st = both(s[0], s[2], False)
        return s[0] - 1, mx, st

    _, _, st = lax.while_loop(cond, body, (qi - 1, mx, st))
    o_ref[0] = jnp.where(lane < DH, st[1], st[3])


def _sb_attention(proj, t=256):
    bsz, s, _ = proj.shape
    t = min(t, s)
    npair = N_SB // 2
    return pl.pallas_call(
        functools.partial(_sb_kernel, t=t),
        grid=(bsz, npair, s // t),
        in_specs=[pl.BlockSpec((1, t, LANES), lambda b, j, i: (b, i, CB_SBQ + j)),
                  pl.BlockSpec((1, s, LANES), lambda b, j, i: (b, 0, CB_SBK + j)),
                  pl.BlockSpec((1, s, LANES), lambda b, j, i: (b, 0, CB_SBV + j))],
        out_specs=pl.BlockSpec((1, t, LANES), lambda b, j, i: (b, i, j)),
        out_shape=jax.ShapeDtypeStruct((bsz, s, N_SB * DH), F32),
        compiler_params=_cparams("arbitrary", "arbitrary", "arbitrary"),
        name="sb_attn",
    )(proj, proj, proj)


def _gelu_tanh(x):
    return 0.5 * x * (1.0 + jnp.tanh(math.sqrt(2.0 / math.pi) * (x + 0.044715 * (x * x * x))))


def _cmp_kernel(xa_ref, xb_ref, pos_ref, w1_ref, w2_ref, o_ref):
    half = w1_ref.shape[1] // 2
    w1a = w1_ref[0, :half, :]
    w1b = w1_ref[0, half:, :]
    pos = pos_ref[0]
    bias = _dot(pos[:, :half], w1a) + _dot(pos[:, half:], w1b)
    hid = _dot(xa_ref[0, 0], w1a) + _dot(xb_ref[0, 0], w1b) + bias[0:1, :]
    o_ref[0, 0] = _dot(_gelu_tanh(hid).astype(BF16), w2_ref[0])


def _compress(proj, pos_k, w1_k, w2_k, pos_v, w1_v, w2_v):
    bsz, s, _ = proj.shape
    n16 = s // CMP_STRIDE
    raw = proj[:, :, CB_KCMP * LANES:(CB_VCMP + 1) * LANES]
    x16 = raw.reshape(bsz, s, 4, DH).transpose(0, 2, 1, 3).reshape(bsz, 4, n16, CMP_STRIDE * DH)
    x16b = jnp.concatenate([x16[:, :, 1:], jnp.zeros_like(x16[:, :, :1])], axis=2)
    hidden = w1_k.shape[1]
    w1 = jnp.stack([w1_k, w1_v]).astype(BF16)
    w2 = jnp.stack([jnp.concatenate([w2_k, w2_k], 1), jnp.concatenate([w2_v, w2_v], 1)]).astype(BF16)
    pos = jnp.stack([pos_k.reshape(1, -1), pos_v.reshape(1, -1)])
    pos = jnp.concatenate([pos, jnp.zeros((2, 7, pos.shape[2]), pos.dtype)], axis=1).astype(BF16)
    k16 = CMP_STRIDE * DH
    return pl.pallas_call(
        _cmp_kernel,
        grid=(bsz, 4),
        in_specs=[pl.BlockSpec((1, 1, n16, k16), lambda b, j: (b, j, 0, 0)),
                  pl.BlockSpec((1, 1, n16, k16), lambda b, j: (b, j, 0, 0)),
                  pl.BlockSpec((1, 8, 2 * k16), lambda b, j: (j // 2, 0, 0)),
                  pl.BlockSpec((1, 2 * k16, hidden), lambda b, j: (j // 2, 0, 0)),
                  pl.BlockSpec((1, hidden, LANES), lambda b, j: (j // 2, 0, 0))],
        out_specs=pl.BlockSpec((1, 1, n16, LANES), lambda b, j: (b, j, 0, 0)),
        out_shape=jax.ShapeDtypeStruct((bsz, 4, n16, LANES), F32),
        compiler_params=_cparams("arbitrary", "arbitrary"),
        name="nsa_compress",
    )(x16, x16b, pos, w1, w2)


def _bucket_table():
    n = np.arange(REL_MAX_DIST + 1)
    exact = REL_BUCKETS // 2
    val = (np.log(np.maximum(n, 1).astype(np.float32) / np.float32(exact)).astype(np.float32)
           / np.float32(math.log(REL_MAX_DIST / exact)) * np.float32(REL_BUCKETS - exact))
    large = np.minimum(exact + val.astype(np.int32), REL_BUCKETS - 1)
    return np.where(n < exact, n, large).astype(np.int32)


def _bias_tiles(rel_table):
    bucket = _bucket_table()
    far_row = rel_table[int(bucket[REL_MAX_DIST])].astype(F32)
    r = np.arange(TQ)[:, None]

    def tile(dist, valid, base):
        onehot = jax.nn.one_hot(bucket[np.clip(dist, 0, REL_MAX_DIST)], REL_BUCKETS, dtype=F32)
        t = jnp.einsum('rwb,bh->hrw', onehot, rel_table.astype(F32),
                       precision=lax.Precision.HIGHEST)
        t = jnp.where(valid[None], t - base, NEG)
        return t.reshape(G_NSA, Z_NSA * TQ, dist.shape[1])

    w = np.arange(LANES)[None, :]
    d_c = r - CMP_STRIDE * (w - KC_FRONT) - (CMP_BLOCK - 1)
    w2 = np.arange(2 * TQ)[None, :]
    d_s = r + TQ - w2
    w5 = np.arange(WINDOW + TQ)[None, :]
    d_w = r + WINDOW - w5
    far = jnp.broadcast_to(far_row[:, None, None], (N_NSA, TQ, LANES))
    far = far.reshape(G_NSA, Z_NSA * TQ, LANES)
    return (tile(d_c, d_c >= 0, far_row[:, None, None]), tile(d_s, d_s >= 0, 0.0),
            tile(d_w, (d_w >= 0) & (d_w < WINDOW), 0.0), far)


def _gate_expand():
    e = np.zeros((G_NSA, LANES, 3 * Z_NSA * DH), np.float32)
    for g in range(G_NSA):
        for z in range(Z_NSA):
            for br in range(3):
                e[g, (g * Z_NSA + z) * 3 + br, br * Z_NSA * DH + z * DH: br * Z_NSA * DH + (z + 1) * DH] = 1.0
    return e


def _overlap_padded(n16, kcp, n_slc):
    i = np.arange(kcp)[:, None] - KC_FRONT
    j = np.arange(LANES)[None, :]
    n_cmp = n16 - 1
    ok = (i >= 0) & (i < n_cmp) & (j < n_slc)
    ov = (i * CMP_STRIDE < j * SLC_BLOCK + SLC_BLOCK) & (i * CMP_STRIDE + CMP_BLOCK > j * SLC_BLOCK)
    return (ok & ov).astype(np.float32)


def _block_onehot(s):
    return (np.arange(s)[:, None] // SLC_BLOCK == np.arange(LANES)[None, :]).astype(np.float32)


def _nsa_kernel(q_ref, kcb_ref, vcb_ref, kcf_ref, vcf_ref, ks_ref, vs_ref, kw_ref, vw_ref, et_ref,
                gate_ref, ovb_ref, ovf_ref, bcn_ref, bsn_ref, bw_ref, cf_ref, eg_ref, o_ref,
                sb0_ref, sb1_ref, *, n_slc, n_round):
    blk = pl.program_id(2)
    c0 = blk * TQ
    r4 = Z_NSA * TQ
    q = q_ref[0]
    lane = lax.broadcasted_iota(I32, (TQ, LANES), 1)
    lo_half = lane < DH
    zq = jnp.zeros((TQ, LANES), BF16)
    qa, qb = q[:, :LANES], q[:, LANES:]
    qs = jnp.concatenate([jnp.where(lo_half, qa, zq), jnp.where(lo_half, zq, qa),
                          jnp.where(lo_half, qb, zq), jnp.where(lo_half, zq, qb)], axis=0)
    ones_k = jnp.ones((SEL_FAR_TILE, LANES), BF16)
    ones_w = jnp.ones((WINDOW + TQ, LANES), BF16)

    kcp = kcb_ref.shape[2]
    prow = lax.broadcasted_iota(I32, (1, kcp), 1)
    far_row = jnp.where((prow >= KC_FRONT) & (prow < 8 * blk), 0.0, NEG)
    lane1 = lax.broadcasted_iota(I32, (1, LANES), 1)
    near_row = jnp.where(lane1 + 8 * blk >= KC_FRONT, 0.0, NEG)
    near0 = pl.multiple_of(8 * blk, 8)
    kcn = kcf_ref[0, 0, pl.ds(near0, LANES), :].astype(BF16)
    vcn = vcf_ref[0, 0, pl.ds(near0, LANES), :].astype(BF16)
    s_far = _nt(qs, kcb_ref[0, 0]) + far_row
    s_near = _nt(qs, kcn) + bcn_ref[0] + near_row
    m_c = jnp.maximum(jnp.max(s_far, axis=1, keepdims=True), jnp.max(s_near, axis=1, keepdims=True))
    m_c = jnp.maximum(m_c, M_INIT)
    p_far = jnp.exp(s_far - m_c)
    p_near = jnp.exp(s_near - m_c)
    den = jnp.sum(p_far, axis=1, keepdims=True) + jnp.sum(p_near, axis=1, keepdims=True)
    inv_c = 1.0 / jnp.maximum(den, 1e-30)
    o_c = (_dot(p_far.astype(BF16), vcb_ref[0, 0]) + _dot(p_near.astype(BF16), vcn)) * inv_c

    pn_far = p_far * inv_c
    pn_near = p_near * inv_c
    pz_far = pn_far[0:TQ] + pn_far[TQ:2 * TQ] + pn_far[2 * TQ:3 * TQ] + pn_far[3 * TQ:]
    pz_near = pn_near[0:TQ] + pn_near[TQ:2 * TQ] + pn_near[2 * TQ:3 * TQ] + pn_near[3 * TQ:]
    ovn = ovf_ref[pl.ds(near0, LANES), :].astype(BF16)
    imp = _dot_hl(pz_far, ovb_ref[...]) + _dot_hl(pz_near, ovn)
    tpos = c0 + lax.broadcasted_iota(I32, (TQ, LANES), 0)
    cur = lax.shift_right_logical(tpos, 6)
    valid = (lane * SLC_BLOCK <= tpos) & (lane < n_slc)
    forced = valid & ((lane == 0) | (lane == cur) | (lane == cur - 1))
    score = jnp.where(valid & jnp.logical_not(forced), imp, -1.0)
    score = jnp.where(lane < n_slc, score, -2.0)

    rest, taken, thr = score, jnp.zeros((TQ, 1), F32), jnp.zeros((TQ, 1), F32)
    for _ in range(n_round):
        m = jnp.max(rest, axis=1, keepdims=True)
        eq = rest == m
        thr = jnp.where(taken < n_round, m, thr)
        taken = taken + jnp.sum(jnp.where(eq, 1.0, 0.0), axis=1, keepdims=True)
        rest = jnp.where(eq, -jnp.inf, rest)
    above = score > thr
    at_thr = score == thr
    need = n_round - jnp.sum(jnp.where(above, 1.0, 0.0), axis=1, keepdims=True)
    brow = lax.broadcasted_iota(I32, (LANES, LANES), 0)
    bcol = lax.broadcasted_iota(I32, (LANES, LANES), 1)
    before = _dot(jnp.where(at_thr, 1.0, 0.0).astype(BF16),
                  jnp.where(brow < bcol, 1.0, 0.0).astype(BF16))
    sel = ((above | (at_thr & (before < need))) & valid) | forced

    n_w = WINDOW // TQ + 1
    kws, vws = [], []
    for w in range(n_w):
        st = pl.multiple_of(jnp.maximum(c0 - WINDOW + w * TQ, 0), TQ)
        kws.append(kw_ref[0, pl.ds(st, TQ), :])
        vws.append(vw_ref[0, pl.ds(st, TQ), :])
    wl = lax.broadcasted_iota(I32, (1, WINDOW + TQ), 1)
    pos_row = jnp.where(c0 - WINDOW + wl >= 0, 0.0, NEG)
    s = _nt(qs, jnp.concatenate(kws, axis=0)) + bw_ref[0] + pos_row
    p = jnp.exp(s - jnp.max(s, axis=1, keepdims=True)).astype(BF16)
    r_w = _dot(p, jnp.concatenate([jnp.concatenate(vws, axis=0), ones_w], axis=1))
    o_w = r_w[:, :LANES] / r_w[:, LANES:]
    gl = _dot(gate_ref[0], eg_ref[0])
    sig = 1.0 / (1.0 + jnp.exp(-gl))
    wide = Z_NSA * DH

    sel_all = jnp.where(sel, 0.0, NEG).astype(BF16)
    sel_far = jnp.where(sel & (lane < 2 * blk - 2), 0.0, NEG).astype(BF16)
    q_far = jnp.concatenate([qs, jnp.concatenate([sel_far] * Z_NSA, axis=0)], axis=1)
    q_near = jnp.concatenate([qs, jnp.concatenate([sel_all] * Z_NSA, axis=0)], axis=1)
    tk = SEL_FAR_TILE
    last_tile = ks_ref.shape[1] // tk - 1

    def qk(kt):
        k0 = pl.multiple_of(kt * tk, tk)
        return _nt(q_far, jnp.concatenate([ks_ref[0, pl.ds(k0, tk), :], et_ref[pl.ds(k0, tk), :]], axis=1))

    def fold(s, v, st):
        m, acc = st
        mn = jnp.maximum(m, jnp.max(s, axis=1, keepdims=True))
        p = jnp.exp(s - mn).astype(BF16)
        return mn, jnp.exp(m - mn) * acc + _dot(p, v)

    def v_tile(kt):
        k0 = pl.multiple_of(kt * tk, tk)
        return jnp.concatenate([vs_ref[0, pl.ds(k0, tk), :], ones_k], axis=1)

    n_pair = lax.shift_right_logical(jnp.maximum(blk - 1, 0) + 7, 3)
    sb0_ref[...] = qk(0)

    def far_step(i, st):
        sb1_ref[...] = qk(2 * i + 1)
        st = fold(sb0_ref[...], v_tile(2 * i), st)
        sb0_ref[...] = qk(jnp.minimum(2 * i + 2, last_tile))
        return fold(sb1_ref[...], v_tile(2 * i + 1), st)

    m0 = jnp.full((r4, 1), M_INIT, F32)
    m_s, acc_s = lax.fori_loop(0, n_pair, far_step, (m0, jnp.zeros((r4, 2 * LANES), F32)))
    m_s = m_s + cf_ref[0][:, 0:1]
    p0 = pl.multiple_of(jnp.maximum(c0 - TQ, 0), TQ)
    d0 = pl.multiple_of(c0, TQ)
    kn = jnp.concatenate([ks_ref[0, pl.ds(p0, TQ), :], ks_ref[0, pl.ds(d0, TQ), :]], axis=0)
    vn = jnp.concatenate([vs_ref[0, pl.ds(p0, TQ), :], vs_ref[0, pl.ds(d0, TQ), :]], axis=0)
    nrow = lax.broadcasted_iota(I32, (2 * TQ, LANES), 0)
    nlane = lax.broadcasted_iota(I32, (2 * TQ, LANES), 1)
    e_n = jnp.where(lax.shift_right_logical(nrow, 6) + 2 * blk - 2 == nlane, 1.0, 0.0).astype(BF16)
    lane2 = lax.broadcasted_iota(I32, (1, 2 * TQ), 1)
    prev_row = jnp.where((lane2 < TQ) & (blk == 0), NEG, 0.0)
    s = _nt(q_near, jnp.concatenate([kn, e_n], axis=1)) + bsn_ref[0] + prev_row
    _, acc_s = fold(s, jnp.concatenate([vn, ones_k[:2 * TQ]], axis=1), (m_s, acc_s))
    o_s = acc_s[:, :LANES] / acc_s[:, LANES:]

    def heads(o):
        return jnp.concatenate([jnp.where(lo_half, o[0:TQ], o[TQ:2 * TQ]),
                                jnp.where(lo_half, o[2 * TQ:3 * TQ], o[3 * TQ:])], axis=1)

    o_ref[0] = (sig[:, 0:wide] * heads(o_c) + sig[:, wide:2 * wide] * heads(o_s)
                + sig[:, 2 * wide:] * heads(o_w))


def _nsa_attention(proj, cmp_out, rel_table):
    bsz, s, _ = proj.shape
    assert s % (2 * SEL_FAR_TILE) == 0 and s >= WINDOW + TQ
    n16 = s // CMP_STRIDE
    n_slc = s // SLC_BLOCK
    assert N_SELECT <= n_slc <= LANES
    nq = s // TQ
    kcp = -(-(max(n16 + KC_FRONT, 8 * (nq - 1) + LANES)) // LANES) * LANES
    pad = ((0, 0), (0, 0), (KC_FRONT, kcp - n16 - KC_FRONT), (0, 0))
    cf = jnp.pad(cmp_out, pad)
    cb = cf.astype(BF16)
    ov = _overlap_padded(n16, kcp, n_slc)
    bcn, bsn, bw, far = _bias_tiles(rel_table)
    eg = jnp.asarray(_gate_expand(), BF16)
    r4 = Z_NSA * TQ
    kv_spec = lambda cb0: pl.BlockSpec((1, s, LANES), lambda b, g, i, cb0=cb0: (b, 0, cb0 + g))
    cmp_spec = lambda j0: pl.BlockSpec((1, 1, kcp, LANES), lambda b, g, i, j0=j0: (b, j0 + g, 0, 0))
    tile_spec = lambda w: pl.BlockSpec((1, r4, w), lambda b, g, i: (g, 0, 0))
    const2 = lambda n: pl.BlockSpec((n, LANES), lambda b, g, i: (0, 0))
    return pl.pallas_call(
        functools.partial(_nsa_kernel, n_slc=n_slc, n_round=N_SELECT - 3),
        grid=(bsz, G_NSA, nq),
        in_specs=[pl.BlockSpec((1, TQ, 2 * LANES), lambda b, g, i: (b, i, CB_NSAQ // 2 + g)),
                  cmp_spec(0), cmp_spec(2), cmp_spec(0), cmp_spec(2),
                  kv_spec(CB_KSLC), kv_spec(CB_VSLC), kv_spec(CB_KWIN), kv_spec(CB_VWIN),
                  const2(s),
                  pl.BlockSpec((1, TQ, LANES), lambda b, g, i: (b, i, CB_GATE)),
                  const2(kcp), const2(kcp),
                  tile_spec(LANES), tile_spec(2 * TQ), tile_spec(WINDOW + TQ), tile_spec(LANES),
                  pl.BlockSpec((1, LANES, 3 * Z_NSA * DH), lambda b, g, i: (g, 0, 0))],
        out_specs=pl.BlockSpec((1, TQ, Z_NSA * DH), lambda b, g, i: (b, i, g)),
        out_shape=jax.ShapeDtypeStruct((bsz, s, N_NSA * DH), F32),
        scratch_shapes=[pltpu.VMEM((r4, SEL_FAR_TILE), F32), pltpu.VMEM((r4, SEL_FAR_TILE), F32)],
        compiler_params=_cparams("arbitrary", "arbitrary", "arbitrary"),
        name="nsa_attn",
    )(proj, cb, cb, cf, cf, proj, proj, proj, proj, jnp.asarray(_block_onehot(s), BF16), proj,
      jnp.asarray(ov, BF16), jnp.asarray(ov, F32), bcn, bsn, bw, far, eg)


def _out_kernel(osb_ref, onsa_ref, x_ref, gsb_ref, gnsa_ref, w_ref, gpost_ref, gate_ref,
                gpre_ref, sc_ref, sh_ref, xo_ref, h_ref):
    half = osb_ref.shape[2]
    a = _rms(osb_ref[0], gsb_ref[...]).astype(BF16)
    b = _rms(onsa_ref[0], gnsa_ref[...]).astype(BF16)
    m = _dot(a, w_ref[:half, :]) + _dot(b, w_ref[half:, :])
    x = x_ref[0] + gate_ref[0] * _rms(m, gpost_ref[...])
    xo_ref[0] = x
    h_ref[0] = (_rms(x, gpre_ref[...]) * (1.0 + sc_ref[0]) + sh_ref[0]).astype(BF16)


def _out_proj(o_sb, o_nsa, x, g_sb, g_nsa, w_out, g_post, gate_m, g_pre_ffn, scale_f, shift_f):
    bsz, s, d = x.shape
    half = o_sb.shape[2]
    tm = min(512, s)
    row = lambda n: pl.BlockSpec((1, n), lambda b, i: (0, 0))
    mod = pl.BlockSpec((1, 1, d), lambda b, i: (b, 0, 0))
    act = lambda n: pl.BlockSpec((1, tm, n), lambda b, i: (b, i, 0))
    return pl.pallas_call(
        _out_kernel,
        grid=(bsz, s // tm),
        in_specs=[act(half), act(half), act(d), row(half), row(half),
                  pl.BlockSpec((2 * half, d), lambda b, i: (0, 0)), row(d), mod, row(d), mod, mod],
        out_specs=[act(d), act(d)],
        out_shape=[jax.ShapeDtypeStruct((bsz, s, d), F32), jax.ShapeDtypeStruct((bsz, s, d), BF16)],
        compiler_params=_cparams("arbitrary", "arbitrary"),
        name="out_proj",
    )(o_sb, o_nsa, x, g_sb.reshape(1, half), g_nsa.reshape(1, half), w_out.astype(BF16),
      g_post.reshape(1, d), gate_m.reshape(bsz, 1, d), g_pre_ffn.reshape(1, d),
      scale_f.reshape(bsz, 1, d), shift_f.reshape(bsz, 1, d))


def _silu(x):
    return x * (1.0 / (1.0 + jnp.exp(-x)))


def _ffn_kernel(h_ref, x_ref, wg_ref, wu_ref, wd_ref, gpost_ref, gate_ref, o_ref, acc_ref):
    j = pl.program_id(2)
    h = h_ref[0]
    a = (_silu(_dot(h, wg_ref[...])) * _dot(h, wu_ref[...])).astype(BF16)
    part = _dot(a, wd_ref[...])

    @pl.when(j == 0)
    def _():
        acc_ref[...] = part

    @pl.when(j > 0)
    def _():
        acc_ref[...] += part

    @pl.when(j == pl.num_programs(2) - 1)
    def _():
        o_ref[0] = x_ref[0] + gate_ref[0] * _rms(acc_ref[...], gpost_ref[...])


def _dense_ffn(h, x, w_gate, w_up, w_down, g_post, gate_f):
    bsz, s, d = x.shape
    ff = w_gate.shape[1]
    tm = min(1024, s)
    tf = 256
    assert ff % tf == 0
    act = pl.BlockSpec((1, tm, d), lambda b, i, j: (b, i, 0))
    return pl.pallas_call(
        _ffn_kernel,
        grid=(bsz, s // tm, ff // tf),
        in_specs=[act, act,
                  pl.BlockSpec((d, tf), lambda b, i, j: (0, j)),
                  pl.BlockSpec((d, tf), lambda b, i, j: (0, j)),
                  pl.BlockSpec((tf, d), lambda b, i, j: (j, 0)),
                  pl.BlockSpec((1, d), lambda b, i, j: (0, 0)),
                  pl.BlockSpec((1, 1, d), lambda b, i, j: (b, 0, 0))],
        out_specs=act,
        out_shape=jax.ShapeDtypeStruct((bsz, s, d), F32),
        scratch_shapes=[pltpu.VMEM((tm, d), F32)],
        compiler_params=_cparams("arbitrary", "arbitrary", "arbitrary"),
        name="dense_ffn",
    )(h, x, w_gate.astype(BF16), w_up.astype(BF16), w_down.astype(BF16),
      g_post.reshape(1, d), gate_f.reshape(bsz, 1, d))


def _router_kernel(x_ref, gpre_ref, sc_ref, sh_ref, wr_ref, br_ref, pos_ref, w_ref, post_ref, cnt_ref):
    tile = x_ref.shape[1]
    h = _rms(x_ref[0], gpre_ref[...]) * (1.0 + sc_ref[0]) + sh_ref[0]
    logits = _dot_f32(h, wr_ref[...]) + br_ref[...]
    lane = lax.broadcasted_iota(I32, (tile, LANES), 1)
    lanef = lane.astype(F32)
    e = jnp.exp(logits - jnp.max(logits, axis=1, keepdims=True))
    probs = e / jnp.sum(e, axis=1, keepdims=True)
    probs = jnp.where(lane < N_EXPERTS, probs, -1.0)
    m1 = jnp.max(probs, axis=1, keepdims=True)
    i1 = jnp.min(jnp.where(probs == m1, lanef, 1e9), axis=1, keepdims=True)
    rest = jnp.where(lanef == i1, -1.0, probs)
    m2 = jnp.max(rest, axis=1, keepdims=True)
    i2 = jnp.min(jnp.where(rest == m2, lanef, 1e9), axis=1, keepdims=True)
    tot = m1 + m2
    wgt = jnp.where(lanef == i1, m1 / tot, jnp.where(lanef == i2, m2 / tot, 0.0))
    mask = (lanef == i1) | (lanef == i2)
    ch = 256
    r = lax.broadcasted_iota(I32, (ch, ch), 0)
    c = lax.broadcasted_iota(I32, (ch, ch), 1)
    lower = jnp.where(c < r, 1.0, 0.0).astype(BF16)
    ones = jnp.ones((8, ch), BF16)
    carry = jnp.zeros((1, LANES), F32)
    ranks = []
    for k in range(tile // ch):
        mk = jnp.where(mask[k * ch:(k + 1) * ch], 1.0, 0.0).astype(BF16)
        ranks.append(_dot(lower, mk) + carry)
        carry = carry + _dot(ones, mk)[0:1]
    rank = jnp.concatenate(ranks, axis=0)
    pos = jnp.where(mask, rank, -1.0)
    pos_ref[0] = pos.astype(I32)
    w_ref[0] = wgt
    post_ref[0] = jnp.transpose(pos)[0:N_EXPERTS].astype(I32)
    cnt_ref[0] = jnp.broadcast_to(carry, (8, LANES)).astype(I32)


def _router(x, g_pre, scale_f, shift_f, w_router, b_router, tile):
    bsz, s, d = x.shape
    nt = s // tile
    wr = jnp.zeros((d, LANES), F32).at[:, :N_EXPERTS].set(w_router.astype(F32))
    br = jnp.full((1, LANES), NEG, F32).at[0, :N_EXPERTS].set(b_router.astype(F32))
    mod = pl.BlockSpec((1, 1, d), lambda b, i: (b, 0, 0))
    tok = lambda n: pl.BlockSpec((1, tile, n), lambda b, i: (b * nt + i, 0, 0))
    pos, wgt, post, cnt = pl.pallas_call(
        _router_kernel,
        grid=(bsz, nt),
        in_specs=[pl.BlockSpec((1, tile, d), lambda b, i: (b, i, 0)),
                  pl.BlockSpec((1, d), lambda b, i: (0, 0)), mod, mod,
                  pl.BlockSpec((d, LANES), lambda b, i: (0, 0)),
                  pl.BlockSpec((1, LANES), lambda b, i: (0, 0))],
        out_specs=[tok(LANES), tok(LANES),
                   pl.BlockSpec((1, N_EXPERTS, tile), lambda b, i: (b * nt + i, 0, 0)),
                   pl.BlockSpec((1, 8, LANES), lambda b, i: (b * nt + i, 0, 0))],
        out_shape=[jax.ShapeDtypeStruct((bsz * nt, tile, LANES), I32),
                   jax.ShapeDtypeStruct((bsz * nt, tile, LANES), F32),
                   jax.ShapeDtypeStruct((bsz * nt, N_EXPERTS, tile), I32),
                   jax.ShapeDtypeStruct((bsz * nt, 8, LANES), I32)],
        compiler_params=_cparams("arbitrary", "arbitrary"),
        name="moe_router",
    )(x, g_pre.reshape(1, d), scale_f.reshape(bsz, 1, d), shift_f.reshape(bsz, 1, d), wr, br)
    return pos, wgt, post, cnt[:, 0, :N_EXPERTS]


def _moe_kernel(cnt_ref, h_ref, pos_ref, w_ref, post_ref, wg_ref, wu_ref, wd_ref, o_ref,
                xg_ref, y_ref, *, rc):
    i, e, j = pl.program_id(0), pl.program_id(1), pl.program_id(2)
    tile = h_ref.shape[1]
    n_chunk = lax.shift_right_logical(cnt_ref[i * N_EXPERTS + e] + (rc - 1), int(math.log2(rc)))
    lane = lax.broadcasted_iota(I32, (tile, LANES), 1)

    @pl.when(jnp.logical_and(e == 0, j == 0))
    def _():
        o_ref[0] = jnp.zeros(o_ref.shape[1:], F32)

    @pl.when(j == 0)
    def _():
        prow = post_ref[0]
        rid = lax.broadcasted_iota(I32, (rc, tile), 0)

        def gather(c, _):
            onehot = jnp.where(prow == rid + c * rc, 1.0, 0.0).astype(BF16)
            r0 = pl.multiple_of(c * rc, rc)
            xg_ref[pl.ds(r0, rc), :] = _dot(onehot, h_ref[0]).astype(BF16)
            return 0

        lax.fori_loop(0, n_chunk, gather, 0)

    def expert(c, _):
        r0 = pl.multiple_of(c * rc, rc)
        xg = xg_ref[pl.ds(r0, rc), :]
        a = (_silu(_dot(xg, wg_ref[0])) * _dot(xg, wu_ref[0])).astype(BF16)
        part = _dot(a, wd_ref[0])

        @pl.when(j == 0)
        def _():
            y_ref[pl.ds(r0, rc), :] = part

        @pl.when(j > 0)
        def _():
            y_ref[pl.ds(r0, rc), :] += part

        return 0

    lax.fori_loop(0, n_chunk, expert, 0)

    @pl.when(j == pl.num_programs(2) - 1)
    def _():
        pcol = jnp.sum(jnp.where(lane == e, pos_ref[0], 0), axis=1, keepdims=True)
        wcol = jnp.sum(jnp.where(lane == e, w_ref[0], 0.0), axis=1, keepdims=True)
        cid = lax.broadcasted_iota(I32, (tile, rc), 1)

        def scatter(c, z):
            r0 = pl.multiple_of(c * rc, rc)
            onehot_t = jnp.where(pcol == cid + c * rc, 1.0, 0.0).astype(BF16)
            return z + _dot(onehot_t, y_ref[pl.ds(r0, rc), :].astype(BF16))

        z = lax.fori_loop(0, n_chunk, scatter, jnp.zeros((tile, o_ref.shape[2]), F32))
        o_ref[0] += wcol * z


def _moe_ffn(h, pos, wgt, post, cnt, w_gate, w_up, w_down, tile, rc=128, tf=512):
    t_tokens, d = h.shape
    nt = t_tokens // tile
    ff = w_gate.shape[2]
    assert ff % tf == 0
    hb = h.reshape(nt, tile, d)
    post3 = post.reshape(nt * N_EXPERTS, 1, tile)
    grid_spec = pltpu.PrefetchScalarGridSpec(
        num_scalar_prefetch=1,
        grid=(nt, N_EXPERTS, ff // tf),
        in_specs=[pl.BlockSpec((1, tile, d), lambda i, e, j, c: (i, 0, 0)),
                  pl.BlockSpec((1, tile, LANES), lambda i, e, j, c: (i, 0, 0)),
                  pl.BlockSpec((1, tile, LANES), lambda i, e, j, c: (i, 0, 0)),
                  pl.BlockSpec((1, 1, tile), lambda i, e, j, c: (i * N_EXPERTS + e, 0, 0)),
                  pl.BlockSpec((1, d, tf), lambda i, e, j, c: (e, 0, j)),
                  pl.BlockSpec((1, d, tf), lambda i, e, j, c: (e, 0, j)),
                  pl.BlockSpec((1, tf, d), lambda i, e, j, c: (e, j, 0))],
        out_specs=pl.BlockSpec((1, tile, d), lambda i, e, j, c: (i, 0, 0)),
        scratch_shapes=[pltpu.VMEM((tile, d), BF16), pltpu.VMEM((tile, d), F32)])
    out = pl.pallas_call(
        functools.partial(_moe_kernel, rc=rc),
        grid_spec=grid_spec,
        out_shape=jax.ShapeDtypeStruct((nt, tile, d), F32),
        compiler_params=_cparams("arbitrary", "arbitrary", "arbitrary"),
        name="moe_ffn",
    )(cnt.reshape(-1), hb, pos, wgt, post3, w_gate.astype(BF16), w_up.astype(BF16), w_down.astype(BF16))
    return out.reshape(t_tokens, d)


MOE_RC = 128
MOE_GC = 256
MOE_SC = 1024


def _moe_rows(tile):
    return -(-(2 * tile + N_EXPERTS * MOE_RC) // MOE_SC) * MOE_SC


def _route_kernel(x_ref, gpre_ref, sc_ref, sh_ref, wr_ref, br_ref, d_ref, w_ref, dt_ref, off_ref, nch_ref):
    tile = x_ref.shape[1]
    h = _rms(x_ref[0], gpre_ref[...]) * (1.0 + sc_ref[0]) + sh_ref[0]
    logits = _dot_f32(h, wr_ref[...]) + br_ref[...]
    lane = lax.broadcasted_iota(I32, (tile, LANES), 1)
    lanef = lane.astype(F32)
    e = jnp.exp(logits - jnp.max(logits, axis=1, keepdims=True))
    probs = e / jnp.sum(e, axis=1, keepdims=True)
    probs = jnp.where(lane < N_EXPERTS, probs, -1.0)
    m1 = jnp.max(probs, axis=1, keepdims=True)
    i1 = jnp.min(jnp.where(probs == m1, lanef, 1e9), axis=1, keepdims=True)
    rest = jnp.where(lanef == i1, -1.0, probs)
    m2 = jnp.max(rest, axis=1, keepdims=True)
    i2 = jnp.min(jnp.where(rest == m2, lanef, 1e9), axis=1, keepdims=True)
    tot = m1 + m2
    hit1 = lanef == i1
    hit2 = lanef == i2
    mask = hit1 | hit2
    ch = 256
    r = lax.broadcasted_iota(I32, (ch, ch), 0)
    c = lax.broadcasted_iota(I32, (ch, ch), 1)
    lower = jnp.where(c < r, 1.0, 0.0).astype(BF16)
    ones = jnp.ones((8, ch), BF16)
    count = jnp.zeros((1, LANES), F32)
    ranks = []
    for k in range(tile // ch):
        mk = jnp.where(mask[k * ch:(k + 1) * ch], 1.0, 0.0).astype(BF16)
        ranks.append(_dot(lower, mk) + count)
        count = count + _dot(ones, mk)[0:1]
    rank = jnp.concatenate(ranks, axis=0)
    shift = int(math.log2(MOE_RC))
    nch = lax.shift_right_logical(count.astype(I32) + (MOE_RC - 1), shift)
    nch8 = jnp.broadcast_to(nch.astype(F32), (8, LANES))
    lr = lax.broadcasted_iota(I32, (LANES, LANES), 0)
    lc = lax.broadcasted_iota(I32, (LANES, LANES), 1)
    offc = _dot(nch8.astype(BF16), jnp.where(lr < lc, 1.0, 0.0).astype(BF16))
    row = offc[0:1] * float(MOE_RC) + rank
    d1 = jnp.sum(jnp.where(hit1, row, 0.0), axis=1, keepdims=True)
    d2 = jnp.sum(jnp.where(hit2, row, 0.0), axis=1, keepdims=True)
    dd = jnp.where(lane == 0, d1, jnp.where(lane == 1, d2, -1.0))
    d_ref[0] = dd.astype(I32)
    w_ref[0] = jnp.where(lane == 0, m1 / tot, jnp.where(lane == 1, m2 / tot, 0.0))
    dt_ref[0] = jnp.transpose(dd)[0:8].astype(I32)
    off_ref[0] = offc.astype(I32)
    nch_ref[0] = jnp.broadcast_to(nch, (8, LANES))


def _route(x, g_pre, scale_f, shift_f, w_router, b_router, tile):
    bsz, s, d = x.shape
    nt = s // tile
    wr = jnp.zeros((d, LANES), F32).at[:, :N_EXPERTS].set(w_router.astype(F32))
    br = jnp.full((1, LANES), NEG, F32).at[0, :N_EXPERTS].set(b_router.astype(F32))
    mod = pl.BlockSpec((1, 1, d), lambda b, i: (b, 0, 0))
    tok = pl.BlockSpec((1, tile, LANES), lambda b, i: (b * nt + i, 0, 0))
    small = pl.BlockSpec((1, 8, LANES), lambda b, i: (b * nt + i, 0, 0))
    dest, wgt, dest_t, offc, nch = pl.pallas_call(
        _route_kernel,
        grid=(bsz, nt),
        in_specs=[pl.BlockSpec((1, tile, d), lambda b, i: (b, i, 0)),
                  pl.BlockSpec((1, d), lambda b, i: (0, 0)), mod, mod,
                  pl.BlockSpec((d, LANES), lambda b, i: (0, 0)),
                  pl.BlockSpec((1, LANES), lambda b, i: (0, 0))],
        out_specs=[tok, tok, pl.BlockSpec((1, 8, tile), lambda b, i: (b * nt + i, 0, 0)), small, small],
        out_shape=[jax.ShapeDtypeStruct((bsz * nt, tile, LANES), I32),
                   jax.ShapeDtypeStruct((bsz * nt, tile, LANES), F32),
                   jax.ShapeDtypeStruct((bsz * nt, 8, tile), I32),
                   jax.ShapeDtypeStruct((bsz * nt, 8, LANES), I32),
                   jax.ShapeDtypeStruct((bsz * nt, 8, LANES), I32)],
        compiler_params=_cparams("arbitrary", "arbitrary"),
        name="moe_router",
    )(x, g_pre.reshape(1, d), scale_f.reshape(bsz, 1, d), shift_f.reshape(bsz, 1, d), wr, br)
    return dest, wgt, dest_t, offc[:, 0, :N_EXPERTS].reshape(-1), nch[:, 0, :N_EXPERTS].reshape(-1)


def _experts_kernel(off_ref, nch_ref, h_ref, d_ref, w_ref, dt_ref, wg_ref, wu_ref, wd_ref, o_ref,
                    xg_ref, y_ref, wr_ref):
    i, e, j = pl.program_id(0), pl.program_id(1), pl.program_id(2)
    tile = h_ref.shape[1]
    rows = xg_ref.shape[0]

    @pl.when(jnp.logical_and(e == 0, j == 0))
    def _():
        y_ref[...] = jnp.zeros(y_ref.shape, F32)
        d1t = dt_ref[0, 0:1, :]
        d2t = dt_ref[0, 1:2, :]
        w = w_ref[0]
        lane = lax.broadcasted_iota(I32, (tile, LANES), 1)

        def pieces(col):
            hi, mid, lo = _split3(jnp.broadcast_to(col, (tile, LANES)))
            z = jnp.zeros((tile, LANES), F32)
            return jnp.where(lane == 0, hi.astype(F32), jnp.where(lane == 1, mid.astype(F32),
                             jnp.where(lane == 2, lo.astype(F32), z))).astype(BF16)

        wp1 = pieces(w[:, 0:1])
        wp2 = pieces(w[:, 1:2])
        hb = h_ref[0]
        rid = lax.broadcasted_iota(I32, (MOE_GC, tile), 0)
        for c in range(rows // MOE_GC):
            p1 = jnp.where(rid + c * MOE_GC == d1t, 1.0, 0.0).astype(BF16)
            p2 = jnp.where(rid + c * MOE_GC == d2t, 1.0, 0.0).astype(BF16)
            xg_ref[c * MOE_GC:(c + 1) * MOE_GC, :] = _dot(p1 + p2, hb).astype(BF16)
            wrow = jnp.sum(_dot(p1, wp1) + _dot(p2, wp2), axis=1, keepdims=True)
            wr_ref[c * MOE_GC:(c + 1) * MOE_GC, :] = jnp.broadcast_to(wrow, (MOE_GC, LANES))

    base = off_ref[i * N_EXPERTS + e] * MOE_RC
    n = nch_ref[i * N_EXPERTS + e]

    def ffn(r0, m):
        xg = xg_ref[pl.ds(r0, m), :]
        a = (_silu(_dot(xg, wg_ref[0, 0])) * _dot(xg, wu_ref[0, 0])).astype(BF16)
        y_ref[pl.ds(r0, m), :] += _dot(a, wd_ref[0])

    def pair(c, _):
        ffn(pl.multiple_of(base + c * (2 * MOE_RC), MOE_RC), 2 * MOE_RC)
        return 0

    lax.fori_loop(0, lax.shift_right_logical(n, 1), pair, 0)

    @pl.when((n & 1) == 1)
    def _():
        ffn(pl.multiple_of(base + (n - 1) * MOE_RC, MOE_RC), MOE_RC)

    @pl.when(jnp.logical_and(e == pl.num_programs(1) - 1, j == pl.num_programs(2) - 1))
    def _():
        dd = d_ref[0]
        d1c = dd[:, 0:1]
        d2c = dd[:, 1:2]
        cid = lax.broadcasted_iota(I32, (tile, MOE_SC), 1)
        z = None
        for k in range(rows // MOE_SC):
            pt = jnp.where((cid + k * MOE_SC == d1c) | (cid + k * MOE_SC == d2c), 1.0, 0.0).astype(BF16)
            yb = (y_ref[k * MOE_SC:(k + 1) * MOE_SC, :] * wr_ref[k * MOE_SC:(k + 1) * MOE_SC, 0:1]).astype(BF16)
            part = _dot(pt, yb)
            z = part if z is None else z + part
        o_ref[0] = z


def _experts(h, dest, wgt, dest_t, offc, nch, w_gate, w_up, w_down, tile, tf=512):
    t_tokens, d = h.shape
    nt = t_tokens // tile
    ff = w_gate.shape[2]
    assert ff % tf == 0 and tf % LANES == 0
    rows = _moe_rows(tile)
    blocked = lambda w: w.astype(BF16).reshape(N_EXPERTS, d, ff // tf, tf).transpose(0, 2, 1, 3)
    tok = lambda n: pl.BlockSpec((1, tile, n), lambda i, e, j, o, c: (i, 0, 0))
    grid_spec = pltpu.PrefetchScalarGridSpec(
        num_scalar_prefetch=2,
        grid=(nt, N_EXPERTS, ff // tf),
        in_specs=[tok(d), tok(LANES), tok(LANES),
                  pl.BlockSpec((1, 8, tile), lambda i, e, j, o, c: (i, 0, 0)),
                  pl.BlockSpec((1, 1, d, tf), lambda i, e, j, o, c: (e, j, 0, 0)),
                  pl.BlockSpec((1, 1, d, tf), lambda i, e, j, o, c: (e, j, 0, 0)),
                  pl.BlockSpec((1, tf, d), lambda i, e, j, o, c: (e, j, 0))],
        out_specs=tok(d),
        scratch_shapes=[pltpu.VMEM((rows, d), BF16), pltpu.VMEM((rows, d), F32),
                        pltpu.VMEM((rows, LANES), F32)])
    out = pl.pallas_call(
        _experts_kernel,
        grid_spec=grid_spec,
        out_shape=jax.ShapeDtypeStruct((nt, tile, d), F32),
        compiler_params=_cparams("arbitrary", "arbitrary", "arbitrary"),
        name="moe_ffn",
    )(offc, nch, h.reshape(nt, tile, d), dest, wgt, dest_t,
      blocked(w_gate), blocked(w_up), w_down.astype(BF16))
    return out.reshape(t_tokens, d)


def _post_kernel(x_ref, f_ref, gpost_ref, gate_ref, o_ref):
    o_ref[0] = x_ref[0] + gate_ref[0] * _rms(f_ref[0], gpost_ref[...])


def _post_residual(x, f, g_post, gate_f):
    bsz, s, d = x.shape
    tm = min(1024, s)
    act = pl.BlockSpec((1, tm, d), lambda b, i: (b, i, 0))
    return pl.pallas_call(
        _post_kernel,
        grid=(bsz, s // tm),
        in_specs=[act, act, pl.BlockSpec((1, d), lambda b, i: (0, 0)),
                  pl.BlockSpec((1, 1, d), lambda b, i: (b, 0, 0))],
        out_specs=act,
        out_shape=jax.ShapeDtypeStruct((bsz, s, d), F32),
        compiler_params=_cparams("arbitrary", "arbitrary"),
        name="post_residual",
    )(x, f, g_post.reshape(1, d), gate_f.reshape(bsz, 1, d))


def _mixer(x, mod, layer, rel_table, g_pre_mix, w_in, cmp_params):
    shift_m, scale_m = mod[:, 0], mod[:, 1]
    proj = _in_proj(x, g_pre_mix, scale_m, shift_m, _arrange_w_in(w_in))
    o_sb = _sb_attention(proj)
    o_nsa = _nsa_attention(proj, _compress(proj, *cmp_params), rel_table)
    return o_sb, o_nsa


def kernel(x, c, rel_table, w_ada, b_ada, g_pre_mix, g_post_mix, g_pre_ffn, g_post_ffn, w_in, w_out, g_sb, g_nsa, cmp_pos_k, cmp_w1_k, cmp_w2_k, cmp_pos_v, cmp_w1_v, cmp_w2_v, ffn_w_gate, ffn_w_up, ffn_w_down, moe_w_router, moe_b_router, moe_w_gate, moe_w_up, moe_w_down):
    bsz, s, d = x.shape
    depth = w_in.shape[0]
    mods = _ada(c, w_ada, b_ada).reshape(depth, bsz, 6, d)
    moe_tile = min(1024, s)
    for layer in range(depth):
        mod = mods[layer]
        cmp_params = (cmp_pos_k[layer], cmp_w1_k[layer], cmp_w2_k[layer],
                      cmp_pos_v[layer], cmp_w1_v[layer], cmp_w2_v[layer])
        o_sb, o_nsa = _mixer(x, mod, layer, rel_table, g_pre_mix[layer], w_in[layer], cmp_params)
        x, h = _out_proj(o_sb, o_nsa, x, g_sb[layer], g_nsa[layer], w_out[layer], g_post_mix[layer],
                         mod[:, 2], g_pre_ffn[layer], mod[:, 4], mod[:, 3])
        i = layer // 2
        if layer % 2 == 0:
            x = _dense_ffn(h, x, ffn_w_gate[i], ffn_w_up[i], ffn_w_down[i], g_post_ffn[layer], mod[:, 5])
        else:
            routing = _route(x, g_pre_ffn[layer], mod[:, 4], mod[:, 3],
                             moe_w_router[i], moe_b_router[i], moe_tile)
            f = _experts(h.reshape(bsz * s, d), *routing,
                         moe_w_gate[i], moe_w_up[i], moe_w_down[i], moe_tile)
            x = _post_residual(x, f.reshape(bsz, s, d), g_post_ffn[layer], mod[:, 5])
    return x
```

```python
import functools
import math

import numpy as np
import jax
import jax.numpy as jnp
from jax import lax
from jax.experimental import pallas as pl
from jax.experimental.pallas import tpu as pltpu

F32 = jnp.float32
BF16 = jnp.bfloat16
I32 = jnp.int32

LANES = 128
DH = 64
N_SB = 8
N_NSA = 8
G_NSA = 2
Z_NSA = 4
CMP_BLOCK = 32
CMP_STRIDE = 16
SLC_BLOCK = 64
N_SELECT = 16
WINDOW = 512
REL_BUCKETS = 32
REL_MAX_DIST = 128
N_EXPERTS = 8
EPS = 1e-6
FORCED = 1e4
NEG = -1e30
M_INIT = -1e29
SB_EXIT = -104.5
VMEM_LIMIT = 56 * 1024 * 1024

TQ = 256
NEAR_BACK = REL_MAX_DIST
KC_FRONT = 16
SEL_FAR_TILE = 512


def _cparams(*sem):
    return pltpu.CompilerParams(dimension_semantics=sem, vmem_limit_bytes=VMEM_LIMIT)


def _nt(a, b):
    return lax.dot_general(a, b, (((1,), (1,)), ((), ())), preferred_element_type=F32)


def _dot(a, b):
    return jnp.dot(a, b, preferred_element_type=F32)


def _split3(a):
    hi = a.astype(BF16)
    r = a - hi.astype(F32)
    mid = r.astype(BF16)
    lo = (r - mid.astype(F32)).astype(BF16)
    return hi, mid, lo


def _dot_hl(a, b):
    hi = a.astype(BF16)
    lo = (a - hi.astype(F32)).astype(BF16)
    return _dot(hi, b) + _dot(lo, b)


def _dot_f32(a, b):
    ah, am, al = _split3(a)
    bh, bm, bl = _split3(b)
    return (_dot(ah, bh) + (_dot(ah, bm) + _dot(am, bh))
            + (_dot(ah, bl) + _dot(am, bm) + _dot(al, bh)))


def _rms(x, g):
    return x * lax.rsqrt(jnp.mean(x * x, axis=-1, keepdims=True) + EPS) * g


def _ada_kernel(c_ref, w_ref, b_ref, o_ref):
    c = c_ref[...]
    ca = c * (1.0 / (1.0 + jnp.exp(-c)))
    o_ref[0] = _dot_f32(ca, w_ref[0]) + b_ref[0]


def _ada(c, w_ada, b_ada):
    depth, d, n = w_ada.shape
    bsz = c.shape[0]
    rows = 8
    tn = 1536
    cp = jnp.zeros((rows, d), F32).at[:bsz].set(c)
    out = pl.pallas_call(
        _ada_kernel,
        grid=(depth, n // tn),
        in_specs=[pl.BlockSpec((rows, d), lambda l, j: (0, 0)),
                  pl.BlockSpec((1, d, tn), lambda l, j: (l, 0, j)),
                  pl.BlockSpec((1, 1, tn), lambda l, j: (l, 0, j))],
        out_specs=pl.BlockSpec((1, rows, tn), lambda l, j: (l, 0, j)),
        out_shape=jax.ShapeDtypeStruct((depth, rows, n), F32),
        compiler_params=_cparams("arbitrary", "arbitrary"),
        name="ada_mod",
    )(cp, w_ada, b_ada.reshape(depth, 1, n))
    return out[:, :bsz]


def _in_kernel(x_ref, g_ref, sc_ref, sh_ref, w_ref, o_ref, *, cn):
    h = _rms(x_ref[0], g_ref[...]) * (1.0 + sc_ref[0]) + sh_ref[0]
    hb = h.astype(BF16)
    for j in range(w_ref.shape[1] // cn):
        o_ref[0, :, j * cn:(j + 1) * cn] = _dot(hb, w_ref[:, j * cn:(j + 1) * cn]).astype(BF16)


def _in_proj(x, g, scale, shift, w):
    bsz, s, d = x.shape
    nc = w.shape[1]
    tm = min(512, s)
    cn = nc // 3 if (nc // 3) % LANES == 0 else nc
    return pl.pallas_call(
        functools.partial(_in_kernel, cn=cn),
        grid=(bsz, s // tm),
        in_specs=[pl.BlockSpec((1, tm, d), lambda b, i: (b, i, 0)),
                  pl.BlockSpec((1, d), lambda b, i: (0, 0)),
                  pl.BlockSpec((1, 1, d), lambda b, i: (b, 0, 0)),
                  pl.BlockSpec((1, 1, d), lambda b, i: (b, 0, 0)),
                  pl.BlockSpec((d, nc), lambda b, i: (0, 0))],
        out_specs=pl.BlockSpec((1, tm, nc), lambda b, i: (b, i, 0)),
        out_shape=jax.ShapeDtypeStruct((bsz, s, nc), BF16),
        compiler_params=_cparams("arbitrary", "arbitrary"),
        name="in_proj",
    )(x, g.reshape(1, d), scale.reshape(bsz, 1, d), shift.reshape(bsz, 1, d), w)


CB_SBQ, CB_SBK, CB_SBV, CB_NSAQ = 0, 4, 8, 12
CB_KCMP, CB_VCMP = 16, 17
CB_KSLC, CB_VSLC, CB_KWIN, CB_VWIN = 18, 20, 22, 24
CB_GATE = 26
N_CB = 27


def _arrange_w_in(w_in):
    d = w_in.shape[0]
    w_sb = N_SB * DH
    off_nsa_q = 3 * w_sb
    off_kv = off_nsa_q + N_NSA * DH
    off_gate = off_kv + 3 * 2 * G_NSA * DH
    scale = DH ** -0.5

    def kv(br, kvi, g):
        lo = off_kv + ((br * 2 + kvi) * G_NSA + g) * DH
        return w_in[:, lo:lo + DH]

    cols = [w_in[:, 0:w_sb] * scale, w_in[:, w_sb:2 * w_sb], w_in[:, 2 * w_sb:3 * w_sb],
            w_in[:, off_nsa_q:off_kv] * scale,
            kv(0, 0, 0), kv(0, 0, 1), kv(0, 1, 0), kv(0, 1, 1)]
    for br in (1, 2):
        for kvi in (0, 1):
            for g in range(G_NSA):
                cols += [kv(br, kvi, g), kv(br, kvi, g)]
    n_gate = 3 * N_NSA
    cols += [w_in[:, off_gate:off_gate + n_gate], jnp.zeros((d, LANES - n_gate), w_in.dtype)]
    out = jnp.concatenate(cols, axis=1).astype(BF16)
    assert out.shape[1] == N_CB * LANES
    return out


def _sb_kernel(q_ref, k_ref, v_ref, o_ref, *, t):
    qi = pl.program_id(2)
    q = q_ref[0]
    lane = lax.broadcasted_iota(I32, (t, LANES), 1)
    row = lax.broadcasted_iota(I32, (t, t), 0)
    col = lax.broadcasted_iota(I32, (t, t), 1)
    upper = jnp.where(row > col, 1.0, 0.0).astype(BF16)
    ones = jnp.ones((t, LANES), BF16)
    causal = col < row
    rep = t // LANES

    def tile(qh, k, v, carry, acc, diag):
        z = _nt(qh, k)
        lk = -(jnp.maximum(z, 0.0) + jnp.log(1.0 + jnp.exp(-jnp.abs(z))))
        if diag:
            lk = jnp.where(causal, lk, 0.0)
        hi = lk.astype(BF16)
        lo = (lk - hi.astype(F32)).astype(BF16)
        later = _dot(hi, upper) + _dot(lo, upper)
        tot = _dot(hi, ones) + _dot(lo, ones)
        cb = carry if rep == 1 else jnp.concatenate([carry] * rep, axis=1)
        w = jnp.exp(z + lk + later + cb)
        if diag:
            w = jnp.where(causal, w, 0.0)
        return carry + tot, acc + _dot(w.astype(BF16), v)

    zq = jnp.zeros_like(q)
    qhs = (jnp.where(lane < DH, q, zq), jnp.where(lane < DH, zq, q))
    zero = jnp.zeros((t, LANES), F32)

    def both(kt, st, diag):
        k = k_ref[0, pl.ds(kt * t, t), :]
        v = v_ref[0, pl.ds(kt * t, t), :]
        c0, a0 = tile(qhs[0], k, v, st[0], st[1], diag)
        c1, a1 = tile(qhs[1], k, v, st[2], st[3], diag)
        return jnp.maximum(jnp.max(c0), jnp.max(c1)), (c0, a0, c1, a1)

    mx, st = both(qi, (zero, zero, zero, zero), True)

    def cond(s):
        return jnp.logical_and(s[0] >= 0, s[1] > SB_EXIT)

    def body(s):
        mx, st = both(s[0], s[2], False)
        return s[0] - 1, mx, st

    _, _, st = lax.while_loop(cond, body, (qi - 1, mx, st))
    o_ref[0] = jnp.where(lane < DH, st[1], st[3])


def _sb_attention(proj, t=256):
    bsz, s, _ = proj.shape
    t = min(t, s)
    npair = N_SB // 2
    return pl.pallas_call(
        functools.partial(_sb_kernel, t=t),
        grid=(bsz, npair, s // t),
        in_specs=[pl.BlockSpec((1, t, LANES), lambda b, j, i: (b, i, CB_SBQ + j)),
                  pl.BlockSpec((1, s, LANES), lambda b, j, i: (b, 0, CB_SBK + j)),
                  pl.BlockSpec((1, s, LANES), lambda b, j, i: (b, 0, CB_SBV + j))],
        out_specs=pl.BlockSpec((1, t, LANES), lambda b, j, i: (b, i, j)),
        out_shape=jax.ShapeDtypeStruct((bsz, s, N_SB * DH), F32),
        compiler_params=_cparams("arbitrary", "arbitrary", "arbitrary"),
        name="sb_attn",
    )(proj, proj, proj)


def _gelu_tanh(x):
    return 0.5 * x * (1.0 + jnp.tanh(math.sqrt(2.0 / math.pi) * (x + 0.044715 * (x * x * x))))


def _cmp_kernel(xa_ref, xb_ref, pos_ref, w1_ref, w2_ref, o_ref):
    half = w1_ref.shape[1] // 2
    w1a = w1_ref[0, :half, :]
    w1b = w1_ref[0, half:, :]
    pos = pos_ref[0]
    bias = _dot(pos[:, :half], w1a) + _dot(pos[:, half:], w1b)
    hid = _dot(xa_ref[0, 0], w1a) + _dot(xb_ref[0, 0], w1b) + bias[0:1, :]
    o_ref[0, 0] = _dot(_gelu_tanh(hid).astype(BF16), w2_ref[0])


def _compress(proj, pos_k, w1_k, w2_k, pos_v, w1_v, w2_v):
    bsz, s, _ = proj.shape
    n16 = s // CMP_STRIDE
    raw = proj[:, :, CB_KCMP * LANES:(CB_VCMP + 1) * LANES]
    x16 = raw.reshape(bsz, s, 4, DH).transpose(0, 2, 1, 3).reshape(bsz, 4, n16, CMP_STRIDE * DH)
    x16b = jnp.concatenate([x16[:, :, 1:], jnp.zeros_like(x16[:, :, :1])], axis=2)
    hidden = w1_k.shape[1]
    w1 = jnp.stack([w1_k, w1_v]).astype(BF16)
    w2 = jnp.stack([jnp.concatenate([w2_k, w2_k], 1), jnp.concatenate([w2_v, w2_v], 1)]).astype(BF16)
    pos = jnp.stack([pos_k.reshape(1, -1), pos_v.reshape(1, -1)])
    pos = jnp.concatenate([pos, jnp.zeros((2, 7, pos.shape[2]), pos.dtype)], axis=1).astype(BF16)
    k16 = CMP_STRIDE * DH
    return pl.pallas_call(
        _cmp_kernel,
        grid=(bsz, 4),
        in_specs=[pl.BlockSpec((1, 1, n16, k16), lambda b, j: (b, j, 0, 0)),
                  pl.BlockSpec((1, 1, n16, k16), lambda b, j: (b, j, 0, 0)),
                  pl.BlockSpec((1, 8, 2 * k16), lambda b, j: (j // 2, 0, 0)),
                  pl.BlockSpec((1, 2 * k16, hidden), lambda b, j: (j // 2, 0, 0)),
                  pl.BlockSpec((1, hidden, LANES), lambda b, j: (j // 2, 0, 0))],
        out_specs=pl.BlockSpec((1, 1, n16, LANES), lambda b, j: (b, j, 0, 0)),
        out_shape=jax.ShapeDtypeStruct((bsz, 4, n16, LANES), F32),
        compiler_params=_cparams("arbitrary", "arbitrary"),
        name="nsa_compress",
    )(x16, x16b, pos, w1, w2)


def _bucket_table():
    n = np.arange(REL_MAX_DIST + 1)
    exact = REL_BUCKETS // 2
    val = (np.log(np.maximum(n, 1).astype(np.float32) / np.float32(exact)).astype(np.float32)
           / np.float32(math.log(REL_MAX_DIST / exact)) * np.float32(REL_BUCKETS - exact))
    large = np.minimum(exact + val.astype(np.int32), REL_BUCKETS - 1)
    return np.where(n < exact, n, large).astype(np.int32)


def _bias_tiles(rel_table):
    bucket = _bucket_table()
    far_row = rel_table[int(bucket[REL_MAX_DIST])].astype(F32)
    r = np.arange(TQ)[:, None]

    def tile(dist, valid, base):
        onehot = jax.nn.one_hot(bucket[np.clip(dist, 0, REL_MAX_DIST)], REL_BUCKETS, dtype=F32)
        t = jnp.einsum('rwb,bh->hrw', onehot, rel_table.astype(F32),
                       precision=lax.Precision.HIGHEST)
        t = jnp.where(valid[None], t - base, NEG)
        return t.reshape(G_NSA, Z_NSA * TQ, dist.shape[1])

    w = np.arange(LANES)[None, :]
    d_c = r - CMP_STRIDE * (w - KC_FRONT) - (CMP_BLOCK - 1)
    w2 = np.arange(NEAR_BACK + TQ)[None, :]
    d_s = r + NEAR_BACK - w2
    w5 = np.arange(WINDOW + TQ)[None, :]
    d_w = r + WINDOW - w5
    far = jnp.broadcast_to(far_row[:, None, None], (N_NSA, TQ, LANES))
    far = far.reshape(G_NSA, Z_NSA * TQ, LANES)
    return (tile(d_c, d_c >= 0, far_row[:, None, None]), tile(d_s, d_s >= 0, 0.0),
            tile(d_w, (d_w >= 0) & (d_w < WINDOW), 0.0), far)


def _gate_expand():
    e = np.zeros((G_NSA, LANES, 3 * Z_NSA * DH), np.float32)
    for g in range(G_NSA):
        for z in range(Z_NSA):
            for br in range(3):
                e[g, (g * Z_NSA + z) * 3 + br, br * Z_NSA * DH + z * DH: br * Z_NSA * DH + (z + 1) * DH] = 1.0
    return e


def _overlap_padded(n16, kcp, n_slc):
    i = np.arange(kcp)[:, None] - KC_FRONT
    j = np.arange(LANES)[None, :]
    n_cmp = n16 - 1
    ok = (i >= 0) & (i < n_cmp) & (j < n_slc)
    ov = (i * CMP_STRIDE < j * SLC_BLOCK + SLC_BLOCK) & (i * CMP_STRIDE + CMP_BLOCK > j * SLC_BLOCK)
    return (ok & ov).astype(np.float32)


def _block_onehot(s):
    return (np.arange(s)[:, None] // SLC_BLOCK == np.arange(LANES)[None, :]).astype(np.float32)


def _nsa_kernel(q_ref, kcb_ref, vcb_ref, kcf_ref, vcf_ref, ks_ref, vs_ref, kw_ref, vw_ref, et_ref,
                gate_ref, ovb_ref, ovf_ref, bcn_ref, bsn_ref, bw_ref, cf_ref, eg_ref, o_ref,
                sb0_ref, sb1_ref, *, n_slc, n_round):
    blk = pl.program_id(2)
    c0 = blk * TQ
    r4 = Z_NSA * TQ
    q = q_ref[0]
    lane = lax.broadcasted_iota(I32, (TQ, LANES), 1)
    lo_half = lane < DH
    zq = jnp.zeros((TQ, LANES), BF16)
    qa, qb = q[:, :LANES], q[:, LANES:]
    qs = jnp.concatenate([jnp.where(lo_half, qa, zq), jnp.where(lo_half, zq, qa),
                          jnp.where(lo_half, qb, zq), jnp.where(lo_half, zq, qb)], axis=0)
    ones_k = jnp.ones((SEL_FAR_TILE, LANES), BF16)
    ones_w = jnp.ones((WINDOW + TQ, LANES), BF16)

    kcp = kcb_ref.shape[2]
    prow = lax.broadcasted_iota(I32, (1, kcp), 1)
    near0 = pl.multiple_of(blk * (TQ // CMP_STRIDE), 8)
    far_row = jnp.where((prow >= KC_FRONT) & (prow < near0), 0.0, NEG)
    lane1 = lax.broadcasted_iota(I32, (1, LANES), 1)
    near_row = jnp.where(lane1 + near0 >= KC_FRONT, 0.0, NEG)
    kcn = kcf_ref[0, 0, pl.ds(near0, LANES), :].astype(BF16)
    vcn = vcf_ref[0, 0, pl.ds(near0, LANES), :].astype(BF16)
    s_far = _nt(qs, kcb_ref[0, 0]) + far_row
    s_near = _nt(qs, kcn) + bcn_ref[0] + near_row
    m_c = jnp.maximum(jnp.max(s_far, axis=1, keepdims=True), jnp.max(s_near, axis=1, keepdims=True))
    m_c = jnp.maximum(m_c, M_INIT)
    p_far = jnp.exp(s_far - m_c)
    p_near = jnp.exp(s_near - m_c)
    den = jnp.sum(p_far, axis=1, keepdims=True) + jnp.sum(p_near, axis=1, keepdims=True)
    inv_c = 1.0 / jnp.maximum(den, 1e-30)
    o_c = (_dot(p_far.astype(BF16), vcb_ref[0, 0]) + _dot(p_near.astype(BF16), vcn)) * inv_c

    pn_far = p_far * inv_c
    pn_near = p_near * inv_c
    pz_far = pn_far[0:TQ] + pn_far[TQ:2 * TQ] + pn_far[2 * TQ:3 * TQ] + pn_far[3 * TQ:]
    pz_near = pn_near[0:TQ] + pn_near[TQ:2 * TQ] + pn_near[2 * TQ:3 * TQ] + pn_near[3 * TQ:]
    ovn = ovf_ref[pl.ds(near0, LANES), :].astype(BF16)
    imp = _dot_hl(pz_far, ovb_ref[...]) + _dot_hl(pz_near, ovn)
    tpos = c0 + lax.broadcasted_iota(I32, (TQ, LANES), 0)
    cur = lax.shift_right_logical(tpos, 6)
    valid = (lane * SLC_BLOCK <= tpos) & (lane < n_slc)
    forced = valid & ((lane == 0) | (lane == cur) | (lane == cur - 1))
    score = jnp.where(valid & jnp.logical_not(forced), imp, -1.0)
    score = jnp.where(lane < n_slc, score, -2.0)

    rest, taken, thr = score, jnp.zeros((TQ, 1), F32), jnp.zeros((TQ, 1), F32)
    for _ in range(n_round):
        m = jnp.max(rest, axis=1, keepdims=True)
        eq = rest == m
        thr = jnp.where(taken < n_round, m, thr)
        taken = taken + jnp.sum(jnp.where(eq, 1.0, 0.0), axis=1, keepdims=True)
        rest = jnp.where(eq, -jnp.inf, rest)
    above = score > thr
    at_thr = score == thr
    need = n_round - jnp.sum(jnp.where(above, 1.0, 0.0), axis=1, keepdims=True)
    brow = lax.broadcasted_iota(I32, (LANES, LANES), 0)
    bcol = lax.broadcasted_iota(I32, (LANES, LANES), 1)
    before = _dot(jnp.where(at_thr, 1.0, 0.0).astype(BF16),
                  jnp.where(brow < bcol, 1.0, 0.0).astype(BF16))
    sel = ((above | (at_thr & (before < need))) & valid) | forced

    kws, vws = [], []
    for w in range((WINDOW + TQ) // LANES):
        st = pl.multiple_of(jnp.maximum(c0 - WINDOW + w * LANES, 0), LANES)
        kws.append(kw_ref[0, pl.ds(st, LANES), :])
        vws.append(vw_ref[0, pl.ds(st, LANES), :])
    wl = lax.broadcasted_iota(I32, (1, WINDOW + TQ), 1)
    pos_row = jnp.where(c0 - WINDOW + wl >= 0, 0.0, NEG)
    s = _nt(qs, jnp.concatenate(kws, axis=0)) + bw_ref[0] + pos_row
    p = jnp.exp(s - jnp.max(s, axis=1, keepdims=True)).astype(BF16)
    r_w = _dot(p, jnp.concatenate([jnp.concatenate(vws, axis=0), ones_w], axis=1))
    o_w = r_w[:, :LANES] / r_w[:, LANES:]
    gl = _dot(gate_ref[0], eg_ref[0])
    sig = 1.0 / (1.0 + jnp.exp(-gl))
    wide = Z_NSA * DH

    sel_all = jnp.where(sel, 0.0, NEG).astype(BF16)
    far_blocks = blk * (TQ // SLC_BLOCK) - NEAR_BACK // SLC_BLOCK
    sel_far = jnp.where(sel & (lane < far_blocks), 0.0, NEG).astype(BF16)
    q_far = jnp.concatenate([qs, jnp.concatenate([sel_far] * Z_NSA, axis=0)], axis=1)
    q_near = jnp.concatenate([qs, jnp.concatenate([sel_all] * Z_NSA, axis=0)], axis=1)
    tk = SEL_FAR_TILE
    last_tile = ks_ref.shape[1] // tk - 1

    def qk(kt):
        k0 = pl.multiple_of(kt * tk, tk)
        return _nt(q_far, jnp.concatenate([ks_ref[0, pl.ds(k0, tk), :], et_ref[pl.ds(k0, tk), :]], axis=1))

    def fold(s, v, st):
        m, acc = st
        mn = jnp.maximum(m, jnp.max(s, axis=1, keepdims=True))
        p = jnp.exp(s - mn).astype(BF16)
        return mn, jnp.exp(m - mn) * acc + _dot(p, v)

    def v_tile(kt):
        k0 = pl.multiple_of(kt * tk, tk)
        return jnp.concatenate([vs_ref[0, pl.ds(k0, tk), :], ones_k], axis=1)

    n_pair = lax.shift_right_logical(jnp.maximum(c0 - NEAR_BACK, 0) + (2 * tk - 1), int(math.log2(2 * tk)))
    sb0_ref[...] = qk(0)

    def far_step(i, st):
        sb1_ref[...] = qk(2 * i + 1)
        st = fold(sb0_ref[...], v_tile(2 * i), st)
        sb0_ref[...] = qk(jnp.minimum(2 * i + 2, last_tile))
        return fold(sb1_ref[...], v_tile(2 * i + 1), st)

    m0 = jnp.full((r4, 1), M_INIT, F32)
    m_s, acc_s = lax.fori_loop(0, n_pair, far_step, (m0, jnp.zeros((r4, 2 * LANES), F32)))
    m_s = m_s + cf_ref[0][:, 0:1]
    n_near = NEAR_BACK + TQ
    p0 = pl.multiple_of(jnp.maximum(c0 - NEAR_BACK, 0), NEAR_BACK)
    d0 = pl.multiple_of(c0, TQ)
    kn = jnp.concatenate([ks_ref[0, pl.ds(p0, NEAR_BACK), :], ks_ref[0, pl.ds(d0, TQ), :]], axis=0)
    vn = jnp.concatenate([vs_ref[0, pl.ds(p0, NEAR_BACK), :], vs_ref[0, pl.ds(d0, TQ), :]], axis=0)
    nrow = lax.broadcasted_iota(I32, (n_near, LANES), 0)
    nlane = lax.broadcasted_iota(I32, (n_near, LANES), 1)
    e_n = jnp.where(lax.shift_right_logical(nrow, 6) + far_blocks == nlane, 1.0, 0.0).astype(BF16)
    lane2 = lax.broadcasted_iota(I32, (1, n_near), 1)
    prev_row = jnp.where((lane2 < NEAR_BACK) & (blk == 0), NEG, 0.0)
    s = _nt(q_near, jnp.concatenate([kn, e_n], axis=1)) + bsn_ref[0] + prev_row
    _, acc_s = fold(s, jnp.concatenate([vn, ones_k[:n_near]], axis=1), (m_s, acc_s))
    o_s = acc_s[:, :LANES] / acc_s[:, LANES:]

    def heads(o):
        return jnp.concatenate([jnp.where(lo_half, o[0:TQ], o[TQ:2 * TQ]),
                                jnp.where(lo_half, o[2 * TQ:3 * TQ], o[3 * TQ:])], axis=1)

    o_ref[0] = (sig[:, 0:wide] * heads(o_c) + sig[:, wide:2 * wide] * heads(o_s)
                + sig[:, 2 * wide:] * heads(o_w))


def _nsa_attention(proj, cmp_out, rel_table):
    bsz, s, _ = proj.shape
    assert s % (2 * SEL_FAR_TILE) == 0 and s >= WINDOW + TQ
    n16 = s // CMP_STRIDE
    n_slc = s // SLC_BLOCK
    assert N_SELECT <= n_slc <= LANES
    nq = s // TQ
    kcp = -(-(max(n16 + KC_FRONT, (TQ // CMP_STRIDE) * (nq - 1) + LANES)) // LANES) * LANES
    pad = ((0, 0), (0, 0), (KC_FRONT, kcp - n16 - KC_FRONT), (0, 0))
    cf = jnp.pad(cmp_out, pad)
    cb = cf.astype(BF16)
    ov = _overlap_padded(n16, kcp, n_slc)
    bcn, bsn, bw, far = _bias_tiles(rel_table)
    eg = jnp.asarray(_gate_expand(), BF16)
    r4 = Z_NSA * TQ
    kv_spec = lambda cb0: pl.BlockSpec((1, s, LANES), lambda b, g, i, cb0=cb0: (b, 0, cb0 + g))
    cmp_spec = lambda j0: pl.BlockSpec((1, 1, kcp, LANES), lambda b, g, i, j0=j0: (b, j0 + g, 0, 0))
    tile_spec = lambda w: pl.BlockSpec((1, r4, w), lambda b, g, i: (g, 0, 0))
    const2 = lambda n: pl.BlockSpec((n, LANES), lambda b, g, i: (0, 0))
    return pl.pallas_call(
        functools.partial(_nsa_kernel, n_slc=n_slc, n_round=N_SELECT - 3),
        grid=(bsz, G_NSA, nq),
        in_specs=[pl.BlockSpec((1, TQ, 2 * LANES), lambda b, g, i: (b, i, CB_NSAQ // 2 + g)),
                  cmp_spec(0), cmp_spec(2), cmp_spec(0), cmp_spec(2),
                  kv_spec(CB_KSLC), kv_spec(CB_VSLC), kv_spec(CB_KWIN), kv_spec(CB_VWIN),
                  const2(s),
                  pl.BlockSpec((1, TQ, LANES), lambda b, g, i: (b, i, CB_GATE)),
                  const2(kcp), const2(kcp),
                  tile_spec(LANES), tile_spec(NEAR_BACK + TQ), tile_spec(WINDOW + TQ), tile_spec(LANES),
                  pl.BlockSpec((1, LANES, 3 * Z_NSA * DH), lambda b, g, i: (g, 0, 0))],
        out_specs=pl.BlockSpec((1, TQ, Z_NSA * DH), lambda b, g, i: (b, i, g)),
        out_shape=jax.ShapeDtypeStruct((bsz, s, N_NSA * DH), F32),
        scratch_shapes=[pltpu.VMEM((r4, SEL_FAR_TILE), F32), pltpu.VMEM((r4, SEL_FAR_TILE), F32)],
        compiler_params=_cparams("arbitrary", "arbitrary", "arbitrary"),
        name="nsa_attn",
    )(proj, cb, cb, cf, cf, proj, proj, proj, proj, jnp.asarray(_block_onehot(s), BF16), proj,
      jnp.asarray(ov, BF16), jnp.asarray(ov, F32), bcn, bsn, bw, far, eg)


def _out_kernel(osb_ref, onsa_ref, x_ref, gsb_ref, gnsa_ref, w_ref, gpost_ref, gate_ref,
                gpre_ref, sc_ref, sh_ref, xo_ref, h_ref):
    half = osb_ref.shape[2]
    a = _rms(osb_ref[0], gsb_ref[...]).astype(BF16)
    b = _rms(onsa_ref[0], gnsa_ref[...]).astype(BF16)
    m = _dot(a, w_ref[:half, :]) + _dot(b, w_ref[half:, :])
    x = x_ref[0] + gate_ref[0] * _rms(m, gpost_ref[...])
    xo_ref[0] = x
    h_ref[0] = (_rms(x, gpre_ref[...]) * (1.0 + sc_ref[0]) + sh_ref[0]).astype(BF16)


def _out_proj(o_sb, o_nsa, x, g_sb, g_nsa, w_out, g_post, gate_m, g_pre_ffn, scale_f, shift_f):
    bsz, s, d = x.shape
    half = o_sb.shape[2]
    tm = min(512, s)
    row = lambda n: pl.BlockSpec((1, n), lambda b, i: (0, 0))
    mod = pl.BlockSpec((1, 1, d), lambda b, i: (b, 0, 0))
    act = lambda n: pl.BlockSpec((1, tm, n), lambda b, i: (b, i, 0))
    return pl.pallas_call(
        _out_kernel,
        grid=(bsz, s // tm),
        in_specs=[act(half), act(half), act(d), row(half), row(half),
                  pl.BlockSpec((2 * half, d), lambda b, i: (0, 0)), row(d), mod, row(d), mod, mod],
        out_specs=[act(d), act(d)],
        out_shape=[jax.ShapeDtypeStruct((bsz, s, d), F32), jax.ShapeDtypeStruct((bsz, s, d), BF16)],
        compiler_params=_cparams("arbitrary", "arbitrary"),
        name="out_proj",
    )(o_sb, o_nsa, x, g_sb.reshape(1, half), g_nsa.reshape(1, half), w_out.astype(BF16),
      g_post.reshape(1, d), gate_m.reshape(bsz, 1, d), g_pre_ffn.reshape(1, d),
      scale_f.reshape(bsz, 1, d), shift_f.reshape(bsz, 1, d))


def _silu(x):
    return x * (1.0 / (1.0 + jnp.exp(-x)))


def _ffn_kernel(h_ref, x_ref, wg_ref, wu_ref, wd_ref, gpost_ref, gate_ref, o_ref, acc_ref):
    j = pl.program_id(2)
    h = h_ref[0]
    a = (_silu(_dot(h, wg_ref[...])) * _dot(h, wu_ref[...])).astype(BF16)
    part = _dot(a, wd_ref[...])

    @pl.when(j == 0)
    def _():
        acc_ref[...] = part

    @pl.when(j > 0)
    def _():
        acc_ref[...] += part

    @pl.when(j == pl.num_programs(2) - 1)
    def _():
        o_ref[0] = x_ref[0] + gate_ref[0] * _rms(acc_ref[...], gpost_ref[...])


def _dense_ffn(h, x, w_gate, w_up, w_down, g_post, gate_f):
    bsz, s, d = x.shape
    ff = w_gate.shape[1]
    tm = min(1024, s)
    tf = 256
    assert ff % tf == 0
    act = pl.BlockSpec((1, tm, d), lambda b, i, j: (b, i, 0))
    return pl.pallas_call(
        _ffn_kernel,
        grid=(bsz, s // tm, ff // tf),
        in_specs=[act, act,
                  pl.BlockSpec((d, tf), lambda b, i, j: (0, j)),
                  pl.BlockSpec((d, tf), lambda b, i, j: (0, j)),
                  pl.BlockSpec((tf, d), lambda b, i, j: (j, 0)),
                  pl.BlockSpec((1, d), lambda b, i, j: (0, 0)),
                  pl.BlockSpec((1, 1, d), lambda b, i, j: (b, 0, 0))],
        out_specs=act,
        out_shape=jax.ShapeDtypeStruct((bsz, s, d), F32),
        scratch_shapes=[pltpu.VMEM((tm, d), F32)],
        compiler_params=_cparams("arbitrary", "arbitrary", "arbitrary"),
        name="dense_ffn",
    )(h, x, w_gate.astype(BF16), w_up.astype(BF16), w_down.astype(BF16),
      g_post.reshape(1, d), gate_f.reshape(bsz, 1, d))


def _router_kernel(x_ref, gpre_ref, sc_ref, sh_ref, wr_ref, br_ref, pos_ref, w_ref, post_ref, cnt_ref):
    tile = x_ref.shape[1]
    h = _rms(x_ref[0], gpre_ref[...]) * (1.0 + sc_ref[0]) + sh_ref[0]
    logits = _dot_f32(h, wr_ref[...]) + br_ref[...]
    lane = lax.broadcasted_iota(I32, (tile, LANES), 1)
    lanef = lane.astype(F32)
    e = jnp.exp(logits - jnp.max(logits, axis=1, keepdims=True))
    probs = e / jnp.sum(e, axis=1, keepdims=True)
    probs = jnp.where(lane < N_EXPERTS, probs, -1.0)
    m1 = jnp.max(probs, axis=1, keepdims=True)
    i1 = jnp.min(jnp.where(probs == m1, lanef, 1e9), axis=1, keepdims=True)
    rest = jnp.where(lanef == i1, -1.0, probs)
    m2 = jnp.max(rest, axis=1, keepdims=True)
    i2 = jnp.min(jnp.where(rest == m2, lanef, 1e9), axis=1, keepdims=True)
    tot = m1 + m2
    wgt = jnp.where(lanef == i1, m1 / tot, jnp.where(lanef == i2, m2 / tot, 0.0))
    mask = (lanef == i1) | (lanef == i2)
    ch = 256
    r = lax.broadcasted_iota(I32, (ch, ch), 0)
    c = lax.broadcasted_iota(I32, (ch, ch), 1)
    lower = jnp.where(c < r, 1.0, 0.0).astype(BF16)
    ones = jnp.ones((8, ch), BF16)
    carry = jnp.zeros((1, LANES), F32)
    ranks = []
    for k in range(tile // ch):
        mk = jnp.where(mask[k * ch:(k + 1) * ch], 1.0, 0.0).astype(BF16)
        ranks.append(_dot(lower, mk) + carry)
        carry = carry + _dot(ones, mk)[0:1]
    rank = jnp.concatenate(ranks, axis=0)
    pos = jnp.where(mask, rank, -1.0)
    pos_ref[0] = pos.astype(I32)
    w_ref[0] = wgt
    post_ref[0] = jnp.transpose(pos)[0:N_EXPERTS].astype(I32)
    cnt_ref[0] = jnp.broadcast_to(carry, (8, LANES)).astype(I32)


def _router(x, g_pre, scale_f, shift_f, w_router, b_router, tile):
    bsz, s, d = x.shape
    nt = s // tile
    wr = jnp.zeros((d, LANES), F32).at[:, :N_EXPERTS].set(w_router.astype(F32))
    br = jnp.full((1, LANES), NEG, F32).at[0, :N_EXPERTS].set(b_router.astype(F32))
    mod = pl.BlockSpec((1, 1, d), lambda b, i: (b, 0, 0))
    tok = lambda n: pl.BlockSpec((1, tile, n), lambda b, i: (b * nt + i, 0, 0))
    pos, wgt, post, cnt = pl.pallas_call(
        _router_kernel,
        grid=(bsz, nt),
        in_specs=[pl.BlockSpec((1, tile, d), lambda b, i: (b, i, 0)),
                  pl.BlockSpec((1, d), lambda b, i: (0, 0)), mod, mod,
                  pl.BlockSpec((d, LANES), lambda b, i: (0, 0)),
                  pl.BlockSpec((1, LANES), lambda b, i: (0, 0))],
        out_specs=[tok(LANES), tok(LANES),
                   pl.BlockSpec((1, N_EXPERTS, tile), lambda b, i: (b * nt + i, 0, 0)),
                   pl.BlockSpec((1, 8, LANES), lambda b, i: (b * nt + i, 0, 0))],
        out_shape=[jax.ShapeDtypeStruct((bsz * nt, tile, LANES), I32),
                   jax.ShapeDtypeStruct((bsz * nt, tile, LANES), F32),
                   jax.ShapeDtypeStruct((bsz * nt, N_EXPERTS, tile), I32),
                   jax.ShapeDtypeStruct((bsz * nt, 8, LANES), I32)],
        compiler_params=_cparams("arbitrary", "arbitrary"),
        name="moe_router",
    )(x, g_pre.reshape(1, d), scale_f.reshape(bsz, 1, d), shift_f.reshape(bsz, 1, d), wr, br)
    return pos, wgt, post, cnt[:, 0, :N_EXPERTS]


def _moe_kernel(cnt_ref, h_ref, pos_ref, w_ref, post_ref, wg_ref, wu_ref, wd_ref, o_ref,
                xg_ref, y_ref, *, rc):
    i, e, j = pl.program_id(0), pl.program_id(1), pl.program_id(2)
    tile = h_ref.shape[1]
    n_chunk = lax.shift_right_logical(cnt_ref[i * N_EXPERTS + e] + (rc - 1), int(math.log2(rc)))
    lane = lax.broadcasted_iota(I32, (tile, LANES), 1)

    @pl.when(jnp.logical_and(e == 0, j == 0))
    def _():
        o_ref[0] = jnp.zeros(o_ref.shape[1:], F32)

    @pl.when(j == 0)
    def _():
        prow = post_ref[0]
        rid = lax.broadcasted_iota(I32, (rc, tile), 0)

        def gather(c, _):
            onehot = jnp.where(prow == rid + c * rc, 1.0, 0.0).astype(BF16)
            r0 = pl.multiple_of(c * rc, rc)
            xg_ref[pl.ds(r0, rc), :] = _dot(onehot, h_ref[0]).astype(BF16)
            return 0

        lax.fori_loop(0, n_chunk, gather, 0)

    def expert(c, _):
        r0 = pl.multiple_of(c * rc, rc)
        xg = xg_ref[pl.ds(r0, rc), :]
        a = (_silu(_dot(xg, wg_ref[0])) * _dot(xg, wu_ref[0])).astype(BF16)
        part = _dot(a, wd_ref[0])

        @pl.when(j == 0)
        def _():
            y_ref[pl.ds(r0, rc), :] = part

        @pl.when(j > 0)
        def _():
            y_ref[pl.ds(r0, rc), :] += part

        return 0

    lax.fori_loop(0, n_chunk, expert, 0)

    @pl.when(j == pl.num_programs(2) - 1)
    def _():
        pcol = jnp.sum(jnp.where(lane == e, pos_ref[0], 0), axis=1, keepdims=True)
        wcol = jnp.sum(jnp.where(lane == e, w_ref[0], 0.0), axis=1, keepdims=True)
        cid = lax.broadcasted_iota(I32, (tile, rc), 1)

        def scatter(c, z):
            r0 = pl.multiple_of(c * rc, rc)
            onehot_t = jnp.where(pcol == cid + c * rc, 1.0, 0.0).astype(BF16)
            return z + _dot(onehot_t, y_ref[pl.ds(r0, rc), :].astype(BF16))

        z = lax.fori_loop(0, n_chunk, scatter, jnp.zeros((tile, o_ref.shape[2]), F32))
        o_ref[0] += wcol * z


def _moe_ffn(h, pos, wgt, post, cnt, w_gate, w_up, w_down, tile, rc=128, tf=512):
    t_tokens, d = h.shape
    nt = t_tokens // tile
    ff = w_gate.shape[2]
    assert ff % tf == 0
    hb = h.reshape(nt, tile, d)
    post3 = post.reshape(nt * N_EXPERTS, 1, tile)
    grid_spec = pltpu.PrefetchScalarGridSpec(
        num_scalar_prefetch=1,
        grid=(nt, N_EXPERTS, ff // tf),
        in_specs=[pl.BlockSpec((1, tile, d), lambda i, e, j, c: (i, 0, 0)),
                  pl.BlockSpec((1, tile, LANES), lambda i, e, j, c: (i, 0, 0)),
                  pl.BlockSpec((1, tile, LANES), lambda i, e, j, c: (i, 0, 0)),
                  pl.BlockSpec((1, 1, tile), lambda i, e, j, c: (i * N_EXPERTS + e, 0, 0)),
                  pl.BlockSpec((1, d, tf), lambda i, e, j, c: (e, 0, j)),
                  pl.BlockSpec((1, d, tf), lambda i, e, j, c: (e, 0, j)),
                  pl.BlockSpec((1, tf, d), lambda i, e, j, c: (e, j, 0))],
        out_specs=pl.BlockSpec((1, tile, d), lambda i, e, j, c: (i, 0, 0)),
        scratch_shapes=[pltpu.VMEM((tile, d), BF16), pltpu.VMEM((tile, d), F32)])
    out = pl.pallas_call(
        functools.partial(_moe_kernel, rc=rc),
        grid_spec=grid_spec,
        out_shape=jax.ShapeDtypeStruct((nt, tile, d), F32),
        compiler_params=_cparams("arbitrary", "arbitrary", "arbitrary"),
        name="moe_ffn",
    )(cnt.reshape(-1), hb, pos, wgt, post3, w_gate.astype(BF16), w_up.astype(BF16), w_down.astype(BF16))
    return out.reshape(t_tokens, d)


MOE_RC = 128
MOE_GC = 256
MOE_SC = 1024


def _moe_rows(tile):
    return -(-(2 * tile + N_EXPERTS * MOE_RC) // MOE_SC) * MOE_SC


def _route_kernel(x_ref, gpre_ref, sc_ref, sh_ref, wr_ref, br_ref, d_ref, w_ref, dt_ref, off_ref, nch_ref):
    tile = x_ref.shape[1]
    h = _rms(x_ref[0], gpre_ref[...]) * (1.0 + sc_ref[0]) + sh_ref[0]
    logits = _dot_f32(h, wr_ref[...]) + br_ref[...]
    lane = lax.broadcasted_iota(I32, (tile, LANES), 1)
    lanef = lane.astype(F32)
    e = jnp.exp(logits - jnp.max(logits, axis=1, keepdims=True))
    probs = e / jnp.sum(e, axis=1, keepdims=True)
    probs = jnp.where(lane < N_EXPERTS, probs, -1.0)
    m1 = jnp.max(probs, axis=1, keepdims=True)
    i1 = jnp.min(jnp.where(probs == m1, lanef, 1e9), axis=1, keepdims=True)
    rest = jnp.where(lanef == i1, -1.0, probs)
    m2 = jnp.max(rest, axis=1, keepdims=True)
    i2 = jnp.min(jnp.where(rest == m2, lanef, 1e9), axis=1, keepdims=True)
    tot = m1 + m2
    hit1 = lanef == i1
    hit2 = lanef == i2
    mask = hit1 | hit2
    ch = 256
    r = lax.broadcasted_iota(I32, (ch, ch), 0)
    c = lax.broadcasted_iota(I32, (ch, ch), 1)
    lower = jnp.where(c < r, 1.0, 0.0).astype(BF16)
    ones = jnp.ones((8, ch), BF16)
    count = jnp.zeros((1, LANES), F32)
    ranks = []
    for k in range(tile // ch):
        mk = jnp.where(mask[k * ch:(k + 1) * ch], 1.0, 0.0).astype(BF16)
        ranks.append(_dot(lower, mk) + count)
        count = count + _dot(ones, mk)[0:1]
    rank = jnp.concatenate(ranks, axis=0)
    shift = int(math.log2(MOE_RC))
    nch = lax.shift_right_logical(count.astype(I32) + (MOE_RC - 1), shift)
    nch8 = jnp.broadcast_to(nch.astype(F32), (8, LANES))
    lr = lax.broadcasted_iota(I32, (LANES, LANES), 0)
    lc = lax.broadcasted_iota(I32, (LANES, LANES), 1)
    offc = _dot(nch8.astype(BF16), jnp.where(lr < lc, 1.0, 0.0).astype(BF16))
    row = offc[0:1] * float(MOE_RC) + rank
    d1 = jnp.sum(jnp.where(hit1, row, 0.0), axis=1, keepdims=True)
    d2 = jnp.sum(jnp.where(hit2, row, 0.0), axis=1, keepdims=True)
    dd = jnp.where(lane == 0, d1, jnp.where(lane == 1, d2, -1.0))
    d_ref[0] = dd.astype(I32)
    w_ref[0] = jnp.where(lane == 0, m1 / tot, jnp.where(lane == 1, m2 / tot, 0.0))
    dt_ref[0] = jnp.transpose(dd)[0:8].astype(I32)
    off_ref[0] = offc.astype(I32)
    nch_ref[0] = jnp.broadcast_to(nch, (8, LANES))


def _route(x, g_pre, scale_f, shift_f, w_router, b_router, tile):
    bsz, s, d = x.shape
    nt = s // tile
    wr = jnp.zeros((d, LANES), F32).at[:, :N_EXPERTS].set(w_router.astype(F32))
    br = jnp.full((1, LANES), NEG, F32).at[0, :N_EXPERTS].set(b_router.astype(F32))
    mod = pl.BlockSpec((1, 1, d), lambda b, i: (b, 0, 0))
    tok = pl.BlockSpec((1, tile, LANES), lambda b, i: (b * nt + i, 0, 0))
    small = pl.BlockSpec((1, 8, LANES), lambda b, i: (b * nt + i, 0, 0))
    dest, wgt, dest_t, offc, nch = pl.pallas_call(
        _route_kernel,
        grid=(bsz, nt),
        in_specs=[pl.BlockSpec((1, tile, d), lambda b, i: (b, i, 0)),
                  pl.BlockSpec((1, d), lambda b, i: (0, 0)), mod, mod,
                  pl.BlockSpec((d, LANES), lambda b, i: (0, 0)),
                  pl.BlockSpec((1, LANES), lambda b, i: (0, 0))],
        out_specs=[tok, tok, pl.BlockSpec((1, 8, tile), lambda b, i: (b * nt + i, 0, 0)), small, small],
        out_shape=[jax.ShapeDtypeStruct((bsz * nt, tile, LANES), I32),
                   jax.ShapeDtypeStruct((bsz * nt, tile, LANES), F32),
                   jax.ShapeDtypeStruct((bsz * nt, 8, tile), I32),
                   jax.ShapeDtypeStruct((bsz * nt, 8, LANES), I32),
                   jax.ShapeDtypeStruct((bsz * nt, 8, LANES), I32)],
        compiler_params=_cparams("arbitrary", "arbitrary"),
        name="moe_router",
    )(x, g_pre.reshape(1, d), scale_f.reshape(bsz, 1, d), shift_f.reshape(bsz, 1, d), wr, br)
    return dest, wgt, dest_t, offc[:, 0, :N_EXPERTS].reshape(-1), nch[:, 0, :N_EXPERTS].reshape(-1)


def _experts_kernel(off_ref, nch_ref, h_ref, d_ref, w_ref, dt_ref, wg_ref, wu_ref, wd_ref, o_ref,
                    xg_ref, y_ref, wr_ref):
    i, e, j = pl.program_id(0), pl.program_id(1), pl.program_id(2)
    tile = h_ref.shape[1]
    rows = xg_ref.shape[0]

    @pl.when(jnp.logical_and(e == 0, j == 0))
    def _():
        y_ref[...] = jnp.zeros(y_ref.shape, F32)
        d1t = dt_ref[0, 0:1, :]
        d2t = dt_ref[0, 1:2, :]
        w = w_ref[0]
        lane = lax.broadcasted_iota(I32, (tile, LANES), 1)

        def pieces(col):
            hi, mid, lo = _split3(jnp.broadcast_to(col, (tile, LANES)))
            z = jnp.zeros((tile, LANES), F32)
            return jnp.where(lane == 0, hi.astype(F32), jnp.where(lane == 1, mid.astype(F32),
                             jnp.where(lane == 2, lo.astype(F32), z))).astype(BF16)

        wp1 = pieces(w[:, 0:1])
        wp2 = pieces(w[:, 1:2])
        hb = h_ref[0]
        rid = lax.broadcasted_iota(I32, (MOE_GC, tile), 0)
        for c in range(rows // MOE_GC):
            p1 = jnp.where(rid + c * MOE_GC == d1t, 1.0, 0.0).astype(BF16)
            p2 = jnp.where(rid + c * MOE_GC == d2t, 1.0, 0.0).astype(BF16)
            xg_ref[c * MOE_GC:(c + 1) * MOE_GC, :] = _dot(p1 + p2, hb).astype(BF16)
            wrow = jnp.sum(_dot(p1, wp1) + _dot(p2, wp2), axis=1, keepdims=True)
            wr_ref[c * MOE_GC:(c + 1) * MOE_GC, :] = jnp.broadcast_to(wrow, (MOE_GC, LANES))

    base = off_ref[i * N_EXPERTS + e] * MOE_RC
    n = nch_ref[i * N_EXPERTS + e]

    def ffn(r0, m):
        xg = xg_ref[pl.ds(r0, m), :]
        a = (_silu(_dot(xg, wg_ref[0])) * _dot(xg, wu_ref[0])).astype(BF16)
        y_ref[pl.ds(r0, m), :] += _dot(a, wd_ref[0])

    def pair(c, _):
        ffn(pl.multiple_of(base + c * (2 * MOE_RC), MOE_RC), 2 * MOE_RC)
        return 0

    lax.fori_loop(0, lax.shift_right_logical(n, 1), pair, 0)

    @pl.when((n & 1) == 1)
    def _():
        ffn(pl.multiple_of(base + (n - 1) * MOE_RC, MOE_RC), MOE_RC)

    @pl.when(jnp.logical_and(e == pl.num_programs(1) - 1, j == pl.num_programs(2) - 1))
    def _():
        dd = d_ref[0]
        d1c = dd[:, 0:1]
        d2c = dd[:, 1:2]
        cid = lax.broadcasted_iota(I32, (tile, MOE_SC), 1)
        z = None
        for k in range(rows // MOE_SC):
            pt = jnp.where((cid + k * MOE_SC == d1c) | (cid + k * MOE_SC == d2c), 1.0, 0.0).astype(BF16)
            yb = (y_ref[k * MOE_SC:(k + 1) * MOE_SC, :] * wr_ref[k * MOE_SC:(k + 1) * MOE_SC, 0:1]).astype(BF16)
            part = _dot(pt, yb)
            z = part if z is None else z + part
        o_ref[0] = z


def _experts(h, dest, wgt, dest_t, offc, nch, w_gate, w_up, w_down, tile, tf=512):
    t_tokens, d = h.shape
    nt = t_tokens // tile
    ff = w_gate.shape[2]
    assert ff % tf == 0 and tf % LANES == 0
    rows = _moe_rows(tile)
    tok = lambda n: pl.BlockSpec((1, tile, n), lambda i, e, j, o, c: (i, 0, 0))
    grid_spec = pltpu.PrefetchScalarGridSpec(
        num_scalar_prefetch=2,
        grid=(nt, N_EXPERTS, ff // tf),
        in_specs=[tok(d), tok(LANES), tok(LANES),
                  pl.BlockSpec((1, 8, tile), lambda i, e, j, o, c: (i, 0, 0)),
                  pl.BlockSpec((1, d, tf), lambda i, e, j, o, c: (e, 0, j)),
                  pl.BlockSpec((1, d, tf), lambda i, e, j, o, c: (e, 0, j)),
                  pl.BlockSpec((1, tf, d), lambda i, e, j, o, c: (e, j, 0))],
        out_specs=tok(d),
        scratch_shapes=[pltpu.VMEM((rows, d), BF16), pltpu.VMEM((rows, d), F32),
                        pltpu.VMEM((rows, LANES), F32)])
    out = pl.pallas_call(
        _experts_kernel,
        grid_spec=grid_spec,
        out_shape=jax.ShapeDtypeStruct((nt, tile, d), F32),
        compiler_params=_cparams("arbitrary", "arbitrary", "arbitrary"),
        name="moe_ffn",
    )(offc, nch, h.reshape(nt, tile, d), dest, wgt, dest_t,
      w_gate.astype(BF16), w_up.astype(BF16), w_down.astype(BF16))
    return out.reshape(t_tokens, d)


def _post_kernel(x_ref, f_ref, gpost_ref, gate_ref, o_ref):
    o_ref[0] = x_ref[0] + gate_ref[0] * _rms(f_ref[0], gpost_ref[...])


def _post_residual(x, f, g_post, gate_f):
    bsz, s, d = x.shape
    tm = min(1024, s)
    act = pl.BlockSpec((1, tm, d), lambda b, i: (b, i, 0))
    return pl.pallas_call(
        _post_kernel,
        grid=(bsz, s // tm),
        in_specs=[act, act, pl.BlockSpec((1, d), lambda b, i: (0, 0)),
                  pl.BlockSpec((1, 1, d), lambda b, i: (b, 0, 0))],
        out_specs=act,
        out_shape=jax.ShapeDtypeStruct((bsz, s, d), F32),
        compiler_params=_cparams("arbitrary", "arbitrary"),
        name="post_residual",
    )(x, f, g_post.reshape(1, d), gate_f.reshape(bsz, 1, d))


def _mixer(x, mod, layer, rel_table, g_pre_mix, w_in, cmp_params):
    shift_m, scale_m = mod[:, 0], mod[:, 1]
    proj = _in_proj(x, g_pre_mix, scale_m, shift_m, _arrange_w_in(w_in))
    o_sb = _sb_attention(proj)
    o_nsa = _nsa_attention(proj, _compress(proj, *cmp_params), rel_table)
    return o_sb, o_nsa


def kernel(x, c, rel_table, w_ada, b_ada, g_pre_mix, g_post_mix, g_pre_ffn, g_post_ffn, w_in, w_out, g_sb, g_nsa, cmp_pos_k, cmp_w1_k, cmp_w2_k, cmp_pos_v, cmp_w1_v, cmp_w2_v, ffn_w_gate, ffn_w_up, ffn_w_down, moe_w_router, moe_b_router, moe_w_gate, moe_w_up, moe_w_down):
    bsz, s, d = x.shape
    depth = w_in.shape[0]
    mods = _ada(c, w_ada, b_ada).reshape(depth, bsz, 6, d)
    moe_tile = min(1024, s)
    for layer in range(depth):
        mod = mods[layer]
        cmp_params = (cmp_pos_k[layer], cmp_w1_k[layer], cmp_w2_k[layer],
                      cmp_pos_v[layer], cmp_w1_v[layer], cmp_w2_v[layer])
        o_sb, o_nsa = _mixer(x, mod, layer, rel_table, g_pre_mix[layer], w_in[layer], cmp_params)
        x, h = _out_proj(o_sb, o_nsa, x, g_sb[layer], g_nsa[layer], w_out[layer], g_post_mix[layer],
                         mod[:, 2], g_pre_ffn[layer], mod[:, 4], mod[:, 3])
        i = layer // 2
        if layer % 2 == 0:
            x = _dense_ffn(h, x, ffn_w_gate[i], ffn_w_up[i], ffn_w_down[i], g_post_ffn[layer], mod[:, 5])
        else:
            routing = _route(x, g_pre_ffn[layer], mod[:, 4], mod[:, 3],
                             moe_w_router[i], moe_b_router[i], moe_tile)
            f = _experts(h.reshape(bsz * s, d), *routing,
                         moe_w_gate[i], moe_w_up[i], moe_w_down[i], moe_tile)
            x = _post_residual(x, f.reshape(bsz, s, d), g_post_ffn[layer], mod[:, 5])
    return x
```

```python
import functools
import math

import numpy as np
import jax
import jax.numpy as jnp
from jax import lax
from jax.experimental import pallas as pl
from jax.experimental.pallas import tpu as pltpu

F32 = jnp.float32
BF16 = jnp.bfloat16
I32 = jnp.int32

LANES = 128
DH = 64
N_SB = 8
N_NSA = 8
G_NSA = 2
Z_NSA = 4
CMP_BLOCK = 32
CMP_STRIDE = 16
SLC_BLOCK = 64
N_SELECT = 16
WINDOW = 512
REL_BUCKETS = 32
REL_MAX_DIST = 128
N_EXPERTS = 8
EPS = 1e-6
FORCED = 1e4
NEG = -1e30
M_INIT = -1e29
SB_EXIT = -104.5
VMEM_LIMIT = 56 * 1024 * 1024

TQ = 256
NEAR_BACK = REL_MAX_DIST
KC_FRONT = 16
SEL_FAR_TILE = 512


def _cparams(*sem):
    return pltpu.CompilerParams(dimension_semantics=sem, vmem_limit_bytes=VMEM_LIMIT)


def _nt(a, b):
    return lax.dot_general(a, b, (((1,), (1,)), ((), ())), preferred_element_type=F32)


def _dot(a, b):
    return jnp.dot(a, b, preferred_element_type=F32)


def _split3(a):
    hi = a.astype(BF16)
    r = a - hi.astype(F32)
    mid = r.astype(BF16)
    lo = (r - mid.astype(F32)).astype(BF16)
    return hi, mid, lo


def _dot_hl(a, b):
    hi = a.astype(BF16)
    lo = (a - hi.astype(F32)).astype(BF16)
    return _dot(hi, b) + _dot(lo, b)


def _dot_f32(a, b):
    ah, am, al = _split3(a)
    bh, bm, bl = _split3(b)
    return (_dot(ah, bh) + (_dot(ah, bm) + _dot(am, bh))
            + (_dot(ah, bl) + _dot(am, bm) + _dot(al, bh)))


def _rms(x, g):
    return x * lax.rsqrt(jnp.mean(x * x, axis=-1, keepdims=True) + EPS) * g


def _ada_kernel(c_ref, w_ref, b_ref, o_ref):
    c = c_ref[...]
    ca = c * (1.0 / (1.0 + jnp.exp(-c)))
    o_ref[0] = _dot_f32(ca, w_ref[0]) + b_ref[0]


def _ada(c, w_ada, b_ada):
    depth, d, n = w_ada.shape
    bsz = c.shape[0]
    rows = 8
    tn = 1536
    cp = jnp.zeros((rows, d), F32).at[:bsz].set(c)
    out = pl.pallas_call(
        _ada_kernel,
        grid=(depth, n // tn),
        in_specs=[pl.BlockSpec((rows, d), lambda l, j: (0, 0)),
                  pl.BlockSpec((1, d, tn), lambda l, j: (l, 0, j)),
                  pl.BlockSpec((1, 1, tn), lambda l, j: (l, 0, j))],
        out_specs=pl.BlockSpec((1, rows, tn), lambda l, j: (l, 0, j)),
        out_shape=jax.ShapeDtypeStruct((depth, rows, n), F32),
        compiler_params=_cparams("arbitrary", "arbitrary"),
        name="ada_mod",
    )(cp, w_ada, b_ada.reshape(depth, 1, n))
    return out[:, :bsz]


def _in_kernel(x_ref, g_ref, sc_ref, sh_ref, w_ref, o_ref, *, cn):
    h = _rms(x_ref[0], g_ref[...]) * (1.0 + sc_ref[0]) + sh_ref[0]
    hb = h.astype(BF16)
    for j in range(w_ref.shape[1] // cn):
        o_ref[0, :, j * cn:(j + 1) * cn] = _dot(hb, w_ref[:, j * cn:(j + 1) * cn]).astype(BF16)


def _in_proj(x, g, scale, shift, w):
    bsz, s, d = x.shape
    nc = w.shape[1]
    tm = min(512, s)
    cn = nc // 3 if (nc // 3) % LANES == 0 else nc
    return pl.pallas_call(
        functools.partial(_in_kernel, cn=cn),
        grid=(bsz, s // tm),
        in_specs=[pl.BlockSpec((1, tm, d), lambda b, i: (b, i, 0)),
                  pl.BlockSpec((1, d), lambda b, i: (0, 0)),
                  pl.BlockSpec((1, 1, d), lambda b, i: (b, 0, 0)),
                  pl.BlockSpec((1, 1, d), lambda b, i: (b, 0, 0)),
                  pl.BlockSpec((d, nc), lambda b, i: (0, 0))],
        out_specs=pl.BlockSpec((1, tm, nc), lambda b, i: (b, i, 0)),
        out_shape=jax.ShapeDtypeStruct((bsz, s, nc), BF16),
        compiler_params=_cparams("arbitrary", "arbitrary"),
        name="in_proj",
    )(x, g.reshape(1, d), scale.reshape(bsz, 1, d), shift.reshape(bsz, 1, d), w)


CB_SBQ, CB_SBK, CB_SBV, CB_NSAQ = 0, 4, 8, 12
CB_KCMP, CB_VCMP = 16, 17
CB_KSLC, CB_VSLC, CB_KWIN, CB_VWIN = 18, 20, 22, 24
CB_GATE = 26
N_CB = 27


def _arrange_w_in(w_in):
    d = w_in.shape[0]
    w_sb = N_SB * DH
    off_nsa_q = 3 * w_sb
    off_kv = off_nsa_q + N_NSA * DH
    off_gate = off_kv + 3 * 2 * G_NSA * DH
    scale = DH ** -0.5

    def kv(br, kvi, g):
        lo = off_kv + ((br * 2 + kvi) * G_NSA + g) * DH
        return w_in[:, lo:lo + DH]

    cols = [w_in[:, 0:w_sb] * scale, w_in[:, w_sb:2 * w_sb], w_in[:, 2 * w_sb:3 * w_sb],
            w_in[:, off_nsa_q:off_kv] * scale,
            kv(0, 0, 0), kv(0, 0, 1), kv(0, 1, 0), kv(0, 1, 1)]
    for br in (1, 2):
        for kvi in (0, 1):
            for g in range(G_NSA):
                cols += [kv(br, kvi, g), kv(br, kvi, g)]
    n_gate = 3 * N_NSA
    cols += [w_in[:, off_gate:off_gate + n_gate], jnp.zeros((d, LANES - n_gate), w_in.dtype)]
    out = jnp.concatenate(cols, axis=1).astype(BF16)
    assert out.shape[1] == N_CB * LANES
    return out


def _sb_kernel(q_ref, k_ref, v_ref, o_ref, *, t):
    qi = pl.program_id(2)
    q = q_ref[0]
    lane = lax.broadcasted_iota(I32, (t, LANES), 1)
    row = lax.broadcasted_iota(I32, (t, t), 0)
    col = lax.broadcasted_iota(I32, (t, t), 1)
    upper = jnp.where(row > col, 1.0, 0.0).astype(BF16)
    upper2 = jnp.concatenate([upper, upper], axis=0)
    ones2 = jnp.ones((2 * t, LANES), BF16)
    causal = col < row
    rep = t // LANES

    def tile(qh, k, v, carry, acc, diag):
        z = _nt(qh, k)
        lk = -(jnp.maximum(z, 0.0) + jnp.log(1.0 + jnp.exp(-jnp.abs(z))))
        if diag:
            lk = jnp.where(causal, lk, 0.0)
        hi = lk.astype(BF16)
        lo = (lk - hi.astype(F32)).astype(BF16)
        hl = jnp.concatenate([hi, lo], axis=1)
        later = _dot(hl, upper2)
        tot = _dot(hl, ones2)
        cb = carry if rep == 1 else jnp.concatenate([carry] * rep, axis=1)
        w = jnp.exp(z + lk + later + cb)
        if diag:
            w = jnp.where(causal, w, 0.0)
        return carry + tot, acc + _dot(w.astype(BF16), v)

    zq = jnp.zeros_like(q)
    qhs = (jnp.where(lane < DH, q, zq), jnp.where(lane < DH, zq, q))
    zero = jnp.zeros((t, LANES), F32)

    def both(kt, st, diag):
        k = k_ref[0, pl.ds(kt * t, t), :]
        v = v_ref[0, pl.ds(kt * t, t), :]
        c0, a0 = tile(qhs[0], k, v, st[0], st[1], diag)
        c1, a1 = tile(qhs[1], k, v, st[2], st[3], diag)
        return jnp.maximum(jnp.max(c0), jnp.max(c1)), (c0, a0, c1, a1)

    mx, st = both(qi, (zero, zero, zero, zero), True)

    def cond(s):
        return jnp.logical_and(s[0] >= 0, s[1] > SB_EXIT)

    def body(s):
        mx, st = both(s[0], s[2], False)
        return s[0] - 1, mx, st

    _, _, st = lax.while_loop(cond, body, (qi - 1, mx, st))
    o_ref[0] = jnp.where(lane < DH, st[1], st[3])


def _sb_attention(proj, t=256):
    bsz, s, _ = proj.shape
    t = min(t, s)
    npair = N_SB // 2
    return pl.pallas_call(
        functools.partial(_sb_kernel, t=t),
        grid=(bsz, npair, s // t),
        in_specs=[pl.BlockSpec((1, t, LANES), lambda b, j, i: (b, i, CB_SBQ + j)),
                  pl.BlockSpec((1, s, LANES), lambda b, j, i: (b, 0, CB_SBK + j)),
                  pl.BlockSpec((1, s, LANES), lambda b, j, i: (b, 0, CB_SBV + j))],
        out_specs=pl.BlockSpec((1, t, LANES), lambda b, j, i: (b, i, j)),
        out_shape=jax.ShapeDtypeStruct((bsz, s, N_SB * DH), F32),
        compiler_params=_cparams("arbitrary", "arbitrary", "arbitrary"),
        name="sb_attn",
    )(proj, proj, proj)


def _gelu_tanh(x):
    return 0.5 * x * (1.0 + jnp.tanh(math.sqrt(2.0 / math.pi) * (x + 0.044715 * (x * x * x))))


def _cmp_kernel(xa_ref, xb_ref, pos_ref, w1_ref, w2_ref, o_ref):
    half = w1_ref.shape[1] // 2
    w1a = w1_ref[0, :half, :]
    w1b = w1_ref[0, half:, :]
    pos = pos_ref[0]
    bias = _dot(pos[:, :half], w1a) + _dot(pos[:, half:], w1b)
    hid = _dot(xa_ref[0, 0], w1a) + _dot(xb_ref[0, 0], w1b) + bias[0:1, :]
    o_ref[0, 0] = _dot(_gelu_tanh(hid).astype(BF16), w2_ref[0])


def _compress(proj, pos_k, w1_k, w2_k, pos_v, w1_v, w2_v):
    bsz, s, _ = proj.shape
    n16 = s // CMP_STRIDE
    raw = proj[:, :, CB_KCMP * LANES:(CB_VCMP + 1) * LANES]
    x16 = raw.reshape(bsz, s, 4, DH).transpose(0, 2, 1, 3).reshape(bsz, 4, n16, CMP_STRIDE * DH)
    x16b = jnp.concatenate([x16[:, :, 1:], jnp.zeros_like(x16[:, :, :1])], axis=2)
    hidden = w1_k.shape[1]
    w1 = jnp.stack([w1_k, w1_v]).astype(BF16)
    w2 = jnp.stack([jnp.concatenate([w2_k, w2_k], 1), jnp.concatenate([w2_v, w2_v], 1)]).astype(BF16)
    pos = jnp.stack([pos_k.reshape(1, -1), pos_v.reshape(1, -1)])
    pos = jnp.concatenate([pos, jnp.zeros((2, 7, pos.shape[2]), pos.dtype)], axis=1).astype(BF16)
    k16 = CMP_STRIDE * DH
    return pl.pallas_call(
        _cmp_kernel,
        grid=(bsz, 4),
        in_specs=[pl.BlockSpec((1, 1, n16, k16), lambda b, j: (b, j, 0, 0)),
                  pl.BlockSpec((1, 1, n16, k16), lambda b, j: (b, j, 0, 0)),
                  pl.BlockSpec((1, 8, 2 * k16), lambda b, j: (j // 2, 0, 0)),
                  pl.BlockSpec((1, 2 * k16, hidden), lambda b, j: (j // 2, 0, 0)),
                  pl.BlockSpec((1, hidden, LANES), lambda b, j: (j // 2, 0, 0))],
        out_specs=pl.BlockSpec((1, 1, n16, LANES), lambda b, j: (b, j, 0, 0)),
        out_shape=jax.ShapeDtypeStruct((bsz, 4, n16, LANES), F32),
        compiler_params=_cparams("arbitrary", "arbitrary"),
        name="nsa_compress",
    )(x16, x16b, pos, w1, w2)


def _bucket_table():
    n = np.arange(REL_MAX_DIST + 1)
    exact = REL_BUCKETS // 2
    val = (np.log(np.maximum(n, 1).astype(np.float32) / np.float32(exact)).astype(np.float32)
           / np.float32(math.log(REL_MAX_DIST / exact)) * np.float32(REL_BUCKETS - exact))
    large = np.minimum(exact + val.astype(np.int32), REL_BUCKETS - 1)
    return np.where(n < exact, n, large).astype(np.int32)


def _bias_tiles(rel_table):
    bucket = _bucket_table()
    far_row = rel_table[int(bucket[REL_MAX_DIST])].astype(F32)
    r = np.arange(TQ)[:, None]

    def tile(dist, valid, base):
        onehot = jax.nn.one_hot(bucket[np.clip(dist, 0, REL_MAX_DIST)], REL_BUCKETS, dtype=F32)
        t = jnp.einsum('rwb,bh->hrw', onehot, rel_table.astype(F32),
                       precision=lax.Precision.HIGHEST)
        t = jnp.where(valid[None], t - base, NEG)
        return t.reshape(G_NSA, Z_NSA * TQ, dist.shape[1])

    w = np.arange(LANES)[None, :]
    d_c = r - CMP_STRIDE * (w - KC_FRONT) - (CMP_BLOCK - 1)
    w2 = np.arange(NEAR_BACK + TQ)[None, :]
    d_s = r + NEAR_BACK - w2
    w5 = np.arange(WINDOW + TQ)[None, :]
    d_w = r + WINDOW - w5
    far = jnp.broadcast_to(far_row[:, None, None], (N_NSA, TQ, LANES))
    far = far.reshape(G_NSA, Z_NSA * TQ, LANES)
    return (tile(d_c, d_c >= 0, far_row[:, None, None]), tile(d_s, d_s >= 0, 0.0),
            tile(d_w, (d_w >= 0) & (d_w < WINDOW), 0.0), far)


def _gate_expand():
    e = np.zeros((G_NSA, LANES, 3 * Z_NSA * DH), np.float32)
    for g in range(G_NSA):
        for z in range(Z_NSA):
            for br in range(3):
                e[g, (g * Z_NSA + z) * 3 + br, br * Z_NSA * DH + z * DH: br * Z_NSA * DH + (z + 1) * DH] = 1.0
    return e


def _overlap_padded(n16, kcp, n_slc):
    i = np.arange(kcp)[:, None] - KC_FRONT
    j = np.arange(LANES)[None, :]
    n_cmp = n16 - 1
    ok = (i >= 0) & (i < n_cmp) & (j < n_slc)
    ov = (i * CMP_STRIDE < j * SLC_BLOCK + SLC_BLOCK) & (i * CMP_STRIDE + CMP_BLOCK > j * SLC_BLOCK)
    return (ok & ov).astype(np.float32)


def _block_onehot(s):
    return (np.arange(s)[:, None] // SLC_BLOCK == np.arange(LANES)[None, :]).astype(np.float32)


def _nsa_kernel(q_ref, kcb_ref, vcb_ref, kcf_ref, vcf_ref, ks_ref, vs_ref, kw_ref, vw_ref, et_ref,
                gate_ref, ovb_ref, ovf_ref, bcn_ref, bsn_ref, bw_ref, cf_ref, eg_ref, o_ref,
                sb0_ref, sb1_ref, *, n_slc, n_round):
    blk = pl.program_id(2)
    c0 = blk * TQ
    r4 = Z_NSA * TQ
    q = q_ref[0]
    lane = lax.broadcasted_iota(I32, (TQ, LANES), 1)
    lo_half = lane < DH
    zq = jnp.zeros((TQ, LANES), BF16)
    qa, qb = q[:, :LANES], q[:, LANES:]
    qs = jnp.concatenate([jnp.where(lo_half, qa, zq), jnp.where(lo_half, zq, qa),
                          jnp.where(lo_half, qb, zq), jnp.where(lo_half, zq, qb)], axis=0)
    ones_k = jnp.ones((SEL_FAR_TILE, LANES), BF16)
    ones_w = jnp.ones((WINDOW + TQ, LANES), BF16)

    kcp = kcb_ref.shape[2]
    prow = lax.broadcasted_iota(I32, (1, kcp), 1)
    near0 = pl.multiple_of(blk * (TQ // CMP_STRIDE), 8)
    far_row = jnp.where((prow >= KC_FRONT) & (prow < near0), 0.0, NEG)
    lane1 = lax.broadcasted_iota(I32, (1, LANES), 1)
    near_row = jnp.where(lane1 + near0 >= KC_FRONT, 0.0, NEG)
    kcn = kcf_ref[0, 0, pl.ds(near0, LANES), :].astype(BF16)
    vcn = vcf_ref[0, 0, pl.ds(near0, LANES), :].astype(BF16)
    s_far = _nt(qs, kcb_ref[0, 0]) + far_row
    s_near = _nt(qs, kcn) + bcn_ref[0] + near_row
    m_c = jnp.maximum(jnp.max(s_far, axis=1, keepdims=True), jnp.max(s_near, axis=1, keepdims=True))
    m_c = jnp.maximum(m_c, M_INIT)
    p_far = jnp.exp(s_far - m_c)
    p_near = jnp.exp(s_near - m_c)
    den = jnp.sum(p_far, axis=1, keepdims=True) + jnp.sum(p_near, axis=1, keepdims=True)
    inv_c = 1.0 / jnp.maximum(den, 1e-30)
    o_c = (_dot(p_far.astype(BF16), vcb_ref[0, 0]) + _dot(p_near.astype(BF16), vcn)) * inv_c

    pn_far = p_far * inv_c
    pn_near = p_near * inv_c
    pz_far = pn_far[0:TQ] + pn_far[TQ:2 * TQ] + pn_far[2 * TQ:3 * TQ] + pn_far[3 * TQ:]
    pz_near = pn_near[0:TQ] + pn_near[TQ:2 * TQ] + pn_near[2 * TQ:3 * TQ] + pn_near[3 * TQ:]
    ovn = ovf_ref[pl.ds(near0, LANES), :].astype(BF16)
    imp = _dot_hl(pz_far, ovb_ref[...]) + _dot_hl(pz_near, ovn)
    tpos = c0 + lax.broadcasted_iota(I32, (TQ, LANES), 0)
    cur = lax.shift_right_logical(tpos, 6)
    valid = (lane * SLC_BLOCK <= tpos) & (lane < n_slc)
    forced = valid & ((lane == 0) | (lane == cur) | (lane == cur - 1))
    score = jnp.where(valid & jnp.logical_not(forced), imp, -1.0)
    score = jnp.where(lane < n_slc, score, -2.0)

    rest, taken, thr = score, jnp.zeros((TQ, 1), F32), jnp.zeros((TQ, 1), F32)
    for _ in range(n_round):
        m = jnp.max(rest, axis=1, keepdims=True)
        eq = rest == m
        thr = jnp.where(taken < n_round, m, thr)
        taken = taken + jnp.sum(jnp.where(eq, 1.0, 0.0), axis=1, keepdims=True)
        rest = jnp.where(eq, -jnp.inf, rest)
    above = score > thr
    at_thr = score == thr
    need = n_round - jnp.sum(jnp.where(above, 1.0, 0.0), axis=1, keepdims=True)
    brow = lax.broadcasted_iota(I32, (LANES, LANES), 0)
    bcol = lax.broadcasted_iota(I32, (LANES, LANES), 1)
    before = _dot(jnp.where(at_thr, 1.0, 0.0).astype(BF16),
                  jnp.where(brow < bcol, 1.0, 0.0).astype(BF16))
    sel = ((above | (at_thr & (before < need))) & valid) | forced

    kws, vws = [], []
    for w in range((WINDOW + TQ) // LANES):
        st = pl.multiple_of(jnp.maximum(c0 - WINDOW + w * LANES, 0), LANES)
        kws.append(kw_ref[0, pl.ds(st, LANES), :])
        vws.append(vw_ref[0, pl.ds(st, LANES), :])
    wl = lax.broadcasted_iota(I32, (1, WINDOW + TQ), 1)
    pos_row = jnp.where(c0 - WINDOW + wl >= 0, 0.0, NEG)
    s = _nt(qs, jnp.concatenate(kws, axis=0)) + bw_ref[0] + pos_row
    p = jnp.exp(s - jnp.max(s, axis=1, keepdims=True)).astype(BF16)
    r_w = _dot(p, jnp.concatenate([jnp.concatenate(vws, axis=0), ones_w], axis=1))
    o_w = r_w[:, :LANES] / r_w[:, LANES:]
    gl = _dot(gate_ref[0], eg_ref[0])
    sig = 1.0 / (1.0 + jnp.exp(-gl))
    wide = Z_NSA * DH

    sel_all = jnp.where(sel, 0.0, NEG).astype(BF16)
    far_blocks = blk * (TQ // SLC_BLOCK) - NEAR_BACK // SLC_BLOCK
    sel_far = jnp.where(sel & (lane < far_blocks), 0.0, NEG).astype(BF16)
    q_far = jnp.concatenate([qs, jnp.concatenate([sel_far] * Z_NSA, axis=0)], axis=1)
    q_near = jnp.concatenate([qs, jnp.concatenate([sel_all] * Z_NSA, axis=0)], axis=1)
    tk = SEL_FAR_TILE
    last_tile = ks_ref.shape[1] // tk - 1

    def qk(kt):
        k0 = pl.multiple_of(kt * tk, tk)
        return _nt(q_far, jnp.concatenate([ks_ref[0, pl.ds(k0, tk), :], et_ref[pl.ds(k0, tk), :]], axis=1))

    def fold(s, v, st):
        m, acc = st
        mn = jnp.maximum(m, jnp.max(s, axis=1, keepdims=True))
        p = jnp.exp(s - mn).astype(BF16)
        return mn, jnp.exp(m - mn) * acc + _dot(p, v)

    def v_tile(kt):
        k0 = pl.multiple_of(kt * tk, tk)
        return jnp.concatenate([vs_ref[0, pl.ds(k0, tk), :], ones_k], axis=1)

    n_pair = lax.shift_right_logical(jnp.maximum(c0 - NEAR_BACK, 0) + (2 * tk - 1), int(math.log2(2 * tk)))
    sb0_ref[...] = qk(0)

    def far_step(i, st):
        sb1_ref[...] = qk(2 * i + 1)
        st = fold(sb0_ref[...], v_tile(2 * i), st)
        sb0_ref[...] = qk(jnp.minimum(2 * i + 2, last_tile))
        return fold(sb1_ref[...], v_tile(2 * i + 1), st)

    m0 = jnp.full((r4, 1), M_INIT, F32)
    m_s, acc_s = lax.fori_loop(0, n_pair, far_step, (m0, jnp.zeros((r4, 2 * LANES), F32)))
    m_s = m_s + cf_ref[0][:, 0:1]
    n_near = NEAR_BACK + TQ
    p0 = pl.multiple_of(jnp.maximum(c0 - NEAR_BACK, 0), NEAR_BACK)
    d0 = pl.multiple_of(c0, TQ)
    kn = jnp.concatenate([ks_ref[0, pl.ds(p0, NEAR_BACK), :], ks_ref[0, pl.ds(d0, TQ), :]], axis=0)
    vn = jnp.concatenate([vs_ref[0, pl.ds(p0, NEAR_BACK), :], vs_ref[0, pl.ds(d0, TQ), :]], axis=0)
    nrow = lax.broadcasted_iota(I32, (n_near, LANES), 0)
    nlane = lax.broadcasted_iota(I32, (n_near, LANES), 1)
    e_n = jnp.where(lax.shift_right_logical(nrow, 6) + far_blocks == nlane, 1.0, 0.0).astype(BF16)
    lane2 = lax.broadcasted_iota(I32, (1, n_near), 1)
    prev_row = jnp.where((lane2 < NEAR_BACK) & (blk == 0), NEG, 0.0)
    s = _nt(q_near, jnp.concatenate([kn, e_n], axis=1)) + bsn_ref[0] + prev_row
    _, acc_s = fold(s, jnp.concatenate([vn, ones_k[:n_near]], axis=1), (m_s, acc_s))
    o_s = acc_s[:, :LANES] / acc_s[:, LANES:]

    def heads(o):
        return jnp.concatenate([jnp.where(lo_half, o[0:TQ], o[TQ:2 * TQ]),
                                jnp.where(lo_half, o[2 * TQ:3 * TQ], o[3 * TQ:])], axis=1)

    o_ref[0] = (sig[:, 0:wide] * heads(o_c) + sig[:, wide:2 * wide] * heads(o_s)
                + sig[:, 2 * wide:] * heads(o_w))


def _nsa_attention(proj, cmp_out, rel_table):
    bsz, s, _ = proj.shape
    assert s % (2 * SEL_FAR_TILE) == 0 and s >= WINDOW + TQ
    n16 = s // CMP_STRIDE
    n_slc = s // SLC_BLOCK
    assert N_SELECT <= n_slc <= LANES
    nq = s // TQ
    kcp = -(-(max(n16 + KC_FRONT, (TQ // CMP_STRIDE) * (nq - 1) + LANES)) // LANES) * LANES
    pad = ((0, 0), (0, 0), (KC_FRONT, kcp - n16 - KC_FRONT), (0, 0))
    cf = jnp.pad(cmp_out, pad)
    cb = cf.astype(BF16)
    ov = _overlap_padded(n16, kcp, n_slc)
    bcn, bsn, bw, far = _bias_tiles(rel_table)
    eg = jnp.asarray(_gate_expand(), BF16)
    r4 = Z_NSA * TQ
    kv_spec = lambda cb0: pl.BlockSpec((1, s, LANES), lambda b, g, i, cb0=cb0: (b, 0, cb0 + g))
    cmp_spec = lambda j0: pl.BlockSpec((1, 1, kcp, LANES), lambda b, g, i, j0=j0: (b, j0 + g, 0, 0))
    tile_spec = lambda w: pl.BlockSpec((1, r4, w), lambda b, g, i: (g, 0, 0))
    const2 = lambda n: pl.BlockSpec((n, LANES), lambda b, g, i: (0, 0))
    return pl.pallas_call(
        functools.partial(_nsa_kernel, n_slc=n_slc, n_round=N_SELECT - 3),
        grid=(bsz, G_NSA, nq),
        in_specs=[pl.BlockSpec((1, TQ, 2 * LANES), lambda b, g, i: (b, i, CB_NSAQ // 2 + g)),
                  cmp_spec(0), cmp_spec(2), cmp_spec(0), cmp_spec(2),
                  kv_spec(CB_KSLC), kv_spec(CB_VSLC), kv_spec(CB_KWIN), kv_spec(CB_VWIN),
                  const2(s),
                  pl.BlockSpec((1, TQ, LANES), lambda b, g, i: (b, i, CB_GATE)),
                  const2(kcp), const2(kcp),
                  tile_spec(LANES), tile_spec(NEAR_BACK + TQ), tile_spec(WINDOW + TQ), tile_spec(LANES),
                  pl.BlockSpec((1, LANES, 3 * Z_NSA * DH), lambda b, g, i: (g, 0, 0))],
        out_specs=pl.BlockSpec((1, TQ, Z_NSA * DH), lambda b, g, i: (b, i, g)),
        out_shape=jax.ShapeDtypeStruct((bsz, s, N_NSA * DH), F32),
        scratch_shapes=[pltpu.VMEM((r4, SEL_FAR_TILE), F32), pltpu.VMEM((r4, SEL_FAR_TILE), F32)],
        compiler_params=_cparams("arbitrary", "arbitrary", "arbitrary"),
        name="nsa_attn",
    )(proj, cb, cb, cf, cf, proj, proj, proj, proj, jnp.asarray(_block_onehot(s), BF16), proj,
      jnp.asarray(ov, BF16), jnp.asarray(ov, F32), bcn, bsn, bw, far, eg)


def _out_kernel(osb_ref, onsa_ref, x_ref, gsb_ref, gnsa_ref, w_ref, gpost_ref, gate_ref,
                gpre_ref, sc_ref, sh_ref, xo_ref, h_ref):
    half = osb_ref.shape[2]
    a = _rms(osb_ref[0], gsb_ref[...]).astype(BF16)
    b = _rms(onsa_ref[0], gnsa_ref[...]).astype(BF16)
    m = _dot(a, w_ref[:half, :]) + _dot(b, w_ref[half:, :])
    x = x_ref[0] + gate_ref[0] * _rms(m, gpost_ref[...])
    xo_ref[0] = x
    h_ref[0] = (_rms(x, gpre_ref[...]) * (1.0 + sc_ref[0]) + sh_ref[0]).astype(BF16)


def _out_proj(o_sb, o_nsa, x, g_sb, g_nsa, w_out, g_post, gate_m, g_pre_ffn, scale_f, shift_f):
    bsz, s, d = x.shape
    half = o_sb.shape[2]
    tm = min(512, s)
    row = lambda n: pl.BlockSpec((1, n), lambda b, i: (0, 0))
    mod = pl.BlockSpec((1, 1, d), lambda b, i: (b, 0, 0))
    act = lambda n: pl.BlockSpec((1, tm, n), lambda b, i: (b, i, 0))
    return pl.pallas_call(
        _out_kernel,
        grid=(bsz, s // tm),
        in_specs=[act(half), act(half), act(d), row(half), row(half),
                  pl.BlockSpec((2 * half, d), lambda b, i: (0, 0)), row(d), mod, row(d), mod, mod],
        out_specs=[act(d), act(d)],
        out_shape=[jax.ShapeDtypeStruct((bsz, s, d), F32), jax.ShapeDtypeStruct((bsz, s, d), BF16)],
        compiler_params=_cparams("arbitrary", "arbitrary"),
        name="out_proj",
    )(o_sb, o_nsa, x, g_sb.reshape(1, half), g_nsa.reshape(1, half), w_out.astype(BF16),
      g_post.reshape(1, d), gate_m.reshape(bsz, 1, d), g_pre_ffn.reshape(1, d),
      scale_f.reshape(bsz, 1, d), shift_f.reshape(bsz, 1, d))


def _silu(x):
    return x * (1.0 / (1.0 + jnp.exp(-x)))


def _ffn_kernel(h_ref, x_ref, wg_ref, wu_ref, wd_ref, gpost_ref, gate_ref, o_ref):
    h = h_ref[0]
    a = (_silu(_dot(h, wg_ref[...])) * _dot(h, wu_ref[...])).astype(BF16)
    o_ref[0] = x_ref[0] + gate_ref[0] * _rms(_dot(a, wd_ref[...]), gpost_ref[...])


def _dense_ffn(h, x, w_gate, w_up, w_down, g_post, gate_f):
    bsz, s, d = x.shape
    ff = w_gate.shape[1]
    tm = min(512, s)
    act = pl.BlockSpec((1, tm, d), lambda b, i: (b, i, 0))
    once = pl.Buffered(1)
    return pl.pallas_call(
        _ffn_kernel,
        grid=(bsz, s // tm),
        in_specs=[act, act,
                  pl.BlockSpec((d, ff), lambda b, i: (0, 0), pipeline_mode=once),
                  pl.BlockSpec((d, ff), lambda b, i: (0, 0), pipeline_mode=once),
                  pl.BlockSpec((ff, d), lambda b, i: (0, 0), pipeline_mode=once),
                  pl.BlockSpec((1, d), lambda b, i: (0, 0)),
                  pl.BlockSpec((1, 1, d), lambda b, i: (b, 0, 0))],
        out_specs=act,
        out_shape=jax.ShapeDtypeStruct((bsz, s, d), F32),
        compiler_params=_cparams("arbitrary", "arbitrary"),
        name="dense_ffn",
    )(h, x, w_gate.astype(BF16), w_up.astype(BF16), w_down.astype(BF16),
      g_post.reshape(1, d), gate_f.reshape(bsz, 1, d))


def _router_kernel(x_ref, gpre_ref, sc_ref, sh_ref, wr_ref, br_ref, pos_ref, w_ref, post_ref, cnt_ref):
    tile = x_ref.shape[1]
    h = _rms(x_ref[0], gpre_ref[...]) * (1.0 + sc_ref[0]) + sh_ref[0]
    logits = _dot_f32(h, wr_ref[...]) + br_ref[...]
    lane = lax.broadcasted_iota(I32, (tile, LANES), 1)
    lanef = lane.astype(F32)
    e = jnp.exp(logits - jnp.max(logits, axis=1, keepdims=True))
    probs = e / jnp.sum(e, axis=1, keepdims=True)
    probs = jnp.where(lane < N_EXPERTS, probs, -1.0)
    m1 = jnp.max(probs, axis=1, keepdims=True)
    i1 = jnp.min(jnp.where(probs == m1, lanef, 1e9), axis=1, keepdims=True)
    rest = jnp.where(lanef == i1, -1.0, probs)
    m2 = jnp.max(rest, axis=1, keepdims=True)
    i2 = jnp.min(jnp.where(rest == m2, lanef, 1e9), axis=1, keepdims=True)
    tot = m1 + m2
    wgt = jnp.where(lanef == i1, m1 / tot, jnp.where(lanef == i2, m2 / tot, 0.0))
    mask = (lanef == i1) | (lanef == i2)
    ch = 256
    r = lax.broadcasted_iota(I32, (ch, ch), 0)
    c = lax.broadcasted_iota(I32, (ch, ch), 1)
    lower = jnp.where(c < r, 1.0, 0.0).astype(BF16)
    ones = jnp.ones((8, ch), BF16)
    carry = jnp.zeros((1, LANES), F32)
    ranks = []
    for k in range(tile // ch):
        mk = jnp.where(mask[k * ch:(k + 1) * ch], 1.0, 0.0).astype(BF16)
        ranks.append(_dot(lower, mk) + carry)
        carry = carry + _dot(ones, mk)[0:1]
    rank = jnp.concatenate(ranks, axis=0)
    pos = jnp.where(mask, rank, -1.0)
    pos_ref[0] = pos.astype(I32)
    w_ref[0] = wgt
    post_ref[0] = jnp.transpose(pos)[0:N_EXPERTS].astype(I32)
    cnt_ref[0] = jnp.broadcast_to(carry, (8, LANES)).astype(I32)


def _router(x, g_pre, scale_f, shift_f, w_router, b_router, tile):
    bsz, s, d = x.shape
    nt = s // tile
    wr = jnp.zeros((d, LANES), F32).at[:, :N_EXPERTS].set(w_router.astype(F32))
    br = jnp.full((1, LANES), NEG, F32).at[0, :N_EXPERTS].set(b_router.astype(F32))
    mod = pl.BlockSpec((1, 1, d), lambda b, i: (b, 0, 0))
    tok = lambda n: pl.BlockSpec((1, tile, n), lambda b, i: (b * nt + i, 0, 0))
    pos, wgt, post, cnt = pl.pallas_call(
        _router_kernel,
        grid=(bsz, nt),
        in_specs=[pl.BlockSpec((1, tile, d), lambda b, i: (b, i, 0)),
                  pl.BlockSpec((1, d), lambda b, i: (0, 0)), mod, mod,
                  pl.BlockSpec((d, LANES), lambda b, i: (0, 0)),
                  pl.BlockSpec((1, LANES), lambda b, i: (0, 0))],
        out_specs=[tok(LANES), tok(LANES),
                   pl.BlockSpec((1, N_EXPERTS, tile), lambda b, i: (b * nt + i, 0, 0)),
                   pl.BlockSpec((1, 8, LANES), lambda b, i: (b * nt + i, 0, 0))],
        out_shape=[jax.ShapeDtypeStruct((bsz * nt, tile, LANES), I32),
                   jax.ShapeDtypeStruct((bsz * nt, tile, LANES), F32),
                   jax.ShapeDtypeStruct((bsz * nt, N_EXPERTS, tile), I32),
                   jax.ShapeDtypeStruct((bsz * nt, 8, LANES), I32)],
        compiler_params=_cparams("arbitrary", "arbitrary"),
        name="moe_router",
    )(x, g_pre.reshape(1, d), scale_f.reshape(bsz, 1, d), shift_f.reshape(bsz, 1, d), wr, br)
    return pos, wgt, post, cnt[:, 0, :N_EXPERTS]


def _moe_kernel(cnt_ref, h_ref, pos_ref, w_ref, post_ref, wg_ref, wu_ref, wd_ref, o_ref,
                xg_ref, y_ref, *, rc):
    i, e, j = pl.program_id(0), pl.program_id(1), pl.program_id(2)
    tile = h_ref.shape[1]
    n_chunk = lax.shift_right_logical(cnt_ref[i * N_EXPERTS + e] + (rc - 1), int(math.log2(rc)))
    lane = lax.broadcasted_iota(I32, (tile, LANES), 1)

    @pl.when(jnp.logical_and(e == 0, j == 0))
    def _():
        o_ref[0] = jnp.zeros(o_ref.shape[1:], F32)

    @pl.when(j == 0)
    def _():
        prow = post_ref[0]
        rid = lax.broadcasted_iota(I32, (rc, tile), 0)

        def gather(c, _):
            onehot = jnp.where(prow == rid + c * rc, 1.0, 0.0).astype(BF16)
            r0 = pl.multiple_of(c * rc, rc)
            xg_ref[pl.ds(r0, rc), :] = _dot(onehot, h_ref[0]).astype(BF16)
            return 0

        lax.fori_loop(0, n_chunk, gather, 0)

    def expert(c, _):
        r0 = pl.multiple_of(c * rc, rc)
        xg = xg_ref[pl.ds(r0, rc), :]
        a = (_silu(_dot(xg, wg_ref[0])) * _dot(xg, wu_ref[0])).astype(BF16)
        part = _dot(a, wd_ref[0])

        @pl.when(j == 0)
        def _():
            y_ref[pl.ds(r0, rc), :] = part

        @pl.when(j > 0)
        def _():
            y_ref[pl.ds(r0, rc), :] += part

        return 0

    lax.fori_loop(0, n_chunk, expert, 0)

    @pl.when(j == pl.num_programs(2) - 1)
    def _():
        pcol = jnp.sum(jnp.where(lane == e, pos_ref[0], 0), axis=1, keepdims=True)
        wcol = jnp.sum(jnp.where(lane == e, w_ref[0], 0.0), axis=1, keepdims=True)
        cid = lax.broadcasted_iota(I32, (tile, rc), 1)

        def scatter(c, z):
            r0 = pl.multiple_of(c * rc, rc)
            onehot_t = jnp.where(pcol == cid + c * rc, 1.0, 0.0).astype(BF16)
            return z + _dot(onehot_t, y_ref[pl.ds(r0, rc), :].astype(BF16))

        z = lax.fori_loop(0, n_chunk, scatter, jnp.zeros((tile, o_ref.shape[2]), F32))
        o_ref[0] += wcol * z


def _moe_ffn(h, pos, wgt, post, cnt, w_gate, w_up, w_down, tile, rc=128, tf=512):
    t_tokens, d = h.shape
    nt = t_tokens // tile
    ff = w_gate.shape[2]
    assert ff % tf == 0
    hb = h.reshape(nt, tile, d)
    post3 = post.reshape(nt * N_EXPERTS, 1, tile)
    grid_spec = pltpu.PrefetchScalarGridSpec(
        num_scalar_prefetch=1,
        grid=(nt, N_EXPERTS, ff // tf),
        in_specs=[pl.BlockSpec((1, tile, d), lambda i, e, j, c: (i, 0, 0)),
                  pl.BlockSpec((1, tile, LANES), lambda i, e, j, c: (i, 0, 0)),
                  pl.BlockSpec((1, tile, LANES), lambda i, e, j, c: (i, 0, 0)),
                  pl.BlockSpec((1, 1, tile), lambda i, e, j, c: (i * N_EXPERTS + e, 0, 0)),
                  pl.BlockSpec((1, d, tf), lambda i, e, j, c: (e, 0, j)),
                  pl.BlockSpec((1, d, tf), lambda i, e, j, c: (e, 0, j)),
                  pl.BlockSpec((1, tf, d), lambda i, e, j, c: (e, j, 0))],
        out_specs=pl.BlockSpec((1, tile, d), lambda i, e, j, c: (i, 0, 0)),
        scratch_shapes=[pltpu.VMEM((tile, d), BF16), pltpu.VMEM((tile, d), F32)])
    out = pl.pallas_call(
        functools.partial(_moe_kernel, rc=rc),
        grid_spec=grid_spec,
        out_shape=jax.ShapeDtypeStruct((nt, tile, d), F32),
        compiler_params=_cparams("arbitrary", "arbitrary", "arbitrary"),
        name="moe_ffn",
    )(cnt.reshape(-1), hb, pos, wgt, post3, w_gate.astype(BF16), w_up.astype(BF16), w_down.astype(BF16))
    return out.reshape(t_tokens, d)


MOE_RC = 128
MOE_GC = 256
MOE_SC = 1024


def _moe_rows(tile):
    return -(-(2 * tile + N_EXPERTS * MOE_RC) // MOE_SC) * MOE_SC


def _route_kernel(x_ref, gpre_ref, sc_ref, sh_ref, wr_ref, br_ref, d_ref, w_ref, dt_ref, off_ref, nch_ref):
    tile = x_ref.shape[1]
    h = _rms(x_ref[0], gpre_ref[...]) * (1.0 + sc_ref[0]) + sh_ref[0]
    logits = _dot_f32(h, wr_ref[...]) + br_ref[...]
    lane = lax.broadcasted_iota(I32, (tile, LANES), 1)
    lanef = lane.astype(F32)
    e = jnp.exp(logits - jnp.max(logits, axis=1, keepdims=True))
    probs = e / jnp.sum(e, axis=1, keepdims=True)
    probs = jnp.where(lane < N_EXPERTS, probs, -1.0)
    m1 = jnp.max(probs, axis=1, keepdims=True)
    i1 = jnp.min(jnp.where(probs == m1, lanef, 1e9), axis=1, keepdims=True)
    rest = jnp.where(lanef == i1, -1.0, probs)
    m2 = jnp.max(rest, axis=1, keepdims=True)
    i2 = jnp.min(jnp.where(rest == m2, lanef, 1e9), axis=1, keepdims=True)
    tot = m1 + m2
    hit1 = lanef == i1
    hit2 = lanef == i2
    mask = hit1 | hit2
    ch = 256
    r = lax.broadcasted_iota(I32, (ch, ch), 0)
    c = lax.broadcasted_iota(I32, (ch, ch), 1)
    lower = jnp.where(c < r, 1.0, 0.0).astype(BF16)
    ones = jnp.ones((8, ch), BF16)
    count = jnp.zeros((1, LANES), F32)
    ranks = []
    for k in range(tile // ch):
        mk = jnp.where(mask[k * ch:(k + 1) * ch], 1.0, 0.0).astype(BF16)
        ranks.append(_dot(lower, mk) + count)
        count = count + _dot(ones, mk)[0:1]
    rank = jnp.concatenate(ranks, axis=0)
    shift = int(math.log2(MOE_RC))
    nch = lax.shift_right_logical(count.astype(I32) + (MOE_RC - 1), shift)
    nch8 = jnp.broadcast_to(nch.astype(F32), (8, LANES))
    lr = lax.broadcasted_iota(I32, (LANES, LANES), 0)
    lc = lax.broadcasted_iota(I32, (LANES, LANES), 1)
    offc = _dot(nch8.astype(BF16), jnp.where(lr < lc, 1.0, 0.0).astype(BF16))
    row = offc[0:1] * float(MOE_RC) + rank
    d1 = jnp.sum(jnp.where(hit1, row, 0.0), axis=1, keepdims=True)
    d2 = jnp.sum(jnp.where(hit2, row, 0.0), axis=1, keepdims=True)
    dd = jnp.where(lane == 0, d1, jnp.where(lane == 1, d2, -1.0))
    d_ref[0] = dd.astype(I32)
    w_ref[0] = jnp.where(lane == 0, m1 / tot, jnp.where(lane == 1, m2 / tot, 0.0))
    dt_ref[0] = jnp.transpose(dd)[0:8].astype(I32)
    off_ref[0] = offc.astype(I32)
    nch_ref[0] = jnp.broadcast_to(nch, (8, LANES))


def _route(x, g_pre, scale_f, shift_f, w_router, b_router, tile):
    bsz, s, d = x.shape
    nt = s // tile
    wr = jnp.zeros((d, LANES), F32).at[:, :N_EXPERTS].set(w_router.astype(F32))
    br = jnp.full((1, LANES), NEG, F32).at[0, :N_EXPERTS].set(b_router.astype(F32))
    mod = pl.BlockSpec((1, 1, d), lambda b, i: (b, 0, 0))
    tok = pl.BlockSpec((1, tile, LANES), lambda b, i: (b * nt + i, 0, 0))
    small = pl.BlockSpec((1, 8, LANES), lambda b, i: (b * nt + i, 0, 0))
    dest, wgt, dest_t, offc, nch = pl.pallas_call(
        _route_kernel,
        grid=(bsz, nt),
        in_specs=[pl.BlockSpec((1, tile, d), lambda b, i: (b, i, 0)),
                  pl.BlockSpec((1, d), lambda b, i: (0, 0)), mod, mod,
                  pl.BlockSpec((d, LANES), lambda b, i: (0, 0)),
                  pl.BlockSpec((1, LANES), lambda b, i: (0, 0))],
        out_specs=[tok, tok, pl.BlockSpec((1, 8, tile), lambda b, i: (b * nt + i, 0, 0)), small, small],
        out_shape=[jax.ShapeDtypeStruct((bsz * nt, tile, LANES), I32),
                   jax.ShapeDtypeStruct((bsz * nt, tile, LANES), F32),
                   jax.ShapeDtypeStruct((bsz * nt, 8, tile), I32),
                   jax.ShapeDtypeStruct((bsz * nt, 8, LANES), I32),
                   jax.ShapeDtypeStruct((bsz * nt, 8, LANES), I32)],
        compiler_params=_cparams("arbitrary", "arbitrary"),
        name="moe_router",
    )(x, g_pre.reshape(1, d), scale_f.reshape(bsz, 1, d), shift_f.reshape(bsz, 1, d), wr, br)
    return dest, wgt, dest_t, offc[:, 0, :N_EXPERTS].reshape(-1), nch[:, 0, :N_EXPERTS].reshape(-1)


def _experts_kernel(off_ref, nch_ref, h_ref, d_ref, w_ref, dt_ref, wg_ref, wu_ref, wd_ref, o_ref,
                    xg_ref, y_ref, wr_ref):
    i, e, j = pl.program_id(0), pl.program_id(1), pl.program_id(2)
    tile = h_ref.shape[1]
    rows = xg_ref.shape[0]

    @pl.when(jnp.logical_and(e == 0, j == 0))
    def _():
        y_ref[...] = jnp.zeros(y_ref.shape, F32)
        d1t = dt_ref[0, 0:1, :]
        d2t = dt_ref[0, 1:2, :]
        w = w_ref[0]
        lane = lax.broadcasted_iota(I32, (tile, LANES), 1)

        def pieces(col):
            hi, mid, lo = _split3(jnp.broadcast_to(col, (tile, LANES)))
            z = jnp.zeros((tile, LANES), F32)
            return jnp.where(lane == 0, hi.astype(F32), jnp.where(lane == 1, mid.astype(F32),
                             jnp.where(lane == 2, lo.astype(F32), z))).astype(BF16)

        wp1 = pieces(w[:, 0:1])
        wp2 = pieces(w[:, 1:2])
        hb = h_ref[0]
        rid = lax.broadcasted_iota(I32, (MOE_GC, tile), 0)
        for c in range(rows // MOE_GC):
            p1 = jnp.where(rid + c * MOE_GC == d1t, 1.0, 0.0).astype(BF16)
            p2 = jnp.where(rid + c * MOE_GC == d2t, 1.0, 0.0).astype(BF16)
            xg_ref[c * MOE_GC:(c + 1) * MOE_GC, :] = _dot(p1 + p2, hb).astype(BF16)
            wrow = jnp.sum(_dot(p1, wp1) + _dot(p2, wp2), axis=1, keepdims=True)
            wr_ref[c * MOE_GC:(c + 1) * MOE_GC, :] = jnp.broadcast_to(wrow, (MOE_GC, LANES))

    base = off_ref[i * N_EXPERTS + e] * MOE_RC
    n = nch_ref[i * N_EXPERTS + e]

    def ffn(r0, m):
        xg = xg_ref[pl.ds(r0, m), :]
        a = (_silu(_dot(xg, wg_ref[0])) * _dot(xg, wu_ref[0])).astype(BF16)
        y_ref[pl.ds(r0, m), :] += _dot(a, wd_ref[0])

    def pair(c, _):
        ffn(pl.multiple_of(base + c * (2 * MOE_RC), MOE_RC), 2 * MOE_RC)
        return 0

    lax.fori_loop(0, lax.shift_right_logical(n, 1), pair, 0)

    @pl.when((n & 1) == 1)
    def _():
        ffn(pl.multiple_of(base + (n - 1) * MOE_RC, MOE_RC), MOE_RC)

    @pl.when(jnp.logical_and(e == pl.num_programs(1) - 1, j == pl.num_programs(2) - 1))
    def _():
        dd = d_ref[0]
        d1c = dd[:, 0:1]
        d2c = dd[:, 1:2]
        cid = lax.broadcasted_iota(I32, (tile, MOE_SC), 1)
        z = None
        for k in range(rows // MOE_SC):
            pt = jnp.where((cid + k * MOE_SC == d1c) | (cid + k * MOE_SC == d2c), 1.0, 0.0).astype(BF16)
            yb = (y_ref[k * MOE_SC:(k + 1) * MOE_SC, :] * wr_ref[k * MOE_SC:(k + 1) * MOE_SC, 0:1]).astype(BF16)
            part = _dot(pt, yb)
            z = part if z is None else z + part
        o_ref[0] = z


def _experts(h, dest, wgt, dest_t, offc, nch, w_gate, w_up, w_down, tile, tf=512):
    t_tokens, d = h.shape
    nt = t_tokens // tile
    ff = w_gate.shape[2]
    assert ff % tf == 0 and tf % LANES == 0
    rows = _moe_rows(tile)
    tok = lambda n: pl.BlockSpec((1, tile, n), lambda i, e, j, o, c: (i, 0, 0))
    grid_spec = pltpu.PrefetchScalarGridSpec(
        num_scalar_prefetch=2,
        grid=(nt, N_EXPERTS, ff // tf),
        in_specs=[tok(d), tok(LANES), tok(LANES),
                  pl.BlockSpec((1, 8, tile), lambda i, e, j, o, c: (i, 0, 0)),
                  pl.BlockSpec((1, d, tf), lambda i, e, j, o, c: (e, 0, j)),
                  pl.BlockSpec((1, d, tf), lambda i, e, j, o, c: (e, 0, j)),
                  pl.BlockSpec((1, tf, d), lambda i, e, j, o, c: (e, j, 0))],
        out_specs=tok(d),
        scratch_shapes=[pltpu.VMEM((rows, d), BF16), pltpu.VMEM((rows, d), F32),
                        pltpu.VMEM((rows, LANES), F32)])
    out = pl.pallas_call(
        _experts_kernel,
        grid_spec=grid_spec,
        out_shape=jax.ShapeDtypeStruct((nt, tile, d), F32),
        compiler_params=_cparams("arbitrary", "arbitrary", "arbitrary"),
        name="moe_ffn",
    )(offc, nch, h.reshape(nt, tile, d), dest, wgt, dest_t,
      w_gate.astype(BF16), w_up.astype(BF16), w_down.astype(BF16))
    return out.reshape(t_tokens, d)


def _post_kernel(x_ref, f_ref, gpost_ref, gate_ref, o_ref):
    o_ref[0] = x_ref[0] + gate_ref[0] * _rms(f_ref[0], gpost_ref[...])


def _post_residual(x, f, g_post, gate_f):
    bsz, s, d = x.shape
    tm = min(1024, s)
    act = pl.BlockSpec((1, tm, d), lambda b, i: (b, i, 0))
    return pl.pallas_call(
        _post_kernel,
        grid=(bsz, s // tm),
        in_specs=[act, act, pl.BlockSpec((1, d), lambda b, i: (0, 0)),
                  pl.BlockSpec((1, 1, d), lambda b, i: (b, 0, 0))],
        out_specs=act,
        out_shape=jax.ShapeDtypeStruct((bsz, s, d), F32),
        compiler_params=_cparams("arbitrary", "arbitrary"),
        name="post_residual",
    )(x, f, g_post.reshape(1, d), gate_f.reshape(bsz, 1, d))


def _mixer(x, mod, layer, rel_table, g_pre_mix, w_in, cmp_params):
    shift_m, scale_m = mod[:, 0], mod[:, 1]
    proj = _in_proj(x, g_pre_mix, scale_m, shift_m, _arrange_w_in(w_in))
    o_sb = _sb_attention(proj)
    o_nsa = _nsa_attention(proj, _compress(proj, *cmp_params), rel_table)
    return o_sb, o_nsa


def kernel(x, c, rel_table, w_ada, b_ada, g_pre_mix, g_post_mix, g_pre_ffn, g_post_ffn, w_in, w_out, g_sb, g_nsa, cmp_pos_k, cmp_w1_k, cmp_w2_k, cmp_pos_v, cmp_w1_v, cmp_w2_v, ffn_w_gate, ffn_w_up, ffn_w_down, moe_w_router, moe_b_router, moe_w_gate, moe_w_up, moe_w_down):
    bsz, s, d = x.shape
    depth = w_in.shape[0]
    mods = _ada(c, w_ada, b_ada).reshape(depth, bsz, 6, d)
    moe_tile = min(1024, s)
    for layer in range(depth):
        mod = mods[layer]
        cmp_params = (cmp_pos_k[layer], cmp_w1_k[layer], cmp_w2_k[layer],
                      cmp_pos_v[layer], cmp_w1_v[layer], cmp_w2_v[layer])
        o_sb, o_nsa = _mixer(x, mod, layer, rel_table, g_pre_mix[layer], w_in[layer], cmp_params)
        x, h = _out_proj(o_sb, o_nsa, x, g_sb[layer], g_nsa[layer], w_out[layer], g_post_mix[layer],
                         mod[:, 2], g_pre_ffn[layer], mod[:, 4], mod[:, 3])
        i = layer // 2
        if layer % 2 == 0:
            x = _dense_ffn(h, x, ffn_w_gate[i], ffn_w_up[i], ffn_w_down[i], g_post_ffn[layer], mod[:, 5])
        else:
            routing = _route(x, g_pre_ffn[layer], mod[:, 4], mod[:, 3],
                             moe_w_router[i], moe_b_router[i], moe_tile)
            f = _experts(h.reshape(bsz * s, d), *routing,
                         moe_w_gate[i], moe_w_up[i], moe_w_down[i], moe_tile)
            x = _post_residual(x, f.reshape(bsz, s, d), g_post_ffn[layer], mod[:, 5])
    return x
```

```python
import functools
import math

import numpy as np
import jax
import jax.numpy as jnp
from jax import lax
from jax.experimental import pallas as pl
from jax.experimental.pallas import tpu as pltpu

F32 = jnp.float32
BF16 = jnp.bfloat16
I32 = jnp.int32

LANES = 128
DH = 64
N_SB = 8
N_NSA = 8
G_NSA = 2
Z_NSA = 4
CMP_BLOCK = 32
CMP_STRIDE = 16
SLC_BLOCK = 64
N_SELECT = 16
WINDOW = 512
REL_BUCKETS = 32
REL_MAX_DIST = 128
N_EXPERTS = 8
EPS = 1e-6
FORCED = 1e4
NEG = -1e30
M_INIT = -1e29
SB_EXIT = -104.5
VMEM_LIMIT = 56 * 1024 * 1024

TQ = 256
NEAR_BACK = REL_MAX_DIST
KC_FRONT = 16
SEL_FAR_TILE = 512


def _cparams(*sem):
    return pltpu.CompilerParams(dimension_semantics=sem, vmem_limit_bytes=VMEM_LIMIT)


def _nt(a, b):
    return lax.dot_general(a, b, (((1,), (1,)), ((), ())), preferred_element_type=F32)


def _dot(a, b):
    return jnp.dot(a, b, preferred_element_type=F32)


def _split3(a):
    hi = a.astype(BF16)
    r = a - hi.astype(F32)
    mid = r.astype(BF16)
    lo = (r - mid.astype(F32)).astype(BF16)
    return hi, mid, lo


def _dot_hl(a, b):
    hi = a.astype(BF16)
    lo = (a - hi.astype(F32)).astype(BF16)
    return _dot(hi, b) + _dot(lo, b)


def _dot_f32(a, b):
    ah, am, al = _split3(a)
    bh, bm, bl = _split3(b)
    return (_dot(ah, bh) + (_dot(ah, bm) + _dot(am, bh))
            + (_dot(ah, bl) + _dot(am, bm) + _dot(al, bh)))


def _rms(x, g):
    return x * lax.rsqrt(jnp.mean(x * x, axis=-1, keepdims=True) + EPS) * g


def _ada_kernel(c_ref, w_ref, b_ref, o_ref):
    c = c_ref[...]
    ca = c * (1.0 / (1.0 + jnp.exp(-c)))
    o_ref[0] = _dot_f32(ca, w_ref[0]) + b_ref[0]


def _ada(c, w_ada, b_ada):
    depth, d, n = w_ada.shape
    bsz = c.shape[0]
    rows = 8
    tn = 1536
    cp = jnp.zeros((rows, d), F32).at[:bsz].set(c)
    out = pl.pallas_call(
        _ada_kernel,
        grid=(depth, n // tn),
        in_specs=[pl.BlockSpec((rows, d), lambda l, j: (0, 0)),
                  pl.BlockSpec((1, d, tn), lambda l, j: (l, 0, j)),
                  pl.BlockSpec((1, 1, tn), lambda l, j: (l, 0, j))],
        out_specs=pl.BlockSpec((1, rows, tn), lambda l, j: (l, 0, j)),
        out_shape=jax.ShapeDtypeStruct((depth, rows, n), F32),
        compiler_params=_cparams("arbitrary", "arbitrary"),
        name="ada_mod",
    )(cp, w_ada, b_ada.reshape(depth, 1, n))
    return out[:, :bsz]


def _in_kernel(x_ref, g_ref, sc_ref, sh_ref, w_ref, o_ref, *, cn):
    h = _rms(x_ref[0], g_ref[...]) * (1.0 + sc_ref[0]) + sh_ref[0]
    hb = h.astype(BF16)
    for j in range(w_ref.shape[1] // cn):
        o_ref[0, :, j * cn:(j + 1) * cn] = _dot(hb, w_ref[:, j * cn:(j + 1) * cn]).astype(BF16)


def _in_proj(x, g, scale, shift, w):
    bsz, s, d = x.shape
    nc = w.shape[1]
    tm = min(512, s)
    cn = nc // 3 if (nc // 3) % LANES == 0 else nc
    return pl.pallas_call(
        functools.partial(_in_kernel, cn=cn),
        grid=(bsz, s // tm),
        in_specs=[pl.BlockSpec((1, tm, d), lambda b, i: (b, i, 0)),
                  pl.BlockSpec((1, d), lambda b, i: (0, 0)),
                  pl.BlockSpec((1, 1, d), lambda b, i: (b, 0, 0)),
                  pl.BlockSpec((1, 1, d), lambda b, i: (b, 0, 0)),
                  pl.BlockSpec((d, nc), lambda b, i: (0, 0))],
        out_specs=pl.BlockSpec((1, tm, nc), lambda b, i: (b, i, 0)),
        out_shape=jax.ShapeDtypeStruct((bsz, s, nc), BF16),
        compiler_params=_cparams("arbitrary", "arbitrary"),
        name="in_proj",
    )(x, g.reshape(1, d), scale.reshape(bsz, 1, d), shift.reshape(bsz, 1, d), w)


CB_SBQ, CB_SBK, CB_SBV, CB_NSAQ = 0, 4, 8, 12
CB_KCMP, CB_VCMP = 16, 17
CB_KSLC, CB_VSLC, CB_KWIN, CB_VWIN = 18, 20, 22, 24
CB_GATE = 26
N_CB = 27


def _arrange_w_in(w_in):
    d = w_in.shape[0]
    w_sb = N_SB * DH
    off_nsa_q = 3 * w_sb
    off_kv = off_nsa_q + N_NSA * DH
    off_gate = off_kv + 3 * 2 * G_NSA * DH
    scale = DH ** -0.5

    def kv(br, kvi, g):
        lo = off_kv + ((br * 2 + kvi) * G_NSA + g) * DH
        return w_in[:, lo:lo + DH]

    cols = [w_in[:, 0:w_sb] * scale, w_in[:, w_sb:2 * w_sb], w_in[:, 2 * w_sb:3 * w_sb],
            w_in[:, off_nsa_q:off_kv] * scale,
            kv(0, 0, 0), kv(0, 0, 1), kv(0, 1, 0), kv(0, 1, 1)]
    for br in (1, 2):
        for kvi in (0, 1):
            for g in range(G_NSA):
                cols += [kv(br, kvi, g), kv(br, kvi, g)]
    n_gate = 3 * N_NSA
    cols += [w_in[:, off_gate:off_gate + n_gate], jnp.zeros((d, LANES - n_gate), w_in.dtype)]
    out = jnp.concatenate(cols, axis=1).astype(BF16)
    assert out.shape[1] == N_CB * LANES
    return out


def _sb_kernel(q_ref, k_ref, v_ref, o_ref, *, t):
    qi = pl.program_id(2)
    q = q_ref[0]
    lane = lax.broadcasted_iota(I32, (t, LANES), 1)
    row = lax.broadcasted_iota(I32, (t, t), 0)
    col = lax.broadcasted_iota(I32, (t, t), 1)
    upper = jnp.where(row > col, 1.0, 0.0).astype(BF16)
    upper2 = jnp.concatenate([upper, upper], axis=0)
    ones2 = jnp.ones((2 * t, LANES), BF16)
    causal = col < row
    rep = t // LANES

    def tile(qh, k, v, carry, acc, diag):
        z = _nt(qh, k)
        lk = -(jnp.maximum(z, 0.0) + jnp.log(1.0 + jnp.exp(-jnp.abs(z))))
        if diag:
            lk = jnp.where(causal, lk, 0.0)
        hi = lk.astype(BF16)
        lo = (lk - hi.astype(F32)).astype(BF16)
        hl = jnp.concatenate([hi, lo], axis=1)
        later = _dot(hl, upper2)
        tot = _dot(hl, ones2)
        cb = carry if rep == 1 else jnp.concatenate([carry] * rep, axis=1)
        w = jnp.exp(z + lk + later + cb)
        if diag:
            w = jnp.where(causal, w, 0.0)
        return carry + tot, acc + _dot(w.astype(BF16), v)

    zq = jnp.zeros_like(q)
    qhs = (jnp.where(lane < DH, q, zq), jnp.where(lane < DH, zq, q))
    zero = jnp.zeros((t, LANES), F32)

    def both(kt, st, diag):
        k = k_ref[0, pl.ds(kt * t, t), :]
        v = v_ref[0, pl.ds(kt * t, t), :]
        c0, a0 = tile(qhs[0], k, v, st[0], st[1], diag)
        c1, a1 = tile(qhs[1], k, v, st[2], st[3], diag)
        return jnp.maximum(jnp.max(c0), jnp.max(c1)), (c0, a0, c1, a1)

    mx, st = both(qi, (zero, zero, zero, zero), True)

    def cond(s):
        return jnp.logical_and(s[0] >= 0, s[1] > SB_EXIT)

    def body(s):
        mx, st = both(s[0], s[2], False)
        return s[0] - 1, mx, st

    _, _, st = lax.while_loop(cond, body, (qi - 1, mx, st))
    o_ref[0] = jnp.where(lane < DH, st[1], st[3])


def _sb_attention(proj, t=256):
    bsz, s, _ = proj.shape
    t = min(t, s)
    npair = N_SB // 2
    return pl.pallas_call(
        functools.partial(_sb_kernel, t=t),
        grid=(bsz, npair, s // t),
        in_specs=[pl.BlockSpec((1, t, LANES), lambda b, j, i: (b, i, CB_SBQ + j)),
                  pl.BlockSpec((1, s, LANES), lambda b, j, i: (b, 0, CB_SBK + j)),
                  pl.BlockSpec((1, s, LANES), lambda b, j, i: (b, 0, CB_SBV + j))],
        out_specs=pl.BlockSpec((1, t, LANES), lambda b, j, i: (b, i, j)),
        out_shape=jax.ShapeDtypeStruct((bsz, s, N_SB * DH), F32),
        compiler_params=_cparams("arbitrary", "arbitrary", "arbitrary"),
        name="sb_attn",
    )(proj, proj, proj)


def _gelu_tanh(x):
    return 0.5 * x * (1.0 + jnp.tanh(math.sqrt(2.0 / math.pi) * (x + 0.044715 * (x * x * x))))


def _cmp_kernel(xa_ref, xb_ref, pos_ref, w1_ref, w2_ref, o_ref):
    half = w1_ref.shape[1] // 2
    w1a = w1_ref[0, :half, :]
    w1b = w1_ref[0, half:, :]
    pos = pos_ref[0]
    bias = _dot(pos[:, :half], w1a) + _dot(pos[:, half:], w1b)
    hid = _dot(xa_ref[0, 0], w1a) + _dot(xb_ref[0, 0], w1b) + bias[0:1, :]
    o_ref[0, 0] = _dot(_gelu_tanh(hid).astype(BF16), w2_ref[0])


def _compress(proj, pos_k, w1_k, w2_k, pos_v, w1_v, w2_v):
    bsz, s, _ = proj.shape
    n16 = s // CMP_STRIDE
    raw = proj[:, :, CB_KCMP * LANES:(CB_VCMP + 1) * LANES]
    x16 = raw.reshape(bsz, s, 4, DH).transpose(0, 2, 1, 3).reshape(bsz, 4, n16, CMP_STRIDE * DH)
    x16b = jnp.concatenate([x16[:, :, 1:], jnp.zeros_like(x16[:, :, :1])], axis=2)
    hidden = w1_k.shape[1]
    w1 = jnp.stack([w1_k, w1_v]).astype(BF16)
    w2 = jnp.stack([jnp.concatenate([w2_k, w2_k], 1), jnp.concatenate([w2_v, w2_v], 1)]).astype(BF16)
    pos = jnp.stack([pos_k.reshape(1, -1), pos_v.reshape(1, -1)])
    pos = jnp.concatenate([pos, jnp.zeros((2, 7, pos.shape[2]), pos.dtype)], axis=1).astype(BF16)
    k16 = CMP_STRIDE * DH
    return pl.pallas_call(
        _cmp_kernel,
        grid=(bsz, 4),
        in_specs=[pl.BlockSpec((1, 1, n16, k16), lambda b, j: (b, j, 0, 0)),
                  pl.BlockSpec((1, 1, n16, k16), lambda b, j: (b, j, 0, 0)),
                  pl.BlockSpec((1, 8, 2 * k16), lambda b, j: (j // 2, 0, 0)),
                  pl.BlockSpec((1, 2 * k16, hidden), lambda b, j: (j // 2, 0, 0)),
                  pl.BlockSpec((1, hidden, LANES), lambda b, j: (j // 2, 0, 0))],
        out_specs=pl.BlockSpec((1, 1, n16, LANES), lambda b, j: (b, j, 0, 0)),
        out_shape=jax.ShapeDtypeStruct((bsz, 4, n16, LANES), F32),
        compiler_params=_cparams("arbitrary", "arbitrary"),
        name="nsa_compress",
    )(x16, x16b, pos, w1, w2)


def _bucket_table():
    n = np.arange(REL_MAX_DIST + 1)
    exact = REL_BUCKETS // 2
    val = (np.log(np.maximum(n, 1).astype(np.float32) / np.float32(exact)).astype(np.float32)
           / np.float32(math.log(REL_MAX_DIST / exact)) * np.float32(REL_BUCKETS - exact))
    large = np.minimum(exact + val.astype(np.int32), REL_BUCKETS - 1)
    return np.where(n < exact, n, large).astype(np.int32)


def _bias_tiles(rel_table):
    bucket = _bucket_table()
    far_row = rel_table[int(bucket[REL_MAX_DIST])].astype(F32)
    r = np.arange(TQ)[:, None]

    def tile(dist, valid, base):
        onehot = jax.nn.one_hot(bucket[np.clip(dist, 0, REL_MAX_DIST)], REL_BUCKETS, dtype=F32)
        t = jnp.einsum('rwb,bh->hrw', onehot, rel_table.astype(F32),
                       precision=lax.Precision.HIGHEST)
        t = jnp.where(valid[None], t - base, NEG)
        return t.reshape(G_NSA, Z_NSA * TQ, dist.shape[1])

    w = np.arange(LANES)[None, :]
    d_c = r - CMP_STRIDE * (w - KC_FRONT) - (CMP_BLOCK - 1)
    w2 = np.arange(NEAR_BACK + TQ)[None, :]
    d_s = r + NEAR_BACK - w2
    w5 = np.arange(WINDOW + TQ)[None, :]
    d_w = r + WINDOW - w5
    far = jnp.broadcast_to(far_row[:, None, None], (N_NSA, TQ, LANES))
    far = far.reshape(G_NSA, Z_NSA * TQ, LANES)
    return (tile(d_c, d_c >= 0, far_row[:, None, None]), tile(d_s, d_s >= 0, 0.0),
            tile(d_w, (d_w >= 0) & (d_w < WINDOW), 0.0), far)


def _gate_expand():
    e = np.zeros((G_NSA, LANES, 3 * Z_NSA * DH), np.float32)
    for g in range(G_NSA):
        for z in range(Z_NSA):
            for br in range(3):
                e[g, (g * Z_NSA + z) * 3 + br, br * Z_NSA * DH + z * DH: br * Z_NSA * DH + (z + 1) * DH] = 1.0
    return e


def _overlap_padded(n16, kcp, n_slc):
    i = np.arange(kcp)[:, None] - KC_FRONT
    j = np.arange(LANES)[None, :]
    n_cmp = n16 - 1
    ok = (i >= 0) & (i < n_cmp) & (j < n_slc)
    ov = (i * CMP_STRIDE < j * SLC_BLOCK + SLC_BLOCK) & (i * CMP_STRIDE + CMP_BLOCK > j * SLC_BLOCK)
    return (ok & ov).astype(np.float32)


def _block_onehot(s):
    return (np.arange(s)[:, None] // SLC_BLOCK == np.arange(LANES)[None, :]).astype(np.float32)


def _nsa_kernel(q_ref, kcb_ref, vcb_ref, kcf_ref, vcf_ref, ks_ref, vs_ref, kw_ref, vw_ref, et_ref,
                gate_ref, ovb_ref, ovf_ref, bcn_ref, bsn_ref, bw_ref, cf_ref, eg_ref, o_ref,
                sb0_ref, sb1_ref, *, n_slc, n_round):
    blk = pl.program_id(2)
    c0 = blk * TQ
    r4 = Z_NSA * TQ
    q = q_ref[0]
    lane = lax.broadcasted_iota(I32, (TQ, LANES), 1)
    lo_half = lane < DH
    zq = jnp.zeros((TQ, LANES), BF16)
    qa, qb = q[:, :LANES], q[:, LANES:]
    qs = jnp.concatenate([jnp.where(lo_half, qa, zq), jnp.where(lo_half, zq, qa),
                          jnp.where(lo_half, qb, zq), jnp.where(lo_half, zq, qb)], axis=0)
    ones_k = jnp.ones((SEL_FAR_TILE, LANES), BF16)
    ones_w = jnp.ones((WINDOW + TQ, LANES), BF16)

    kcp = kcb_ref.shape[2]
    prow = lax.broadcasted_iota(I32, (1, kcp), 1)
    near0 = pl.multiple_of(blk * (TQ // CMP_STRIDE), 8)
    far_row = jnp.where((prow >= KC_FRONT) & (prow < near0), 0.0, NEG)
    lane1 = lax.broadcasted_iota(I32, (1, LANES), 1)
    near_row = jnp.where(lane1 + near0 >= KC_FRONT, 0.0, NEG)
    kcn = kcf_ref[0, 0, pl.ds(near0, LANES), :].astype(BF16)
    vcn = vcf_ref[0, 0, pl.ds(near0, LANES), :].astype(BF16)
    s_far = _nt(qs, kcb_ref[0, 0]) + far_row
    s_near = _nt(qs, kcn) + bcn_ref[0] + near_row
    m_c = jnp.maximum(jnp.max(s_far, axis=1, keepdims=True), jnp.max(s_near, axis=1, keepdims=True))
    m_c = jnp.maximum(m_c, M_INIT)
    p_far = jnp.exp(s_far - m_c)
    p_near = jnp.exp(s_near - m_c)
    den = jnp.sum(p_far, axis=1, keepdims=True) + jnp.sum(p_near, axis=1, keepdims=True)
    inv_c = 1.0 / jnp.maximum(den, 1e-30)
    o_c = (_dot(p_far.astype(BF16), vcb_ref[0, 0]) + _dot(p_near.astype(BF16), vcn)) * inv_c

    pn_far = p_far * inv_c
    pn_near = p_near * inv_c
    pz_far = pn_far[0:TQ] + pn_far[TQ:2 * TQ] + pn_far[2 * TQ:3 * TQ] + pn_far[3 * TQ:]
    pz_near = pn_near[0:TQ] + pn_near[TQ:2 * TQ] + pn_near[2 * TQ:3 * TQ] + pn_near[3 * TQ:]
    ovn = ovf_ref[pl.ds(near0, LANES), :].astype(BF16)
    imp = _dot_hl(pz_far, ovb_ref[...]) + _dot_hl(pz_near, ovn)
    tpos = c0 + lax.broadcasted_iota(I32, (TQ, LANES), 0)
    cur = lax.shift_right_logical(tpos, 6)
    valid = (lane * SLC_BLOCK <= tpos) & (lane < n_slc)
    forced = valid & ((lane == 0) | (lane == cur) | (lane == cur - 1))
    score = jnp.where(valid & jnp.logical_not(forced), imp, -1.0)
    score = jnp.where(lane < n_slc, score, -2.0)

    rest, taken, thr = score, jnp.zeros((TQ, 1), F32), jnp.zeros((TQ, 1), F32)
    for _ in range(n_round):
        m = jnp.max(rest, axis=1, keepdims=True)
        eq = rest == m
        thr = jnp.where(taken < n_round, m, thr)
        taken = taken + jnp.sum(jnp.where(eq, 1.0, 0.0), axis=1, keepdims=True)
        rest = jnp.where(eq, -jnp.inf, rest)
    above = score > thr
    at_thr = score == thr
    need = n_round - jnp.sum(jnp.where(above, 1.0, 0.0), axis=1, keepdims=True)
    brow = lax.broadcasted_iota(I32, (LANES, LANES), 0)
    bcol = lax.broadcasted_iota(I32, (LANES, LANES), 1)
    before = _dot(jnp.where(at_thr, 1.0, 0.0).astype(BF16),
                  jnp.where(brow < bcol, 1.0, 0.0).astype(BF16))
    sel = ((above | (at_thr & (before < need))) & valid) | forced

    kws, vws = [], []
    for w in range((WINDOW + TQ) // LANES):
        st = pl.multiple_of(jnp.maximum(c0 - WINDOW + w * LANES, 0), LANES)
        kws.append(kw_ref[0, pl.ds(st, LANES), :])
        vws.append(vw_ref[0, pl.ds(st, LANES), :])
    wl = lax.broadcasted_iota(I32, (1, WINDOW + TQ), 1)
    pos_row = jnp.where(c0 - WINDOW + wl >= 0, 0.0, NEG)
    s = _nt(qs, jnp.concatenate(kws, axis=0)) + bw_ref[0] + pos_row
    p = jnp.exp(s - jnp.max(s, axis=1, keepdims=True)).astype(BF16)
    r_w = _dot(p, jnp.concatenate([jnp.concatenate(vws, axis=0), ones_w], axis=1))
    o_w = r_w[:, :LANES] / r_w[:, LANES:]
    gl = _dot(gate_ref[0], eg_ref[0])
    sig = 1.0 / (1.0 + jnp.exp(-gl))
    wide = Z_NSA * DH

    sel_all = jnp.where(sel, 0.0, NEG).astype(BF16)
    far_blocks = blk * (TQ // SLC_BLOCK) - NEAR_BACK // SLC_BLOCK
    sel_far = jnp.where(sel & (lane < far_blocks), 0.0, NEG).astype(BF16)
    q_far = jnp.concatenate([qs, jnp.concatenate([sel_far] * Z_NSA, axis=0)], axis=1)
    q_near = jnp.concatenate([qs, jnp.concatenate([sel_all] * Z_NSA, axis=0)], axis=1)
    tk = SEL_FAR_TILE
    last_tile = ks_ref.shape[1] // tk - 1

    def qk(kt):
        k0 = pl.multiple_of(kt * tk, tk)
        return _nt(q_far, jnp.concatenate([ks_ref[0, pl.ds(k0, tk), :], et_ref[pl.ds(k0, tk), :]], axis=1))

    def fold(s, v, st):
        m, acc = st
        mn = jnp.maximum(m, jnp.max(s, axis=1, keepdims=True))
        p = jnp.exp(s - mn).astype(BF16)
        return mn, jnp.exp(m - mn) * acc + _dot(p, v)

    def v_tile(kt):
        k0 = pl.multiple_of(kt * tk, tk)
        return jnp.concatenate([vs_ref[0, pl.ds(k0, tk), :], ones_k], axis=1)

    n_pair = lax.shift_right_logical(jnp.maximum(c0 - NEAR_BACK, 0) + (2 * tk - 1), int(math.log2(2 * tk)))
    sb0_ref[...] = qk(0)

    def far_step(i, st):
        sb1_ref[...] = qk(2 * i + 1)
        st = fold(sb0_ref[...], v_tile(2 * i), st)
        sb0_ref[...] = qk(jnp.minimum(2 * i + 2, last_tile))
        return fold(sb1_ref[...], v_tile(2 * i + 1), st)

    m0 = jnp.full((r4, 1), M_INIT, F32)
    m_s, acc_s = lax.fori_loop(0, n_pair, far_step, (m0, jnp.zeros((r4, 2 * LANES), F32)))
    m_s = m_s + cf_ref[0][:, 0:1]
    n_near = NEAR_BACK + TQ
    p0 = pl.multiple_of(jnp.maximum(c0 - NEAR_BACK, 0), NEAR_BACK)
    d0 = pl.multiple_of(c0, TQ)
    kn = jnp.concatenate([ks_ref[0, pl.ds(p0, NEAR_BACK), :], ks_ref[0, pl.ds(d0, TQ), :]], axis=0)
    vn = jnp.concatenate([vs_ref[0, pl.ds(p0, NEAR_BACK), :], vs_ref[0, pl.ds(d0, TQ), :]], axis=0)
    nrow = lax.broadcasted_iota(I32, (n_near, LANES), 0)
    nlane = lax.broadcasted_iota(I32, (n_near, LANES), 1)
    e_n = jnp.where(lax.shift_right_logical(nrow, 6) + far_blocks == nlane, 1.0, 0.0).astype(BF16)
    lane2 = lax.broadcasted_iota(I32, (1, n_near), 1)
    prev_row = jnp.where((lane2 < NEAR_BACK) & (blk == 0), NEG, 0.0)
    s = _nt(q_near, jnp.concatenate([kn, e_n], axis=1)) + bsn_ref[0] + prev_row
    _, acc_s = fold(s, jnp.concatenate([vn, ones_k[:n_near]], axis=1), (m_s, acc_s))
    o_s = acc_s[:, :LANES] / acc_s[:, LANES:]

    def heads(o):
        return jnp.concatenate([jnp.where(lo_half, o[0:TQ], o[TQ:2 * TQ]),
                                jnp.where(lo_half, o[2 * TQ:3 * TQ], o[3 * TQ:])], axis=1)

    o_ref[0] = (sig[:, 0:wide] * heads(o_c) + sig[:, wide:2 * wide] * heads(o_s)
                + sig[:, 2 * wide:] * heads(o_w))


def _nsa_attention(proj, cmp_out, rel_table):
    bsz, s, _ = proj.shape
    assert s % (2 * SEL_FAR_TILE) == 0 and s >= WINDOW + TQ
    n16 = s // CMP_STRIDE
    n_slc = s // SLC_BLOCK
    assert N_SELECT <= n_slc <= LANES
    nq = s // TQ
    kcp = -(-(max(n16 + KC_FRONT, (TQ // CMP_STRIDE) * (nq - 1) + LANES)) // LANES) * LANES
    pad = ((0, 0), (0, 0), (KC_FRONT, kcp - n16 - KC_FRONT), (0, 0))
    cf = jnp.pad(cmp_out, pad)
    cb = cf.astype(BF16)
    ov = _overlap_padded(n16, kcp, n_slc)
    bcn, bsn, bw, far = _bias_tiles(rel_table)
    eg = jnp.asarray(_gate_expand(), BF16)
    r4 = Z_NSA * TQ
    kv_spec = lambda cb0: pl.BlockSpec((1, s, LANES), lambda b, g, i, cb0=cb0: (b, 0, cb0 + g))
    cmp_spec = lambda j0: pl.BlockSpec((1, 1, kcp, LANES), lambda b, g, i, j0=j0: (b, j0 + g, 0, 0))
    tile_spec = lambda w: pl.BlockSpec((1, r4, w), lambda b, g, i: (g, 0, 0))
    const2 = lambda n: pl.BlockSpec((n, LANES), lambda b, g, i: (0, 0))
    return pl.pallas_call(
        functools.partial(_nsa_kernel, n_slc=n_slc, n_round=N_SELECT - 3),
        grid=(bsz, G_NSA, nq),
        in_specs=[pl.BlockSpec((1, TQ, 2 * LANES), lambda b, g, i: (b, i, CB_NSAQ // 2 + g)),
                  cmp_spec(0), cmp_spec(2), cmp_spec(0), cmp_spec(2),
                  kv_spec(CB_KSLC), kv_spec(CB_VSLC), kv_spec(CB_KWIN), kv_spec(CB_VWIN),
                  const2(s),
                  pl.BlockSpec((1, TQ, LANES), lambda b, g, i: (b, i, CB_GATE)),
                  const2(kcp), const2(kcp),
                  tile_spec(LANES), tile_spec(NEAR_BACK + TQ), tile_spec(WINDOW + TQ), tile_spec(LANES),
                  pl.BlockSpec((1, LANES, 3 * Z_NSA * DH), lambda b, g, i: (g, 0, 0))],
        out_specs=pl.BlockSpec((1, TQ, Z_NSA * DH), lambda b, g, i: (b, i, g)),
        out_shape=jax.ShapeDtypeStruct((bsz, s, N_NSA * DH), F32),
        scratch_shapes=[pltpu.VMEM((r4, SEL_FAR_TILE), F32), pltpu.VMEM((r4, SEL_FAR_TILE), F32)],
        compiler_params=_cparams("arbitrary", "arbitrary", "arbitrary"),
        name="nsa_attn",
    )(proj, cb, cb, cf, cf, proj, proj, proj, proj, jnp.asarray(_block_onehot(s), BF16), proj,
      jnp.asarray(ov, BF16), jnp.asarray(ov, F32), bcn, bsn, bw, far, eg)


def _out_kernel(osb_ref, onsa_ref, x_ref, gsb_ref, gnsa_ref, w_ref, gpost_ref, gate_ref,
                gpre_ref, sc_ref, sh_ref, xo_ref, h_ref):
    half = osb_ref.shape[2]
    a = _rms(osb_ref[0], gsb_ref[...]).astype(BF16)
    b = _rms(onsa_ref[0], gnsa_ref[...]).astype(BF16)
    m = _dot(a, w_ref[:half, :]) + _dot(b, w_ref[half:, :])
    x = x_ref[0] + gate_ref[0] * _rms(m, gpost_ref[...])
    xo_ref[0] = x
    h_ref[0] = (_rms(x, gpre_ref[...]) * (1.0 + sc_ref[0]) + sh_ref[0]).astype(BF16)


def _out_proj(o_sb, o_nsa, x, g_sb, g_nsa, w_out, g_post, gate_m, g_pre_ffn, scale_f, shift_f):
    bsz, s, d = x.shape
    half = o_sb.shape[2]
    tm = min(512, s)
    row = lambda n: pl.BlockSpec((1, n), lambda b, i: (0, 0))
    mod = pl.BlockSpec((1, 1, d), lambda b, i: (b, 0, 0))
    act = lambda n: pl.BlockSpec((1, tm, n), lambda b, i: (b, i, 0))
    return pl.pallas_call(
        _out_kernel,
        grid=(bsz, s // tm),
        in_specs=[act(half), act(half), act(d), row(half), row(half),
                  pl.BlockSpec((2 * half, d), lambda b, i: (0, 0)), row(d), mod, row(d), mod, mod],
        out_specs=[act(d), act(d)],
        out_shape=[jax.ShapeDtypeStruct((bsz, s, d), F32), jax.ShapeDtypeStruct((bsz, s, d), BF16)],
        compiler_params=_cparams("arbitrary", "arbitrary"),
        name="out_proj",
    )(o_sb, o_nsa, x, g_sb.reshape(1, half), g_nsa.reshape(1, half), w_out.astype(BF16),
      g_post.reshape(1, d), gate_m.reshape(bsz, 1, d), g_pre_ffn.reshape(1, d),
      scale_f.reshape(bsz, 1, d), shift_f.reshape(bsz, 1, d))


def _silu(x):
    return x * (1.0 / (1.0 + jnp.exp(-x)))


def _ffn_kernel(h_ref, x_ref, wg_ref, wu_ref, wd_ref, gpost_ref, gate_ref, o_ref):
    h = h_ref[0]
    a = (_silu(_dot(h, wg_ref[...])) * _dot(h, wu_ref[...])).astype(BF16)
    o_ref[0] = x_ref[0] + gate_ref[0] * _rms(_dot(a, wd_ref[...]), gpost_ref[...])


def _dense_ffn(h, x, w_gate, w_up, w_down, g_post, gate_f):
    bsz, s, d = x.shape
    ff = w_gate.shape[1]
    tm = min(512, s)
    act = pl.BlockSpec((1, tm, d), lambda b, i: (b, i, 0))
    once = pl.Buffered(1)
    return pl.pallas_call(
        _ffn_kernel,
        grid=(bsz, s // tm),
        in_specs=[act, act,
                  pl.BlockSpec((d, ff), lambda b, i: (0, 0), pipeline_mode=once),
                  pl.BlockSpec((d, ff), lambda b, i: (0, 0), pipeline_mode=once),
                  pl.BlockSpec((ff, d), lambda b, i: (0, 0), pipeline_mode=once),
                  pl.BlockSpec((1, d), lambda b, i: (0, 0)),
                  pl.BlockSpec((1, 1, d), lambda b, i: (b, 0, 0))],
        out_specs=act,
        out_shape=jax.ShapeDtypeStruct((bsz, s, d), F32),
        compiler_params=_cparams("arbitrary", "arbitrary"),
        name="dense_ffn",
    )(h, x, w_gate.astype(BF16), w_up.astype(BF16), w_down.astype(BF16),
      g_post.reshape(1, d), gate_f.reshape(bsz, 1, d))


def _router_kernel(x_ref, gpre_ref, sc_ref, sh_ref, wr_ref, br_ref, pos_ref, w_ref, post_ref, cnt_ref):
    tile = x_ref.shape[1]
    h = _rms(x_ref[0], gpre_ref[...]) * (1.0 + sc_ref[0]) + sh_ref[0]
    logits = _dot_f32(h, wr_ref[...]) + br_ref[...]
    lane = lax.broadcasted_iota(I32, (tile, LANES), 1)
    lanef = lane.astype(F32)
    e = jnp.exp(logits - jnp.max(logits, axis=1, keepdims=True))
    probs = e / jnp.sum(e, axis=1, keepdims=True)
    probs = jnp.where(lane < N_EXPERTS, probs, -1.0)
    m1 = jnp.max(probs, axis=1, keepdims=True)
    i1 = jnp.min(jnp.where(probs == m1, lanef, 1e9), axis=1, keepdims=True)
    rest = jnp.where(lanef == i1, -1.0, probs)
    m2 = jnp.max(rest, axis=1, keepdims=True)
    i2 = jnp.min(jnp.where(rest == m2, lanef, 1e9), axis=1, keepdims=True)
    tot = m1 + m2
    wgt = jnp.where(lanef == i1, m1 / tot, jnp.where(lanef == i2, m2 / tot, 0.0))
    mask = (lanef == i1) | (lanef == i2)
    ch = 256
    r = lax.broadcasted_iota(I32, (ch, ch), 0)
    c = lax.broadcasted_iota(I32, (ch, ch), 1)
    lower = jnp.where(c < r, 1.0, 0.0).astype(BF16)
    ones = jnp.ones((8, ch), BF16)
    carry = jnp.zeros((1, LANES), F32)
    ranks = []
    for k in range(tile // ch):
        mk = jnp.where(mask[k * ch:(k + 1) * ch], 1.0, 0.0).astype(BF16)
        ranks.append(_dot(lower, mk) + carry)
        carry = carry + _dot(ones, mk)[0:1]
    rank = jnp.concatenate(ranks, axis=0)
    pos = jnp.where(mask, rank, -1.0)
    pos_ref[0] = pos.astype(I32)
    w_ref[0] = wgt
    post_ref[0] = jnp.transpose(pos)[0:N_EXPERTS].astype(I32)
    cnt_ref[0] = jnp.broadcast_to(carry, (8, LANES)).astype(I32)


def _router(x, g_pre, scale_f, shift_f, w_router, b_router, tile):
    bsz, s, d = x.shape
    nt = s // tile
    wr = jnp.zeros((d, LANES), F32).at[:, :N_EXPERTS].set(w_router.astype(F32))
    br = jnp.full((1, LANES), NEG, F32).at[0, :N_EXPERTS].set(b_router.astype(F32))
    mod = pl.BlockSpec((1, 1, d), lambda b, i: (b, 0, 0))
    tok = lambda n: pl.BlockSpec((1, tile, n), lambda b, i: (b * nt + i, 0, 0))
    pos, wgt, post, cnt = pl.pallas_call(
        _router_kernel,
        grid=(bsz, nt),
        in_specs=[pl.BlockSpec((1, tile, d), lambda b, i: (b, i, 0)),
                  pl.BlockSpec((1, d), lambda b, i: (0, 0)), mod, mod,
                  pl.BlockSpec((d, LANES), lambda b, i: (0, 0)),
                  pl.BlockSpec((1, LANES), lambda b, i: (0, 0))],
        out_specs=[tok(LANES), tok(LANES),
                   pl.BlockSpec((1, N_EXPERTS, tile), lambda b, i: (b * nt + i, 0, 0)),
                   pl.BlockSpec((1, 8, LANES), lambda b, i: (b * nt + i, 0, 0))],
        out_shape=[jax.ShapeDtypeStruct((bsz * nt, tile, LANES), I32),
                   jax.ShapeDtypeStruct((bsz * nt, tile, LANES), F32),
                   jax.ShapeDtypeStruct((bsz * nt, N_EXPERTS, tile), I32),
                   jax.ShapeDtypeStruct((bsz * nt, 8, LANES), I32)],
        compiler_params=_cparams("arbitrary", "arbitrary"),
        name="moe_router",
    )(x, g_pre.reshape(1, d), scale_f.reshape(bsz, 1, d), shift_f.reshape(bsz, 1, d), wr, br)
    return pos, wgt, post, cnt[:, 0, :N_EXPERTS]


def _moe_kernel(cnt_ref, h_ref, pos_ref, w_ref, post_ref, wg_ref, wu_ref, wd_ref, o_ref,
                xg_ref, y_ref, *, rc):
    i, e, j = pl.program_id(0), pl.program_id(1), pl.program_id(2)
    tile = h_ref.shape[1]
    n_chunk = lax.shift_right_logical(cnt_ref[i * N_EXPERTS + e] + (rc - 1), int(math.log2(rc)))
    lane = lax.broadcasted_iota(I32, (tile, LANES), 1)

    @pl.when(jnp.logical_and(e == 0, j == 0))
    def _():
        o_ref[0] = jnp.zeros(o_ref.shape[1:], F32)

    @pl.when(j == 0)
    def _():
        prow = post_ref[0]
        rid = lax.broadcasted_iota(I32, (rc, tile), 0)

        def gather(c, _):
            onehot = jnp.where(prow == rid + c * rc, 1.0, 0.0).astype(BF16)
            r0 = pl.multiple_of(c * rc, rc)
            xg_ref[pl.ds(r0, rc), :] = _dot(onehot, h_ref[0]).astype(BF16)
            return 0

        lax.fori_loop(0, n_chunk, gather, 0)

    def expert(c, _):
        r0 = pl.multiple_of(c * rc, rc)
        xg = xg_ref[pl.ds(r0, rc), :]
        a = (_silu(_dot(xg, wg_ref[0])) * _dot(xg, wu_ref[0])).astype(BF16)
        part = _dot(a, wd_ref[0])

        @pl.when(j == 0)
        def _():
            y_ref[pl.ds(r0, rc), :] = part

        @pl.when(j > 0)
        def _():
            y_ref[pl.ds(r0, rc), :] += part

        return 0

    lax.fori_loop(0, n_chunk, expert, 0)

    @pl.when(j == pl.num_programs(2) - 1)
    def _():
        pcol = jnp.sum(jnp.where(lane == e, pos_ref[0], 0), axis=1, keepdims=True)
        wcol = jnp.sum(jnp.where(lane == e, w_ref[0], 0.0), axis=1, keepdims=True)
        cid = lax.broadcasted_iota(I32, (tile, rc), 1)

        def scatter(c, z):
            r0 = pl.multiple_of(c * rc, rc)
            onehot_t = jnp.where(pcol == cid + c * rc, 1.0, 0.0).astype(BF16)
            return z + _dot(onehot_t, y_ref[pl.ds(r0, rc), :].astype(BF16))

        z = lax.fori_loop(0, n_chunk, scatter, jnp.zeros((tile, o_ref.shape[2]), F32))
        o_ref[0] += wcol * z


def _moe_ffn(h, pos, wgt, post, cnt, w_gate, w_up, w_down, tile, rc=128, tf=512):
    t_tokens, d = h.shape
    nt = t_tokens // tile
    ff = w_gate.shape[2]
    assert ff % tf == 0
    hb = h.reshape(nt, tile, d)
    post3 = post.reshape(nt * N_EXPERTS, 1, tile)
    grid_spec = pltpu.PrefetchScalarGridSpec(
        num_scalar_prefetch=1,
        grid=(nt, N_EXPERTS, ff // tf),
        in_specs=[pl.BlockSpec((1, tile, d), lambda i, e, j, c: (i, 0, 0)),
                  pl.BlockSpec((1, tile, LANES), lambda i, e, j, c: (i, 0, 0)),
                  pl.BlockSpec((1, tile, LANES), lambda i, e, j, c: (i, 0, 0)),
                  pl.BlockSpec((1, 1, tile), lambda i, e, j, c: (i * N_EXPERTS + e, 0, 0)),
                  pl.BlockSpec((1, d, tf), lambda i, e, j, c: (e, 0, j)),
                  pl.BlockSpec((1, d, tf), lambda i, e, j, c: (e, 0, j)),
                  pl.BlockSpec((1, tf, d), lambda i, e, j, c: (e, j, 0))],
        out_specs=pl.BlockSpec((1, tile, d), lambda i, e, j, c: (i, 0, 0)),
        scratch_shapes=[pltpu.VMEM((tile, d), BF16), pltpu.VMEM((tile, d), F32)])
    out = pl.pallas_call(
        functools.partial(_moe_kernel, rc=rc),
        grid_spec=grid_spec,
        out_shape=jax.ShapeDtypeStruct((nt, tile, d), F32),
        compiler_params=_cparams("arbitrary", "arbitrary", "arbitrary"),
        name="moe_ffn",
    )(cnt.reshape(-1), hb, pos, wgt, post3, w_gate.astype(BF16), w_up.astype(BF16), w_down.astype(BF16))
    return out.reshape(t_tokens, d)


MOE_RC = 128
MOE_GC = 256
MOE_SC = 1024


def _moe_rows(tile):
    return -(-(2 * tile + N_EXPERTS * MOE_RC) // MOE_SC) * MOE_SC


def _route_kernel(x_ref, gpre_ref, sc_ref, sh_ref, wr_ref, br_ref, d_ref, w_ref, dt_ref, off_ref, nch_ref):
    tile = x_ref.shape[1]
    h = _rms(x_ref[0], gpre_ref[...]) * (1.0 + sc_ref[0]) + sh_ref[0]
    logits = _dot_f32(h, wr_ref[...]) + br_ref[...]
    lane = lax.broadcasted_iota(I32, (tile, LANES), 1)
    lanef = lane.astype(F32)
    e = jnp.exp(logits - jnp.max(logits, axis=1, keepdims=True))
    probs = e / jnp.sum(e, axis=1, keepdims=True)
    probs = jnp.where(lane < N_EXPERTS, probs, -1.0)
    m1 = jnp.max(probs, axis=1, keepdims=True)
    i1 = jnp.min(jnp.where(probs == m1, lanef, 1e9), axis=1, keepdims=True)
    rest = jnp.where(lanef == i1, -1.0, probs)
    m2 = jnp.max(rest, axis=1, keepdims=True)
    i2 = jnp.min(jnp.where(rest == m2, lanef, 1e9), axis=1, keepdims=True)
    tot = m1 + m2
    hit1 = lanef == i1
    hit2 = lanef == i2
    mask = hit1 | hit2
    ch = 256
    r = lax.broadcasted_iota(I32, (ch, ch), 0)
    c = lax.broadcasted_iota(I32, (ch, ch), 1)
    lower = jnp.where(c < r, 1.0, 0.0).astype(BF16)
    ones = jnp.ones((8, ch), BF16)
    count = jnp.zeros((1, LANES), F32)
    ranks = []
    for k in range(tile // ch):
        mk = jnp.where(mask[k * ch:(k + 1) * ch], 1.0, 0.0).astype(BF16)
        ranks.append(_dot(lower, mk) + count)
        count = count + _dot(ones, mk)[0:1]
    rank = jnp.concatenate(ranks, axis=0)
    shift = int(math.log2(MOE_RC))
    nch = lax.shift_right_logical(count.astype(I32) + (MOE_RC - 1), shift)
    nch8 = jnp.broadcast_to(nch.astype(F32), (8, LANES))
    lr = lax.broadcasted_iota(I32, (LANES, LANES), 0)
    lc = lax.broadcasted_iota(I32, (LANES, LANES), 1)
    offc = _dot(nch8.astype(BF16), jnp.where(lr < lc, 1.0, 0.0).astype(BF16))
    row = offc[0:1] * float(MOE_RC) + rank
    d1 = jnp.sum(jnp.where(hit1, row, 0.0), axis=1, keepdims=True)
    d2 = jnp.sum(jnp.where(hit2, row, 0.0), axis=1, keepdims=True)
    dd = jnp.where(lane == 0, d1, jnp.where(lane == 1, d2, -1.0))
    d_ref[0] = dd.astype(I32)
    w_ref[0] = jnp.where(lane == 0, m1 / tot, jnp.where(lane == 1, m2 / tot, 0.0))
    dt_ref[0] = jnp.transpose(dd)[0:8].astype(I32)
    off_ref[0] = offc.astype(I32)
    nch_ref[0] = jnp.broadcast_to(nch, (8, LANES))


def _route(x, g_pre, scale_f, shift_f, w_router, b_router, tile):
    bsz, s, d = x.shape
    nt = s // tile
    wr = jnp.zeros((d, LANES), F32).at[:, :N_EXPERTS].set(w_router.astype(F32))
    br = jnp.full((1, LANES), NEG, F32).at[0, :N_EXPERTS].set(b_router.astype(F32))
    mod = pl.BlockSpec((1, 1, d), lambda b, i: (b, 0, 0))
    tok = pl.BlockSpec((1, tile, LANES), lambda b, i: (b * nt + i, 0, 0))
    small = pl.BlockSpec((1, 8, LANES), lambda b, i: (b * nt + i, 0, 0))
    dest, wgt, dest_t, offc, nch = pl.pallas_call(
        _route_kernel,
        grid=(bsz, nt),
        in_specs=[pl.BlockSpec((1, tile, d), lambda b, i: (b, i, 0)),
                  pl.BlockSpec((1, d), lambda b, i: (0, 0)), mod, mod,
                  pl.BlockSpec((d, LANES), lambda b, i: (0, 0)),
                  pl.BlockSpec((1, LANES), lambda b, i: (0, 0))],
        out_specs=[tok, tok, pl.BlockSpec((1, 8, tile), lambda b, i: (b * nt + i, 0, 0)), small, small],
        out_shape=[jax.ShapeDtypeStruct((bsz * nt, tile, LANES), I32),
                   jax.ShapeDtypeStruct((bsz * nt, tile, LANES), F32),
                   jax.ShapeDtypeStruct((bsz * nt, 8, tile), I32),
                   jax.ShapeDtypeStruct((bsz * nt, 8, LANES), I32),
                   jax.ShapeDtypeStruct((bsz * nt, 8, LANES), I32)],
        compiler_params=_cparams("arbitrary", "arbitrary"),
        name="moe_router",
    )(x, g_pre.reshape(1, d), scale_f.reshape(bsz, 1, d), shift_f.reshape(bsz, 1, d), wr, br)
    return dest, wgt, dest_t, offc[:, 0, :N_EXPERTS].reshape(-1), nch[:, 0, :N_EXPERTS].reshape(-1)


def _experts_kernel(off_ref, nch_ref, h_ref, d_ref, w_ref, dt_ref, wg_ref, wu_ref, wd_ref, o_ref,
                    xg_ref, y_ref, wr_ref):
    i, e, j = pl.program_id(0), pl.program_id(1), pl.program_id(2)
    tile = h_ref.shape[1]
    rows = xg_ref.shape[0]

    @pl.when(jnp.logical_and(e == 0, j == 0))
    def _():
        y_ref[...] = jnp.zeros(y_ref.shape, F32)
        d1t = dt_ref[0, 0:1, :]
        d2t = dt_ref[0, 1:2, :]
        w = w_ref[0]
        lane = lax.broadcasted_iota(I32, (tile, LANES), 1)

        def pieces(col):
            hi, mid, lo = _split3(jnp.broadcast_to(col, (tile, LANES)))
            z = jnp.zeros((tile, LANES), F32)
            return jnp.where(lane == 0, hi.astype(F32), jnp.where(lane == 1, mid.astype(F32),
                             jnp.where(lane == 2, lo.astype(F32), z))).astype(BF16)

        wp1 = pieces(w[:, 0:1])
        wp2 = pieces(w[:, 1:2])
        hb = h_ref[0]
        rid = lax.broadcasted_iota(I32, (MOE_GC, tile), 0)
        for c in range(rows // MOE_GC):
            p1 = jnp.where(rid + c * MOE_GC == d1t, 1.0, 0.0).astype(BF16)
            p2 = jnp.where(rid + c * MOE_GC == d2t, 1.0, 0.0).astype(BF16)
            xg_ref[c * MOE_GC:(c + 1) * MOE_GC, :] = _dot(p1 + p2, hb).astype(BF16)
            wrow = jnp.sum(_dot(p1, wp1) + _dot(p2, wp2), axis=1, keepdims=True)
            wr_ref[c * MOE_GC:(c + 1) * MOE_GC, :] = jnp.broadcast_to(wrow, (MOE_GC, LANES))

    base = off_ref[i * N_EXPERTS + e] * MOE_RC
    n = nch_ref[i * N_EXPERTS + e]

    def ffn(r0, m):
        xg = xg_ref[pl.ds(r0, m), :]
        a = (_silu(_dot(xg, wg_ref[0])) * _dot(xg, wu_ref[0])).astype(BF16)
        y_ref[pl.ds(r0, m), :] += _dot(a, wd_ref[0])

    def pair(c, _):
        ffn(pl.multiple_of(base + c * (2 * MOE_RC), MOE_RC), 2 * MOE_RC)
        return 0

    lax.fori_loop(0, lax.shift_right_logical(n, 1), pair, 0)

    @pl.when((n & 1) == 1)
    def _():
        ffn(pl.multiple_of(base + (n - 1) * MOE_RC, MOE_RC), MOE_RC)

    @pl.when(jnp.logical_and(e == pl.num_programs(1) - 1, j == pl.num_programs(2) - 1))
    def _():
        dd = d_ref[0]
        d1c = dd[:, 0:1]
        d2c = dd[:, 1:2]
        cid = lax.broadcasted_iota(I32, (tile, MOE_SC), 1)
        z = None
        for k in range(rows // MOE_SC):
            pt = jnp.where((cid + k * MOE_SC == d1c) | (cid + k * MOE_SC == d2c), 1.0, 0.0).astype(BF16)
            yb = (y_ref[k * MOE_SC:(k + 1) * MOE_SC, :] * wr_ref[k * MOE_SC:(k + 1) * MOE_SC, 0:1]).astype(BF16)
            part = _dot(pt, yb)
            z = part if z is None else z + part
        o_ref[0] = z


def _experts(h, dest, wgt, dest_t, offc, nch, w_gate, w_up, w_down, tile, tf=512):
    t_tokens, d = h.shape
    nt = t_tokens // tile
    ff = w_gate.shape[2]
    assert ff % tf == 0 and tf % LANES == 0
    rows = _moe_rows(tile)
    tok = lambda n: pl.BlockSpec((1, tile, n), lambda i, e, j, o, c: (i, 0, 0))
    grid_spec = pltpu.PrefetchScalarGridSpec(
        num_scalar_prefetch=2,
        grid=(nt, N_EXPERTS, ff // tf),
        in_specs=[tok(d), tok(LANES), tok(LANES),
                  pl.BlockSpec((1, 8, tile), lambda i, e, j, o, c: (i, 0, 0)),
                  pl.BlockSpec((1, d, tf), lambda i, e, j, o, c: (e, 0, j)),
                  pl.BlockSpec((1, d, tf), lambda i, e, j, o, c: (e, 0, j)),
                  pl.BlockSpec((1, tf, d), lambda i, e, j, o, c: (e, j, 0))],
        out_specs=tok(d),
        scratch_shapes=[pltpu.VMEM((rows, d), BF16), pltpu.VMEM((rows, d), F32),
                        pltpu.VMEM((rows, LANES), F32)])
    out = pl.pallas_call(
        _experts_kernel,
        grid_spec=grid_spec,
        out_shape=jax.ShapeDtypeStruct((nt, tile, d), F32),
        compiler_params=_cparams("arbitrary", "arbitrary", "arbitrary"),
        name="moe_ffn",
    )(offc, nch, h.reshape(nt, tile, d), dest, wgt, dest_t,
      w_gate.astype(BF16), w_up.astype(BF16), w_down.astype(BF16))
    return out.reshape(t_tokens, d)


def _dispatch_kernel(x_ref, gpre_ref, sc_ref, sh_ref, wr_ref, br_ref,
                     xg_ref, wrow_ref, d_ref, off_ref, nch_ref):
    tile = x_ref.shape[1]
    rows = xg_ref.shape[1]
    h = _rms(x_ref[0], gpre_ref[...]) * (1.0 + sc_ref[0]) + sh_ref[0]
    logits = _dot_f32(h, wr_ref[...]) + br_ref[...]
    lane = lax.broadcasted_iota(I32, (tile, LANES), 1)
    lanef = lane.astype(F32)
    e = jnp.exp(logits - jnp.max(logits, axis=1, keepdims=True))
    probs = e / jnp.sum(e, axis=1, keepdims=True)
    probs = jnp.where(lane < N_EXPERTS, probs, -1.0)
    m1 = jnp.max(probs, axis=1, keepdims=True)
    i1 = jnp.min(jnp.where(probs == m1, lanef, 1e9), axis=1, keepdims=True)
    rest = jnp.where(lanef == i1, -1.0, probs)
    m2 = jnp.max(rest, axis=1, keepdims=True)
    i2 = jnp.min(jnp.where(rest == m2, lanef, 1e9), axis=1, keepdims=True)
    tot = m1 + m2
    hit1 = lanef == i1
    hit2 = lanef == i2
    mask = hit1 | hit2
    ch = 256
    r = lax.broadcasted_iota(I32, (ch, ch), 0)
    c = lax.broadcasted_iota(I32, (ch, ch), 1)
    lower = jnp.where(c < r, 1.0, 0.0).astype(BF16)
    ones = jnp.ones((8, ch), BF16)
    count = jnp.zeros((1, LANES), F32)
    ranks = []
    for k in range(tile // ch):
        mk = jnp.where(mask[k * ch:(k + 1) * ch], 1.0, 0.0).astype(BF16)
        ranks.append(_dot(lower, mk) + count)
        count = count + _dot(ones, mk)[0:1]
    rank = jnp.concatenate(ranks, axis=0)
    shift = int(math.log2(MOE_RC))
    nch = lax.shift_right_logical(count.astype(I32) + (MOE_RC - 1), shift)
    nch8 = jnp.broadcast_to(nch.astype(F32), (8, LANES))
    lr = lax.broadcasted_iota(I32, (LANES, LANES), 0)
    lc = lax.broadcasted_iota(I32, (LANES, LANES), 1)
    offc = _dot(nch8.astype(BF16), jnp.where(lr < lc, 1.0, 0.0).astype(BF16))
    row = offc[0:1] * float(MOE_RC) + rank
    d1 = jnp.sum(jnp.where(hit1, row, 0.0), axis=1, keepdims=True)
    d2 = jnp.sum(jnp.where(hit2, row, 0.0), axis=1, keepdims=True)
    dd = jnp.where(lane == 0, d1, jnp.where(lane == 1, d2, -1.0))
    d_ref[0] = dd.astype(I32)
    off_ref[0] = offc.astype(I32)
    nch_ref[0] = jnp.broadcast_to(nch, (8, LANES))

    ddt = jnp.transpose(dd)
    d1t = ddt[0:1].astype(I32)
    d2t = ddt[1:2].astype(I32)

    def pieces(col):
        hi, mid, lo = _split3(jnp.broadcast_to(col, (tile, LANES)))
        z = jnp.zeros((tile, LANES), F32)
        return jnp.where(lane == 0, hi.astype(F32), jnp.where(lane == 1, mid.astype(F32),
                         jnp.where(lane == 2, lo.astype(F32), z))).astype(BF16)

    wp1 = pieces(m1 / tot)
    wp2 = pieces(m2 / tot)
    hb = h.astype(BF16)
    rid = lax.broadcasted_iota(I32, (MOE_GC, tile), 0)
    for k in range(rows // MOE_GC):
        p1 = jnp.where(rid + k * MOE_GC == d1t, 1.0, 0.0).astype(BF16)
        p2 = jnp.where(rid + k * MOE_GC == d2t, 1.0, 0.0).astype(BF16)
        xg_ref[0, k * MOE_GC:(k + 1) * MOE_GC, :] = _dot(p1 + p2, hb).astype(BF16)
        wrow = jnp.sum(_dot(p1, wp1) + _dot(p2, wp2), axis=1, keepdims=True)
        wrow_ref[0, k * MOE_GC:(k + 1) * MOE_GC, :] = jnp.broadcast_to(wrow, (MOE_GC, LANES))


def _dispatch(x, g_pre, scale_f, shift_f, w_router, b_router, tile):
    bsz, s, d = x.shape
    nt = s // tile
    rows = _moe_rows(tile)
    wr = jnp.zeros((d, LANES), F32).at[:, :N_EXPERTS].set(w_router.astype(F32))
    br = jnp.full((1, LANES), NEG, F32).at[0, :N_EXPERTS].set(b_router.astype(F32))
    mod = pl.BlockSpec((1, 1, d), lambda b, i: (b, 0, 0))
    per_tile = lambda r, n: pl.BlockSpec((1, r, n), lambda b, i: (b * nt + i, 0, 0))
    xg, wrow, dest, offc, nch = pl.pallas_call(
        _dispatch_kernel,
        grid=(bsz, nt),
        in_specs=[pl.BlockSpec((1, tile, d), lambda b, i: (b, i, 0)),
                  pl.BlockSpec((1, d), lambda b, i: (0, 0)), mod, mod,
                  pl.BlockSpec((d, LANES), lambda b, i: (0, 0)),
                  pl.BlockSpec((1, LANES), lambda b, i: (0, 0))],
        out_specs=[per_tile(rows, d), per_tile(rows, LANES), per_tile(tile, LANES),
                   per_tile(8, LANES), per_tile(8, LANES)],
        out_shape=[jax.ShapeDtypeStruct((bsz * nt, rows, d), BF16),
                   jax.ShapeDtypeStruct((bsz * nt, rows, LANES), F32),
                   jax.ShapeDtypeStruct((bsz * nt, tile, LANES), I32),
                   jax.ShapeDtypeStruct((bsz * nt, 8, LANES), I32),
                   jax.ShapeDtypeStruct((bsz * nt, 8, LANES), I32)],
        compiler_params=_cparams("arbitrary", "arbitrary"),
        name="moe_dispatch",
    )(x, g_pre.reshape(1, d), scale_f.reshape(bsz, 1, d), shift_f.reshape(bsz, 1, d), wr, br)
    return xg, wrow, dest, offc[:, 0, :N_EXPERTS], nch[:, 0, :N_EXPERTS]


def _slot_order(offc, nch, slots_per_tile):
    ends = offc + nch
    c = jnp.arange(slots_per_tile, dtype=I32)[None, :, None]
    expert = jnp.sum((c >= ends[:, None, :]).astype(I32), axis=-1).reshape(-1)
    n = expert.shape[0]
    order = jnp.argsort(expert * n + jnp.arange(n, dtype=I32)).astype(I32)
    exp_sorted = expert[order]
    used = (exp_sorted < N_EXPERTS).astype(I32)
    return order, jnp.minimum(exp_sorted, N_EXPERTS - 1).astype(I32), used


def _slots_kernel(slot_ref, exp_ref, used_ref, x_ref, wrow_ref, wg_ref, wu_ref, wd_ref, y_ref):
    p = pl.program_id(0)

    @pl.when(used_ref[p] == 1)
    def _():
        x = x_ref[0]
        a = (_silu(_dot(x, wg_ref[0])) * _dot(x, wu_ref[0])).astype(BF16)
        y_ref[0] = (_dot(a, wd_ref[0]) * wrow_ref[0][:, 0:1]).astype(BF16)

    @pl.when(used_ref[p] == 0)
    def _():
        y_ref[0] = jnp.zeros(y_ref.shape[1:], BF16)


def _expert_slots(xg, wrow, order, exp_sorted, used, w_gate, w_up, w_down):
    ntile, rows, d = xg.shape
    nslot = ntile * rows // MOE_RC
    ff = w_gate.shape[2]
    once = pl.Buffered(1)
    slot = lambda n: pl.BlockSpec((1, MOE_RC, n), lambda p, s, e, u: (s[p], 0, 0))
    grid_spec = pltpu.PrefetchScalarGridSpec(
        num_scalar_prefetch=3,
        grid=(nslot,),
        in_specs=[slot(d), slot(LANES),
                  pl.BlockSpec((1, d, ff), lambda p, s, e, u: (e[p], 0, 0), pipeline_mode=once),
                  pl.BlockSpec((1, d, ff), lambda p, s, e, u: (e[p], 0, 0), pipeline_mode=once),
                  pl.BlockSpec((1, ff, d), lambda p, s, e, u: (e[p], 0, 0), pipeline_mode=once)],
        out_specs=slot(d))
    y = pl.pallas_call(
        _slots_kernel,
        grid_spec=grid_spec,
        out_shape=jax.ShapeDtypeStruct((nslot, MOE_RC, d), BF16),
        input_output_aliases={3: 0},
        compiler_params=_cparams("arbitrary"),
        name="moe_slots",
    )(order, exp_sorted, used, xg.reshape(nslot, MOE_RC, d), wrow.reshape(nslot, MOE_RC, LANES),
      w_gate.astype(BF16), w_up.astype(BF16), w_down.astype(BF16))
    return y.reshape(ntile, rows, d)


def _combine_kernel(y_ref, d_ref, x_ref, gpost_ref, gate_ref, o_ref):
    tile = x_ref.shape[1]
    rows = y_ref.shape[1]
    dd = d_ref[0]
    d1c = dd[:, 0:1]
    d2c = dd[:, 1:2]
    cid = lax.broadcasted_iota(I32, (tile, MOE_SC), 1)
    z = None
    for k in range(rows // MOE_SC):
        pt = jnp.where((cid + k * MOE_SC == d1c) | (cid + k * MOE_SC == d2c), 1.0, 0.0).astype(BF16)
        part = _dot(pt, y_ref[0, k * MOE_SC:(k + 1) * MOE_SC, :])
        z = part if z is None else z + part
    o_ref[0] = x_ref[0] + gate_ref[0] * _rms(z, gpost_ref[...])


def _combine(y, dest, x, g_post, gate_f):
    bsz, s, d = x.shape
    ntile, rows, _ = y.shape
    tile = dest.shape[1]
    nt = s // tile
    per_tile = lambda r, n: pl.BlockSpec((1, r, n), lambda i: (i, 0, 0))
    out = pl.pallas_call(
        _combine_kernel,
        grid=(ntile,),
        in_specs=[per_tile(rows, d), per_tile(tile, LANES), per_tile(tile, d),
                  pl.BlockSpec((1, d), lambda i: (0, 0)),
                  pl.BlockSpec((1, 1, d), lambda i: (i // nt, 0, 0))],
        out_specs=per_tile(tile, d),
        out_shape=jax.ShapeDtypeStruct((ntile, tile, d), F32),
        compiler_params=_cparams("arbitrary"),
        name="moe_combine",
    )(y, dest, x.reshape(ntile, tile, d), g_post.reshape(1, d), gate_f.reshape(bsz, 1, d))
    return out.reshape(bsz, s, d)


def _post_kernel(x_ref, f_ref, gpost_ref, gate_ref, o_ref):
    o_ref[0] = x_ref[0] + gate_ref[0] * _rms(f_ref[0], gpost_ref[...])


def _post_residual(x, f, g_post, gate_f):
    bsz, s, d = x.shape
    tm = min(1024, s)
    act = pl.BlockSpec((1, tm, d), lambda b, i: (b, i, 0))
    return pl.pallas_call(
        _post_kernel,
        grid=(bsz, s // tm),
        in_specs=[act, act, pl.BlockSpec((1, d), lambda b, i: (0, 0)),
                  pl.BlockSpec((1, 1, d), lambda b, i: (b, 0, 0))],
        out_specs=act,
        out_shape=jax.ShapeDtypeStruct((bsz, s, d), F32),
        compiler_params=_cparams("arbitrary", "arbitrary"),
        name="post_residual",
    )(x, f, g_post.reshape(1, d), gate_f.reshape(bsz, 1, d))


def _mixer(x, mod, layer, rel_table, g_pre_mix, w_in, cmp_params):
    shift_m, scale_m = mod[:, 0], mod[:, 1]
    proj = _in_proj(x, g_pre_mix, scale_m, shift_m, _arrange_w_in(w_in))
    o_sb = _sb_attention(proj)
    o_nsa = _nsa_attention(proj, _compress(proj, *cmp_params), rel_table)
    return o_sb, o_nsa


def kernel(x, c, rel_table, w_ada, b_ada, g_pre_mix, g_post_mix, g_pre_ffn, g_post_ffn, w_in, w_out, g_sb, g_nsa, cmp_pos_k, cmp_w1_k, cmp_w2_k, cmp_pos_v, cmp_w1_v, cmp_w2_v, ffn_w_gate, ffn_w_up, ffn_w_down, moe_w_router, moe_b_router, moe_w_gate, moe_w_up, moe_w_down):
    bsz, s, d = x.shape
    depth = w_in.shape[0]
    mods = _ada(c, w_ada, b_ada).reshape(depth, bsz, 6, d)
    moe_tile = min(1024, s)
    for layer in range(depth):
        mod = mods[layer]
        cmp_params = (cmp_pos_k[layer], cmp_w1_k[layer], cmp_w2_k[layer],
                      cmp_pos_v[layer], cmp_w1_v[layer], cmp_w2_v[layer])
        o_sb, o_nsa = _mixer(x, mod, layer, rel_table, g_pre_mix[layer], w_in[layer], cmp_params)
        x, h = _out_proj(o_sb, o_nsa, x, g_sb[layer], g_nsa[layer], w_out[layer], g_post_mix[layer],
                         mod[:, 2], g_pre_ffn[layer], mod[:, 4], mod[:, 3])
        i = layer // 2
        if layer % 2 == 0:
            x = _dense_ffn(h, x, ffn_w_gate[i], ffn_w_up[i], ffn_w_down[i], g_post_ffn[layer], mod[:, 5])
        else:
            xg, wrow, dest, offc, nch = _dispatch(x, g_pre_ffn[layer], mod[:, 4], mod[:, 3],
                                                  moe_w_router[i], moe_b_router[i], moe_tile)
            order = _slot_order(offc, nch, xg.shape[1] // MOE_RC)
            y = _expert_slots(xg, wrow, *order, moe_w_gate[i], moe_w_up[i], moe_w_down[i])
            x = _combine(y, dest, x, g_post_ffn[layer], mod[:, 5])
    return x
```

```python
import functools
import math

import numpy as np
import jax
import jax.numpy as jnp
from jax import lax
from jax.experimental import pallas as pl
from jax.experimental.pallas import tpu as pltpu

F32 = jnp.float32
BF16 = jnp.bfloat16
I32 = jnp.int32

LANES = 128
DH = 64
N_SB = 8
N_NSA = 8
G_NSA = 2
Z_NSA = 4
CMP_BLOCK = 32
CMP_STRIDE = 16
SLC_BLOCK = 64
N_SELECT = 16
WINDOW = 512
REL_BUCKETS = 32
REL_MAX_DIST = 128
N_EXPERTS = 8
EPS = 1e-6
FORCED = 1e4
NEG = -1e30
M_INIT = -1e29
SB_EXIT = -104.5
VMEM_LIMIT = 56 * 1024 * 1024

TQ = 256
NEAR_BACK = REL_MAX_DIST
KC_FRONT = 16
SEL_FAR_TILE = 512


def _cparams(*sem):
    return pltpu.CompilerParams(dimension_semantics=sem, vmem_limit_bytes=VMEM_LIMIT)


def _nt(a, b):
    return lax.dot_general(a, b, (((1,), (1,)), ((), ())), preferred_element_type=F32)


def _dot(a, b):
    return jnp.dot(a, b, preferred_element_type=F32)


def _split3(a):
    hi = a.astype(BF16)
    r = a - hi.astype(F32)
    mid = r.astype(BF16)
    lo = (r - mid.astype(F32)).astype(BF16)
    return hi, mid, lo


def _dot_hl(a, b):
    hi = a.astype(BF16)
    lo = (a - hi.astype(F32)).astype(BF16)
    return _dot(hi, b) + _dot(lo, b)


def _dot_f32(a, b):
    ah, am, al = _split3(a)
    bh, bm, bl = _split3(b)
    return (_dot(ah, bh) + (_dot(ah, bm) + _dot(am, bh))
            + (_dot(ah, bl) + _dot(am, bm) + _dot(al, bh)))


def _rms(x, g):
    return x * lax.rsqrt(jnp.mean(x * x, axis=-1, keepdims=True) + EPS) * g


def _ada_kernel(c_ref, w_ref, b_ref, o_ref):
    c = c_ref[...]
    ca = c * (1.0 / (1.0 + jnp.exp(-c)))
    o_ref[0] = _dot_f32(ca, w_ref[0]) + b_ref[0]


def _ada(c, w_ada, b_ada):
    depth, d, n = w_ada.shape
    bsz = c.shape[0]
    rows = 8
    tn = 1536
    cp = jnp.zeros((rows, d), F32).at[:bsz].set(c)
    out = pl.pallas_call(
        _ada_kernel,
        grid=(depth, n // tn),
        in_specs=[pl.BlockSpec((rows, d), lambda l, j: (0, 0)),
                  pl.BlockSpec((1, d, tn), lambda l, j: (l, 0, j)),
                  pl.BlockSpec((1, 1, tn), lambda l, j: (l, 0, j))],
        out_specs=pl.BlockSpec((1, rows, tn), lambda l, j: (l, 0, j)),
        out_shape=jax.ShapeDtypeStruct((depth, rows, n), F32),
        compiler_params=_cparams("arbitrary", "arbitrary"),
        name="ada_mod",
    )(cp, w_ada, b_ada.reshape(depth, 1, n))
    return out[:, :bsz]


def _in_kernel(x_ref, g_ref, sc_ref, sh_ref, w_ref, o_ref, *, cn):
    h = _rms(x_ref[0], g_ref[...]) * (1.0 + sc_ref[0]) + sh_ref[0]
    hb = h.astype(BF16)
    for j in range(w_ref.shape[1] // cn):
        o_ref[0, :, j * cn:(j + 1) * cn] = _dot(hb, w_ref[:, j * cn:(j + 1) * cn]).astype(BF16)


def _in_proj(x, g, scale, shift, w):
    bsz, s, d = x.shape
    nc = w.shape[1]
    tm = min(512, s)
    cn = nc // 3 if (nc // 3) % LANES == 0 else nc
    return pl.pallas_call(
        functools.partial(_in_kernel, cn=cn),
        grid=(bsz, s // tm),
        in_specs=[pl.BlockSpec((1, tm, d), lambda b, i: (b, i, 0)),
                  pl.BlockSpec((1, d), lambda b, i: (0, 0)),
                  pl.BlockSpec((1, 1, d), lambda b, i: (b, 0, 0)),
                  pl.BlockSpec((1, 1, d), lambda b, i: (b, 0, 0)),
                  pl.BlockSpec((d, nc), lambda b, i: (0, 0))],
        out_specs=pl.BlockSpec((1, tm, nc), lambda b, i: (b, i, 0)),
        out_shape=jax.ShapeDtypeStruct((bsz, s, nc), BF16),
        compiler_params=_cparams("arbitrary", "arbitrary"),
        name="in_proj",
    )(x, g.reshape(1, d), scale.reshape(bsz, 1, d), shift.reshape(bsz, 1, d), w)


CB_SBQ, CB_SBK, CB_SBV, CB_NSAQ = 0, 4, 8, 12
CB_KCMP, CB_VCMP = 16, 17
CB_KSLC, CB_VSLC, CB_KWIN, CB_VWIN = 18, 20, 22, 24
CB_GATE = 26
N_CB = 27


def _arrange_w_in(w_in):
    d = w_in.shape[0]
    w_sb = N_SB * DH
    off_nsa_q = 3 * w_sb
    off_kv = off_nsa_q + N_NSA * DH
    off_gate = off_kv + 3 * 2 * G_NSA * DH
    scale = DH ** -0.5

    def kv(br, kvi, g):
        lo = off_kv + ((br * 2 + kvi) * G_NSA + g) * DH
        return w_in[:, lo:lo + DH]

    cols = [w_in[:, 0:w_sb] * scale, w_in[:, w_sb:2 * w_sb], w_in[:, 2 * w_sb:3 * w_sb],
            w_in[:, off_nsa_q:off_kv] * scale,
            kv(0, 0, 0), kv(0, 0, 1), kv(0, 1, 0), kv(0, 1, 1)]
    for br in (1, 2):
        for kvi in (0, 1):
            for g in range(G_NSA):
                cols += [kv(br, kvi, g), kv(br, kvi, g)]
    n_gate = 3 * N_NSA
    cols += [w_in[:, off_gate:off_gate + n_gate], jnp.zeros((d, LANES - n_gate), w_in.dtype)]
    out = jnp.concatenate(cols, axis=1).astype(BF16)
    assert out.shape[1] == N_CB * LANES
    return out


def _sb_kernel(q_ref, k_ref, v_ref, o_ref, *, t):
    qi = pl.program_id(2)
    q = q_ref[0]
    lane = lax.broadcasted_iota(I32, (t, LANES), 1)
    row = lax.broadcasted_iota(I32, (t, t), 0)
    col = lax.broadcasted_iota(I32, (t, t), 1)
    upper = jnp.where(row > col, 1.0, 0.0).astype(BF16)
    upper2 = jnp.concatenate([upper, upper], axis=0)
    ones2 = jnp.ones((2 * t, LANES), BF16)
    causal = col < row
    rep = t // LANES

    def tile(qh, k, v, carry, acc, diag):
        z = _nt(qh, k)
        lk = -(jnp.maximum(z, 0.0) + jnp.log(1.0 + jnp.exp(-jnp.abs(z))))
        if diag:
            lk = jnp.where(causal, lk, 0.0)
        hi = lk.astype(BF16)
        lo = (lk - hi.astype(F32)).astype(BF16)
        hl = jnp.concatenate([hi, lo], axis=1)
        later = _dot(hl, upper2)
        tot = _dot(hl, ones2)
        cb = carry if rep == 1 else jnp.concatenate([carry] * rep, axis=1)
        w = jnp.exp(z + lk + later + cb)
        if diag:
            w = jnp.where(causal, w, 0.0)
        return carry + tot, acc + _dot(w.astype(BF16), v)

    n_head = 2 * (q.shape[1] // LANES)
    zq = jnp.zeros((t, LANES), BF16)
    qhs = []
    for c in range(q.shape[1] // LANES):
        qc = q[:, c * LANES:(c + 1) * LANES]
        qhs += [jnp.where(lane < DH, qc, zq), jnp.where(lane < DH, zq, qc)]
    zero = jnp.zeros((t, LANES), F32)

    def step(kt, st, diag):
        k = k_ref[0, pl.ds(kt * t, t), :]
        v = v_ref[0, pl.ds(kt * t, t), :]
        out, mx = [], None
        for h in range(n_head):
            c = (h // 2) * LANES
            carry, acc = tile(qhs[h], k[:, c:c + LANES], v[:, c:c + LANES], st[2 * h], st[2 * h + 1], diag)
            out += [carry, acc]
            mx = jnp.max(carry) if mx is None else jnp.maximum(mx, jnp.max(carry))
        return mx, tuple(out)

    mx, st = step(qi, (zero,) * (2 * n_head), True)

    def cond(s):
        return jnp.logical_and(s[0] >= 0, s[1] > SB_EXIT)

    def body(s):
        mx, st = step(s[0], s[2], False)
        return s[0] - 1, mx, st

    _, _, st = lax.while_loop(cond, body, (qi - 1, mx, st))
    o_ref[0] = jnp.concatenate([jnp.where(lane < DH, st[4 * c + 1], st[4 * c + 3])
                                for c in range(n_head // 2)], axis=1)


def _sb_attention(proj, t=256, pairs=4):
    bsz, s, _ = proj.shape
    t = min(t, s)
    npair = N_SB // 2
    w = pairs * LANES
    return pl.pallas_call(
        functools.partial(_sb_kernel, t=t),
        grid=(bsz, npair // pairs, s // t),
        in_specs=[pl.BlockSpec((1, t, w), lambda b, j, i: (b, i, CB_SBQ // pairs + j)),
                  pl.BlockSpec((1, s, w), lambda b, j, i: (b, 0, CB_SBK // pairs + j)),
                  pl.BlockSpec((1, s, w), lambda b, j, i: (b, 0, CB_SBV // pairs + j))],
        out_specs=pl.BlockSpec((1, t, w), lambda b, j, i: (b, i, j)),
        out_shape=jax.ShapeDtypeStruct((bsz, s, N_SB * DH), F32),
        compiler_params=_cparams("arbitrary", "arbitrary", "arbitrary"),
        name="sb_attn",
    )(proj, proj, proj)


def _gelu_tanh(x):
    return 0.5 * x * (1.0 + jnp.tanh(math.sqrt(2.0 / math.pi) * (x + 0.044715 * (x * x * x))))


def _cmp_kernel(xa_ref, xb_ref, pos_ref, w1_ref, w2_ref, o_ref):
    half = w1_ref.shape[1] // 2
    w1a = w1_ref[0, :half, :]
    w1b = w1_ref[0, half:, :]
    pos = pos_ref[0]
    bias = _dot(pos[:, :half], w1a) + _dot(pos[:, half:], w1b)
    hid = _dot(xa_ref[0, 0], w1a) + _dot(xb_ref[0, 0], w1b) + bias[0:1, :]
    o_ref[0, 0] = _dot(_gelu_tanh(hid).astype(BF16), w2_ref[0])


def _compress(proj, pos_k, w1_k, w2_k, pos_v, w1_v, w2_v):
    bsz, s, _ = proj.shape
    n16 = s // CMP_STRIDE
    raw = proj[:, :, CB_KCMP * LANES:(CB_VCMP + 1) * LANES]
    x16 = raw.reshape(bsz, s, 4, DH).transpose(0, 2, 1, 3).reshape(bsz, 4, n16, CMP_STRIDE * DH)
    x16b = jnp.concatenate([x16[:, :, 1:], jnp.zeros_like(x16[:, :, :1])], axis=2)
    hidden = w1_k.shape[1]
    w1 = jnp.stack([w1_k, w1_v]).astype(BF16)
    w2 = jnp.stack([jnp.concatenate([w2_k, w2_k], 1), jnp.concatenate([w2_v, w2_v], 1)]).astype(BF16)
    pos = jnp.stack([pos_k.reshape(1, -1), pos_v.reshape(1, -1)])
    pos = jnp.concatenate([pos, jnp.zeros((2, 7, pos.shape[2]), pos.dtype)], axis=1).astype(BF16)
    k16 = CMP_STRIDE * DH
    return pl.pallas_call(
        _cmp_kernel,
        grid=(bsz, 4),
        in_specs=[pl.BlockSpec((1, 1, n16, k16), lambda b, j: (b, j, 0, 0)),
                  pl.BlockSpec((1, 1, n16, k16), lambda b, j: (b, j, 0, 0)),
                  pl.BlockSpec((1, 8, 2 * k16), lambda b, j: (j // 2, 0, 0)),
                  pl.BlockSpec((1, 2 * k16, hidden), lambda b, j: (j // 2, 0, 0)),
                  pl.BlockSpec((1, hidden, LANES), lambda b, j: (j // 2, 0, 0))],
        out_specs=pl.BlockSpec((1, 1, n16, LANES), lambda b, j: (b, j, 0, 0)),
        out_shape=jax.ShapeDtypeStruct((bsz, 4, n16, LANES), F32),
        compiler_params=_cparams("arbitrary", "arbitrary"),
        name="nsa_compress",
    )(x16, x16b, pos, w1, w2)


def _bucket_table():
    n = np.arange(REL_MAX_DIST + 1)
    exact = REL_BUCKETS // 2
    val = (np.log(np.maximum(n, 1).astype(np.float32) / np.float32(exact)).astype(np.float32)
           / np.float32(math.log(REL_MAX_DIST / exact)) * np.float32(REL_BUCKETS - exact))
    large = np.minimum(exact + val.astype(np.int32), REL_BUCKETS - 1)
    return np.where(n < exact, n, large).astype(np.int32)


def _bias_tiles(rel_table):
    bucket = _bucket_table()
    far_row = rel_table[int(bucket[REL_MAX_DIST])].astype(F32)
    r = np.arange(TQ)[:, None]

    def tile(dist, valid, base):
        onehot = jax.nn.one_hot(bucket[np.clip(dist, 0, REL_MAX_DIST)], REL_BUCKETS, dtype=F32)
        t = jnp.einsum('rwb,bh->hrw', onehot, rel_table.astype(F32),
                       precision=lax.Precision.HIGHEST)
        t = jnp.where(valid[None], t - base, NEG)
        return t.reshape(G_NSA, Z_NSA * TQ, dist.shape[1])

    w = np.arange(LANES)[None, :]
    d_c = r - CMP_STRIDE * (w - KC_FRONT) - (CMP_BLOCK - 1)
    w2 = np.arange(NEAR_BACK + TQ)[None, :]
    d_s = r + NEAR_BACK - w2
    w5 = np.arange(WINDOW + TQ)[None, :]
    d_w = r + WINDOW - w5
    far = jnp.broadcast_to(far_row[:, None, None], (N_NSA, TQ, LANES))
    far = far.reshape(G_NSA, Z_NSA * TQ, LANES)
    return (tile(d_c, d_c >= 0, far_row[:, None, None]), tile(d_s, d_s >= 0, 0.0),
            tile(d_w, (d_w >= 0) & (d_w < WINDOW), 0.0), far)


def _gate_expand():
    e = np.zeros((G_NSA, LANES, 3 * Z_NSA * DH), np.float32)
    for g in range(G_NSA):
        for z in range(Z_NSA):
            for br in range(3):
                e[g, (g * Z_NSA + z) * 3 + br, br * Z_NSA * DH + z * DH: br * Z_NSA * DH + (z + 1) * DH] = 1.0
    return e


def _overlap_padded(n16, kcp, n_slc):
    i = np.arange(kcp)[:, None] - KC_FRONT
    j = np.arange(LANES)[None, :]
    n_cmp = n16 - 1
    ok = (i >= 0) & (i < n_cmp) & (j < n_slc)
    ov = (i * CMP_STRIDE < j * SLC_BLOCK + SLC_BLOCK) & (i * CMP_STRIDE + CMP_BLOCK > j * SLC_BLOCK)
    return (ok & ov).astype(np.float32)


def _block_onehot(s):
    return (np.arange(s)[:, None] // SLC_BLOCK == np.arange(LANES)[None, :]).astype(np.float32)


def _nsa_kernel(q_ref, kcb_ref, vcb_ref, kcf_ref, vcf_ref, ks_ref, vs_ref, kw_ref, vw_ref, et_ref,
                gate_ref, ovb_ref, ovf_ref, bcn_ref, bsn_ref, bw_ref, cf_ref, eg_ref, o_ref,
                sb0_ref, sb1_ref, *, n_slc, n_round):
    blk = pl.program_id(2)
    c0 = blk * TQ
    r4 = Z_NSA * TQ
    q = q_ref[0]
    lane = lax.broadcasted_iota(I32, (TQ, LANES), 1)
    lo_half = lane < DH
    zq = jnp.zeros((TQ, LANES), BF16)
    qa, qb = q[:, :LANES], q[:, LANES:]
    qs = jnp.concatenate([jnp.where(lo_half, qa, zq), jnp.where(lo_half, zq, qa),
                          jnp.where(lo_half, qb, zq), jnp.where(lo_half, zq, qb)], axis=0)
    ones_k = jnp.ones((SEL_FAR_TILE, LANES), BF16)
    ones_w = jnp.ones((WINDOW + TQ, LANES), BF16)

    kcp = kcb_ref.shape[2]
    prow = lax.broadcasted_iota(I32, (1, kcp), 1)
    near0 = pl.multiple_of(blk * (TQ // CMP_STRIDE), 8)
    far_row = jnp.where((prow >= KC_FRONT) & (prow < near0), 0.0, NEG)
    lane1 = lax.broadcasted_iota(I32, (1, LANES), 1)
    near_row = jnp.where(lane1 + near0 >= KC_FRONT, 0.0, NEG)
    kcn = kcf_ref[0, 0, pl.ds(near0, LANES), :].astype(BF16)
    vcn = vcf_ref[0, 0, pl.ds(near0, LANES), :].astype(BF16)
    s_far = _nt(qs, kcb_ref[0, 0]) + far_row
    s_near = _nt(qs, kcn) + bcn_ref[0] + near_row
    m_c = jnp.maximum(jnp.max(s_far, axis=1, keepdims=True), jnp.max(s_near, axis=1, keepdims=True))
    m_c = jnp.maximum(m_c, M_INIT)
    p_far = jnp.exp(s_far - m_c)
    p_near = jnp.exp(s_near - m_c)
    den = jnp.sum(p_far, axis=1, keepdims=True) + jnp.sum(p_near, axis=1, keepdims=True)
    inv_c = 1.0 / jnp.maximum(den, 1e-30)
    o_c = (_dot(p_far.astype(BF16), vcb_ref[0, 0]) + _dot(p_near.astype(BF16), vcn)) * inv_c

    pn_far = p_far * inv_c
    pn_near = p_near * inv_c
    pz_far = pn_far[0:TQ] + pn_far[TQ:2 * TQ] + pn_far[2 * TQ:3 * TQ] + pn_far[3 * TQ:]
    pz_near = pn_near[0:TQ] + pn_near[TQ:2 * TQ] + pn_near[2 * TQ:3 * TQ] + pn_near[3 * TQ:]
    ovn = ovf_ref[pl.ds(near0, LANES), :].astype(BF16)
    imp = _dot_hl(pz_far, ovb_ref[...]) + _dot_hl(pz_near, ovn)
    tpos = c0 + lax.broadcasted_iota(I32, (TQ, LANES), 0)
    cur = lax.shift_right_logical(tpos, 6)
    valid = (lane * SLC_BLOCK <= tpos) & (lane < n_slc)
    forced = valid & ((lane == 0) | (lane == cur) | (lane == cur - 1))
    score = jnp.where(valid & jnp.logical_not(forced), imp, -1.0)
    score = jnp.where(lane < n_slc, score, -2.0)

    rest, taken, thr = score, jnp.zeros((TQ, 1), F32), jnp.zeros((TQ, 1), F32)
    for _ in range(n_round):
        m = jnp.max(rest, axis=1, keepdims=True)
        eq = rest == m
        thr = jnp.where(taken < n_round, m, thr)
        taken = taken + jnp.sum(jnp.where(eq, 1.0, 0.0), axis=1, keepdims=True)
        rest = jnp.where(eq, -jnp.inf, rest)
    above = score > thr
    at_thr = score == thr
    need = n_round - jnp.sum(jnp.where(above, 1.0, 0.0), axis=1, keepdims=True)
    brow = lax.broadcasted_iota(I32, (LANES, LANES), 0)
    bcol = lax.broadcasted_iota(I32, (LANES, LANES), 1)
    before = _dot(jnp.where(at_thr, 1.0, 0.0).astype(BF16),
                  jnp.where(brow < bcol, 1.0, 0.0).astype(BF16))
    sel = ((above | (at_thr & (before < need))) & valid) | forced

    kws, vws = [], []
    for w in range((WINDOW + TQ) // LANES):
        st = pl.multiple_of(jnp.maximum(c0 - WINDOW + w * LANES, 0), LANES)
        kws.append(kw_ref[0, pl.ds(st, LANES), :])
        vws.append(vw_ref[0, pl.ds(st, LANES), :])
    wl = lax.broadcasted_iota(I32, (1, WINDOW + TQ), 1)
    pos_row = jnp.where(c0 - WINDOW + wl >= 0, 0.0, NEG)
    s = _nt(qs, jnp.concatenate(kws, axis=0)) + bw_ref[0] + pos_row
    p = jnp.exp(s - jnp.max(s, axis=1, keepdims=True)).astype(BF16)
    r_w = _dot(p, jnp.concatenate([jnp.concatenate(vws, axis=0), ones_w], axis=1))
    o_w = r_w[:, :LANES] / r_w[:, LANES:]
    gl = _dot(gate_ref[0], eg_ref[0])
    sig = 1.0 / (1.0 + jnp.exp(-gl))
    wide = Z_NSA * DH

    sel_all = jnp.where(sel, 0.0, NEG).astype(BF16)
    far_blocks = blk * (TQ // SLC_BLOCK) - NEAR_BACK // SLC_BLOCK
    sel_far = jnp.where(sel & (lane < far_blocks), 0.0, NEG).astype(BF16)
    q_far = jnp.concatenate([qs, jnp.concatenate([sel_far] * Z_NSA, axis=0)], axis=1)
    q_near = jnp.concatenate([qs, jnp.concatenate([sel_all] * Z_NSA, axis=0)], axis=1)
    tk = SEL_FAR_TILE
    last_tile = ks_ref.shape[1] // tk - 1

    def qk(kt):
        k0 = pl.multiple_of(kt * tk, tk)
        return _nt(q_far, jnp.concatenate([ks_ref[0, pl.ds(k0, tk), :], et_ref[pl.ds(k0, tk), :]], axis=1))

    def fold(s, v, st):
        m, acc = st
        mn = jnp.maximum(m, jnp.max(s, axis=1, keepdims=True))
        p = jnp.exp(s - mn).astype(BF16)
        return mn, jnp.exp(m - mn) * acc + _dot(p, v)

    def v_tile(kt):
        k0 = pl.multiple_of(kt * tk, tk)
        return jnp.concatenate([vs_ref[0, pl.ds(k0, tk), :], ones_k], axis=1)

    n_pair = lax.shift_right_logical(jnp.maximum(c0 - NEAR_BACK, 0) + (2 * tk - 1), int(math.log2(2 * tk)))
    sb0_ref[...] = qk(0)

    def far_step(i, st):
        sb1_ref[...] = qk(2 * i + 1)
        st = fold(sb0_ref[...], v_tile(2 * i), st)
        sb0_ref[...] = qk(jnp.minimum(2 * i + 2, last_tile))
        return fold(sb1_ref[...], v_tile(2 * i + 1), st)

    m0 = jnp.full((r4, 1), M_INIT, F32)
    m_s, acc_s = lax.fori_loop(0, n_pair, far_step, (m0, jnp.zeros((r4, 2 * LANES), F32)))
    m_s = m_s + cf_ref[0][:, 0:1]
    n_near = NEAR_BACK + TQ
    p0 = pl.multiple_of(jnp.maximum(c0 - NEAR_BACK, 0), NEAR_BACK)
    d0 = pl.multiple_of(c0, TQ)
    kn = jnp.concatenate([ks_ref[0, pl.ds(p0, NEAR_BACK), :], ks_ref[0, pl.ds(d0, TQ), :]], axis=0)
    vn = jnp.concatenate([vs_ref[0, pl.ds(p0, NEAR_BACK), :], vs_ref[0, pl.ds(d0, TQ), :]], axis=0)
    nrow = lax.broadcasted_iota(I32, (n_near, LANES), 0)
    nlane = lax.broadcasted_iota(I32, (n_near, LANES), 1)
    e_n = jnp.where(lax.shift_right_logical(nrow, 6) + far_blocks == nlane, 1.0, 0.0).astype(BF16)
    lane2 = lax.broadcasted_iota(I32, (1, n_near), 1)
    prev_row = jnp.where((lane2 < NEAR_BACK) & (blk == 0), NEG, 0.0)
    s = _nt(q_near, jnp.concatenate([kn, e_n], axis=1)) + bsn_ref[0] + prev_row
    _, acc_s = fold(s, jnp.concatenate([vn, ones_k[:n_near]], axis=1), (m_s, acc_s))
    o_s = acc_s[:, :LANES] / acc_s[:, LANES:]

    def heads(o):
        return jnp.concatenate([jnp.where(lo_half, o[0:TQ], o[TQ:2 * TQ]),
                                jnp.where(lo_half, o[2 * TQ:3 * TQ], o[3 * TQ:])], axis=1)

    o_ref[0] = (sig[:, 0:wide] * heads(o_c) + sig[:, wide:2 * wide] * heads(o_s)
                + sig[:, 2 * wide:] * heads(o_w))


def _nsa_attention(proj, cmp_out, rel_table):
    bsz, s, _ = proj.shape
    assert s % (2 * SEL_FAR_TILE) == 0 and s >= WINDOW + TQ
    n16 = s // CMP_STRIDE
    n_slc = s // SLC_BLOCK
    assert N_SELECT <= n_slc <= LANES
    nq = s // TQ
    kcp = -(-(max(n16 + KC_FRONT, (TQ // CMP_STRIDE) * (nq - 1) + LANES)) // LANES) * LANES
    pad = ((0, 0), (0, 0), (KC_FRONT, kcp - n16 - KC_FRONT), (0, 0))
    cf = jnp.pad(cmp_out, pad)
    cb = cf.astype(BF16)
    ov = _overlap_padded(n16, kcp, n_slc)
    bcn, bsn, bw, far = _bias_tiles(rel_table)
    eg = jnp.asarray(_gate_expand(), BF16)
    r4 = Z_NSA * TQ
    kv_spec = lambda cb0: pl.BlockSpec((1, s, LANES), lambda b, g, i, cb0=cb0: (b, 0, cb0 + g))
    cmp_spec = lambda j0: pl.BlockSpec((1, 1, kcp, LANES), lambda b, g, i, j0=j0: (b, j0 + g, 0, 0))
    tile_spec = lambda w: pl.BlockSpec((1, r4, w), lambda b, g, i: (g, 0, 0))
    const2 = lambda n: pl.BlockSpec((n, LANES), lambda b, g, i: (0, 0))
    return pl.pallas_call(
        functools.partial(_nsa_kernel, n_slc=n_slc, n_round=N_SELECT - 3),
        grid=(bsz, G_NSA, nq),
        in_specs=[pl.BlockSpec((1, TQ, 2 * LANES), lambda b, g, i: (b, i, CB_NSAQ // 2 + g)),
                  cmp_spec(0), cmp_spec(2), cmp_spec(0), cmp_spec(2),
                  kv_spec(CB_KSLC), kv_spec(CB_VSLC), kv_spec(CB_KWIN), kv_spec(CB_VWIN),
                  const2(s),
                  pl.BlockSpec((1, TQ, LANES), lambda b, g, i: (b, i, CB_GATE)),
                  const2(kcp), const2(kcp),
                  tile_spec(LANES), tile_spec(NEAR_BACK + TQ), tile_spec(WINDOW + TQ), tile_spec(LANES),
                  pl.BlockSpec((1, LANES, 3 * Z_NSA * DH), lambda b, g, i: (g, 0, 0))],
        out_specs=pl.BlockSpec((1, TQ, Z_NSA * DH), lambda b, g, i: (b, i, g)),
        out_shape=jax.ShapeDtypeStruct((bsz, s, N_NSA * DH), F32),
        scratch_shapes=[pltpu.VMEM((r4, SEL_FAR_TILE), F32), pltpu.VMEM((r4, SEL_FAR_TILE), F32)],
        compiler_params=_cparams("arbitrary", "arbitrary", "arbitrary"),
        name="nsa_attn",
    )(proj, cb, cb, cf, cf, proj, proj, proj, proj, jnp.asarray(_block_onehot(s), BF16), proj,
      jnp.asarray(ov, BF16), jnp.asarray(ov, F32), bcn, bsn, bw, far, eg)


def _out_kernel(osb_ref, onsa_ref, x_ref, gsb_ref, gnsa_ref, w_ref, gpost_ref, gate_ref,
                gpre_ref, sc_ref, sh_ref, xo_ref, h_ref):
    half = osb_ref.shape[2]
    a = _rms(osb_ref[0], gsb_ref[...]).astype(BF16)
    b = _rms(onsa_ref[0], gnsa_ref[...]).astype(BF16)
    m = _dot(a, w_ref[:half, :]) + _dot(b, w_ref[half:, :])
    x = x_ref[0] + gate_ref[0] * _rms(m, gpost_ref[...])
    xo_ref[0] = x
    h_ref[0] = (_rms(x, gpre_ref[...]) * (1.0 + sc_ref[0]) + sh_ref[0]).astype(BF16)


def _out_proj(o_sb, o_nsa, x, g_sb, g_nsa, w_out, g_post, gate_m, g_pre_ffn, scale_f, shift_f):
    bsz, s, d = x.shape
    half = o_sb.shape[2]
    tm = min(512, s)
    row = lambda n: pl.BlockSpec((1, n), lambda b, i: (0, 0))
    mod = pl.BlockSpec((1, 1, d), lambda b, i: (b, 0, 0))
    act = lambda n: pl.BlockSpec((1, tm, n), lambda b, i: (b, i, 0))
    return pl.pallas_call(
        _out_kernel,
        grid=(bsz, s // tm),
        in_specs=[act(half), act(half), act(d), row(half), row(half),
                  pl.BlockSpec((2 * half, d), lambda b, i: (0, 0)), row(d), mod, row(d), mod, mod],
        out_specs=[act(d), act(d)],
        out_shape=[jax.ShapeDtypeStruct((bsz, s, d), F32), jax.ShapeDtypeStruct((bsz, s, d), BF16)],
        compiler_params=_cparams("arbitrary", "arbitrary"),
        name="out_proj",
    )(o_sb, o_nsa, x, g_sb.reshape(1, half), g_nsa.reshape(1, half), w_out.astype(BF16),
      g_post.reshape(1, d), gate_m.reshape(bsz, 1, d), g_pre_ffn.reshape(1, d),
      scale_f.reshape(bsz, 1, d), shift_f.reshape(bsz, 1, d))


def _silu(x):
    return x * (1.0 / (1.0 + jnp.exp(-x)))


def _ffn_kernel(h_ref, x_ref, wg_ref, wu_ref, wd_ref, gpost_ref, gate_ref, o_ref):
    h = h_ref[0]
    a = (_silu(_dot(h, wg_ref[...])) * _dot(h, wu_ref[...])).astype(BF16)
    o_ref[0] = x_ref[0] + gate_ref[0] * _rms(_dot(a, wd_ref[...]), gpost_ref[...])


def _dense_ffn(h, x, w_gate, w_up, w_down, g_post, gate_f):
    bsz, s, d = x.shape
    ff = w_gate.shape[1]
    tm = min(512, s)
    act = pl.BlockSpec((1, tm, d), lambda b, i: (b, i, 0))
    once = pl.Buffered(1)
    return pl.pallas_call(
        _ffn_kernel,
        grid=(bsz, s // tm),
        in_specs=[act, act,
                  pl.BlockSpec((d, ff), lambda b, i: (0, 0), pipeline_mode=once),
                  pl.BlockSpec((d, ff), lambda b, i: (0, 0), pipeline_mode=once),
                  pl.BlockSpec((ff, d), lambda b, i: (0, 0), pipeline_mode=once),
                  pl.BlockSpec((1, d), lambda b, i: (0, 0)),
                  pl.BlockSpec((1, 1, d), lambda b, i: (b, 0, 0))],
        out_specs=act,
        out_shape=jax.ShapeDtypeStruct((bsz, s, d), F32),
        compiler_params=_cparams("arbitrary", "arbitrary"),
        name="dense_ffn",
    )(h, x, w_gate.astype(BF16), w_up.astype(BF16), w_down.astype(BF16),
      g_post.reshape(1, d), gate_f.reshape(bsz, 1, d))


def _router_kernel(x_ref, gpre_ref, sc_ref, sh_ref, wr_ref, br_ref, pos_ref, w_ref, post_ref, cnt_ref):
    tile = x_ref.shape[1]
    h = _rms(x_ref[0], gpre_ref[...]) * (1.0 + sc_ref[0]) + sh_ref[0]
    logits = _dot_f32(h, wr_ref[...]) + br_ref[...]
    lane = lax.broadcasted_iota(I32, (tile, LANES), 1)
    lanef = lane.astype(F32)
    e = jnp.exp(logits - jnp.max(logits, axis=1, keepdims=True))
    probs = e / jnp.sum(e, axis=1, keepdims=True)
    probs = jnp.where(lane < N_EXPERTS, probs, -1.0)
    m1 = jnp.max(probs, axis=1, keepdims=True)
    i1 = jnp.min(jnp.where(probs == m1, lanef, 1e9), axis=1, keepdims=True)
    rest = jnp.where(lanef == i1, -1.0, probs)
    m2 = jnp.max(rest, axis=1, keepdims=True)
    i2 = jnp.min(jnp.where(rest == m2, lanef, 1e9), axis=1, keepdims=True)
    tot = m1 + m2
    wgt = jnp.where(lanef == i1, m1 / tot, jnp.where(lanef == i2, m2 / tot, 0.0))
    mask = (lanef == i1) | (lanef == i2)
    ch = 256
    r = lax.broadcasted_iota(I32, (ch, ch), 0)
    c = lax.broadcasted_iota(I32, (ch, ch), 1)
    lower = jnp.where(c < r, 1.0, 0.0).astype(BF16)
    ones = jnp.ones((8, ch), BF16)
    carry = jnp.zeros((1, LANES), F32)
    ranks = []
    for k in range(tile // ch):
        mk = jnp.where(mask[k * ch:(k + 1) * ch], 1.0, 0.0).astype(BF16)
        ranks.append(_dot(lower, mk) + carry)
        carry = carry + _dot(ones, mk)[0:1]
    rank = jnp.concatenate(ranks, axis=0)
    pos = jnp.where(mask, rank, -1.0)
    pos_ref[0] = pos.astype(I32)
    w_ref[0] = wgt
    post_ref[0] = jnp.transpose(pos)[0:N_EXPERTS].astype(I32)
    cnt_ref[0] = jnp.broadcast_to(carry, (8, LANES)).astype(I32)


def _router(x, g_pre, scale_f, shift_f, w_router, b_router, tile):
    bsz, s, d = x.shape
    nt = s // tile
    wr = jnp.zeros((d, LANES), F32).at[:, :N_EXPERTS].set(w_router.astype(F32))
    br = jnp.full((1, LANES), NEG, F32).at[0, :N_EXPERTS].set(b_router.astype(F32))
    mod = pl.BlockSpec((1, 1, d), lambda b, i: (b, 0, 0))
    tok = lambda n: pl.BlockSpec((1, tile, n), lambda b, i: (b * nt + i, 0, 0))
    pos, wgt, post, cnt = pl.pallas_call(
        _router_kernel,
        grid=(bsz, nt),
        in_specs=[pl.BlockSpec((1, tile, d), lambda b, i: (b, i, 0)),
                  pl.BlockSpec((1, d), lambda b, i: (0, 0)), mod, mod,
                  pl.BlockSpec((d, LANES), lambda b, i: (0, 0)),
                  pl.BlockSpec((1, LANES), lambda b, i: (0, 0))],
        out_specs=[tok(LANES), tok(LANES),
                   pl.BlockSpec((1, N_EXPERTS, tile), lambda b, i: (b * nt + i, 0, 0)),
                   pl.BlockSpec((1, 8, LANES), lambda b, i: (b * nt + i, 0, 0))],
        out_shape=[jax.ShapeDtypeStruct((bsz * nt, tile, LANES), I32),
                   jax.ShapeDtypeStruct((bsz * nt, tile, LANES), F32),
                   jax.ShapeDtypeStruct((bsz * nt, N_EXPERTS, tile), I32),
                   jax.ShapeDtypeStruct((bsz * nt, 8, LANES), I32)],
        compiler_params=_cparams("arbitrary", "arbitrary"),
        name="moe_router",
    )(x, g_pre.reshape(1, d), scale_f.reshape(bsz, 1, d), shift_f.reshape(bsz, 1, d), wr, br)
    return pos, wgt, post, cnt[:, 0, :N_EXPERTS]


def _moe_kernel(cnt_ref, h_ref, pos_ref, w_ref, post_ref, wg_ref, wu_ref, wd_ref, o_ref,
                xg_ref, y_ref, *, rc):
    i, e, j = pl.program_id(0), pl.program_id(1), pl.program_id(2)
    tile = h_ref.shape[1]
    n_chunk = lax.shift_right_logical(cnt_ref[i * N_EXPERTS + e] + (rc - 1), int(math.log2(rc)))
    lane = lax.broadcasted_iota(I32, (tile, LANES), 1)

    @pl.when(jnp.logical_and(e == 0, j == 0))
    def _():
        o_ref[0] = jnp.zeros(o_ref.shape[1:], F32)

    @pl.when(j == 0)
    def _():
        prow = post_ref[0]
        rid = lax.broadcasted_iota(I32, (rc, tile), 0)

        def gather(c, _):
            onehot = jnp.where(prow == rid + c * rc, 1.0, 0.0).astype(BF16)
            r0 = pl.multiple_of(c * rc, rc)
            xg_ref[pl.ds(r0, rc), :] = _dot(onehot, h_ref[0]).astype(BF16)
            return 0

        lax.fori_loop(0, n_chunk, gather, 0)

    def expert(c, _):
        r0 = pl.multiple_of(c * rc, rc)
        xg = xg_ref[pl.ds(r0, rc), :]
        a = (_silu(_dot(xg, wg_ref[0])) * _dot(xg, wu_ref[0])).astype(BF16)
        part = _dot(a, wd_ref[0])

        @pl.when(j == 0)
        def _():
            y_ref[pl.ds(r0, rc), :] = part

        @pl.when(j > 0)
        def _():
            y_ref[pl.ds(r0, rc), :] += part

        return 0

    lax.fori_loop(0, n_chunk, expert, 0)

    @pl.when(j == pl.num_programs(2) - 1)
    def _():
        pcol = jnp.sum(jnp.where(lane == e, pos_ref[0], 0), axis=1, keepdims=True)
        wcol = jnp.sum(jnp.where(lane == e, w_ref[0], 0.0), axis=1, keepdims=True)
        cid = lax.broadcasted_iota(I32, (tile, rc), 1)

        def scatter(c, z):
            r0 = pl.multiple_of(c * rc, rc)
            onehot_t = jnp.where(pcol == cid + c * rc, 1.0, 0.0).astype(BF16)
            return z + _dot(onehot_t, y_ref[pl.ds(r0, rc), :].astype(BF16))

        z = lax.fori_loop(0, n_chunk, scatter, jnp.zeros((tile, o_ref.shape[2]), F32))
        o_ref[0] += wcol * z


def _moe_ffn(h, pos, wgt, post, cnt, w_gate, w_up, w_down, tile, rc=128, tf=512):
    t_tokens, d = h.shape
    nt = t_tokens // tile
    ff = w_gate.shape[2]
    assert ff % tf == 0
    hb = h.reshape(nt, tile, d)
    post3 = post.reshape(nt * N_EXPERTS, 1, tile)
    grid_spec = pltpu.PrefetchScalarGridSpec(
        num_scalar_prefetch=1,
        grid=(nt, N_EXPERTS, ff // tf),
        in_specs=[pl.BlockSpec((1, tile, d), lambda i, e, j, c: (i, 0, 0)),
                  pl.BlockSpec((1, tile, LANES), lambda i, e, j, c: (i, 0, 0)),
                  pl.BlockSpec((1, tile, LANES), lambda i, e, j, c: (i, 0, 0)),
                  pl.BlockSpec((1, 1, tile), lambda i, e, j, c: (i * N_EXPERTS + e, 0, 0)),
                  pl.BlockSpec((1, d, tf), lambda i, e, j, c: (e, 0, j)),
                  pl.BlockSpec((1, d, tf), lambda i, e, j, c: (e, 0, j)),
                  pl.BlockSpec((1, tf, d), lambda i, e, j, c: (e, j, 0))],
        out_specs=pl.BlockSpec((1, tile, d), lambda i, e, j, c: (i, 0, 0)),
        scratch_shapes=[pltpu.VMEM((tile, d), BF16), pltpu.VMEM((tile, d), F32)])
    out = pl.pallas_call(
        functools.partial(_moe_kernel, rc=rc),
        grid_spec=grid_spec,
        out_shape=jax.ShapeDtypeStruct((nt, tile, d), F32),
        compiler_params=_cparams("arbitrary", "arbitrary", "arbitrary"),
        name="moe_ffn",
    )(cnt.reshape(-1), hb, pos, wgt, post3, w_gate.astype(BF16), w_up.astype(BF16), w_down.astype(BF16))
    return out.reshape(t_tokens, d)


MOE_RC = 128
MOE_GC = 256
MOE_SC = 1024


def _moe_rows(tile):
    return -(-(2 * tile + N_EXPERTS * MOE_RC) // MOE_SC) * MOE_SC


def _route_kernel(x_ref, gpre_ref, sc_ref, sh_ref, wr_ref, br_ref, d_ref, w_ref, dt_ref, off_ref, nch_ref):
    tile = x_ref.shape[1]
    h = _rms(x_ref[0], gpre_ref[...]) * (1.0 + sc_ref[0]) + sh_ref[0]
    logits = _dot_f32(h, wr_ref[...]) + br_ref[...]
    lane = lax.broadcasted_iota(I32, (tile, LANES), 1)
    lanef = lane.astype(F32)
    e = jnp.exp(logits - jnp.max(logits, axis=1, keepdims=True))
    probs = e / jnp.sum(e, axis=1, keepdims=True)
    probs = jnp.where(lane < N_EXPERTS, probs, -1.0)
    m1 = jnp.max(probs, axis=1, keepdims=True)
    i1 = jnp.min(jnp.where(probs == m1, lanef, 1e9), axis=1, keepdims=True)
    rest = jnp.where(lanef == i1, -1.0, probs)
    m2 = jnp.max(rest, axis=1, keepdims=True)
    i2 = jnp.min(jnp.where(rest == m2, lanef, 1e9), axis=1, keepdims=True)
    tot = m1 + m2
    hit1 = lanef == i1
    hit2 = lanef == i2
    mask = hit1 | hit2
    ch = 256
    r = lax.broadcasted_iota(I32, (ch, ch), 0)
    c = lax.broadcasted_iota(I32, (ch, ch), 1)
    lower = jnp.where(c < r, 1.0, 0.0).astype(BF16)
    ones = jnp.ones((8, ch), BF16)
    count = jnp.zeros((1, LANES), F32)
    ranks = []
    for k in range(tile // ch):
        mk = jnp.where(mask[k * ch:(k + 1) * ch], 1.0, 0.0).astype(BF16)
        ranks.append(_dot(lower, mk) + count)
        count = count + _dot(ones, mk)[0:1]
    rank = jnp.concatenate(ranks, axis=0)
    shift = int(math.log2(MOE_RC))
    nch = lax.shift_right_logical(count.astype(I32) + (MOE_RC - 1), shift)
    nch8 = jnp.broadcast_to(nch.astype(F32), (8, LANES))
    lr = lax.broadcasted_iota(I32, (LANES, LANES), 0)
    lc = lax.broadcasted_iota(I32, (LANES, LANES), 1)
    offc = _dot(nch8.astype(BF16), jnp.where(lr < lc, 1.0, 0.0).astype(BF16))
    row = offc[0:1] * float(MOE_RC) + rank
    d1 = jnp.sum(jnp.where(hit1, row, 0.0), axis=1, keepdims=True)
    d2 = jnp.sum(jnp.where(hit2, row, 0.0), axis=1, keepdims=True)
    dd = jnp.where(lane == 0, d1, jnp.where(lane == 1, d2, -1.0))
    d_ref[0] = dd.astype(I32)
    w_ref[0] = jnp.where(lane == 0, m1 / tot, jnp.where(lane == 1, m2 / tot, 0.0))
    dt_ref[0] = jnp.transpose(dd)[0:8].astype(I32)
    off_ref[0] = offc.astype(I32)
    nch_ref[0] = jnp.broadcast_to(nch, (8, LANES))


def _route(x, g_pre, scale_f, shift_f, w_router, b_router, tile):
    bsz, s, d = x.shape
    nt = s // tile
    wr = jnp.zeros((d, LANES), F32).at[:, :N_EXPERTS].set(w_router.astype(F32))
    br = jnp.full((1, LANES), NEG, F32).at[0, :N_EXPERTS].set(b_router.astype(F32))
    mod = pl.BlockSpec((1, 1, d), lambda b, i: (b, 0, 0))
    tok = pl.BlockSpec((1, tile, LANES), lambda b, i: (b * nt + i, 0, 0))
    small = pl.BlockSpec((1, 8, LANES), lambda b, i: (b * nt + i, 0, 0))
    dest, wgt, dest_t, offc, nch = pl.pallas_call(
        _route_kernel,
        grid=(bsz, nt),
        in_specs=[pl.BlockSpec((1, tile, d), lambda b, i: (b, i, 0)),
                  pl.BlockSpec((1, d), lambda b, i: (0, 0)), mod, mod,
                  pl.BlockSpec((d, LANES), lambda b, i: (0, 0)),
                  pl.BlockSpec((1, LANES), lambda b, i: (0, 0))],
        out_specs=[tok, tok, pl.BlockSpec((1, 8, tile), lambda b, i: (b * nt + i, 0, 0)), small, small],
        out_shape=[jax.ShapeDtypeStruct((bsz * nt, tile, LANES), I32),
                   jax.ShapeDtypeStruct((bsz * nt, tile, LANES), F32),
                   jax.ShapeDtypeStruct((bsz * nt, 8, tile), I32),
                   jax.ShapeDtypeStruct((bsz * nt, 8, LANES), I32),
                   jax.ShapeDtypeStruct((bsz * nt, 8, LANES), I32)],
        compiler_params=_cparams("arbitrary", "arbitrary"),
        name="moe_router",
    )(x, g_pre.reshape(1, d), scale_f.reshape(bsz, 1, d), shift_f.reshape(bsz, 1, d), wr, br)
    return dest, wgt, dest_t, offc[:, 0, :N_EXPERTS].reshape(-1), nch[:, 0, :N_EXPERTS].reshape(-1)


def _experts_kernel(off_ref, nch_ref, h_ref, d_ref, w_ref, dt_ref, wg_ref, wu_ref, wd_ref, o_ref,
                    xg_ref, y_ref, wr_ref):
    i, e, j = pl.program_id(0), pl.program_id(1), pl.program_id(2)
    tile = h_ref.shape[1]
    rows = xg_ref.shape[0]

    @pl.when(jnp.logical_and(e == 0, j == 0))
    def _():
        y_ref[...] = jnp.zeros(y_ref.shape, F32)
        d1t = dt_ref[0, 0:1, :]
        d2t = dt_ref[0, 1:2, :]
        w = w_ref[0]
        lane = lax.broadcasted_iota(I32, (tile, LANES), 1)

        def pieces(col):
            hi, mid, lo = _split3(jnp.broadcast_to(col, (tile, LANES)))
            z = jnp.zeros((tile, LANES), F32)
            return jnp.where(lane == 0, hi.astype(F32), jnp.where(lane == 1, mid.astype(F32),
                             jnp.where(lane == 2, lo.astype(F32), z))).astype(BF16)

        wp1 = pieces(w[:, 0:1])
        wp2 = pieces(w[:, 1:2])
        hb = h_ref[0]
        rid = lax.broadcasted_iota(I32, (MOE_GC, tile), 0)
        for c in range(rows // MOE_GC):
            p1 = jnp.where(rid + c * MOE_GC == d1t, 1.0, 0.0).astype(BF16)
            p2 = jnp.where(rid + c * MOE_GC == d2t, 1.0, 0.0).astype(BF16)
            xg_ref[c * MOE_GC:(c + 1) * MOE_GC, :] = _dot(p1 + p2, hb).astype(BF16)
            wrow = jnp.sum(_dot(p1, wp1) + _dot(p2, wp2), axis=1, keepdims=True)
            wr_ref[c * MOE_GC:(c + 1) * MOE_GC, :] = jnp.broadcast_to(wrow, (MOE_GC, LANES))

    base = off_ref[i * N_EXPERTS + e] * MOE_RC
    n = nch_ref[i * N_EXPERTS + e]

    def ffn(r0, m):
        xg = xg_ref[pl.ds(r0, m), :]
        a = (_silu(_dot(xg, wg_ref[0])) * _dot(xg, wu_ref[0])).astype(BF16)
        y_ref[pl.ds(r0, m), :] += _dot(a, wd_ref[0])

    def pair(c, _):
        ffn(pl.multiple_of(base + c * (2 * MOE_RC), MOE_RC), 2 * MOE_RC)
        return 0

    lax.fori_loop(0, lax.shift_right_logical(n, 1), pair, 0)

    @pl.when((n & 1) == 1)
    def _():
        ffn(pl.multiple_of(base + (n - 1) * MOE_RC, MOE_RC), MOE_RC)

    @pl.when(jnp.logical_and(e == pl.num_programs(1) - 1, j == pl.num_programs(2) - 1))
    def _():
        dd = d_ref[0]
        d1c = dd[:, 0:1]
        d2c = dd[:, 1:2]
        cid = lax.broadcasted_iota(I32, (tile, MOE_SC), 1)
        z = None
        for k in range(rows // MOE_SC):
            pt = jnp.where((cid + k * MOE_SC == d1c) | (cid + k * MOE_SC == d2c), 1.0, 0.0).astype(BF16)
            yb = (y_ref[k * MOE_SC:(k + 1) * MOE_SC, :] * wr_ref[k * MOE_SC:(k + 1) * MOE_SC, 0:1]).astype(BF16)
            part = _dot(pt, yb)
            z = part if z is None else z + part
        o_ref[0] = z


def _experts(h, dest, wgt, dest_t, offc, nch, w_gate, w_up, w_down, tile, tf=512):
    t_tokens, d = h.shape
    nt = t_tokens // tile
    ff = w_gate.shape[2]
    assert ff % tf == 0 and tf % LANES == 0
    rows = _moe_rows(tile)
    tok = lambda n: pl.BlockSpec((1, tile, n), lambda i, e, j, o, c: (i, 0, 0))
    grid_spec = pltpu.PrefetchScalarGridSpec(
        num_scalar_prefetch=2,
        grid=(nt, N_EXPERTS, ff // tf),
        in_specs=[tok(d), tok(LANES), tok(LANES),
                  pl.BlockSpec((1, 8, tile), lambda i, e, j, o, c: (i, 0, 0)),
                  pl.BlockSpec((1, d, tf), lambda i, e, j, o, c: (e, 0, j)),
                  pl.BlockSpec((1, d, tf), lambda i, e, j, o, c: (e, 0, j)),
                  pl.BlockSpec((1, tf, d), lambda i, e, j, o, c: (e, j, 0))],
        out_specs=tok(d),
        scratch_shapes=[pltpu.VMEM((rows, d), BF16), pltpu.VMEM((rows, d), F32),
                        pltpu.VMEM((rows, LANES), F32)])
    out = pl.pallas_call(
        _experts_kernel,
        grid_spec=grid_spec,
        out_shape=jax.ShapeDtypeStruct((nt, tile, d), F32),
        compiler_params=_cparams("arbitrary", "arbitrary", "arbitrary"),
        name="moe_ffn",
    )(offc, nch, h.reshape(nt, tile, d), dest, wgt, dest_t,
      w_gate.astype(BF16), w_up.astype(BF16), w_down.astype(BF16))
    return out.reshape(t_tokens, d)


def _dispatch_kernel(x_ref, gpre_ref, sc_ref, sh_ref, wr_ref, br_ref,
                     xg_ref, wrow_ref, d_ref, off_ref, nch_ref):
    tile = x_ref.shape[1]
    rows = xg_ref.shape[1]
    h = _rms(x_ref[0], gpre_ref[...]) * (1.0 + sc_ref[0]) + sh_ref[0]
    logits = _dot_f32(h, wr_ref[...]) + br_ref[...]
    lane = lax.broadcasted_iota(I32, (tile, LANES), 1)
    lanef = lane.astype(F32)
    e = jnp.exp(logits - jnp.max(logits, axis=1, keepdims=True))
    probs = e / jnp.sum(e, axis=1, keepdims=True)
    probs = jnp.where(lane < N_EXPERTS, probs, -1.0)
    m1 = jnp.max(probs, axis=1, keepdims=True)
    i1 = jnp.min(jnp.where(probs == m1, lanef, 1e9), axis=1, keepdims=True)
    rest = jnp.where(lanef == i1, -1.0, probs)
    m2 = jnp.max(rest, axis=1, keepdims=True)
    i2 = jnp.min(jnp.where(rest == m2, lanef, 1e9), axis=1, keepdims=True)
    tot = m1 + m2
    hit1 = lanef == i1
    hit2 = lanef == i2
    mask = hit1 | hit2
    ch = 256
    r = lax.broadcasted_iota(I32, (ch, ch), 0)
    c = lax.broadcasted_iota(I32, (ch, ch), 1)
    lower = jnp.where(c < r, 1.0, 0.0).astype(BF16)
    ones = jnp.ones((8, ch), BF16)
    count = jnp.zeros((1, LANES), F32)
    ranks = []
    for k in range(tile // ch):
        mk = jnp.where(mask[k * ch:(k + 1) * ch], 1.0, 0.0).astype(BF16)
        ranks.append(_dot(lower, mk) + count)
        count = count + _dot(ones, mk)[0:1]
    rank = jnp.concatenate(ranks, axis=0)
    shift = int(math.log2(MOE_RC))
    nch = lax.shift_right_logical(count.astype(I32) + (MOE_RC - 1), shift)
    nch8 = jnp.broadcast_to(nch.astype(F32), (8, LANES))
    lr = lax.broadcasted_iota(I32, (LANES, LANES), 0)
    lc = lax.broadcasted_iota(I32, (LANES, LANES), 1)
    offc = _dot(nch8.astype(BF16), jnp.where(lr < lc, 1.0, 0.0).astype(BF16))
    row = offc[0:1] * float(MOE_RC) + rank
    d1 = jnp.sum(jnp.where(hit1, row, 0.0), axis=1, keepdims=True)
    d2 = jnp.sum(jnp.where(hit2, row, 0.0), axis=1, keepdims=True)
    dd = jnp.where(lane == 0, d1, jnp.where(lane == 1, d2, -1.0))
    d_ref[0] = dd.astype(I32)
    off_ref[0] = offc.astype(I32)
    nch_ref[0] = jnp.broadcast_to(nch, (8, LANES))

    ddt = jnp.transpose(dd)
    d1t = ddt[0:1].astype(I32)
    d2t = ddt[1:2].astype(I32)

    wt = jnp.transpose(jnp.where(lane == 0, m1 / tot, jnp.where(lane == 1, m2 / tot, 0.0)))
    w1t = wt[0:1]
    w2t = wt[1:2]
    hb = h.astype(BF16)
    rid = lax.broadcasted_iota(I32, (MOE_GC, tile), 0)
    for k in range(rows // MOE_GC):
        hit1 = rid + k * MOE_GC == d1t
        hit2 = rid + k * MOE_GC == d2t
        onehot = jnp.where(hit1 | hit2, 1.0, 0.0).astype(BF16)
        xg_ref[0, k * MOE_GC:(k + 1) * MOE_GC, :] = _dot(onehot, hb).astype(BF16)
        wrow = jnp.sum(jnp.where(hit1, w1t, jnp.where(hit2, w2t, 0.0)), axis=1, keepdims=True)
        wrow_ref[0, k * MOE_GC:(k + 1) * MOE_GC, :] = jnp.broadcast_to(wrow, (MOE_GC, LANES))


def _dispatch(x, g_pre, scale_f, shift_f, w_router, b_router, tile):
    bsz, s, d = x.shape
    nt = s // tile
    rows = _moe_rows(tile)
    wr = jnp.zeros((d, LANES), F32).at[:, :N_EXPERTS].set(w_router.astype(F32))
    br = jnp.full((1, LANES), NEG, F32).at[0, :N_EXPERTS].set(b_router.astype(F32))
    mod = pl.BlockSpec((1, 1, d), lambda b, i: (b, 0, 0))
    per_tile = lambda r, n: pl.BlockSpec((1, r, n), lambda b, i: (b * nt + i, 0, 0))
    xg, wrow, dest, offc, nch = pl.pallas_call(
        _dispatch_kernel,
        grid=(bsz, nt),
        in_specs=[pl.BlockSpec((1, tile, d), lambda b, i: (b, i, 0)),
                  pl.BlockSpec((1, d), lambda b, i: (0, 0)), mod, mod,
                  pl.BlockSpec((d, LANES), lambda b, i: (0, 0)),
                  pl.BlockSpec((1, LANES), lambda b, i: (0, 0))],
        out_specs=[per_tile(rows, d), per_tile(rows, LANES), per_tile(tile, LANES),
                   per_tile(8, LANES), per_tile(8, LANES)],
        out_shape=[jax.ShapeDtypeStruct((bsz * nt, rows, d), BF16),
                   jax.ShapeDtypeStruct((bsz * nt, rows, LANES), F32),
                   jax.ShapeDtypeStruct((bsz * nt, tile, LANES), I32),
                   jax.ShapeDtypeStruct((bsz * nt, 8, LANES), I32),
                   jax.ShapeDtypeStruct((bsz * nt, 8, LANES), I32)],
        compiler_params=_cparams("arbitrary", "arbitrary"),
        name="moe_dispatch",
    )(x, g_pre.reshape(1, d), scale_f.reshape(bsz, 1, d), shift_f.reshape(bsz, 1, d), wr, br)
    return xg, wrow, dest, offc[:, 0, :N_EXPERTS], nch[:, 0, :N_EXPERTS]


def _slot_order(offc, nch, slots_per_tile):
    ends = offc + nch
    c = jnp.arange(slots_per_tile, dtype=I32)[None, :, None]
    expert = jnp.sum((c >= ends[:, None, :]).astype(I32), axis=-1).reshape(-1)
    n = expert.shape[0]
    order = jnp.argsort(expert * n + jnp.arange(n, dtype=I32)).astype(I32)
    exp_sorted = expert[order]
    used = (exp_sorted < N_EXPERTS).astype(I32)
    return order, jnp.minimum(exp_sorted, N_EXPERTS - 1).astype(I32), used


def _slots_kernel(slot_ref, exp_ref, used_ref, x_ref, wrow_ref, wg_ref, wu_ref, wd_ref, y_ref):
    p = pl.program_id(0)

    @pl.when(used_ref[p] == 1)
    def _():
        x = x_ref[0]
        a = (_silu(_dot(x, wg_ref[0])) * _dot(x, wu_ref[0])).astype(BF16)
        y_ref[0] = (_dot(a, wd_ref[0]) * wrow_ref[0][:, 0:1]).astype(BF16)

    @pl.when(used_ref[p] == 0)
    def _():
        y_ref[0] = jnp.zeros(y_ref.shape[1:], BF16)


def _expert_slots(xg, wrow, order, exp_sorted, used, w_gate, w_up, w_down):
    ntile, rows, d = xg.shape
    nslot = ntile * rows // MOE_RC
    ff = w_gate.shape[2]
    once = pl.Buffered(1)
    slot = lambda n: pl.BlockSpec((1, MOE_RC, n), lambda p, s, e, u: (s[p], 0, 0))
    grid_spec = pltpu.PrefetchScalarGridSpec(
        num_scalar_prefetch=3,
        grid=(nslot,),
        in_specs=[slot(d), slot(LANES),
                  pl.BlockSpec((1, d, ff), lambda p, s, e, u: (e[p], 0, 0), pipeline_mode=once),
                  pl.BlockSpec((1, d, ff), lambda p, s, e, u: (e[p], 0, 0), pipeline_mode=once),
                  pl.BlockSpec((1, ff, d), lambda p, s, e, u: (e[p], 0, 0), pipeline_mode=once)],
        out_specs=slot(d))
    y = pl.pallas_call(
        _slots_kernel,
        grid_spec=grid_spec,
        out_shape=jax.ShapeDtypeStruct((nslot, MOE_RC, d), BF16),
        input_output_aliases={3: 0},
        compiler_params=_cparams("arbitrary"),
        name="moe_slots",
    )(order, exp_sorted, used, xg.reshape(nslot, MOE_RC, d), wrow.reshape(nslot, MOE_RC, LANES),
      w_gate.astype(BF16), w_up.astype(BF16), w_down.astype(BF16))
    return y.reshape(ntile, rows, d)


def _combine_kernel(y_ref, d_ref, x_ref, gpost_ref, gate_ref, o_ref):
    tile = x_ref.shape[1]
    rows = y_ref.shape[1]
    dd = d_ref[0]
    d1c = dd[:, 0:1]
    d2c = dd[:, 1:2]
    cid = lax.broadcasted_iota(I32, (tile, MOE_SC), 1)
    z = None
    for k in range(rows // MOE_SC):
        pt = jnp.where((cid + k * MOE_SC == d1c) | (cid + k * MOE_SC == d2c), 1.0, 0.0).astype(BF16)
        part = _dot(pt, y_ref[0, k * MOE_SC:(k + 1) * MOE_SC, :])
        z = part if z is None else z + part
    o_ref[0] = x_ref[0] + gate_ref[0] * _rms(z, gpost_ref[...])


def _combine(y, dest, x, g_post, gate_f):
    bsz, s, d = x.shape
    ntile, rows, _ = y.shape
    tile = dest.shape[1]
    nt = s // tile
    per_tile = lambda r, n: pl.BlockSpec((1, r, n), lambda i: (i, 0, 0))
    out = pl.pallas_call(
        _combine_kernel,
        grid=(ntile,),
        in_specs=[per_tile(rows, d), per_tile(tile, LANES), per_tile(tile, d),
                  pl.BlockSpec((1, d), lambda i: (0, 0)),
                  pl.BlockSpec((1, 1, d), lambda i: (i // nt, 0, 0))],
        out_specs=per_tile(tile, d),
        out_shape=jax.ShapeDtypeStruct((ntile, tile, d), F32),
        compiler_params=_cparams("arbitrary"),
        name="moe_combine",
    )(y, dest, x.reshape(ntile, tile, d), g_post.reshape(1, d), gate_f.reshape(bsz, 1, d))
    return out.reshape(bsz, s, d)


def _post_kernel(x_ref, f_ref, gpost_ref, gate_ref, o_ref):
    o_ref[0] = x_ref[0] + gate_ref[0] * _rms(f_ref[0], gpost_ref[...])


def _post_residual(x, f, g_post, gate_f):
    bsz, s, d = x.shape
    tm = min(1024, s)
    act = pl.BlockSpec((1, tm, d), lambda b, i: (b, i, 0))
    return pl.pallas_call(
        _post_kernel,
        grid=(bsz, s // tm),
        in_specs=[act, act, pl.BlockSpec((1, d), lambda b, i: (0, 0)),
                  pl.BlockSpec((1, 1, d), lambda b, i: (b, 0, 0))],
        out_specs=act,
        out_shape=jax.ShapeDtypeStruct((bsz, s, d), F32),
        compiler_params=_cparams("arbitrary", "arbitrary"),
        name="post_residual",
    )(x, f, g_post.reshape(1, d), gate_f.reshape(bsz, 1, d))


def _mixer(x, mod, layer, rel_table, g_pre_mix, w_in, cmp_params):
    shift_m, scale_m = mod[:, 0], mod[:, 1]
    proj = _in_proj(x, g_pre_mix, scale_m, shift_m, _arrange_w_in(w_in))
    o_sb = _sb_attention(proj)
    o_nsa = _nsa_attention(proj, _compress(proj, *cmp_params), rel_table)
    return o_sb, o_nsa


def kernel(x, c, rel_table, w_ada, b_ada, g_pre_mix, g_post_mix, g_pre_ffn, g_post_ffn, w_in, w_out, g_sb, g_nsa, cmp_pos_k, cmp_w1_k, cmp_w2_k, cmp_pos_v, cmp_w1_v, cmp_w2_v, ffn_w_gate, ffn_w_up, ffn_w_down, moe_w_router, moe_b_router, moe_w_gate, moe_w_up, moe_w_down):
    bsz, s, d = x.shape
    depth = w_in.shape[0]
    mods = _ada(c, w_ada, b_ada).reshape(depth, bsz, 6, d)
    moe_tile = min(1024, s)
    for layer in range(depth):
        mod = mods[layer]
        cmp_params = (cmp_pos_k[layer], cmp_w1_k[layer], cmp_w2_k[layer],
                      cmp_pos_v[layer], cmp_w1_v[layer], cmp_w2_v[layer])
        o_sb, o_nsa = _mixer(x, mod, layer, rel_table, g_pre_mix[layer], w_in[layer], cmp_params)
        x, h = _out_proj(o_sb, o_nsa, x, g_sb[layer], g_nsa[layer], w_out[layer], g_post_mix[layer],
                         mod[:, 2], g_pre_ffn[layer], mod[:, 4], mod[:, 3])
        i = layer // 2
        if layer % 2 == 0:
            x = _dense_ffn(h, x, ffn_w_gate[i], ffn_w_up[i], ffn_w_down[i], g_post_ffn[layer], mod[:, 5])
        else:
            xg, wrow, dest, offc, nch = _dispatch(x, g_pre_ffn[layer], mod[:, 4], mod[:, 3],
                                                  moe_w_router[i], moe_b_router[i], moe_tile)
            order = _slot_order(offc, nch, xg.shape[1] // MOE_RC)
            y = _expert_slots(xg, wrow, *order, moe_w_gate[i], moe_w_up[i], moe_w_down[i])
            x = _combine(y, dest, x, g_post_ffn[layer], mod[:, 5])
    return x
```

```python
import functools
import math

import numpy as np
import jax
import jax.numpy as jnp
from jax import lax
from jax.experimental import pallas as pl
from jax.experimental.pallas import tpu as pltpu

F32 = jnp.float32
BF16 = jnp.bfloat16
I32 = jnp.int32

LANES = 128
DH = 64
N_SB = 8
N_NSA = 8
G_NSA = 2
Z_NSA = 4
CMP_BLOCK = 32
CMP_STRIDE = 16
SLC_BLOCK = 64
N_SELECT = 16
WINDOW = 512
REL_BUCKETS = 32
REL_MAX_DIST = 128
N_EXPERTS = 8
EPS = 1e-6
FORCED = 1e4
NEG = -1e30
M_INIT = -1e29
SB_EXIT = -104.5
VMEM_LIMIT = 56 * 1024 * 1024

TQ = 256
NEAR_BACK = REL_MAX_DIST
KC_FRONT = 16
SEL_FAR_TILE = 512


def _cparams(*sem):
    return pltpu.CompilerParams(dimension_semantics=sem, vmem_limit_bytes=VMEM_LIMIT)


def _nt(a, b):
    return lax.dot_general(a, b, (((1,), (1,)), ((), ())), preferred_element_type=F32)


def _dot(a, b):
    return jnp.dot(a, b, preferred_element_type=F32)


def _split3(a):
    hi = a.astype(BF16)
    r = a - hi.astype(F32)
    mid = r.astype(BF16)
    lo = (r - mid.astype(F32)).astype(BF16)
    return hi, mid, lo


def _dot_hl(a, b):
    hi = a.astype(BF16)
    lo = (a - hi.astype(F32)).astype(BF16)
    return _dot(hi, b) + _dot(lo, b)


def _dot_f32(a, b):
    ah, am, al = _split3(a)
    bh, bm, bl = _split3(b)
    return (_dot(ah, bh) + (_dot(ah, bm) + _dot(am, bh))
            + (_dot(ah, bl) + _dot(am, bm) + _dot(al, bh)))


def _rms(x, g):
    return x * lax.rsqrt(jnp.mean(x * x, axis=-1, keepdims=True) + EPS) * g


def _ada_kernel(c_ref, w_ref, b_ref, o_ref):
    c = c_ref[...]
    ca = c * (1.0 / (1.0 + jnp.exp(-c)))
    o_ref[0] = _dot_f32(ca, w_ref[0]) + b_ref[0]


def _ada(c, w_ada, b_ada):
    depth, d, n = w_ada.shape
    bsz = c.shape[0]
    rows = 8
    tn = 1536
    cp = jnp.zeros((rows, d), F32).at[:bsz].set(c)
    out = pl.pallas_call(
        _ada_kernel,
        grid=(depth, n // tn),
        in_specs=[pl.BlockSpec((rows, d), lambda l, j: (0, 0)),
                  pl.BlockSpec((1, d, tn), lambda l, j: (l, 0, j)),
                  pl.BlockSpec((1, 1, tn), lambda l, j: (l, 0, j))],
        out_specs=pl.BlockSpec((1, rows, tn), lambda l, j: (l, 0, j)),
        out_shape=jax.ShapeDtypeStruct((depth, rows, n), F32),
        compiler_params=_cparams("arbitrary", "arbitrary"),
        name="ada_mod",
    )(cp, w_ada, b_ada.reshape(depth, 1, n))
    return out[:, :bsz]


def _in_kernel(x_ref, g_ref, sc_ref, sh_ref, w_ref, o_ref, *, cn):
    h = _rms(x_ref[0], g_ref[...]) * (1.0 + sc_ref[0]) + sh_ref[0]
    hb = h.astype(BF16)
    for j in range(w_ref.shape[1] // cn):
        o_ref[0, :, j * cn:(j + 1) * cn] = _dot(hb, w_ref[:, j * cn:(j + 1) * cn]).astype(BF16)


def _in_proj(x, g, scale, shift, w):
    bsz, s, d = x.shape
    nc = w.shape[1]
    tm = min(512, s)
    cn = nc // 3 if (nc // 3) % LANES == 0 else nc
    return pl.pallas_call(
        functools.partial(_in_kernel, cn=cn),
        grid=(bsz, s // tm),
        in_specs=[pl.BlockSpec((1, tm, d), lambda b, i: (b, i, 0)),
                  pl.BlockSpec((1, d), lambda b, i: (0, 0)),
                  pl.BlockSpec((1, 1, d), lambda b, i: (b, 0, 0)),
                  pl.BlockSpec((1, 1, d), lambda b, i: (b, 0, 0)),
                  pl.BlockSpec((d, nc), lambda b, i: (0, 0))],
        out_specs=pl.BlockSpec((1, tm, nc), lambda b, i: (b, i, 0)),
        out_shape=jax.ShapeDtypeStruct((bsz, s, nc), BF16),
        compiler_params=_cparams("arbitrary", "arbitrary"),
        name="in_proj",
    )(x, g.reshape(1, d), scale.reshape(bsz, 1, d), shift.reshape(bsz, 1, d), w)


CB_SBQ, CB_SBK, CB_SBV, CB_NSAQ = 0, 4, 8, 12
CB_KCMP, CB_VCMP = 16, 17
CB_KSLC, CB_VSLC, CB_KWIN, CB_VWIN = 18, 20, 22, 24
CB_GATE = 26
N_CB = 27


def _arrange_w_in(w_in):
    d = w_in.shape[0]
    w_sb = N_SB * DH
    off_nsa_q = 3 * w_sb
    off_kv = off_nsa_q + N_NSA * DH
    off_gate = off_kv + 3 * 2 * G_NSA * DH
    scale = DH ** -0.5

    def kv(br, kvi, g):
        lo = off_kv + ((br * 2 + kvi) * G_NSA + g) * DH
        return w_in[:, lo:lo + DH]

    cols = [w_in[:, 0:w_sb] * scale, w_in[:, w_sb:2 * w_sb], w_in[:, 2 * w_sb:3 * w_sb],
            w_in[:, off_nsa_q:off_kv] * scale,
            kv(0, 0, 0), kv(0, 0, 1), kv(0, 1, 0), kv(0, 1, 1)]
    for br in (1, 2):
        for kvi in (0, 1):
            for g in range(G_NSA):
                cols += [kv(br, kvi, g), kv(br, kvi, g)]
    n_gate = 3 * N_NSA
    cols += [w_in[:, off_gate:off_gate + n_gate], jnp.zeros((d, LANES - n_gate), w_in.dtype)]
    out = jnp.concatenate(cols, axis=1).astype(BF16)
    assert out.shape[1] == N_CB * LANES
    return out


def _sb_kernel(q_ref, k_ref, v_ref, o_ref, *, t):
    qi = pl.program_id(2)
    q = q_ref[0]
    lane = lax.broadcasted_iota(I32, (t, LANES), 1)
    row = lax.broadcasted_iota(I32, (t, t), 0)
    col = lax.broadcasted_iota(I32, (t, t), 1)
    upper = jnp.where(row > col, 1.0, 0.0).astype(BF16)
    upper2 = jnp.concatenate([upper, upper], axis=0)
    ones2 = jnp.ones((2 * t, LANES), BF16)
    causal = col < row
    rep = t // LANES

    def tile(qh, k, v, carry, acc, diag):
        z = _nt(qh, k)
        lk = -(jnp.maximum(z, 0.0) + jnp.log(1.0 + jnp.exp(-jnp.abs(z))))
        if diag:
            lk = jnp.where(causal, lk, 0.0)
        hi = lk.astype(BF16)
        lo = (lk - hi.astype(F32)).astype(BF16)
        hl = jnp.concatenate([hi, lo], axis=1)
        later = _dot(hl, upper2)
        tot = _dot(hl, ones2)
        cb = carry if rep == 1 else jnp.concatenate([carry] * rep, axis=1)
        w = jnp.exp(z + lk + later + cb)
        if diag:
            w = jnp.where(causal, w, 0.0)
        return carry + tot, acc + _dot(w.astype(BF16), v)

    n_head = 2 * (q.shape[1] // LANES)
    zq = jnp.zeros((t, LANES), BF16)
    qhs = []
    for c in range(q.shape[1] // LANES):
        qc = q[:, c * LANES:(c + 1) * LANES]
        qhs += [jnp.where(lane < DH, qc, zq), jnp.where(lane < DH, zq, qc)]
    zero = jnp.zeros((t, LANES), F32)

    def step(kt, st, diag):
        k = k_ref[0, pl.ds(kt * t, t), :]
        v = v_ref[0, pl.ds(kt * t, t), :]
        out, mx = [], None
        for h in range(n_head):
            c = (h // 2) * LANES
            carry, acc = tile(qhs[h], k[:, c:c + LANES], v[:, c:c + LANES], st[2 * h], st[2 * h + 1], diag)
            out += [carry, acc]
            mx = jnp.max(carry) if mx is None else jnp.maximum(mx, jnp.max(carry))
        return mx, tuple(out)

    mx, st = step(qi, (zero,) * (2 * n_head), True)

    def cond(s):
        return jnp.logical_and(s[0] >= 0, s[1] > SB_EXIT)

    def body(s):
        mx, st = step(s[0], s[2], False)
        return s[0] - 1, mx, st

    _, _, st = lax.while_loop(cond, body, (qi - 1, mx, st))
    o_ref[0] = jnp.concatenate([jnp.where(lane < DH, st[4 * c + 1], st[4 * c + 3])
                                for c in range(n_head // 2)], axis=1)


def _sb_attention(proj, t=256, pairs=4):
    bsz, s, _ = proj.shape
    t = min(t, s)
    npair = N_SB // 2
    w = pairs * LANES
    return pl.pallas_call(
        functools.partial(_sb_kernel, t=t),
        grid=(bsz, npair // pairs, s // t),
        in_specs=[pl.BlockSpec((1, t, w), lambda b, j, i: (b, i, CB_SBQ // pairs + j)),
                  pl.BlockSpec((1, s, w), lambda b, j, i: (b, 0, CB_SBK // pairs + j)),
                  pl.BlockSpec((1, s, w), lambda b, j, i: (b, 0, CB_SBV // pairs + j))],
        out_specs=pl.BlockSpec((1, t, w), lambda b, j, i: (b, i, j)),
        out_shape=jax.ShapeDtypeStruct((bsz, s, N_SB * DH), F32),
        compiler_params=_cparams("arbitrary", "arbitrary", "arbitrary"),
        name="sb_attn",
    )(proj, proj, proj)


def _gelu_tanh(x):
    return 0.5 * x * (1.0 + jnp.tanh(math.sqrt(2.0 / math.pi) * (x + 0.044715 * (x * x * x))))


def _cmp_kernel(xa_ref, xb_ref, pos_ref, w1_ref, w2_ref, o_ref):
    half = w1_ref.shape[1] // 2
    w1a = w1_ref[0, :half, :]
    w1b = w1_ref[0, half:, :]
    pos = pos_ref[0]
    bias = _dot(pos[:, :half], w1a) + _dot(pos[:, half:], w1b)
    hid = _dot(xa_ref[0, 0], w1a) + _dot(xb_ref[0, 0], w1b) + bias[0:1, :]
    o_ref[0, 0] = _dot(_gelu_tanh(hid).astype(BF16), w2_ref[0])


def _compress(proj, pos_k, w1_k, w2_k, pos_v, w1_v, w2_v):
    bsz, s, _ = proj.shape
    n16 = s // CMP_STRIDE
    raw = proj[:, :, CB_KCMP * LANES:(CB_VCMP + 1) * LANES]
    x16 = raw.reshape(bsz, s, 4, DH).transpose(0, 2, 1, 3).reshape(bsz, 4, n16, CMP_STRIDE * DH)
    x16b = jnp.concatenate([x16[:, :, 1:], jnp.zeros_like(x16[:, :, :1])], axis=2)
    hidden = w1_k.shape[1]
    w1 = jnp.stack([w1_k, w1_v]).astype(BF16)
    w2 = jnp.stack([jnp.concatenate([w2_k, w2_k], 1), jnp.concatenate([w2_v, w2_v], 1)]).astype(BF16)
    pos = jnp.stack([pos_k.reshape(1, -1), pos_v.reshape(1, -1)])
    pos = jnp.concatenate([pos, jnp.zeros((2, 7, pos.shape[2]), pos.dtype)], axis=1).astype(BF16)
    k16 = CMP_STRIDE * DH
    return pl.pallas_call(
        _cmp_kernel,
        grid=(bsz, 4),
        in_specs=[pl.BlockSpec((1, 1, n16, k16), lambda b, j: (b, j, 0, 0)),
                  pl.BlockSpec((1, 1, n16, k16), lambda b, j: (b, j, 0, 0)),
                  pl.BlockSpec((1, 8, 2 * k16), lambda b, j: (j // 2, 0, 0)),
                  pl.BlockSpec((1, 2 * k16, hidden), lambda b, j: (j // 2, 0, 0)),
                  pl.BlockSpec((1, hidden, LANES), lambda b, j: (j // 2, 0, 0))],
        out_specs=pl.BlockSpec((1, 1, n16, LANES), lambda b, j: (b, j, 0, 0)),
        out_shape=jax.ShapeDtypeStruct((bsz, 4, n16, LANES), F32),
        compiler_params=_cparams("arbitrary", "arbitrary"),
        name="nsa_compress",
    )(x16, x16b, pos, w1, w2)


def _bucket_table():
    n = np.arange(REL_MAX_DIST + 1)
    exact = REL_BUCKETS // 2
    val = (np.log(np.maximum(n, 1).astype(np.float32) / np.float32(exact)).astype(np.float32)
           / np.float32(math.log(REL_MAX_DIST / exact)) * np.float32(REL_BUCKETS - exact))
    large = np.minimum(exact + val.astype(np.int32), REL_BUCKETS - 1)
    return np.where(n < exact, n, large).astype(np.int32)


def _bias_tiles(rel_table):
    bucket = _bucket_table()
    far_row = rel_table[int(bucket[REL_MAX_DIST])].astype(F32)
    r = np.arange(TQ)[:, None]

    def tile(dist, valid, base):
        onehot = jax.nn.one_hot(bucket[np.clip(dist, 0, REL_MAX_DIST)], REL_BUCKETS, dtype=F32)
        t = jnp.einsum('rwb,bh->hrw', onehot, rel_table.astype(F32),
                       precision=lax.Precision.HIGHEST)
        t = jnp.where(valid[None], t - base, NEG)
        return t.reshape(G_NSA, Z_NSA * TQ, dist.shape[1])

    w = np.arange(LANES)[None, :]
    d_c = r - CMP_STRIDE * (w - KC_FRONT) - (CMP_BLOCK - 1)
    w2 = np.arange(NEAR_BACK + TQ)[None, :]
    d_s = r + NEAR_BACK - w2
    w5 = np.arange(WINDOW + TQ)[None, :]
    d_w = r + WINDOW - w5
    far = jnp.broadcast_to(far_row[:, None, None], (N_NSA, TQ, LANES))
    far = far.reshape(G_NSA, Z_NSA * TQ, LANES)
    return (tile(d_c, d_c >= 0, far_row[:, None, None]), tile(d_s, d_s >= 0, 0.0),
            tile(d_w, (d_w >= 0) & (d_w < WINDOW), 0.0), far)


def _gate_expand():
    e = np.zeros((G_NSA, LANES, 3 * Z_NSA * DH), np.float32)
    for g in range(G_NSA):
        for z in range(Z_NSA):
            for br in range(3):
                e[g, (g * Z_NSA + z) * 3 + br, br * Z_NSA * DH + z * DH: br * Z_NSA * DH + (z + 1) * DH] = 1.0
    return e


def _overlap_padded(n16, kcp, n_slc):
    i = np.arange(kcp)[:, None] - KC_FRONT
    j = np.arange(LANES)[None, :]
    n_cmp = n16 - 1
    ok = (i >= 0) & (i < n_cmp) & (j < n_slc)
    ov = (i * CMP_STRIDE < j * SLC_BLOCK + SLC_BLOCK) & (i * CMP_STRIDE + CMP_BLOCK > j * SLC_BLOCK)
    return (ok & ov).astype(np.float32)


def _block_onehot(s):
    return (np.arange(s)[:, None] // SLC_BLOCK == np.arange(LANES)[None, :]).astype(np.float32)


def _nsa_kernel(q_ref, kcb_ref, vcb_ref, kcf_ref, vcf_ref, ks_ref, vs_ref, kw_ref, vw_ref, et_ref,
                gate_ref, ovb_ref, ovf_ref, bcn_ref, bsn_ref, bw_ref, cf_ref, eg_ref, o_ref,
                sb0_ref, sb1_ref, *, n_slc, n_round):
    blk = pl.program_id(2)
    c0 = blk * TQ
    r4 = Z_NSA * TQ
    q = q_ref[0]
    lane = lax.broadcasted_iota(I32, (TQ, LANES), 1)
    lo_half = lane < DH
    zq = jnp.zeros((TQ, LANES), BF16)
    qa, qb = q[:, :LANES], q[:, LANES:]
    qs = jnp.concatenate([jnp.where(lo_half, qa, zq), jnp.where(lo_half, zq, qa),
                          jnp.where(lo_half, qb, zq), jnp.where(lo_half, zq, qb)], axis=0)
    ones_k = jnp.ones((SEL_FAR_TILE, LANES), BF16)
    ones_w = jnp.ones((WINDOW + TQ, LANES), BF16)

    kcp = kcb_ref.shape[2]
    prow = lax.broadcasted_iota(I32, (1, kcp), 1)
    near0 = pl.multiple_of(blk * (TQ // CMP_STRIDE), 8)
    far_row = jnp.where((prow >= KC_FRONT) & (prow < near0), 0.0, NEG)
    lane1 = lax.broadcasted_iota(I32, (1, LANES), 1)
    near_row = jnp.where(lane1 + near0 >= KC_FRONT, 0.0, NEG)
    kcn = kcf_ref[0, 0, pl.ds(near0, LANES), :].astype(BF16)
    vcn = vcf_ref[0, 0, pl.ds(near0, LANES), :].astype(BF16)
    s_far = _nt(qs, kcb_ref[0, 0]) + far_row
    s_near = _nt(qs, kcn) + bcn_ref[0] + near_row
    m_c = jnp.maximum(jnp.max(s_far, axis=1, keepdims=True), jnp.max(s_near, axis=1, keepdims=True))
    m_c = jnp.maximum(m_c, M_INIT)
    p_far = jnp.exp(s_far - m_c)
    p_near = jnp.exp(s_near - m_c)
    den = jnp.sum(p_far, axis=1, keepdims=True) + jnp.sum(p_near, axis=1, keepdims=True)
    inv_c = 1.0 / jnp.maximum(den, 1e-30)
    o_c = (_dot(p_far.astype(BF16), vcb_ref[0, 0]) + _dot(p_near.astype(BF16), vcn)) * inv_c

    pn_far = p_far * inv_c
    pn_near = p_near * inv_c
    pz_far = pn_far[0:TQ] + pn_far[TQ:2 * TQ] + pn_far[2 * TQ:3 * TQ] + pn_far[3 * TQ:]
    pz_near = pn_near[0:TQ] + pn_near[TQ:2 * TQ] + pn_near[2 * TQ:3 * TQ] + pn_near[3 * TQ:]
    ovn = ovf_ref[pl.ds(near0, LANES), :].astype(BF16)
    imp = _dot_hl(pz_far, ovb_ref[...]) + _dot_hl(pz_near, ovn)
    tpos = c0 + lax.broadcasted_iota(I32, (TQ, LANES), 0)
    cur = lax.shift_right_logical(tpos, 6)
    valid = (lane * SLC_BLOCK <= tpos) & (lane < n_slc)
    forced = valid & ((lane == 0) | (lane == cur) | (lane == cur - 1))
    score = jnp.where(valid & jnp.logical_not(forced), imp, -1.0)
    score = jnp.where(lane < n_slc, score, -2.0)

    rest, taken, thr = score, jnp.zeros((TQ, 1), F32), jnp.zeros((TQ, 1), F32)
    for _ in range(n_round):
        m = jnp.max(rest, axis=1, keepdims=True)
        eq = rest == m
        thr = jnp.where(taken < n_round, m, thr)
        taken = taken + jnp.sum(jnp.where(eq, 1.0, 0.0), axis=1, keepdims=True)
        rest = jnp.where(eq, -jnp.inf, rest)
    above = score > thr
    at_thr = score == thr
    need = n_round - jnp.sum(jnp.where(above, 1.0, 0.0), axis=1, keepdims=True)
    brow = lax.broadcasted_iota(I32, (LANES, LANES), 0)
    bcol = lax.broadcasted_iota(I32, (LANES, LANES), 1)
    before = _dot(jnp.where(at_thr, 1.0, 0.0).astype(BF16),
                  jnp.where(brow < bcol, 1.0, 0.0).astype(BF16))
    sel = ((above | (at_thr & (before < need))) & valid) | forced

    kws, vws = [], []
    for w in range((WINDOW + TQ) // LANES):
        st = pl.multiple_of(jnp.maximum(c0 - WINDOW + w * LANES, 0), LANES)
        kws.append(kw_ref[0, pl.ds(st, LANES), :])
        vws.append(vw_ref[0, pl.ds(st, LANES), :])
    wl = lax.broadcasted_iota(I32, (1, WINDOW + TQ), 1)
    pos_row = jnp.where(c0 - WINDOW + wl >= 0, 0.0, NEG)
    s = _nt(qs, jnp.concatenate(kws, axis=0)) + bw_ref[0] + pos_row
    p = jnp.exp(s - jnp.max(s, axis=1, keepdims=True)).astype(BF16)
    r_w = _dot(p, jnp.concatenate([jnp.concatenate(vws, axis=0), ones_w], axis=1))
    o_w = r_w[:, :LANES] / r_w[:, LANES:]
    gl = _dot(gate_ref[0], eg_ref[0])
    sig = 1.0 / (1.0 + jnp.exp(-gl))
    wide = Z_NSA * DH

    sel_all = jnp.where(sel, 0.0, NEG).astype(BF16)
    far_blocks = blk * (TQ // SLC_BLOCK) - NEAR_BACK // SLC_BLOCK
    sel_far = jnp.where(sel & (lane < far_blocks), 0.0, NEG).astype(BF16)
    q_far = jnp.concatenate([qs, jnp.concatenate([sel_far] * Z_NSA, axis=0)], axis=1)
    q_near = jnp.concatenate([qs, jnp.concatenate([sel_all] * Z_NSA, axis=0)], axis=1)
    tk = SEL_FAR_TILE
    last_tile = ks_ref.shape[1] // tk - 1

    def qk(kt):
        k0 = pl.multiple_of(kt * tk, tk)
        return _nt(q_far, jnp.concatenate([ks_ref[0, pl.ds(k0, tk), :], et_ref[pl.ds(k0, tk), :]], axis=1))

    def fold(s, v, st):
        m, acc = st
        mn = jnp.maximum(m, jnp.max(s, axis=1, keepdims=True))
        p = jnp.exp(s - mn).astype(BF16)
        return mn, jnp.exp(m - mn) * acc + _dot(p, v)

    def v_tile(kt):
        k0 = pl.multiple_of(kt * tk, tk)
        return jnp.concatenate([vs_ref[0, pl.ds(k0, tk), :], ones_k], axis=1)

    n_pair = lax.shift_right_logical(jnp.maximum(c0 - NEAR_BACK, 0) + (2 * tk - 1), int(math.log2(2 * tk)))
    sb0_ref[...] = qk(0)

    def far_step(i, st):
        sb1_ref[...] = qk(2 * i + 1)
        st = fold(sb0_ref[...], v_tile(2 * i), st)
        sb0_ref[...] = qk(jnp.minimum(2 * i + 2, last_tile))
        return fold(sb1_ref[...], v_tile(2 * i + 1), st)

    m0 = jnp.full((r4, 1), M_INIT, F32)
    m_s, acc_s = lax.fori_loop(0, n_pair, far_step, (m0, jnp.zeros((r4, 2 * LANES), F32)))
    m_s = m_s + cf_ref[0][:, 0:1]
    n_near = NEAR_BACK + TQ
    p0 = pl.multiple_of(jnp.maximum(c0 - NEAR_BACK, 0), NEAR_BACK)
    d0 = pl.multiple_of(c0, TQ)
    kn = jnp.concatenate([ks_ref[0, pl.ds(p0, NEAR_BACK), :], ks_ref[0, pl.ds(d0, TQ), :]], axis=0)
    vn = jnp.concatenate([vs_ref[0, pl.ds(p0, NEAR_BACK), :], vs_ref[0, pl.ds(d0, TQ), :]], axis=0)
    nrow = lax.broadcasted_iota(I32, (n_near, LANES), 0)
    nlane = lax.broadcasted_iota(I32, (n_near, LANES), 1)
    e_n = jnp.where(lax.shift_right_logical(nrow, 6) + far_blocks == nlane, 1.0, 0.0).astype(BF16)
    lane2 = lax.broadcasted_iota(I32, (1, n_near), 1)
    prev_row = jnp.where((lane2 < NEAR_BACK) & (blk == 0), NEG, 0.0)
    s = _nt(q_near, jnp.concatenate([kn, e_n], axis=1)) + bsn_ref[0] + prev_row
    _, acc_s = fold(s, jnp.concatenate([vn, ones_k[:n_near]], axis=1), (m_s, acc_s))
    o_s = acc_s[:, :LANES] / acc_s[:, LANES:]

    def heads(o):
        return jnp.concatenate([jnp.where(lo_half, o[0:TQ], o[TQ:2 * TQ]),
                                jnp.where(lo_half, o[2 * TQ:3 * TQ], o[3 * TQ:])], axis=1)

    o_ref[0] = (sig[:, 0:wide] * heads(o_c) + sig[:, wide:2 * wide] * heads(o_s)
                + sig[:, 2 * wide:] * heads(o_w))


def _nsa_attention(proj, cmp_out, rel_table):
    bsz, s, _ = proj.shape
    assert s % (2 * SEL_FAR_TILE) == 0 and s >= WINDOW + TQ
    n16 = s // CMP_STRIDE
    n_slc = s // SLC_BLOCK
    assert N_SELECT <= n_slc <= LANES
    nq = s // TQ
    kcp = -(-(max(n16 + KC_FRONT, (TQ // CMP_STRIDE) * (nq - 1) + LANES)) // LANES) * LANES
    pad = ((0, 0), (0, 0), (KC_FRONT, kcp - n16 - KC_FRONT), (0, 0))
    cf = jnp.pad(cmp_out, pad)
    cb = cf.astype(BF16)
    ov = _overlap_padded(n16, kcp, n_slc)
    bcn, bsn, bw, far = _bias_tiles(rel_table)
    eg = jnp.asarray(_gate_expand(), BF16)
    r4 = Z_NSA * TQ
    kv_spec = lambda cb0: pl.BlockSpec((1, s, LANES), lambda b, g, i, cb0=cb0: (b, 0, cb0 + g))
    cmp_spec = lambda j0: pl.BlockSpec((1, 1, kcp, LANES), lambda b, g, i, j0=j0: (b, j0 + g, 0, 0))
    tile_spec = lambda w: pl.BlockSpec((1, r4, w), lambda b, g, i: (g, 0, 0))
    const2 = lambda n: pl.BlockSpec((n, LANES), lambda b, g, i: (0, 0))
    return pl.pallas_call(
        functools.partial(_nsa_kernel, n_slc=n_slc, n_round=N_SELECT - 3),
        grid=(bsz, G_NSA, nq),
        in_specs=[pl.BlockSpec((1, TQ, 2 * LANES), lambda b, g, i: (b, i, CB_NSAQ // 2 + g)),
                  cmp_spec(0), cmp_spec(2), cmp_spec(0), cmp_spec(2),
                  kv_spec(CB_KSLC), kv_spec(CB_VSLC), kv_spec(CB_KWIN), kv_spec(CB_VWIN),
                  const2(s),
                  pl.BlockSpec((1, TQ, LANES), lambda b, g, i: (b, i, CB_GATE)),
                  const2(kcp), const2(kcp),
                  tile_spec(LANES), tile_spec(NEAR_BACK + TQ), tile_spec(WINDOW + TQ), tile_spec(LANES),
                  pl.BlockSpec((1, LANES, 3 * Z_NSA * DH), lambda b, g, i: (g, 0, 0))],
        out_specs=pl.BlockSpec((1, TQ, Z_NSA * DH), lambda b, g, i: (b, i, g)),
        out_shape=jax.ShapeDtypeStruct((bsz, s, N_NSA * DH), F32),
        scratch_shapes=[pltpu.VMEM((r4, SEL_FAR_TILE), F32), pltpu.VMEM((r4, SEL_FAR_TILE), F32)],
        compiler_params=_cparams("arbitrary", "arbitrary", "arbitrary"),
        name="nsa_attn",
    )(proj, cb, cb, cf, cf, proj, proj, proj, proj, jnp.asarray(_block_onehot(s), BF16), proj,
      jnp.asarray(ov, BF16), jnp.asarray(ov, F32), bcn, bsn, bw, far, eg)


def _out_kernel(osb_ref, onsa_ref, x_ref, gsb_ref, gnsa_ref, w_ref, gpost_ref, gate_ref,
                gpre_ref, sc_ref, sh_ref, xo_ref, h_ref):
    half = osb_ref.shape[2]
    a = _rms(osb_ref[0], gsb_ref[...]).astype(BF16)
    b = _rms(onsa_ref[0], gnsa_ref[...]).astype(BF16)
    m = _dot(a, w_ref[:half, :]) + _dot(b, w_ref[half:, :])
    x = x_ref[0] + gate_ref[0] * _rms(m, gpost_ref[...])
    xo_ref[0] = x
    h_ref[0] = (_rms(x, gpre_ref[...]) * (1.0 + sc_ref[0]) + sh_ref[0]).astype(BF16)


def _out_proj(o_sb, o_nsa, x, g_sb, g_nsa, w_out, g_post, gate_m, g_pre_ffn, scale_f, shift_f):
    bsz, s, d = x.shape
    half = o_sb.shape[2]
    tm = min(512, s)
    row = lambda n: pl.BlockSpec((1, n), lambda b, i: (0, 0))
    mod = pl.BlockSpec((1, 1, d), lambda b, i: (b, 0, 0))
    act = lambda n: pl.BlockSpec((1, tm, n), lambda b, i: (b, i, 0))
    return pl.pallas_call(
        _out_kernel,
        grid=(bsz, s // tm),
        in_specs=[act(half), act(half), act(d), row(half), row(half),
                  pl.BlockSpec((2 * half, d), lambda b, i: (0, 0)), row(d), mod, row(d), mod, mod],
        out_specs=[act(d), act(d)],
        out_shape=[jax.ShapeDtypeStruct((bsz, s, d), F32), jax.ShapeDtypeStruct((bsz, s, d), BF16)],
        compiler_params=_cparams("arbitrary", "arbitrary"),
        name="out_proj",
    )(o_sb, o_nsa, x, g_sb.reshape(1, half), g_nsa.reshape(1, half), w_out.astype(BF16),
      g_post.reshape(1, d), gate_m.reshape(bsz, 1, d), g_pre_ffn.reshape(1, d),
      scale_f.reshape(bsz, 1, d), shift_f.reshape(bsz, 1, d))


def _silu(x):
    return x * (1.0 / (1.0 + jnp.exp(-x)))


def _ffn_kernel(h_ref, x_ref, wg_ref, wu_ref, wd_ref, gpost_ref, gate_ref, o_ref):
    h = h_ref[0]
    a = (_silu(_dot(h, wg_ref[...])) * _dot(h, wu_ref[...])).astype(BF16)
    o_ref[0] = x_ref[0] + gate_ref[0] * _rms(_dot(a, wd_ref[...]), gpost_ref[...])


def _dense_ffn(h, x, w_gate, w_up, w_down, g_post, gate_f):
    bsz, s, d = x.shape
    ff = w_gate.shape[1]
    tm = min(512, s)
    act = pl.BlockSpec((1, tm, d), lambda b, i: (b, i, 0))
    once = pl.Buffered(1)
    return pl.pallas_call(
        _ffn_kernel,
        grid=(bsz, s // tm),
        in_specs=[act, act,
                  pl.BlockSpec((d, ff), lambda b, i: (0, 0), pipeline_mode=once),
                  pl.BlockSpec((d, ff), lambda b, i: (0, 0), pipeline_mode=once),
                  pl.BlockSpec((ff, d), lambda b, i: (0, 0), pipeline_mode=once),
                  pl.BlockSpec((1, d), lambda b, i: (0, 0)),
                  pl.BlockSpec((1, 1, d), lambda b, i: (b, 0, 0))],
        out_specs=act,
        out_shape=jax.ShapeDtypeStruct((bsz, s, d), F32),
        compiler_params=_cparams("arbitrary", "arbitrary"),
        name="dense_ffn",
    )(h, x, w_gate.astype(BF16), w_up.astype(BF16), w_down.astype(BF16),
      g_post.reshape(1, d), gate_f.reshape(bsz, 1, d))


MOE_RC = 128
MOE_GC = 256
MOE_SC = 1024


def _moe_rows(tile):
    return -(-(2 * tile + N_EXPERTS * MOE_RC) // MOE_SC) * MOE_SC


def _dispatch_kernel(x_ref, gpre_ref, sc_ref, sh_ref, wr_ref, br_ref,
                     xg_ref, wrow_ref, d_ref, off_ref, nch_ref):
    tile = x_ref.shape[1]
    rows = xg_ref.shape[1]
    h = _rms(x_ref[0], gpre_ref[...]) * (1.0 + sc_ref[0]) + sh_ref[0]
    logits = _dot_f32(h, wr_ref[...]) + br_ref[...]
    lane = lax.broadcasted_iota(I32, (tile, LANES), 1)
    lanef = lane.astype(F32)
    e = jnp.exp(logits - jnp.max(logits, axis=1, keepdims=True))
    probs = e / jnp.sum(e, axis=1, keepdims=True)
    probs = jnp.where(lane < N_EXPERTS, probs, -1.0)
    m1 = jnp.max(probs, axis=1, keepdims=True)
    i1 = jnp.min(jnp.where(probs == m1, lanef, 1e9), axis=1, keepdims=True)
    rest = jnp.where(lanef == i1, -1.0, probs)
    m2 = jnp.max(rest, axis=1, keepdims=True)
    i2 = jnp.min(jnp.where(rest == m2, lanef, 1e9), axis=1, keepdims=True)
    tot = m1 + m2
    hit1 = lanef == i1
    hit2 = lanef == i2
    mask = hit1 | hit2
    ch = 256
    r = lax.broadcasted_iota(I32, (ch, ch), 0)
    c = lax.broadcasted_iota(I32, (ch, ch), 1)
    lower = jnp.where(c < r, 1.0, 0.0).astype(BF16)
    ones = jnp.ones((8, ch), BF16)
    count = jnp.zeros((1, LANES), F32)
    ranks = []
    for k in range(tile // ch):
        mk = jnp.where(mask[k * ch:(k + 1) * ch], 1.0, 0.0).astype(BF16)
        ranks.append(_dot(lower, mk) + count)
        count = count + _dot(ones, mk)[0:1]
    rank = jnp.concatenate(ranks, axis=0)
    shift = int(math.log2(MOE_RC))
    nch = lax.shift_right_logical(count.astype(I32) + (MOE_RC - 1), shift)
    nch8 = jnp.broadcast_to(nch.astype(F32), (8, LANES))
    lr = lax.broadcasted_iota(I32, (LANES, LANES), 0)
    lc = lax.broadcasted_iota(I32, (LANES, LANES), 1)
    offc = _dot(nch8.astype(BF16), jnp.where(lr < lc, 1.0, 0.0).astype(BF16))
    row = offc[0:1] * float(MOE_RC) + rank
    d1 = jnp.sum(jnp.where(hit1, row, 0.0), axis=1, keepdims=True)
    d2 = jnp.sum(jnp.where(hit2, row, 0.0), axis=1, keepdims=True)
    dd = jnp.where(lane == 0, d1, jnp.where(lane == 1, d2, -1.0))
    d_ref[0] = dd.astype(I32)
    off_ref[0] = offc.astype(I32)
    nch_ref[0] = jnp.broadcast_to(nch, (8, LANES))

    ddt = jnp.transpose(dd)
    d1t = ddt[0:1].astype(I32)
    d2t = ddt[1:2].astype(I32)

    wt = jnp.transpose(jnp.where(lane == 0, m1 / tot, jnp.where(lane == 1, m2 / tot, 0.0)))
    w1t = wt[0:1]
    w2t = wt[1:2]
    hb = h.astype(BF16)
    rid = lax.broadcasted_iota(I32, (MOE_GC, tile), 0)
    for k in range(rows // MOE_GC):
        hit1 = rid + k * MOE_GC == d1t
        hit2 = rid + k * MOE_GC == d2t
        onehot = jnp.where(hit1 | hit2, 1.0, 0.0).astype(BF16)
        xg_ref[0, k * MOE_GC:(k + 1) * MOE_GC, :] = _dot(onehot, hb).astype(BF16)
        wrow = jnp.sum(jnp.where(hit1, w1t, jnp.where(hit2, w2t, 0.0)), axis=1, keepdims=True)
        wrow_ref[0, k * MOE_GC:(k + 1) * MOE_GC, :] = jnp.broadcast_to(wrow, (MOE_GC, LANES))


def _dispatch(x, g_pre, scale_f, shift_f, w_router, b_router, tile):
    bsz, s, d = x.shape
    nt = s // tile
    rows = _moe_rows(tile)
    wr = jnp.zeros((d, LANES), F32).at[:, :N_EXPERTS].set(w_router.astype(F32))
    br = jnp.full((1, LANES), NEG, F32).at[0, :N_EXPERTS].set(b_router.astype(F32))
    mod = pl.BlockSpec((1, 1, d), lambda b, i: (b, 0, 0))
    per_tile = lambda r, n: pl.BlockSpec((1, r, n), lambda b, i: (b * nt + i, 0, 0))
    xg, wrow, dest, offc, nch = pl.pallas_call(
        _dispatch_kernel,
        grid=(bsz, nt),
        in_specs=[pl.BlockSpec((1, tile, d), lambda b, i: (b, i, 0)),
                  pl.BlockSpec((1, d), lambda b, i: (0, 0)), mod, mod,
                  pl.BlockSpec((d, LANES), lambda b, i: (0, 0)),
                  pl.BlockSpec((1, LANES), lambda b, i: (0, 0))],
        out_specs=[per_tile(rows, d), per_tile(rows, LANES), per_tile(tile, LANES),
                   per_tile(8, LANES), per_tile(8, LANES)],
        out_shape=[jax.ShapeDtypeStruct((bsz * nt, rows, d), BF16),
                   jax.ShapeDtypeStruct((bsz * nt, rows, LANES), F32),
                   jax.ShapeDtypeStruct((bsz * nt, tile, LANES), I32),
                   jax.ShapeDtypeStruct((bsz * nt, 8, LANES), I32),
                   jax.ShapeDtypeStruct((bsz * nt, 8, LANES), I32)],
        compiler_params=_cparams("arbitrary", "arbitrary"),
        name="moe_dispatch",
    )(x, g_pre.reshape(1, d), scale_f.reshape(bsz, 1, d), shift_f.reshape(bsz, 1, d), wr, br)
    return xg, wrow, dest, offc[:, 0, :N_EXPERTS], nch[:, 0, :N_EXPERTS]


def _slot_order(offc, nch, slots_per_tile):
    ends = offc + nch
    c = jnp.arange(slots_per_tile, dtype=I32)[None, :, None]
    expert = jnp.sum((c >= ends[:, None, :]).astype(I32), axis=-1).reshape(-1)
    n = expert.shape[0]
    order = jnp.argsort(expert * n + jnp.arange(n, dtype=I32)).astype(I32)
    exp_sorted = expert[order]
    used = (exp_sorted < N_EXPERTS).astype(I32)
    return order, jnp.minimum(exp_sorted, N_EXPERTS - 1).astype(I32), used


def _slots_kernel(slot_ref, exp_ref, used_ref, x_ref, wrow_ref, wg_ref, wu_ref, wd_ref, y_ref):
    p = pl.program_id(0)

    @pl.when(used_ref[p] == 1)
    def _():
        x = x_ref[0]
        a = (_silu(_dot(x, wg_ref[0])) * _dot(x, wu_ref[0])).astype(BF16)
        y_ref[0] = (_dot(a, wd_ref[0]) * wrow_ref[0][:, 0:1]).astype(BF16)

    @pl.when(used_ref[p] == 0)
    def _():
        y_ref[0] = jnp.zeros(y_ref.shape[1:], BF16)


def _expert_slots(xg, wrow, order, exp_sorted, used, w_gate, w_up, w_down):
    ntile, rows, d = xg.shape
    nslot = ntile * rows // MOE_RC
    ff = w_gate.shape[2]
    once = pl.Buffered(1)
    slot = lambda n: pl.BlockSpec((1, MOE_RC, n), lambda p, s, e, u: (s[p], 0, 0))
    grid_spec = pltpu.PrefetchScalarGridSpec(
        num_scalar_prefetch=3,
        grid=(nslot,),
        in_specs=[slot(d), slot(LANES),
                  pl.BlockSpec((1, d, ff), lambda p, s, e, u: (e[p], 0, 0), pipeline_mode=once),
                  pl.BlockSpec((1, d, ff), lambda p, s, e, u: (e[p], 0, 0), pipeline_mode=once),
                  pl.BlockSpec((1, ff, d), lambda p, s, e, u: (e[p], 0, 0), pipeline_mode=once)],
        out_specs=slot(d))
    y = pl.pallas_call(
        _slots_kernel,
        grid_spec=grid_spec,
        out_shape=jax.ShapeDtypeStruct((nslot, MOE_RC, d), BF16),
        input_output_aliases={3: 0},
        compiler_params=_cparams("arbitrary"),
        name="moe_slots",
    )(order, exp_sorted, used, xg.reshape(nslot, MOE_RC, d), wrow.reshape(nslot, MOE_RC, LANES),
      w_gate.astype(BF16), w_up.astype(BF16), w_down.astype(BF16))
    return y.reshape(ntile, rows, d)


def _combine_kernel(y_ref, d_ref, x_ref, gpost_ref, gate_ref, o_ref):
    tile = x_ref.shape[1]
    rows = y_ref.shape[1]
    dd = d_ref[0]
    d1c = dd[:, 0:1]
    d2c = dd[:, 1:2]
    cid = lax.broadcasted_iota(I32, (tile, MOE_SC), 1)
    z = None
    for k in range(rows // MOE_SC):
        pt = jnp.where((cid + k * MOE_SC == d1c) | (cid + k * MOE_SC == d2c), 1.0, 0.0).astype(BF16)
        part = _dot(pt, y_ref[0, k * MOE_SC:(k + 1) * MOE_SC, :])
        z = part if z is None else z + part
    o_ref[0] = x_ref[0] + gate_ref[0] * _rms(z, gpost_ref[...])


def _combine(y, dest, x, g_post, gate_f):
    bsz, s, d = x.shape
    ntile, rows, _ = y.shape
    tile = dest.shape[1]
    nt = s // tile
    per_tile = lambda r, n: pl.BlockSpec((1, r, n), lambda i: (i, 0, 0))
    out = pl.pallas_call(
        _combine_kernel,
        grid=(ntile,),
        in_specs=[per_tile(rows, d), per_tile(tile, LANES), per_tile(tile, d),
                  pl.BlockSpec((1, d), lambda i: (0, 0)),
                  pl.BlockSpec((1, 1, d), lambda i: (i // nt, 0, 0))],
        out_specs=per_tile(tile, d),
        out_shape=jax.ShapeDtypeStruct((ntile, tile, d), F32),
        compiler_params=_cparams("arbitrary"),
        name="moe_combine",
    )(y, dest, x.reshape(ntile, tile, d), g_post.reshape(1, d), gate_f.reshape(bsz, 1, d))
    return out.reshape(bsz, s, d)


def _mixer(x, mod, layer, rel_table, g_pre_mix, w_in, cmp_params):
    shift_m, scale_m = mod[:, 0], mod[:, 1]
    proj = _in_proj(x, g_pre_mix, scale_m, shift_m, _arrange_w_in(w_in))
    o_sb = _sb_attention(proj)
    o_nsa = _nsa_attention(proj, _compress(proj, *cmp_params), rel_table)
    return o_sb, o_nsa


def kernel(x, c, rel_table, w_ada, b_ada, g_pre_mix, g_post_mix, g_pre_ffn, g_post_ffn, w_in, w_out, g_sb, g_nsa, cmp_pos_k, cmp_w1_k, cmp_w2_k, cmp_pos_v, cmp_w1_v, cmp_w2_v, ffn_w_gate, ffn_w_up, ffn_w_down, moe_w_router, moe_b_router, moe_w_gate, moe_w_up, moe_w_down):
    bsz, s, d = x.shape
    depth = w_in.shape[0]
    mods = _ada(c, w_ada, b_ada).reshape(depth, bsz, 6, d)
    moe_tile = min(1024, s)
    for layer in range(depth):
        mod = mods[layer]
        cmp_params = (cmp_pos_k[layer], cmp_w1_k[layer], cmp_w2_k[layer],
                      cmp_pos_v[layer], cmp_w1_v[layer], cmp_w2_v[layer])
        o_sb, o_nsa = _mixer(x, mod, layer, rel_table, g_pre_mix[layer], w_in[layer], cmp_params)
        x, h = _out_proj(o_sb, o_nsa, x, g_sb[layer], g_nsa[layer], w_out[layer], g_post_mix[layer],
                         mod[:, 2], g_pre_ffn[layer], mod[:, 4], mod[:, 3])
        i = layer // 2
        if layer % 2 == 0:
            x = _dense_ffn(h, x, ffn_w_gate[i], ffn_w_up[i], ffn_w_down[i], g_post_ffn[layer], mod[:, 5])
        else:
            xg, wrow, dest, offc, nch = _dispatch(x, g_pre_ffn[layer], mod[:, 4], mod[:, 3],
                                                  moe_w_router[i], moe_b_router[i], moe_tile)
            order = _slot_order(offc, nch, xg.shape[1] // MOE_RC)
            y = _expert_slots(xg, wrow, *order, moe_w_gate[i], moe_w_up[i], moe_w_down[i])
            x = _combine(y, dest, x, g_post_ffn[layer], mod[:, 5])
    return x
```

```python
import functools
import math

import numpy as np
import jax
import jax.numpy as jnp
from jax import lax
from jax.experimental import pallas as pl
from jax.experimental.pallas import tpu as pltpu

F32 = jnp.float32
BF16 = jnp.bfloat16
I32 = jnp.int32

LANES = 128
DH = 64
N_SB = 8
N_NSA = 8
G_NSA = 2
Z_NSA = 4
CMP_BLOCK = 32
CMP_STRIDE = 16
SLC_BLOCK = 64
N_SELECT = 16
WINDOW = 512
REL_BUCKETS = 32
REL_MAX_DIST = 128
N_EXPERTS = 8
EPS = 1e-6
FORCED = 1e4
NEG = -1e30
M_INIT = -1e29
SB_EXIT = -104.5
VMEM_LIMIT = 56 * 1024 * 1024

TQ = 256
NEAR_BACK = REL_MAX_DIST
KC_FRONT = 16
SEL_FAR_TILE = 512


def _cparams(*sem):
    return pltpu.CompilerParams(dimension_semantics=sem, vmem_limit_bytes=VMEM_LIMIT)


def _nt(a, b):
    return lax.dot_general(a, b, (((1,), (1,)), ((), ())), preferred_element_type=F32)


def _dot(a, b):
    return jnp.dot(a, b, preferred_element_type=F32)


def _split3(a):
    hi = a.astype(BF16)
    r = a - hi.astype(F32)
    mid = r.astype(BF16)
    lo = (r - mid.astype(F32)).astype(BF16)
    return hi, mid, lo


def _dot_hl(a, b):
    hi = a.astype(BF16)
    lo = (a - hi.astype(F32)).astype(BF16)
    return _dot(hi, b) + _dot(lo, b)


def _dot_f32(a, b):
    ah, am, al = _split3(a)
    bh, bm, bl = _split3(b)
    return (_dot(ah, bh) + (_dot(ah, bm) + _dot(am, bh))
            + (_dot(ah, bl) + _dot(am, bm) + _dot(al, bh)))


def _rms(x, g):
    return x * lax.rsqrt(jnp.mean(x * x, axis=-1, keepdims=True) + EPS) * g


def _ada_kernel(c_ref, w_ref, b_ref, o_ref):
    c = c_ref[...]
    ca = c * (1.0 / (1.0 + jnp.exp(-c)))
    o_ref[0] = _dot_f32(ca, w_ref[0]) + b_ref[0]


def _ada(c, w_ada, b_ada):
    depth, d, n = w_ada.shape
    bsz = c.shape[0]
    rows = 8
    tn = 1536
    cp = jnp.zeros((rows, d), F32).at[:bsz].set(c)
    out = pl.pallas_call(
        _ada_kernel,
        grid=(depth, n // tn),
        in_specs=[pl.BlockSpec((rows, d), lambda l, j: (0, 0)),
                  pl.BlockSpec((1, d, tn), lambda l, j: (l, 0, j)),
                  pl.BlockSpec((1, 1, tn), lambda l, j: (l, 0, j))],
        out_specs=pl.BlockSpec((1, rows, tn), lambda l, j: (l, 0, j)),
        out_shape=jax.ShapeDtypeStruct((depth, rows, n), F32),
        compiler_params=_cparams("arbitrary", "arbitrary"),
        name="ada_mod",
    )(cp, w_ada, b_ada.reshape(depth, 1, n))
    return out[:, :bsz]


def _in_kernel(x_ref, g_ref, sc_ref, sh_ref, w_ref, o_ref, *, cn):
    h = _rms(x_ref[0], g_ref[...]) * (1.0 + sc_ref[0]) + sh_ref[0]
    hb = h.astype(BF16)
    for j in range(w_ref.shape[1] // cn):
        o_ref[0, :, j * cn:(j + 1) * cn] = _dot(hb, w_ref[:, j * cn:(j + 1) * cn]).astype(BF16)


def _in_proj(x, g, scale, shift, w):
    bsz, s, d = x.shape
    nc = w.shape[1]
    tm = min(512, s)
    cn = nc // 3 if (nc // 3) % LANES == 0 else nc
    return pl.pallas_call(
        functools.partial(_in_kernel, cn=cn),
        grid=(bsz, s // tm),
        in_specs=[pl.BlockSpec((1, tm, d), lambda b, i: (b, i, 0)),
                  pl.BlockSpec((1, d), lambda b, i: (0, 0)),
                  pl.BlockSpec((1, 1, d), lambda b, i: (b, 0, 0)),
                  pl.BlockSpec((1, 1, d), lambda b, i: (b, 0, 0)),
                  pl.BlockSpec((d, nc), lambda b, i: (0, 0))],
        out_specs=pl.BlockSpec((1, tm, nc), lambda b, i: (b, i, 0)),
        out_shape=jax.ShapeDtypeStruct((bsz, s, nc), BF16),
        compiler_params=_cparams("arbitrary", "arbitrary"),
        name="in_proj",
    )(x, g.reshape(1, d), scale.reshape(bsz, 1, d), shift.reshape(bsz, 1, d), w)


CB_SBQ, CB_SBK, CB_SBV, CB_NSAQ = 0, 4, 8, 12
CB_KCMP, CB_VCMP = 16, 17
CB_KSLC, CB_VSLC, CB_KWIN, CB_VWIN = 18, 20, 22, 24
CB_GATE = 26
N_CB = 27


def _arrange_w_in(w_in):
    d = w_in.shape[0]
    w_sb = N_SB * DH
    off_nsa_q = 3 * w_sb
    off_kv = off_nsa_q + N_NSA * DH
    off_gate = off_kv + 3 * 2 * G_NSA * DH
    scale = DH ** -0.5

    def kv(br, kvi, g):
        lo = off_kv + ((br * 2 + kvi) * G_NSA + g) * DH
        return w_in[:, lo:lo + DH]

    cols = [w_in[:, 0:w_sb] * scale, w_in[:, w_sb:2 * w_sb], w_in[:, 2 * w_sb:3 * w_sb],
            w_in[:, off_nsa_q:off_kv] * scale,
            kv(0, 0, 0), kv(0, 0, 1), kv(0, 1, 0), kv(0, 1, 1)]
    for br in (1, 2):
        for kvi in (0, 1):
            for g in range(G_NSA):
                cols += [kv(br, kvi, g), kv(br, kvi, g)]
    n_gate = 3 * N_NSA
    cols += [w_in[:, off_gate:off_gate + n_gate], jnp.zeros((d, LANES - n_gate), w_in.dtype)]
    out = jnp.concatenate(cols, axis=1).astype(BF16)
    assert out.shape[1] == N_CB * LANES
    return out


def _sb_kernel(q_ref, k_ref, v_ref, o_ref, *, t):
    qi = pl.program_id(2)
    q = q_ref[0]
    lane = lax.broadcasted_iota(I32, (t, LANES), 1)
    row = lax.broadcasted_iota(I32, (t, t), 0)
    col = lax.broadcasted_iota(I32, (t, t), 1)
    upper = jnp.where(row > col, 1.0, 0.0).astype(BF16)
    upper2 = jnp.concatenate([upper, upper], axis=0)
    ones2 = jnp.ones((2 * t, LANES), BF16)
    causal = col < row
    rep = t // LANES

    def tile(qh, k, v, carry, acc, diag):
        z = _nt(qh, k)
        lk = -(jnp.maximum(z, 0.0) + jnp.log(1.0 + jnp.exp(-jnp.abs(z))))
        if diag:
            lk = jnp.where(causal, lk, 0.0)
        hi = lk.astype(BF16)
        lo = (lk - hi.astype(F32)).astype(BF16)
        hl = jnp.concatenate([hi, lo], axis=1)
        later = _dot(hl, upper2)
        tot = _dot(hl, ones2)
        cb = carry if rep == 1 else jnp.concatenate([carry] * rep, axis=1)
        w = jnp.exp(z + lk + later + cb)
        if diag:
            w = jnp.where(causal, w, 0.0)
        return carry + tot, acc + _dot(w.astype(BF16), v)

    n_head = 2 * (q.shape[1] // LANES)
    zq = jnp.zeros((t, LANES), BF16)
    qhs = []
    for c in range(q.shape[1] // LANES):
        qc = q[:, c * LANES:(c + 1) * LANES]
        qhs += [jnp.where(lane < DH, qc, zq), jnp.where(lane < DH, zq, qc)]
    zero = jnp.zeros((t, LANES), F32)

    def step(kt, st, diag):
        k = k_ref[0, pl.ds(kt * t, t), :]
        v = v_ref[0, pl.ds(kt * t, t), :]
        out, mx = [], None
        for h in range(n_head):
            c = (h // 2) * LANES
            carry, acc = tile(qhs[h], k[:, c:c + LANES], v[:, c:c + LANES], st[2 * h], st[2 * h + 1], diag)
            out += [carry, acc]
            mx = jnp.max(carry) if mx is None else jnp.maximum(mx, jnp.max(carry))
        return mx, tuple(out)

    mx, st = step(qi, (zero,) * (2 * n_head), True)

    def cond(s):
        return jnp.logical_and(s[0] >= 0, s[1] > SB_EXIT)

    def body(s):
        mx, st = step(s[0], s[2], False)
        return s[0] - 1, mx, st

    _, _, st = lax.while_loop(cond, body, (qi - 1, mx, st))
    o_ref[0] = jnp.concatenate([jnp.where(lane < DH, st[4 * c + 1], st[4 * c + 3])
                                for c in range(n_head // 2)], axis=1)


def _sb_attention(proj, t=256, pairs=4):
    bsz, s, _ = proj.shape
    t = min(t, s)
    npair = N_SB // 2
    w = pairs * LANES
    return pl.pallas_call(
        functools.partial(_sb_kernel, t=t),
        grid=(bsz, npair // pairs, s // t),
        in_specs=[pl.BlockSpec((1, t, w), lambda b, j, i: (b, i, CB_SBQ // pairs + j)),
                  pl.BlockSpec((1, s, w), lambda b, j, i: (b, 0, CB_SBK // pairs + j)),
                  pl.BlockSpec((1, s, w), lambda b, j, i: (b, 0, CB_SBV // pairs + j))],
        out_specs=pl.BlockSpec((1, t, w), lambda b, j, i: (b, i, j)),
        out_shape=jax.ShapeDtypeStruct((bsz, s, N_SB * DH), F32),
        compiler_params=_cparams("arbitrary", "arbitrary", "arbitrary"),
        name="sb_attn",
    )(proj, proj, proj)


def _gelu_tanh(x):
    return 0.5 * x * (1.0 + jnp.tanh(math.sqrt(2.0 / math.pi) * (x + 0.044715 * (x * x * x))))


def _cmp_kernel(xa_ref, xb_ref, pos_ref, w1_ref, w2_ref, o_ref):
    half = w1_ref.shape[1] // 2
    w1a = w1_ref[0, :half, :]
    w1b = w1_ref[0, half:, :]
    pos = pos_ref[0]
    bias = _dot(pos[:, :half], w1a) + _dot(pos[:, half:], w1b)
    hid = _dot(xa_ref[0, 0], w1a) + _dot(xb_ref[0, 0], w1b) + bias[0:1, :]
    o_ref[0, 0] = _dot(_gelu_tanh(hid).astype(BF16), w2_ref[0])


def _compress(proj, pos_k, w1_k, w2_k, pos_v, w1_v, w2_v):
    bsz, s, _ = proj.shape
    n16 = s // CMP_STRIDE
    raw = proj[:, :, CB_KCMP * LANES:(CB_VCMP + 1) * LANES]
    x16 = raw.reshape(bsz, s, 4, DH).transpose(0, 2, 1, 3).reshape(bsz, 4, n16, CMP_STRIDE * DH)
    x16b = jnp.concatenate([x16[:, :, 1:], jnp.zeros_like(x16[:, :, :1])], axis=2)
    hidden = w1_k.shape[1]
    w1 = jnp.stack([w1_k, w1_v]).astype(BF16)
    w2 = jnp.stack([jnp.concatenate([w2_k, w2_k], 1), jnp.concatenate([w2_v, w2_v], 1)]).astype(BF16)
    pos = jnp.stack([pos_k.reshape(1, -1), pos_v.reshape(1, -1)])
    pos = jnp.concatenate([pos, jnp.zeros((2, 7, pos.shape[2]), pos.dtype)], axis=1).astype(BF16)
    k16 = CMP_STRIDE * DH
    return pl.pallas_call(
        _cmp_kernel,
        grid=(bsz, 4),
        in_specs=[pl.BlockSpec((1, 1, n16, k16), lambda b, j: (b, j, 0, 0)),
                  pl.BlockSpec((1, 1, n16, k16), lambda b, j: (b, j, 0, 0)),
                  pl.BlockSpec((1, 8, 2 * k16), lambda b, j: (j // 2, 0, 0)),
                  pl.BlockSpec((1, 2 * k16, hidden), lambda b, j: (j // 2, 0, 0)),
                  pl.BlockSpec((1, hidden, LANES), lambda b, j: (j // 2, 0, 0))],
        out_specs=pl.BlockSpec((1, 1, n16, LANES), lambda b, j: (b, j, 0, 0)),
        out_shape=jax.ShapeDtypeStruct((bsz, 4, n16, LANES), F32),
        compiler_params=_cparams("arbitrary", "arbitrary"),
        name="nsa_compress",
    )(x16, x16b, pos, w1, w2)


def _bucket_table():
    n = np.arange(REL_MAX_DIST + 1)
    exact = REL_BUCKETS // 2
    val = (np.log(np.maximum(n, 1).astype(np.float32) / np.float32(exact)).astype(np.float32)
           / np.float32(math.log(REL_MAX_DIST / exact)) * np.float32(REL_BUCKETS - exact))
    large = np.minimum(exact + val.astype(np.int32), REL_BUCKETS - 1)
    return np.where(n < exact, n, large).astype(np.int32)


def _bias_tiles(rel_table):
    bucket = _bucket_table()
    far_row = rel_table[int(bucket[REL_MAX_DIST])].astype(F32)
    r = np.arange(TQ)[:, None]

    def tile(dist, valid, base):
        onehot = jax.nn.one_hot(bucket[np.clip(dist, 0, REL_MAX_DIST)], REL_BUCKETS, dtype=F32)
        t = jnp.einsum('rwb,bh->hrw', onehot, rel_table.astype(F32),
                       precision=lax.Precision.HIGHEST)
        t = jnp.where(valid[None], t - base, NEG)
        return t.reshape(G_NSA, Z_NSA * TQ, dist.shape[1])

    w = np.arange(LANES)[None, :]
    d_c = r - CMP_STRIDE * (w - KC_FRONT) - (CMP_BLOCK - 1)
    w2 = np.arange(NEAR_BACK + TQ)[None, :]
    d_s = r + NEAR_BACK - w2
    w5 = np.arange(WINDOW + TQ)[None, :]
    d_w = r + WINDOW - w5
    far = jnp.broadcast_to(far_row[:, None, None], (N_NSA, TQ, LANES))
    far = far.reshape(G_NSA, Z_NSA * TQ, LANES)
    return (tile(d_c, d_c >= 0, far_row[:, None, None]), tile(d_s, d_s >= 0, 0.0),
            tile(d_w, (d_w >= 0) & (d_w < WINDOW), 0.0), far)


def _gate_expand():
    e = np.zeros((G_NSA, LANES, 3 * Z_NSA * DH), np.float32)
    for g in range(G_NSA):
        for z in range(Z_NSA):
            for br in range(3):
                e[g, (g * Z_NSA + z) * 3 + br, br * Z_NSA * DH + z * DH: br * Z_NSA * DH + (z + 1) * DH] = 1.0
    return e


def _overlap_padded(n16, kcp, n_slc):
    i = np.arange(kcp)[:, None] - KC_FRONT
    j = np.arange(LANES)[None, :]
    n_cmp = n16 - 1
    ok = (i >= 0) & (i < n_cmp) & (j < n_slc)
    ov = (i * CMP_STRIDE < j * SLC_BLOCK + SLC_BLOCK) & (i * CMP_STRIDE + CMP_BLOCK > j * SLC_BLOCK)
    return (ok & ov).astype(np.float32)


def _block_onehot(s):
    return (np.arange(s)[:, None] // SLC_BLOCK == np.arange(LANES)[None, :]).astype(np.float32)


def _nsa_kernel(q_ref, kcb_ref, vcb_ref, kcf_ref, vcf_ref, ks_ref, vs_ref, kw_ref, vw_ref, et_ref,
                gate_ref, ovb_ref, ovf_ref, bcn_ref, bsn_ref, bw_ref, cf_ref, eg_ref, o_ref,
                sb0_ref, sb1_ref, *, n_slc, n_round):
    blk = pl.program_id(2)
    c0 = blk * TQ
    r4 = Z_NSA * TQ
    q = q_ref[0]
    lane = lax.broadcasted_iota(I32, (TQ, LANES), 1)
    lo_half = lane < DH
    zq = jnp.zeros((TQ, LANES), BF16)
    qa, qb = q[:, :LANES], q[:, LANES:]
    qs = jnp.concatenate([jnp.where(lo_half, qa, zq), jnp.where(lo_half, zq, qa),
                          jnp.where(lo_half, qb, zq), jnp.where(lo_half, zq, qb)], axis=0)
    ones_k = jnp.ones((SEL_FAR_TILE, LANES), BF16)
    ones_w = jnp.ones((WINDOW + TQ, LANES), BF16)

    kcp = kcb_ref.shape[2]
    prow = lax.broadcasted_iota(I32, (1, kcp), 1)
    near0 = pl.multiple_of(blk * (TQ // CMP_STRIDE), 8)
    far_row = jnp.where((prow >= KC_FRONT) & (prow < near0), 0.0, NEG)
    lane1 = lax.broadcasted_iota(I32, (1, LANES), 1)
    near_row = jnp.where(lane1 + near0 >= KC_FRONT, 0.0, NEG)
    kcn = kcf_ref[0, 0, pl.ds(near0, LANES), :].astype(BF16)
    vcn = vcf_ref[0, 0, pl.ds(near0, LANES), :].astype(BF16)
    s_far = _nt(qs, kcb_ref[0, 0]) + far_row
    s_near = _nt(qs, kcn) + bcn_ref[0] + near_row
    m_c = jnp.maximum(jnp.max(s_far, axis=1, keepdims=True), jnp.max(s_near, axis=1, keepdims=True))
    m_c = jnp.maximum(m_c, M_INIT)
    p_far = jnp.exp(s_far - m_c)
    p_near = jnp.exp(s_near - m_c)
    den = jnp.sum(p_far, axis=1, keepdims=True) + jnp.sum(p_near, axis=1, keepdims=True)
    inv_c = 1.0 / jnp.maximum(den, 1e-30)
    o_c = (_dot(p_far.astype(BF16), vcb_ref[0, 0]) + _dot(p_near.astype(BF16), vcn)) * inv_c

    pn_far = p_far * inv_c
    pn_near = p_near * inv_c
    pz_far = pn_far[0:TQ] + pn_far[TQ:2 * TQ] + pn_far[2 * TQ:3 * TQ] + pn_far[3 * TQ:]
    pz_near = pn_near[0:TQ] + pn_near[TQ:2 * TQ] + pn_near[2 * TQ:3 * TQ] + pn_near[3 * TQ:]
    ovn = ovf_ref[pl.ds(near0, LANES), :].astype(BF16)
    imp = _dot_hl(pz_far, ovb_ref[...]) + _dot_hl(pz_near, ovn)
    tpos = c0 + lax.broadcasted_iota(I32, (TQ, LANES), 0)
    cur = lax.shift_right_logical(tpos, 6)
    valid = (lane * SLC_BLOCK <= tpos) & (lane < n_slc)
    forced = valid & ((lane == 0) | (lane == cur) | (lane == cur - 1))
    score = jnp.where(valid & jnp.logical_not(forced), imp, -1.0)
    score = jnp.where(lane < n_slc, score, -2.0)

    rest, taken, thr = score, jnp.zeros((TQ, 1), F32), jnp.zeros((TQ, 1), F32)
    for _ in range(n_round):
        m = jnp.max(rest, axis=1, keepdims=True)
        eq = rest == m
        thr = jnp.where(taken < n_round, m, thr)
        taken = taken + jnp.sum(jnp.where(eq, 1.0, 0.0), axis=1, keepdims=True)
        rest = jnp.where(eq, -jnp.inf, rest)
    above = score > thr
    at_thr = score == thr
    need = n_round - jnp.sum(jnp.where(above, 1.0, 0.0), axis=1, keepdims=True)
    brow = lax.broadcasted_iota(I32, (LANES, LANES), 0)
    bcol = lax.broadcasted_iota(I32, (LANES, LANES), 1)
    before = _dot(jnp.where(at_thr, 1.0, 0.0).astype(BF16),
                  jnp.where(brow < bcol, 1.0, 0.0).astype(BF16))
    sel = ((above | (at_thr & (before < need))) & valid) | forced

    kws, vws = [], []
    for w in range((WINDOW + TQ) // LANES):
        st = pl.multiple_of(jnp.maximum(c0 - WINDOW + w * LANES, 0), LANES)
        kws.append(kw_ref[0, pl.ds(st, LANES), :])
        vws.append(vw_ref[0, pl.ds(st, LANES), :])
    wl = lax.broadcasted_iota(I32, (1, WINDOW + TQ), 1)
    pos_row = jnp.where(c0 - WINDOW + wl >= 0, 0.0, NEG)
    s = _nt(qs, jnp.concatenate(kws, axis=0)) + bw_ref[0] + pos_row
    p = jnp.exp(s - jnp.max(s, axis=1, keepdims=True)).astype(BF16)
    r_w = _dot(p, jnp.concatenate([jnp.concatenate(vws, axis=0), ones_w], axis=1))
    o_w = r_w[:, :LANES] / r_w[:, LANES:]
    gl = _dot(gate_ref[0], eg_ref[0])
    sig = 1.0 / (1.0 + jnp.exp(-gl))
    wide = Z_NSA * DH

    sel_all = jnp.where(sel, 0.0, NEG).astype(BF16)
    far_blocks = blk * (TQ // SLC_BLOCK) - NEAR_BACK // SLC_BLOCK
    sel_far = jnp.where(sel & (lane < far_blocks), 0.0, NEG).astype(BF16)
    q_far = jnp.concatenate([qs, jnp.concatenate([sel_far] * Z_NSA, axis=0)], axis=1)
    q_near = jnp.concatenate([qs, jnp.concatenate([sel_all] * Z_NSA, axis=0)], axis=1)
    tk = SEL_FAR_TILE
    last_tile = ks_ref.shape[1] // tk - 1

    def qk(kt):
        k0 = pl.multiple_of(kt * tk, tk)
        return _nt(q_far, jnp.concatenate([ks_ref[0, pl.ds(k0, tk), :], et_ref[pl.ds(k0, tk), :]], axis=1))

    def fold(s, v, st):
        m, acc = st
        mn = jnp.maximum(m, jnp.max(s, axis=1, keepdims=True))
        p = jnp.exp(s - mn).astype(BF16)
        return mn, jnp.exp(m - mn) * acc + _dot(p, v)

    def v_tile(kt):
        k0 = pl.multiple_of(kt * tk, tk)
        return jnp.concatenate([vs_ref[0, pl.ds(k0, tk), :], ones_k], axis=1)

    n_pair = lax.shift_right_logical(jnp.maximum(c0 - NEAR_BACK, 0) + (2 * tk - 1), int(math.log2(2 * tk)))
    sb0_ref[...] = qk(0)

    def far_step(i, st):
        sb1_ref[...] = qk(2 * i + 1)
        st = fold(sb0_ref[...], v_tile(2 * i), st)
        sb0_ref[...] = qk(jnp.minimum(2 * i + 2, last_tile))
        return fold(sb1_ref[...], v_tile(2 * i + 1), st)

    def far_step2(i, st):
        return far_step(2 * i + 1, far_step(2 * i, st))

    m0 = jnp.full((r4, 1), M_INIT, F32)
    n_quad = lax.shift_right_logical(n_pair, 1)
    st = lax.fori_loop(0, n_quad, far_step2, (m0, jnp.zeros((r4, 2 * LANES), F32)))
    m_s, acc_s = lax.fori_loop(2 * n_quad, n_pair, far_step, st)
    m_s = m_s + cf_ref[0][:, 0:1]
    n_near = NEAR_BACK + TQ
    p0 = pl.multiple_of(jnp.maximum(c0 - NEAR_BACK, 0), NEAR_BACK)
    d0 = pl.multiple_of(c0, TQ)
    kn = jnp.concatenate([ks_ref[0, pl.ds(p0, NEAR_BACK), :], ks_ref[0, pl.ds(d0, TQ), :]], axis=0)
    vn = jnp.concatenate([vs_ref[0, pl.ds(p0, NEAR_BACK), :], vs_ref[0, pl.ds(d0, TQ), :]], axis=0)
    nrow = lax.broadcasted_iota(I32, (n_near, LANES), 0)
    nlane = lax.broadcasted_iota(I32, (n_near, LANES), 1)
    e_n = jnp.where(lax.shift_right_logical(nrow, 6) + far_blocks == nlane, 1.0, 0.0).astype(BF16)
    lane2 = lax.broadcasted_iota(I32, (1, n_near), 1)
    prev_row = jnp.where((lane2 < NEAR_BACK) & (blk == 0), NEG, 0.0)
    s = _nt(q_near, jnp.concatenate([kn, e_n], axis=1)) + bsn_ref[0] + prev_row
    _, acc_s = fold(s, jnp.concatenate([vn, ones_k[:n_near]], axis=1), (m_s, acc_s))
    o_s = acc_s[:, :LANES] / acc_s[:, LANES:]

    def heads(o):
        return jnp.concatenate([jnp.where(lo_half, o[0:TQ], o[TQ:2 * TQ]),
                                jnp.where(lo_half, o[2 * TQ:3 * TQ], o[3 * TQ:])], axis=1)

    o_ref[0] = (sig[:, 0:wide] * heads(o_c) + sig[:, wide:2 * wide] * heads(o_s)
                + sig[:, 2 * wide:] * heads(o_w))


def _nsa_attention(proj, cmp_out, rel_table):
    bsz, s, _ = proj.shape
    assert s % (2 * SEL_FAR_TILE) == 0 and s >= WINDOW + TQ
    n16 = s // CMP_STRIDE
    n_slc = s // SLC_BLOCK
    assert N_SELECT <= n_slc <= LANES
    nq = s // TQ
    kcp = -(-(max(n16 + KC_FRONT, (TQ // CMP_STRIDE) * (nq - 1) + LANES)) // LANES) * LANES
    pad = ((0, 0), (0, 0), (KC_FRONT, kcp - n16 - KC_FRONT), (0, 0))
    cf = jnp.pad(cmp_out, pad)
    cb = cf.astype(BF16)
    ov = _overlap_padded(n16, kcp, n_slc)
    bcn, bsn, bw, far = _bias_tiles(rel_table)
    eg = jnp.asarray(_gate_expand(), BF16)
    r4 = Z_NSA * TQ
    kv_spec = lambda cb0: pl.BlockSpec((1, s, LANES), lambda b, g, i, cb0=cb0: (b, 0, cb0 + g))
    cmp_spec = lambda j0: pl.BlockSpec((1, 1, kcp, LANES), lambda b, g, i, j0=j0: (b, j0 + g, 0, 0))
    tile_spec = lambda w: pl.BlockSpec((1, r4, w), lambda b, g, i: (g, 0, 0))
    const2 = lambda n: pl.BlockSpec((n, LANES), lambda b, g, i: (0, 0))
    return pl.pallas_call(
        functools.partial(_nsa_kernel, n_slc=n_slc, n_round=N_SELECT - 3),
        grid=(bsz, G_NSA, nq),
        in_specs=[pl.BlockSpec((1, TQ, 2 * LANES), lambda b, g, i: (b, i, CB_NSAQ // 2 + g)),
                  cmp_spec(0), cmp_spec(2), cmp_spec(0), cmp_spec(2),
                  kv_spec(CB_KSLC), kv_spec(CB_VSLC), kv_spec(CB_KWIN), kv_spec(CB_VWIN),
                  const2(s),
                  pl.BlockSpec((1, TQ, LANES), lambda b, g, i: (b, i, CB_GATE)),
                  const2(kcp), const2(kcp),
                  tile_spec(LANES), tile_spec(NEAR_BACK + TQ), tile_spec(WINDOW + TQ), tile_spec(LANES),
                  pl.BlockSpec((1, LANES, 3 * Z_NSA * DH), lambda b, g, i: (g, 0, 0))],
        out_specs=pl.BlockSpec((1, TQ, Z_NSA * DH), lambda b, g, i: (b, i, g)),
        out_shape=jax.ShapeDtypeStruct((bsz, s, N_NSA * DH), F32),
        scratch_shapes=[pltpu.VMEM((r4, SEL_FAR_TILE), F32), pltpu.VMEM((r4, SEL_FAR_TILE), F32)],
        compiler_params=_cparams("arbitrary", "arbitrary", "arbitrary"),
        name="nsa_attn",
    )(proj, cb, cb, cf, cf, proj, proj, proj, proj, jnp.asarray(_block_onehot(s), BF16), proj,
      jnp.asarray(ov, BF16), jnp.asarray(ov, F32), bcn, bsn, bw, far, eg)


def _out_kernel(osb_ref, onsa_ref, x_ref, gsb_ref, gnsa_ref, w_ref, gpost_ref, gate_ref,
                gpre_ref, sc_ref, sh_ref, xo_ref, h_ref):
    half = osb_ref.shape[2]
    a = _rms(osb_ref[0], gsb_ref[...]).astype(BF16)
    b = _rms(onsa_ref[0], gnsa_ref[...]).astype(BF16)
    m = _dot(a, w_ref[:half, :]) + _dot(b, w_ref[half:, :])
    x = x_ref[0] + gate_ref[0] * _rms(m, gpost_ref[...])
    xo_ref[0] = x
    h_ref[0] = (_rms(x, gpre_ref[...]) * (1.0 + sc_ref[0]) + sh_ref[0]).astype(BF16)


def _out_proj(o_sb, o_nsa, x, g_sb, g_nsa, w_out, g_post, gate_m, g_pre_ffn, scale_f, shift_f):
    bsz, s, d = x.shape
    half = o_sb.shape[2]
    tm = min(512, s)
    row = lambda n: pl.BlockSpec((1, n), lambda b, i: (0, 0))
    mod = pl.BlockSpec((1, 1, d), lambda b, i: (b, 0, 0))
    act = lambda n: pl.BlockSpec((1, tm, n), lambda b, i: (b, i, 0))
    return pl.pallas_call(
        _out_kernel,
        grid=(bsz, s // tm),
        in_specs=[act(half), act(half), act(d), row(half), row(half),
                  pl.BlockSpec((2 * half, d), lambda b, i: (0, 0)), row(d), mod, row(d), mod, mod],
        out_specs=[act(d), act(d)],
        out_shape=[jax.ShapeDtypeStruct((bsz, s, d), F32), jax.ShapeDtypeStruct((bsz, s, d), BF16)],
        compiler_params=_cparams("arbitrary", "arbitrary"),
        name="out_proj",
    )(o_sb, o_nsa, x, g_sb.reshape(1, half), g_nsa.reshape(1, half), w_out.astype(BF16),
      g_post.reshape(1, d), gate_m.reshape(bsz, 1, d), g_pre_ffn.reshape(1, d),
      scale_f.reshape(bsz, 1, d), shift_f.reshape(bsz, 1, d))


def _silu(x):
    return x * (1.0 / (1.0 + jnp.exp(-x)))


def _ffn_kernel(h_ref, x_ref, wg_ref, wu_ref, wd_ref, gpost_ref, gate_ref, o_ref):
    h = h_ref[0]
    a = (_silu(_dot(h, wg_ref[...])) * _dot(h, wu_ref[...])).astype(BF16)
    o_ref[0] = x_ref[0] + gate_ref[0] * _rms(_dot(a, wd_ref[...]), gpost_ref[...])


def _dense_ffn(h, x, w_gate, w_up, w_down, g_post, gate_f):
    bsz, s, d = x.shape
    ff = w_gate.shape[1]
    tm = min(512, s)
    act = pl.BlockSpec((1, tm, d), lambda b, i: (b, i, 0))
    once = pl.Buffered(1)
    return pl.pallas_call(
        _ffn_kernel,
        grid=(bsz, s // tm),
        in_specs=[act, act,
                  pl.BlockSpec((d, ff), lambda b, i: (0, 0), pipeline_mode=once),
                  pl.BlockSpec((d, ff), lambda b, i: (0, 0), pipeline_mode=once),
                  pl.BlockSpec((ff, d), lambda b, i: (0, 0), pipeline_mode=once),
                  pl.BlockSpec((1, d), lambda b, i: (0, 0)),
                  pl.BlockSpec((1, 1, d), lambda b, i: (b, 0, 0))],
        out_specs=act,
        out_shape=jax.ShapeDtypeStruct((bsz, s, d), F32),
        compiler_params=_cparams("arbitrary", "arbitrary"),
        name="dense_ffn",
    )(h, x, w_gate.astype(BF16), w_up.astype(BF16), w_down.astype(BF16),
      g_post.reshape(1, d), gate_f.reshape(bsz, 1, d))


MOE_RC = 128
MOE_GC = 256
MOE_SC = 1024


def _moe_rows(tile):
    return -(-(2 * tile + N_EXPERTS * MOE_RC) // MOE_SC) * MOE_SC


def _dispatch_kernel(x_ref, gpre_ref, sc_ref, sh_ref, wr_ref, br_ref,
                     xg_ref, wrow_ref, d_ref, off_ref, nch_ref):
    tile = x_ref.shape[1]
    rows = xg_ref.shape[1]
    h = _rms(x_ref[0], gpre_ref[...]) * (1.0 + sc_ref[0]) + sh_ref[0]
    logits = _dot_f32(h, wr_ref[...]) + br_ref[...]
    lane = lax.broadcasted_iota(I32, (tile, LANES), 1)
    lanef = lane.astype(F32)
    e = jnp.exp(logits - jnp.max(logits, axis=1, keepdims=True))
    probs = e / jnp.sum(e, axis=1, keepdims=True)
    probs = jnp.where(lane < N_EXPERTS, probs, -1.0)
    m1 = jnp.max(probs, axis=1, keepdims=True)
    i1 = jnp.min(jnp.where(probs == m1, lanef, 1e9), axis=1, keepdims=True)
    rest = jnp.where(lanef == i1, -1.0, probs)
    m2 = jnp.max(rest, axis=1, keepdims=True)
    i2 = jnp.min(jnp.where(rest == m2, lanef, 1e9), axis=1, keepdims=True)
    tot = m1 + m2
    hit1 = lanef == i1
    hit2 = lanef == i2
    mask = hit1 | hit2
    ch = 256
    r = lax.broadcasted_iota(I32, (ch, ch), 0)
    c = lax.broadcasted_iota(I32, (ch, ch), 1)
    lower = jnp.where(c < r, 1.0, 0.0).astype(BF16)
    ones = jnp.ones((8, ch), BF16)
    count = jnp.zeros((1, LANES), F32)
    ranks = []
    for k in range(tile // ch):
        mk = jnp.where(mask[k * ch:(k + 1) * ch], 1.0, 0.0).astype(BF16)
        ranks.append(_dot(lower, mk) + count)
        count = count + _dot(ones, mk)[0:1]
    rank = jnp.concatenate(ranks, axis=0)
    shift = int(math.log2(MOE_RC))
    nch = lax.shift_right_logical(count.astype(I32) + (MOE_RC - 1), shift)
    nch8 = jnp.broadcast_to(nch.astype(F32), (8, LANES))
    lr = lax.broadcasted_iota(I32, (LANES, LANES), 0)
    lc = lax.broadcasted_iota(I32, (LANES, LANES), 1)
    offc = _dot(nch8.astype(BF16), jnp.where(lr < lc, 1.0, 0.0).astype(BF16))
    row = offc[0:1] * float(MOE_RC) + rank
    d1 = jnp.sum(jnp.where(hit1, row, 0.0), axis=1, keepdims=True)
    d2 = jnp.sum(jnp.where(hit2, row, 0.0), axis=1, keepdims=True)
    dd = jnp.where(lane == 0, d1, jnp.where(lane == 1, d2, -1.0))
    d_ref[0] = dd.astype(I32)
    off_ref[0] = offc.astype(I32)
    nch_ref[0] = jnp.broadcast_to(nch, (8, LANES))

    ddt = jnp.transpose(dd)
    d1t = ddt[0:1].astype(I32)
    d2t = ddt[1:2].astype(I32)

    wt = jnp.transpose(jnp.where(lane == 0, m1 / tot, jnp.where(lane == 1, m2 / tot, 0.0)))
    w1t = wt[0:1]
    w2t = wt[1:2]
    hb = h.astype(BF16)
    rid = lax.broadcasted_iota(I32, (MOE_GC, tile), 0)
    for k in range(rows // MOE_GC):
        hit1 = rid + k * MOE_GC == d1t
        hit2 = rid + k * MOE_GC == d2t
        onehot = jnp.where(hit1 | hit2, 1.0, 0.0).astype(BF16)
        xg_ref[0, k * MOE_GC:(k + 1) * MOE_GC, :] = _dot(onehot, hb).astype(BF16)
        wrow = jnp.sum(jnp.where(hit1, w1t, jnp.where(hit2, w2t, 0.0)), axis=1, keepdims=True)
        wrow_ref[0, k * MOE_GC:(k + 1) * MOE_GC, :] = jnp.broadcast_to(wrow, (MOE_GC, LANES))


def _dispatch(x, g_pre, scale_f, shift_f, w_router, b_router, tile):
    bsz, s, d = x.shape
    nt = s // tile
    rows = _moe_rows(tile)
    wr = jnp.zeros((d, LANES), F32).at[:, :N_EXPERTS].set(w_router.astype(F32))
    br = jnp.full((1, LANES), NEG, F32).at[0, :N_EXPERTS].set(b_router.astype(F32))
    mod = pl.BlockSpec((1, 1, d), lambda b, i: (b, 0, 0))
    per_tile = lambda r, n: pl.BlockSpec((1, r, n), lambda b, i: (b * nt + i, 0, 0))
    xg, wrow, dest, offc, nch = pl.pallas_call(
        _dispatch_kernel,
        grid=(bsz, nt),
        in_specs=[pl.BlockSpec((1, tile, d), lambda b, i: (b, i, 0)),
                  pl.BlockSpec((1, d), lambda b, i: (0, 0)), mod, mod,
                  pl.BlockSpec((d, LANES), lambda b, i: (0, 0)),
                  pl.BlockSpec((1, LANES), lambda b, i: (0, 0))],
        out_specs=[per_tile(rows, d), per_tile(rows, LANES), per_tile(tile, LANES),
                   per_tile(8, LANES), per_tile(8, LANES)],
        out_shape=[jax.ShapeDtypeStruct((bsz * nt, rows, d), BF16),
                   jax.ShapeDtypeStruct((bsz * nt, rows, LANES), F32),
                   jax.ShapeDtypeStruct((bsz * nt, tile, LANES), I32),
                   jax.ShapeDtypeStruct((bsz * nt, 8, LANES), I32),
                   jax.ShapeDtypeStruct((bsz * nt, 8, LANES), I32)],
        compiler_params=_cparams("arbitrary", "arbitrary"),
        name="moe_dispatch",
    )(x, g_pre.reshape(1, d), scale_f.reshape(bsz, 1, d), shift_f.reshape(bsz, 1, d), wr, br)
    return xg, wrow, dest, offc[:, 0, :N_EXPERTS], nch[:, 0, :N_EXPERTS]


def _slot_order(offc, nch, slots_per_tile):
    ends = offc + nch
    c = jnp.arange(slots_per_tile, dtype=I32)[None, :, None]
    expert = jnp.sum((c >= ends[:, None, :]).astype(I32), axis=-1).reshape(-1)
    n = expert.shape[0]
    order = jnp.argsort(expert * n + jnp.arange(n, dtype=I32)).astype(I32)
    exp_sorted = expert[order]
    used = (exp_sorted < N_EXPERTS).astype(I32)
    return order, jnp.minimum(exp_sorted, N_EXPERTS - 1).astype(I32), used


def _slots_kernel(slot_ref, exp_ref, used_ref, x_ref, wrow_ref, wg_ref, wu_ref, wd_ref, y_ref):
    p = pl.program_id(0)

    @pl.when(used_ref[p] == 1)
    def _():
        x = x_ref[0]
        a = (_silu(_dot(x, wg_ref[0])) * _dot(x, wu_ref[0])).astype(BF16)
        y_ref[0] = (_dot(a, wd_ref[0]) * wrow_ref[0][:, 0:1]).astype(BF16)

    @pl.when(used_ref[p] == 0)
    def _():
        y_ref[0] = jnp.zeros(y_ref.shape[1:], BF16)


def _expert_slots(xg, wrow, order, exp_sorted, used, w_gate, w_up, w_down):
    ntile, rows, d = xg.shape
    nslot = ntile * rows // MOE_RC
    ff = w_gate.shape[2]
    once = pl.Buffered(1)
    slot = lambda n: pl.BlockSpec((1, MOE_RC, n), lambda p, s, e, u: (s[p], 0, 0))
    grid_spec = pltpu.PrefetchScalarGridSpec(
        num_scalar_prefetch=3,
        grid=(nslot,),
        in_specs=[slot(d), slot(LANES),
                  pl.BlockSpec((1, d, ff), lambda p, s, e, u: (e[p], 0, 0), pipeline_mode=once),
                  pl.BlockSpec((1, d, ff), lambda p, s, e, u: (e[p], 0, 0), pipeline_mode=once),
                  pl.BlockSpec((1, ff, d), lambda p, s, e, u: (e[p], 0, 0), pipeline_mode=once)],
        out_specs=slot(d))
    y = pl.pallas_call(
        _slots_kernel,
        grid_spec=grid_spec,
        out_shape=jax.ShapeDtypeStruct((nslot, MOE_RC, d), BF16),
        input_output_aliases={3: 0},
        compiler_params=_cparams("arbitrary"),
        name="moe_slots",
    )(order, exp_sorted, used, xg.reshape(nslot, MOE_RC, d), wrow.reshape(nslot, MOE_RC, LANES),
      w_gate.astype(BF16), w_up.astype(BF16), w_down.astype(BF16))
    return y.reshape(ntile, rows, d)


def _combine_kernel(y_ref, d_ref, x_ref, gpost_ref, gate_ref, o_ref):
    tile = x_ref.shape[1]
    rows = y_ref.shape[1]
    dd = d_ref[0]
    d1c = dd[:, 0:1]
    d2c = dd[:, 1:2]
    cid = lax.broadcasted_iota(I32, (tile, MOE_SC), 1)
    z = None
    for k in range(rows // MOE_SC):
        pt = jnp.where((cid + k * MOE_SC == d1c) | (cid + k * MOE_SC == d2c), 1.0, 0.0).astype(BF16)
        part = _dot(pt, y_ref[0, k * MOE_SC:(k + 1) * MOE_SC, :])
        z = part if z is None else z + part
    o_ref[0] = x_ref[0] + gate_ref[0] * _rms(z, gpost_ref[...])


def _combine(y, dest, x, g_post, gate_f):
    bsz, s, d = x.shape
    ntile, rows, _ = y.shape
    tile = dest.shape[1]
    nt = s // tile
    per_tile = lambda r, n: pl.BlockSpec((1, r, n), lambda i: (i, 0, 0))
    out = pl.pallas_call(
        _combine_kernel,
        grid=(ntile,),
        in_specs=[per_tile(rows, d), per_tile(tile, LANES), per_tile(tile, d),
                  pl.BlockSpec((1, d), lambda i: (0, 0)),
                  pl.BlockSpec((1, 1, d), lambda i: (i // nt, 0, 0))],
        out_specs=per_tile(tile, d),
        out_shape=jax.ShapeDtypeStruct((ntile, tile, d), F32),
        compiler_params=_cparams("arbitrary"),
        name="moe_combine",
    )(y, dest, x.reshape(ntile, tile, d), g_post.reshape(1, d), gate_f.reshape(bsz, 1, d))
    return out.reshape(bsz, s, d)


def _mixer(x, mod, layer, rel_table, g_pre_mix, w_in, cmp_params):
    shift_m, scale_m = mod[:, 0], mod[:, 1]
    proj = _in_proj(x, g_pre_mix, scale_m, shift_m, _arrange_w_in(w_in))
    o_sb = _sb_attention(proj)
    o_nsa = _nsa_attention(proj, _compress(proj, *cmp_params), rel_table)
    return o_sb, o_nsa


def kernel(x, c, rel_table, w_ada, b_ada, g_pre_mix, g_post_mix, g_pre_ffn, g_post_ffn, w_in, w_out, g_sb, g_nsa, cmp_pos_k, cmp_w1_k, cmp_w2_k, cmp_pos_v, cmp_w1_v, cmp_w2_v, ffn_w_gate, ffn_w_up, ffn_w_down, moe_w_router, moe_b_router, moe_w_gate, moe_w_up, moe_w_down):
    bsz, s, d = x.shape
    depth = w_in.shape[0]
    mods = _ada(c, w_ada, b_ada).reshape(depth, bsz, 6, d)
    moe_tile = min(1024, s)
    for layer in range(depth):
        mod = mods[layer]
        cmp_params = (cmp_pos_k[layer], cmp_w1_k[layer], cmp_w2_k[layer],
                      cmp_pos_v[layer], cmp_w1_v[layer], cmp_w2_v[layer])
        o_sb, o_nsa = _mixer(x, mod, layer, rel_table, g_pre_mix[layer], w_in[layer], cmp_params)
        x, h = _out_proj(o_sb, o_nsa, x, g_sb[layer], g_nsa[layer], w_out[layer], g_post_mix[layer],
                         mod[:, 2], g_pre_ffn[layer], mod[:, 4], mod[:, 3])
        i = layer // 2
        if layer % 2 == 0:
            x = _dense_ffn(h, x, ffn_w_gate[i], ffn_w_up[i], ffn_w_down[i], g_post_ffn[layer], mod[:, 5])
        else:
            xg, wrow, dest, offc, nch = _dispatch(x, g_pre_ffn[layer], mod[:, 4], mod[:, 3],
                                                  moe_w_router[i], moe_b_router[i], moe_tile)
            order = _slot_order(offc, nch, xg.shape[1] // MOE_RC)
            y = _expert_slots(xg, wrow, *order, moe_w_gate[i], moe_w_up[i], moe_w_down[i])
            x = _combine(y, dest, x, g_post_ffn[layer], mod[:, 5])
    return x
```

```python
import functools
import math

import numpy as np
import jax
import jax.numpy as jnp
from jax import lax
from jax.experimental import pallas as pl
from jax.experimental.pallas import tpu as pltpu

F32 = jnp.float32
BF16 = jnp.bfloat16
I32 = jnp.int32

LANES = 128
DH = 64
N_SB = 8
N_NSA = 8
G_NSA = 2
Z_NSA = 4
CMP_BLOCK = 32
CMP_STRIDE = 16
SLC_BLOCK = 64
N_SELECT = 16
WINDOW = 512
REL_BUCKETS = 32
REL_MAX_DIST = 128
N_EXPERTS = 8
EPS = 1e-6
FORCED = 1e4
NEG = -1e30
M_INIT = -1e29
SB_EXIT = -104.5
VMEM_LIMIT = 56 * 1024 * 1024

TQ = 256
NEAR_BACK = REL_MAX_DIST
KC_FRONT = 16
SEL_FAR_TILE = 512


def _cparams(*sem):
    return pltpu.CompilerParams(dimension_semantics=sem, vmem_limit_bytes=VMEM_LIMIT)


def _nt(a, b):
    return lax.dot_general(a, b, (((1,), (1,)), ((), ())), preferred_element_type=F32)


def _dot(a, b):
    return jnp.dot(a, b, preferred_element_type=F32)


def _split3(a):
    hi = a.astype(BF16)
    r = a - hi.astype(F32)
    mid = r.astype(BF16)
    lo = (r - mid.astype(F32)).astype(BF16)
    return hi, mid, lo


def _dot_hl(a, b):
    hi = a.astype(BF16)
    lo = (a - hi.astype(F32)).astype(BF16)
    return _dot(hi, b) + _dot(lo, b)


def _dot_f32(a, b):
    ah, am, al = _split3(a)
    bh, bm, bl = _split3(b)
    return (_dot(ah, bh) + (_dot(ah, bm) + _dot(am, bh))
            + (_dot(ah, bl) + _dot(am, bm) + _dot(al, bh)))


def _rms(x, g):
    return x * lax.rsqrt(jnp.mean(x * x, axis=-1, keepdims=True) + EPS) * g


def _ada_kernel(c_ref, w_ref, b_ref, o_ref):
    c = c_ref[...]
    ca = c * (1.0 / (1.0 + jnp.exp(-c)))
    o_ref[0] = _dot_f32(ca, w_ref[0]) + b_ref[0]


def _ada(c, w_ada, b_ada):
    depth, d, n = w_ada.shape
    bsz = c.shape[0]
    rows = 8
    tn = 1536
    cp = jnp.zeros((rows, d), F32).at[:bsz].set(c)
    out = pl.pallas_call(
        _ada_kernel,
        grid=(depth, n // tn),
        in_specs=[pl.BlockSpec((rows, d), lambda l, j: (0, 0)),
                  pl.BlockSpec((1, d, tn), lambda l, j: (l, 0, j)),
                  pl.BlockSpec((1, 1, tn), lambda l, j: (l, 0, j))],
        out_specs=pl.BlockSpec((1, rows, tn), lambda l, j: (l, 0, j)),
        out_shape=jax.ShapeDtypeStruct((depth, rows, n), F32),
        compiler_params=_cparams("arbitrary", "arbitrary"),
        name="ada_mod",
    )(cp, w_ada, b_ada.reshape(depth, 1, n))
    return out[:, :bsz]


def _in_kernel(x_ref, g_ref, sc_ref, sh_ref, w_ref, o_ref, *, cn):
    h = _rms(x_ref[0], g_ref[...]) * (1.0 + sc_ref[0]) + sh_ref[0]
    hb = h.astype(BF16)
    for j in range(w_ref.shape[1] // cn):
        o_ref[0, :, j * cn:(j + 1) * cn] = _dot(hb, w_ref[:, j * cn:(j + 1) * cn]).astype(BF16)


def _in_proj(x, g, scale, shift, w):
    bsz, s, d = x.shape
    nc = w.shape[1]
    tm = min(512, s)
    cn = nc // 3 if (nc // 3) % LANES == 0 else nc
    return pl.pallas_call(
        functools.partial(_in_kernel, cn=cn),
        grid=(bsz, s // tm),
        in_specs=[pl.BlockSpec((1, tm, d), lambda b, i: (b, i, 0)),
                  pl.BlockSpec((1, d), lambda b, i: (0, 0)),
                  pl.BlockSpec((1, 1, d), lambda b, i: (b, 0, 0)),
                  pl.BlockSpec((1, 1, d), lambda b, i: (b, 0, 0)),
                  pl.BlockSpec((d, nc), lambda b, i: (0, 0))],
        out_specs=pl.BlockSpec((1, tm, nc), lambda b, i: (b, i, 0)),
        out_shape=jax.ShapeDtypeStruct((bsz, s, nc), BF16),
        compiler_params=_cparams("arbitrary", "arbitrary"),
        name="in_proj",
    )(x, g.reshape(1, d), scale.reshape(bsz, 1, d), shift.reshape(bsz, 1, d), w)


CB_SBQ, CB_SBK, CB_SBV, CB_NSAQ = 0, 4, 8, 12
CB_KCMP, CB_VCMP = 16, 17
CB_KSLC, CB_VSLC, CB_KWIN, CB_VWIN = 18, 20, 22, 24
CB_GATE = 26
N_CB = 27


def _arrange_w_in(w_in):
    d = w_in.shape[0]
    w_sb = N_SB * DH
    off_nsa_q = 3 * w_sb
    off_kv = off_nsa_q + N_NSA * DH
    off_gate = off_kv + 3 * 2 * G_NSA * DH
    scale = DH ** -0.5

    def kv(br, kvi, g):
        lo = off_kv + ((br * 2 + kvi) * G_NSA + g) * DH
        return w_in[:, lo:lo + DH]

    cols = [w_in[:, 0:w_sb] * scale, w_in[:, w_sb:2 * w_sb], w_in[:, 2 * w_sb:3 * w_sb],
            w_in[:, off_nsa_q:off_kv] * scale,
            kv(0, 0, 0), kv(0, 0, 1), kv(0, 1, 0), kv(0, 1, 1)]
    for br in (1, 2):
        for kvi in (0, 1):
            for g in range(G_NSA):
                cols += [kv(br, kvi, g), kv(br, kvi, g)]
    n_gate = 3 * N_NSA
    cols += [w_in[:, off_gate:off_gate + n_gate], jnp.zeros((d, LANES - n_gate), w_in.dtype)]
    out = jnp.concatenate(cols, axis=1).astype(BF16)
    assert out.shape[1] == N_CB * LANES
    return out


def _sb_kernel(q_ref, k_ref, v_ref, o_ref, *, t):
    qi = pl.program_id(2)
    q = q_ref[0]
    lane = lax.broadcasted_iota(I32, (t, LANES), 1)
    row = lax.broadcasted_iota(I32, (t, t), 0)
    col = lax.broadcasted_iota(I32, (t, t), 1)
    upper = jnp.where(row > col, 1.0, 0.0).astype(BF16)
    upper2 = jnp.concatenate([upper, upper], axis=0)
    ones2 = jnp.ones((2 * t, LANES), BF16)
    causal = jnp.concatenate([col < row, col < row], axis=0)
    rep = t // LANES

    def logits(q2, k, diag):
        z = _nt(q2, k)
        lk = -(jnp.maximum(z, 0.0) + jnp.log(1.0 + jnp.exp(-jnp.abs(z))))
        if diag:
            lk = jnp.where(causal, lk, 0.0)
        hi = lk.astype(BF16)
        lo = (lk - hi.astype(F32)).astype(BF16)
        return z + lk, jnp.concatenate([hi, lo], axis=1)

    def weigh(ls, later, tot, v, carry, acc, diag):
        cb = carry if rep == 1 else jnp.concatenate([carry] * rep, axis=1)
        w = jnp.exp(ls + later + cb)
        if diag:
            w = jnp.where(causal, w, 0.0)
        r = _dot(w.astype(BF16), v)
        return carry + tot, acc + jnp.where(lane < DH, r[:t], r[t:])

    n_blk = q.shape[1] // LANES
    zq = jnp.zeros((t, LANES), BF16)
    q2s = []
    for c in range(n_blk):
        qc = q[:, c * LANES:(c + 1) * LANES]
        q2s.append(jnp.concatenate([jnp.where(lane < DH, qc, zq), jnp.where(lane < DH, zq, qc)], axis=0))

    def step(kt, st, diag):
        k = k_ref[0, pl.ds(kt * t, t), :]
        v = v_ref[0, pl.ds(kt * t, t), :]
        sls = [slice(c * LANES, (c + 1) * LANES) for c in range(n_blk)]
        parts = [logits(q2s[c], k[:, sls[c]], diag) for c in range(n_blk)]
        hl = jnp.concatenate([p[1] for p in parts], axis=0)
        later = _dot(hl, upper2)
        tot = _dot(hl, ones2)
        out, mx = [], None
        for c in range(n_blk):
            rows = slice(c * 2 * t, (c + 1) * 2 * t)
            carry, acc = weigh(parts[c][0], later[rows], tot[rows], v[:, sls[c]],
                               st[2 * c], st[2 * c + 1], diag)
            out += [carry, acc]
            mx = jnp.max(carry) if mx is None else jnp.maximum(mx, jnp.max(carry))
        return mx, tuple(out)

    st0 = (jnp.zeros((2 * t, LANES), F32), jnp.zeros((t, LANES), F32)) * n_blk
    mx, st = step(qi, st0, True)

    def cond(s):
        return jnp.logical_and(s[0] >= 0, s[1] > SB_EXIT)

    def body(s):
        mx, st = step(s[0], s[2], False)
        return s[0] - 1, mx, st

    _, _, st = lax.while_loop(cond, body, (qi - 1, mx, st))
    o_ref[0] = jnp.concatenate([st[2 * c + 1] for c in range(n_blk)], axis=1)


def _sb_attention(proj, t=256, pairs=4):
    bsz, s, _ = proj.shape
    t = min(t, s)
    npair = N_SB // 2
    w = pairs * LANES
    return pl.pallas_call(
        functools.partial(_sb_kernel, t=t),
        grid=(bsz, npair // pairs, s // t),
        in_specs=[pl.BlockSpec((1, t, w), lambda b, j, i: (b, i, CB_SBQ // pairs + j)),
                  pl.BlockSpec((1, s, w), lambda b, j, i: (b, 0, CB_SBK // pairs + j)),
                  pl.BlockSpec((1, s, w), lambda b, j, i: (b, 0, CB_SBV // pairs + j))],
        out_specs=pl.BlockSpec((1, t, w), lambda b, j, i: (b, i, j)),
        out_shape=jax.ShapeDtypeStruct((bsz, s, N_SB * DH), F32),
        compiler_params=_cparams("arbitrary", "arbitrary", "arbitrary"),
        name="sb_attn",
    )(proj, proj, proj)


def _gelu_tanh(x):
    return 0.5 * x * (1.0 + jnp.tanh(math.sqrt(2.0 / math.pi) * (x + 0.044715 * (x * x * x))))


def _cmp_kernel(xa_ref, xb_ref, pos_ref, w1_ref, w2_ref, o_ref):
    half = w1_ref.shape[1] // 2
    w1a = w1_ref[0, :half, :]
    w1b = w1_ref[0, half:, :]
    pos = pos_ref[0]
    bias = _dot(pos[:, :half], w1a) + _dot(pos[:, half:], w1b)
    hid = _dot(xa_ref[0, 0], w1a) + _dot(xb_ref[0, 0], w1b) + bias[0:1, :]
    o_ref[0, 0] = _dot(_gelu_tanh(hid).astype(BF16), w2_ref[0])


def _compress(proj, pos_k, w1_k, w2_k, pos_v, w1_v, w2_v):
    bsz, s, _ = proj.shape
    n16 = s // CMP_STRIDE
    raw = proj[:, :, CB_KCMP * LANES:(CB_VCMP + 1) * LANES]
    x16 = raw.reshape(bsz, s, 4, DH).transpose(0, 2, 1, 3).reshape(bsz, 4, n16, CMP_STRIDE * DH)
    x16b = jnp.concatenate([x16[:, :, 1:], jnp.zeros_like(x16[:, :, :1])], axis=2)
    hidden = w1_k.shape[1]
    w1 = jnp.stack([w1_k, w1_v]).astype(BF16)
    w2 = jnp.stack([jnp.concatenate([w2_k, w2_k], 1), jnp.concatenate([w2_v, w2_v], 1)]).astype(BF16)
    pos = jnp.stack([pos_k.reshape(1, -1), pos_v.reshape(1, -1)])
    pos = jnp.concatenate([pos, jnp.zeros((2, 7, pos.shape[2]), pos.dtype)], axis=1).astype(BF16)
    k16 = CMP_STRIDE * DH
    return pl.pallas_call(
        _cmp_kernel,
        grid=(bsz, 4),
        in_specs=[pl.BlockSpec((1, 1, n16, k16), lambda b, j: (b, j, 0, 0)),
                  pl.BlockSpec((1, 1, n16, k16), lambda b, j: (b, j, 0, 0)),
                  pl.BlockSpec((1, 8, 2 * k16), lambda b, j: (j // 2, 0, 0)),
                  pl.BlockSpec((1, 2 * k16, hidden), lambda b, j: (j // 2, 0, 0)),
                  pl.BlockSpec((1, hidden, LANES), lambda b, j: (j // 2, 0, 0))],
        out_specs=pl.BlockSpec((1, 1, n16, LANES), lambda b, j: (b, j, 0, 0)),
        out_shape=jax.ShapeDtypeStruct((bsz, 4, n16, LANES), F32),
        compiler_params=_cparams("arbitrary", "arbitrary"),
        name="nsa_compress",
    )(x16, x16b, pos, w1, w2)


def _bucket_table():
    n = np.arange(REL_MAX_DIST + 1)
    exact = REL_BUCKETS // 2
    val = (np.log(np.maximum(n, 1).astype(np.float32) / np.float32(exact)).astype(np.float32)
           / np.float32(math.log(REL_MAX_DIST / exact)) * np.float32(REL_BUCKETS - exact))
    large = np.minimum(exact + val.astype(np.int32), REL_BUCKETS - 1)
    return np.where(n < exact, n, large).astype(np.int32)


def _bias_tiles(rel_table):
    bucket = _bucket_table()
    far_row = rel_table[int(bucket[REL_MAX_DIST])].astype(F32)
    r = np.arange(TQ)[:, None]

    def tile(dist, valid, base):
        onehot = jax.nn.one_hot(bucket[np.clip(dist, 0, REL_MAX_DIST)], REL_BUCKETS, dtype=F32)
        t = jnp.einsum('rwb,bh->hrw', onehot, rel_table.astype(F32),
                       precision=lax.Precision.HIGHEST)
        t = jnp.where(valid[None], t - base, NEG)
        return t.reshape(G_NSA, Z_NSA * TQ, dist.shape[1])

    w = np.arange(LANES)[None, :]
    d_c = r - CMP_STRIDE * (w - KC_FRONT) - (CMP_BLOCK - 1)
    w2 = np.arange(NEAR_BACK + TQ)[None, :]
    d_s = r + NEAR_BACK - w2
    w5 = np.arange(WINDOW + TQ)[None, :]
    d_w = r + WINDOW - w5
    far = jnp.broadcast_to(far_row[:, None, None], (N_NSA, TQ, LANES))
    far = far.reshape(G_NSA, Z_NSA * TQ, LANES)
    return (tile(d_c, d_c >= 0, far_row[:, None, None]), tile(d_s, d_s >= 0, 0.0),
            tile(d_w, (d_w >= 0) & (d_w < WINDOW), 0.0), far)


def _gate_expand():
    e = np.zeros((G_NSA, LANES, 3 * Z_NSA * DH), np.float32)
    for g in range(G_NSA):
        for z in range(Z_NSA):
            for br in range(3):
                e[g, (g * Z_NSA + z) * 3 + br, br * Z_NSA * DH + z * DH: br * Z_NSA * DH + (z + 1) * DH] = 1.0
    return e


def _overlap_padded(n16, kcp, n_slc):
    i = np.arange(kcp)[:, None] - KC_FRONT
    j = np.arange(LANES)[None, :]
    n_cmp = n16 - 1
    ok = (i >= 0) & (i < n_cmp) & (j < n_slc)
    ov = (i * CMP_STRIDE < j * SLC_BLOCK + SLC_BLOCK) & (i * CMP_STRIDE + CMP_BLOCK > j * SLC_BLOCK)
    return (ok & ov).astype(np.float32)


def _block_onehot(s):
    return (np.arange(s)[:, None] // SLC_BLOCK == np.arange(LANES)[None, :]).astype(np.float32)


def _nsa_kernel(q_ref, kcb_ref, vcb_ref, kcf_ref, vcf_ref, ks_ref, vs_ref, kw_ref, vw_ref, et_ref,
                gate_ref, ovb_ref, ovf_ref, bcn_ref, bsn_ref, bw_ref, cf_ref, eg_ref, o_ref,
                sb0_ref, sb1_ref, *, n_slc, n_round):
    blk = pl.program_id(2)
    c0 = blk * TQ
    r4 = Z_NSA * TQ
    q = q_ref[0]
    lane = lax.broadcasted_iota(I32, (TQ, LANES), 1)
    lo_half = lane < DH
    zq = jnp.zeros((TQ, LANES), BF16)
    qa, qb = q[:, :LANES], q[:, LANES:]
    qs = jnp.concatenate([jnp.where(lo_half, qa, zq), jnp.where(lo_half, zq, qa),
                          jnp.where(lo_half, qb, zq), jnp.where(lo_half, zq, qb)], axis=0)
    ones_k = jnp.ones((SEL_FAR_TILE, LANES), BF16)
    ones_w = jnp.ones((WINDOW + TQ, LANES), BF16)

    kcp = kcb_ref.shape[2]
    prow = lax.broadcasted_iota(I32, (1, kcp), 1)
    near0 = pl.multiple_of(blk * (TQ // CMP_STRIDE), 8)
    far_row = jnp.where((prow >= KC_FRONT) & (prow < near0), 0.0, NEG)
    lane1 = lax.broadcasted_iota(I32, (1, LANES), 1)
    near_row = jnp.where(lane1 + near0 >= KC_FRONT, 0.0, NEG)
    kcn = kcf_ref[0, 0, pl.ds(near0, LANES), :].astype(BF16)
    vcn = vcf_ref[0, 0, pl.ds(near0, LANES), :].astype(BF16)
    s_far = _nt(qs, kcb_ref[0, 0]) + far_row
    s_near = _nt(qs, kcn) + bcn_ref[0] + near_row
    m_c = jnp.maximum(jnp.max(s_far, axis=1, keepdims=True), jnp.max(s_near, axis=1, keepdims=True))
    m_c = jnp.maximum(m_c, M_INIT)
    p_far = jnp.exp(s_far - m_c)
    p_near = jnp.exp(s_near - m_c)
    den = jnp.sum(p_far, axis=1, keepdims=True) + jnp.sum(p_near, axis=1, keepdims=True)
    inv_c = 1.0 / jnp.maximum(den, 1e-30)
    o_c = (_dot(p_far.astype(BF16), vcb_ref[0, 0]) + _dot(p_near.astype(BF16), vcn)) * inv_c

    pn_far = p_far * inv_c
    pn_near = p_near * inv_c
    pz_far = pn_far[0:TQ] + pn_far[TQ:2 * TQ] + pn_far[2 * TQ:3 * TQ] + pn_far[3 * TQ:]
    pz_near = pn_near[0:TQ] + pn_near[TQ:2 * TQ] + pn_near[2 * TQ:3 * TQ] + pn_near[3 * TQ:]
    ovn = ovf_ref[pl.ds(near0, LANES), :].astype(BF16)
    imp = _dot_hl(pz_far, ovb_ref[...]) + _dot_hl(pz_near, ovn)
    tpos = c0 + lax.broadcasted_iota(I32, (TQ, LANES), 0)
    cur = lax.shift_right_logical(tpos, 6)
    valid = (lane * SLC_BLOCK <= tpos) & (lane < n_slc)
    forced = valid & ((lane == 0) | (lane == cur) | (lane == cur - 1))
    score = jnp.where(valid & jnp.logical_not(forced), imp, -1.0)
    score = jnp.where(lane < n_slc, score, -2.0)

    rest, taken, thr = score, jnp.zeros((TQ, 1), F32), jnp.zeros((TQ, 1), F32)
    for _ in range(n_round):
        m = jnp.max(rest, axis=1, keepdims=True)
        eq = rest == m
        thr = jnp.where(taken < n_round, m, thr)
        taken = taken + jnp.sum(jnp.where(eq, 1.0, 0.0), axis=1, keepdims=True)
        rest = jnp.where(eq, -jnp.inf, rest)
    above = score > thr
    at_thr = score == thr
    need = n_round - jnp.sum(jnp.where(above, 1.0, 0.0), axis=1, keepdims=True)
    brow = lax.broadcasted_iota(I32, (LANES, LANES), 0)
    bcol = lax.broadcasted_iota(I32, (LANES, LANES), 1)
    before = _dot(jnp.where(at_thr, 1.0, 0.0).astype(BF16),
                  jnp.where(brow < bcol, 1.0, 0.0).astype(BF16))
    sel = ((above | (at_thr & (before < need))) & valid) | forced

    kws, vws = [], []
    for w in range((WINDOW + TQ) // LANES):
        st = pl.multiple_of(jnp.maximum(c0 - WINDOW + w * LANES, 0), LANES)
        kws.append(kw_ref[0, pl.ds(st, LANES), :])
        vws.append(vw_ref[0, pl.ds(st, LANES), :])
    wl = lax.broadcasted_iota(I32, (1, WINDOW + TQ), 1)
    pos_row = jnp.where(c0 - WINDOW + wl >= 0, 0.0, NEG)
    s = _nt(qs, jnp.concatenate(kws, axis=0)) + bw_ref[0] + pos_row
    p = jnp.exp(s - jnp.max(s, axis=1, keepdims=True)).astype(BF16)
    r_w = _dot(p, jnp.concatenate([jnp.concatenate(vws, axis=0), ones_w], axis=1))
    o_w = r_w[:, :LANES] / r_w[:, LANES:]
    gl = _dot(gate_ref[0], eg_ref[0])
    sig = 1.0 / (1.0 + jnp.exp(-gl))
    wide = Z_NSA * DH

    sel_all = jnp.where(sel, 0.0, NEG).astype(BF16)
    far_blocks = blk * (TQ // SLC_BLOCK) - NEAR_BACK // SLC_BLOCK
    sel_far = jnp.where(sel & (lane < far_blocks), 0.0, NEG).astype(BF16)
    q_far = jnp.concatenate([qs, jnp.concatenate([sel_far] * Z_NSA, axis=0)], axis=1)
    q_near = jnp.concatenate([qs, jnp.concatenate([sel_all] * Z_NSA, axis=0)], axis=1)
    tk = SEL_FAR_TILE
    last_tile = ks_ref.shape[1] // tk - 1

    def qk(kt):
        k0 = pl.multiple_of(kt * tk, tk)
        return _nt(q_far, jnp.concatenate([ks_ref[0, pl.ds(k0, tk), :], et_ref[pl.ds(k0, tk), :]], axis=1))

    def fold(s, v, st):
        m, acc = st
        mn = jnp.maximum(m, jnp.max(s, axis=1, keepdims=True))
        p = jnp.exp(s - mn).astype(BF16)
        return mn, jnp.exp(m - mn) * acc + _dot(p, v)

    def v_tile(kt):
        k0 = pl.multiple_of(kt * tk, tk)
        return jnp.concatenate([vs_ref[0, pl.ds(k0, tk), :], ones_k], axis=1)

    n_pair = lax.shift_right_logical(jnp.maximum(c0 - NEAR_BACK, 0) + (2 * tk - 1), int(math.log2(2 * tk)))
    sb0_ref[...] = qk(0)

    def far_step(i, st):
        sb1_ref[...] = qk(2 * i + 1)
        st = fold(sb0_ref[...], v_tile(2 * i), st)
        sb0_ref[...] = qk(jnp.minimum(2 * i + 2, last_tile))
        return fold(sb1_ref[...], v_tile(2 * i + 1), st)

    def far_step2(i, st):
        return far_step(2 * i + 1, far_step(2 * i, st))

    m0 = jnp.full((r4, 1), M_INIT, F32)
    n_quad = lax.shift_right_logical(n_pair, 1)
    st = lax.fori_loop(0, n_quad, far_step2, (m0, jnp.zeros((r4, 2 * LANES), F32)))
    m_s, acc_s = lax.fori_loop(2 * n_quad, n_pair, far_step, st)
    m_s = m_s + cf_ref[0][:, 0:1]
    n_near = NEAR_BACK + TQ
    p0 = pl.multiple_of(jnp.maximum(c0 - NEAR_BACK, 0), NEAR_BACK)
    d0 = pl.multiple_of(c0, TQ)
    kn = jnp.concatenate([ks_ref[0, pl.ds(p0, NEAR_BACK), :], ks_ref[0, pl.ds(d0, TQ), :]], axis=0)
    vn = jnp.concatenate([vs_ref[0, pl.ds(p0, NEAR_BACK), :], vs_ref[0, pl.ds(d0, TQ), :]], axis=0)
    nrow = lax.broadcasted_iota(I32, (n_near, LANES), 0)
    nlane = lax.broadcasted_iota(I32, (n_near, LANES), 1)
    e_n = jnp.where(lax.shift_right_logical(nrow, 6) + far_blocks == nlane, 1.0, 0.0).astype(BF16)
    lane2 = lax.broadcasted_iota(I32, (1, n_near), 1)
    prev_row = jnp.where((lane2 < NEAR_BACK) & (blk == 0), NEG, 0.0)
    s = _nt(q_near, jnp.concatenate([kn, e_n], axis=1)) + bsn_ref[0] + prev_row
    _, acc_s = fold(s, jnp.concatenate([vn, ones_k[:n_near]], axis=1), (m_s, acc_s))
    o_s = acc_s[:, :LANES] / acc_s[:, LANES:]

    def heads(o):
        return jnp.concatenate([jnp.where(lo_half, o[0:TQ], o[TQ:2 * TQ]),
                                jnp.where(lo_half, o[2 * TQ:3 * TQ], o[3 * TQ:])], axis=1)

    o_ref[0] = (sig[:, 0:wide] * heads(o_c) + sig[:, wide:2 * wide] * heads(o_s)
                + sig[:, 2 * wide:] * heads(o_w))


def _nsa_attention(proj, cmp_out, rel_table):
    bsz, s, _ = proj.shape
    assert s % (2 * SEL_FAR_TILE) == 0 and s >= WINDOW + TQ
    n16 = s // CMP_STRIDE
    n_slc = s // SLC_BLOCK
    assert N_SELECT <= n_slc <= LANES
    nq = s // TQ
    kcp = -(-(max(n16 + KC_FRONT, (TQ // CMP_STRIDE) * (nq - 1) + LANES)) // LANES) * LANES
    pad = ((0, 0), (0, 0), (KC_FRONT, kcp - n16 - KC_FRONT), (0, 0))
    cf = jnp.pad(cmp_out, pad)
    cb = cf.astype(BF16)
    ov = _overlap_padded(n16, kcp, n_slc)
    bcn, bsn, bw, far = _bias_tiles(rel_table)
    eg = jnp.asarray(_gate_expand(), BF16)
    r4 = Z_NSA * TQ
    kv_spec = lambda cb0: pl.BlockSpec((1, s, LANES), lambda b, g, i, cb0=cb0: (b, 0, cb0 + g))
    cmp_spec = lambda j0: pl.BlockSpec((1, 1, kcp, LANES), lambda b, g, i, j0=j0: (b, j0 + g, 0, 0))
    tile_spec = lambda w: pl.BlockSpec((1, r4, w), lambda b, g, i: (g, 0, 0))
    const2 = lambda n: pl.BlockSpec((n, LANES), lambda b, g, i: (0, 0))
    return pl.pallas_call(
        functools.partial(_nsa_kernel, n_slc=n_slc, n_round=N_SELECT - 3),
        grid=(bsz, G_NSA, nq),
        in_specs=[pl.BlockSpec((1, TQ, 2 * LANES), lambda b, g, i: (b, i, CB_NSAQ // 2 + g)),
                  cmp_spec(0), cmp_spec(2), cmp_spec(0), cmp_spec(2),
                  kv_spec(CB_KSLC), kv_spec(CB_VSLC), kv_spec(CB_KWIN), kv_spec(CB_VWIN),
                  const2(s),
                  pl.BlockSpec((1, TQ, LANES), lambda b, g, i: (b, i, CB_GATE)),
                  const2(kcp), const2(kcp),
                  tile_spec(LANES), tile_spec(NEAR_BACK + TQ), tile_spec(WINDOW + TQ), tile_spec(LANES),
                  pl.BlockSpec((1, LANES, 3 * Z_NSA * DH), lambda b, g, i: (g, 0, 0))],
        out_specs=pl.BlockSpec((1, TQ, Z_NSA * DH), lambda b, g, i: (b, i, g)),
        out_shape=jax.ShapeDtypeStruct((bsz, s, N_NSA * DH), F32),
        scratch_shapes=[pltpu.VMEM((r4, SEL_FAR_TILE), F32), pltpu.VMEM((r4, SEL_FAR_TILE), F32)],
        compiler_params=_cparams("arbitrary", "arbitrary", "arbitrary"),
        name="nsa_attn",
    )(proj, cb, cb, cf, cf, proj, proj, proj, proj, jnp.asarray(_block_onehot(s), BF16), proj,
      jnp.asarray(ov, BF16), jnp.asarray(ov, F32), bcn, bsn, bw, far, eg)


def _out_kernel(osb_ref, onsa_ref, x_ref, gsb_ref, gnsa_ref, w_ref, gpost_ref, gate_ref,
                gpre_ref, sc_ref, sh_ref, xo_ref, h_ref):
    half = osb_ref.shape[2]
    a = _rms(osb_ref[0], gsb_ref[...]).astype(BF16)
    b = _rms(onsa_ref[0], gnsa_ref[...]).astype(BF16)
    m = _dot(a, w_ref[:half, :]) + _dot(b, w_ref[half:, :])
    x = x_ref[0] + gate_ref[0] * _rms(m, gpost_ref[...])
    xo_ref[0] = x
    h_ref[0] = (_rms(x, gpre_ref[...]) * (1.0 + sc_ref[0]) + sh_ref[0]).astype(BF16)


def _out_proj(o_sb, o_nsa, x, g_sb, g_nsa, w_out, g_post, gate_m, g_pre_ffn, scale_f, shift_f):
    bsz, s, d = x.shape
    half = o_sb.shape[2]
    tm = min(512, s)
    row = lambda n: pl.BlockSpec((1, n), lambda b, i: (0, 0))
    mod = pl.BlockSpec((1, 1, d), lambda b, i: (b, 0, 0))
    act = lambda n: pl.BlockSpec((1, tm, n), lambda b, i: (b, i, 0))
    return pl.pallas_call(
        _out_kernel,
        grid=(bsz, s // tm),
        in_specs=[act(half), act(half), act(d), row(half), row(half),
                  pl.BlockSpec((2 * half, d), lambda b, i: (0, 0)), row(d), mod, row(d), mod, mod],
        out_specs=[act(d), act(d)],
        out_shape=[jax.ShapeDtypeStruct((bsz, s, d), F32), jax.ShapeDtypeStruct((bsz, s, d), BF16)],
        compiler_params=_cparams("arbitrary", "arbitrary"),
        name="out_proj",
    )(o_sb, o_nsa, x, g_sb.reshape(1, half), g_nsa.reshape(1, half), w_out.astype(BF16),
      g_post.reshape(1, d), gate_m.reshape(bsz, 1, d), g_pre_ffn.reshape(1, d),
      scale_f.reshape(bsz, 1, d), shift_f.reshape(bsz, 1, d))


def _silu(x):
    return x * (1.0 / (1.0 + jnp.exp(-x)))


def _ffn_kernel(h_ref, x_ref, wg_ref, wu_ref, wd_ref, gpost_ref, gate_ref, o_ref):
    h = h_ref[0]
    a = (_silu(_dot(h, wg_ref[...])) * _dot(h, wu_ref[...])).astype(BF16)
    o_ref[0] = x_ref[0] + gate_ref[0] * _rms(_dot(a, wd_ref[...]), gpost_ref[...])


def _dense_ffn(h, x, w_gate, w_up, w_down, g_post, gate_f):
    bsz, s, d = x.shape
    ff = w_gate.shape[1]
    tm = min(512, s)
    act = pl.BlockSpec((1, tm, d), lambda b, i: (b, i, 0))
    once = pl.Buffered(1)
    return pl.pallas_call(
        _ffn_kernel,
        grid=(bsz, s // tm),
        in_specs=[act, act,
                  pl.BlockSpec((d, ff), lambda b, i: (0, 0), pipeline_mode=once),
                  pl.BlockSpec((d, ff), lambda b, i: (0, 0), pipeline_mode=once),
                  pl.BlockSpec((ff, d), lambda b, i: (0, 0), pipeline_mode=once),
                  pl.BlockSpec((1, d), lambda b, i: (0, 0)),
                  pl.BlockSpec((1, 1, d), lambda b, i: (b, 0, 0))],
        out_specs=act,
        out_shape=jax.ShapeDtypeStruct((bsz, s, d), F32),
        compiler_params=_cparams("arbitrary", "arbitrary"),
        name="dense_ffn",
    )(h, x, w_gate.astype(BF16), w_up.astype(BF16), w_down.astype(BF16),
      g_post.reshape(1, d), gate_f.reshape(bsz, 1, d))


MOE_RC = 128
MOE_GC = 256
MOE_SC = 1024


def _moe_rows(tile):
    return -(-(2 * tile + N_EXPERTS * MOE_RC) // MOE_SC) * MOE_SC


def _dispatch_kernel(x_ref, gpre_ref, sc_ref, sh_ref, wr_ref, br_ref,
                     xg_ref, wrow_ref, d_ref, off_ref, nch_ref):
    tile = x_ref.shape[1]
    rows = xg_ref.shape[1]
    h = _rms(x_ref[0], gpre_ref[...]) * (1.0 + sc_ref[0]) + sh_ref[0]
    logits = _dot_f32(h, wr_ref[...]) + br_ref[...]
    lane = lax.broadcasted_iota(I32, (tile, LANES), 1)
    lanef = lane.astype(F32)
    e = jnp.exp(logits - jnp.max(logits, axis=1, keepdims=True))
    probs = e / jnp.sum(e, axis=1, keepdims=True)
    probs = jnp.where(lane < N_EXPERTS, probs, -1.0)
    m1 = jnp.max(probs, axis=1, keepdims=True)
    i1 = jnp.min(jnp.where(probs == m1, lanef, 1e9), axis=1, keepdims=True)
    rest = jnp.where(lanef == i1, -1.0, probs)
    m2 = jnp.max(rest, axis=1, keepdims=True)
    i2 = jnp.min(jnp.where(rest == m2, lanef, 1e9), axis=1, keepdims=True)
    tot = m1 + m2
    hit1 = lanef == i1
    hit2 = lanef == i2
    mask = hit1 | hit2
    ch = 256
    r = lax.broadcasted_iota(I32, (ch, ch), 0)
    c = lax.broadcasted_iota(I32, (ch, ch), 1)
    lower = jnp.where(c < r, 1.0, 0.0).astype(BF16)
    ones = jnp.ones((8, ch), BF16)
    count = jnp.zeros((1, LANES), F32)
    ranks = []
    for k in range(tile // ch):
        mk = jnp.where(mask[k * ch:(k + 1) * ch], 1.0, 0.0).astype(BF16)
        ranks.append(_dot(lower, mk) + count)
        count = count + _dot(ones, mk)[0:1]
    rank = jnp.concatenate(ranks, axis=0)
    shift = int(math.log2(MOE_RC))
    nch = lax.shift_right_logical(count.astype(I32) + (MOE_RC - 1), shift)
    nch8 = jnp.broadcast_to(nch.astype(F32), (8, LANES))
    lr = lax.broadcasted_iota(I32, (LANES, LANES), 0)
    lc = lax.broadcasted_iota(I32, (LANES, LANES), 1)
    offc = _dot(nch8.astype(BF16), jnp.where(lr < lc, 1.0, 0.0).astype(BF16))
    row = offc[0:1] * float(MOE_RC) + rank
    d1 = jnp.sum(jnp.where(hit1, row, 0.0), axis=1, keepdims=True)
    d2 = jnp.sum(jnp.where(hit2, row, 0.0), axis=1, keepdims=True)
    dd = jnp.where(lane == 0, d1, jnp.where(lane == 1, d2, -1.0))
    d_ref[0] = dd.astype(I32)
    off_ref[0] = offc.astype(I32)
    nch_ref[0] = jnp.broadcast_to(nch, (8, LANES))

    ddt = jnp.transpose(dd)
    d1t = ddt[0:1].astype(I32)
    d2t = ddt[1:2].astype(I32)

    wt = jnp.transpose(jnp.where(lane == 0, m1 / tot, jnp.where(lane == 1, m2 / tot, 0.0)))
    w1t = wt[0:1]
    w2t = wt[1:2]
    hb = h.astype(BF16)
    rid = lax.broadcasted_iota(I32, (MOE_GC, tile), 0)
    for k in range(rows // MOE_GC):
        hit1 = rid + k * MOE_GC == d1t
        hit2 = rid + k * MOE_GC == d2t
        onehot = jnp.where(hit1 | hit2, 1.0, 0.0).astype(BF16)
        xg_ref[0, k * MOE_GC:(k + 1) * MOE_GC, :] = _dot(onehot, hb).astype(BF16)
        wrow = jnp.sum(jnp.where(hit1, w1t, jnp.where(hit2, w2t, 0.0)), axis=1, keepdims=True)
        wrow_ref[0, k * MOE_GC:(k + 1) * MOE_GC, :] = jnp.broadcast_to(wrow, (MOE_GC, LANES))


def _dispatch(x, g_pre, scale_f, shift_f, w_router, b_router, tile):
    bsz, s, d = x.shape
    nt = s // tile
    rows = _moe_rows(tile)
    wr = jnp.zeros((d, LANES), F32).at[:, :N_EXPERTS].set(w_router.astype(F32))
    br = jnp.full((1, LANES), NEG, F32).at[0, :N_EXPERTS].set(b_router.astype(F32))
    mod = pl.BlockSpec((1, 1, d), lambda b, i: (b, 0, 0))
    per_tile = lambda r, n: pl.BlockSpec((1, r, n), lambda b, i: (b * nt + i, 0, 0))
    xg, wrow, dest, offc, nch = pl.pallas_call(
        _dispatch_kernel,
        grid=(bsz, nt),
        in_specs=[pl.BlockSpec((1, tile, d), lambda b, i: (b, i, 0)),
                  pl.BlockSpec((1, d), lambda b, i: (0, 0)), mod, mod,
                  pl.BlockSpec((d, LANES), lambda b, i: (0, 0)),
                  pl.BlockSpec((1, LANES), lambda b, i: (0, 0))],
        out_specs=[per_tile(rows, d), per_tile(rows, LANES), per_tile(tile, LANES),
                   per_tile(8, LANES), per_tile(8, LANES)],
        out_shape=[jax.ShapeDtypeStruct((bsz * nt, rows, d), BF16),
                   jax.ShapeDtypeStruct((bsz * nt, rows, LANES), F32),
                   jax.ShapeDtypeStruct((bsz * nt, tile, LANES), I32),
                   jax.ShapeDtypeStruct((bsz * nt, 8, LANES), I32),
                   jax.ShapeDtypeStruct((bsz * nt, 8, LANES), I32)],
        compiler_params=_cparams("arbitrary", "arbitrary"),
        name="moe_dispatch",
    )(x, g_pre.reshape(1, d), scale_f.reshape(bsz, 1, d), shift_f.reshape(bsz, 1, d), wr, br)
    return xg, wrow, dest, offc[:, 0, :N_EXPERTS], nch[:, 0, :N_EXPERTS]


def _slot_order(offc, nch, slots_per_tile):
    ends = offc + nch
    c = jnp.arange(slots_per_tile, dtype=I32)[None, :, None]
    expert = jnp.sum((c >= ends[:, None, :]).astype(I32), axis=-1).reshape(-1)
    n = expert.shape[0]
    order = jnp.argsort(expert * n + jnp.arange(n, dtype=I32)).astype(I32)
    exp_sorted = expert[order]
    used = (exp_sorted < N_EXPERTS).astype(I32)
    return order, jnp.minimum(exp_sorted, N_EXPERTS - 1).astype(I32), used


def _slots_kernel(slot_ref, exp_ref, used_ref, x_ref, wrow_ref, wg_ref, wu_ref, wd_ref, y_ref):
    p = pl.program_id(0)

    @pl.when(used_ref[p] == 1)
    def _():
        x = x_ref[0]
        a = (_silu(_dot(x, wg_ref[0])) * _dot(x, wu_ref[0])).astype(BF16)
        y_ref[0] = (_dot(a, wd_ref[0]) * wrow_ref[0][:, 0:1]).astype(BF16)

    @pl.when(used_ref[p] == 0)
    def _():
        y_ref[0] = jnp.zeros(y_ref.shape[1:], BF16)


def _expert_slots(xg, wrow, order, exp_sorted, used, w_gate, w_up, w_down):
    ntile, rows, d = xg.shape
    nslot = ntile * rows // MOE_RC
    ff = w_gate.shape[2]
    once = pl.Buffered(1)
    slot = lambda n: pl.BlockSpec((1, MOE_RC, n), lambda p, s, e, u: (s[p], 0, 0))
    grid_spec = pltpu.PrefetchScalarGridSpec(
        num_scalar_prefetch=3,
        grid=(nslot,),
        in_specs=[slot(d), slot(LANES),
                  pl.BlockSpec((1, d, ff), lambda p, s, e, u: (e[p], 0, 0), pipeline_mode=once),
                  pl.BlockSpec((1, d, ff), lambda p, s, e, u: (e[p], 0, 0), pipeline_mode=once),
                  pl.BlockSpec((1, ff, d), lambda p, s, e, u: (e[p], 0, 0), pipeline_mode=once)],
        out_specs=slot(d))
    y = pl.pallas_call(
        _slots_kernel,
        grid_spec=grid_spec,
        out_shape=jax.ShapeDtypeStruct((nslot, MOE_RC, d), BF16),
        input_output_aliases={3: 0},
        compiler_params=_cparams("arbitrary"),
        name="moe_slots",
    )(order, exp_sorted, used, xg.reshape(nslot, MOE_RC, d), wrow.reshape(nslot, MOE_RC, LANES),
      w_gate.astype(BF16), w_up.astype(BF16), w_down.astype(BF16))
    return y.reshape(ntile, rows, d)


def _combine_kernel(y_ref, d_ref, x_ref, gpost_ref, gate_ref, o_ref):
    tile = x_ref.shape[1]
    rows = y_ref.shape[1]
    dd = d_ref[0]
    d1c = dd[:, 0:1]
    d2c = dd[:, 1:2]
    cid = lax.broadcasted_iota(I32, (tile, MOE_SC), 1)
    z = None
    for k in range(rows // MOE_SC):
        pt = jnp.where((cid + k * MOE_SC == d1c) | (cid + k * MOE_SC == d2c), 1.0, 0.0).astype(BF16)
        part = _dot(pt, y_ref[0, k * MOE_SC:(k + 1) * MOE_SC, :])
        z = part if z is None else z + part
    o_ref[0] = x_ref[0] + gate_ref[0] * _rms(z, gpost_ref[...])


def _combine(y, dest, x, g_post, gate_f):
    bsz, s, d = x.shape
    ntile, rows, _ = y.shape
    tile = dest.shape[1]
    nt = s // tile
    per_tile = lambda r, n: pl.BlockSpec((1, r, n), lambda i: (i, 0, 0))
    out = pl.pallas_call(
        _combine_kernel,
        grid=(ntile,),
        in_specs=[per_tile(rows, d), per_tile(tile, LANES), per_tile(tile, d),
                  pl.BlockSpec((1, d), lambda i: (0, 0)),
                  pl.BlockSpec((1, 1, d), lambda i: (i // nt, 0, 0))],
        out_specs=per_tile(tile, d),
        out_shape=jax.ShapeDtypeStruct((ntile, tile, d), F32),
        compiler_params=_cparams("arbitrary"),
        name="moe_combine",
    )(y, dest, x.reshape(ntile, tile, d), g_post.reshape(1, d), gate_f.reshape(bsz, 1, d))
    return out.reshape(bsz, s, d)


def _mixer(x, mod, layer, rel_table, g_pre_mix, w_in, cmp_params):
    shift_m, scale_m = mod[:, 0], mod[:, 1]
    proj = _in_proj(x, g_pre_mix, scale_m, shift_m, _arrange_w_in(w_in))
    o_sb = _sb_attention(proj)
    o_nsa = _nsa_attention(proj, _compress(proj, *cmp_params), rel_table)
    return o_sb, o_nsa


def kernel(x, c, rel_table, w_ada, b_ada, g_pre_mix, g_post_mix, g_pre_ffn, g_post_ffn, w_in, w_out, g_sb, g_nsa, cmp_pos_k, cmp_w1_k, cmp_w2_k, cmp_pos_v, cmp_w1_v, cmp_w2_v, ffn_w_gate, ffn_w_up, ffn_w_down, moe_w_router, moe_b_router, moe_w_gate, moe_w_up, moe_w_down):
    bsz, s, d = x.shape
    depth = w_in.shape[0]
    mods = _ada(c, w_ada, b_ada).reshape(depth, bsz, 6, d)
    moe_tile = min(1024, s)
    for layer in range(depth):
        mod = mods[layer]
        cmp_params = (cmp_pos_k[layer], cmp_w1_k[layer], cmp_w2_k[layer],
                      cmp_pos_v[layer], cmp_w1_v[layer], cmp_w2_v[layer])
        o_sb, o_nsa = _mixer(x, mod, layer, rel_table, g_pre_mix[layer], w_in[layer], cmp_params)
        x, h = _out_proj(o_sb, o_nsa, x, g_sb[layer], g_nsa[layer], w_out[layer], g_post_mix[layer],
                         mod[:, 2], g_pre_ffn[layer], mod[:, 4], mod[:, 3])
        i = layer // 2
        if layer % 2 == 0:
            x = _dense_ffn(h, x, ffn_w_gate[i], ffn_w_up[i], ffn_w_down[i], g_post_ffn[layer], mod[:, 5])
        else:
            xg, wrow, dest, offc, nch = _dispatch(x, g_pre_ffn[layer], mod[:, 4], mod[:, 3],
                                                  moe_w_router[i], moe_b_router[i], moe_tile)
            order = _slot_order(offc, nch, xg.shape[1] // MOE_RC)
            y = _expert_slots(xg, wrow, *order, moe_w_gate[i], moe_w_up[i], moe_w_down[i])
            x = _combine(y, dest, x, g_post_ffn[layer], mod[:, 5])
    return x
```

```python
import functools
import math

import numpy as np
import jax
import jax.numpy as jnp
from jax import lax
from jax.experimental import pallas as pl
from jax.experimental.pallas import tpu as pltpu

F32 = jnp.float32
BF16 = jnp.bfloat16
I32 = jnp.int32

LANES = 128
DH = 64
N_SB = 8
N_NSA = 8
G_NSA = 2
Z_NSA = 4
CMP_BLOCK = 32
CMP_STRIDE = 16
SLC_BLOCK = 64
N_SELECT = 16
WINDOW = 512
REL_BUCKETS = 32
REL_MAX_DIST = 128
N_EXPERTS = 8
EPS = 1e-6
FORCED = 1e4
NEG = -1e30
M_INIT = -1e29
SB_EXIT = -104.5
VMEM_LIMIT = 56 * 1024 * 1024

TQ = 256
NEAR_BACK = REL_MAX_DIST
KC_FRONT = 16
SEL_FAR_TILE = 512


def _cparams(*sem):
    return pltpu.CompilerParams(dimension_semantics=sem, vmem_limit_bytes=VMEM_LIMIT)


def _nt(a, b):
    return lax.dot_general(a, b, (((1,), (1,)), ((), ())), preferred_element_type=F32)


def _dot(a, b):
    return jnp.dot(a, b, preferred_element_type=F32)


def _split3(a):
    hi = a.astype(BF16)
    r = a - hi.astype(F32)
    mid = r.astype(BF16)
    lo = (r - mid.astype(F32)).astype(BF16)
    return hi, mid, lo


def _dot_hl(a, b):
    hi = a.astype(BF16)
    lo = (a - hi.astype(F32)).astype(BF16)
    return _dot(hi, b) + _dot(lo, b)


def _dot_f32(a, b):
    ah, am, al = _split3(a)
    bh, bm, bl = _split3(b)
    return (_dot(ah, bh) + (_dot(ah, bm) + _dot(am, bh))
            + (_dot(ah, bl) + _dot(am, bm) + _dot(al, bh)))


def _rms(x, g):
    return x * lax.rsqrt(jnp.mean(x * x, axis=-1, keepdims=True) + EPS) * g


def _ada_kernel(c_ref, w_ref, b_ref, o_ref):
    c = c_ref[...]
    ca = c * (1.0 / (1.0 + jnp.exp(-c)))
    o_ref[0] = _dot_f32(ca, w_ref[0]) + b_ref[0]


def _ada(c, w_ada, b_ada):
    depth, d, n = w_ada.shape
    bsz = c.shape[0]
    rows = 8
    tn = 1536
    cp = jnp.zeros((rows, d), F32).at[:bsz].set(c)
    out = pl.pallas_call(
        _ada_kernel,
        grid=(depth, n // tn),
        in_specs=[pl.BlockSpec((rows, d), lambda l, j: (0, 0)),
                  pl.BlockSpec((1, d, tn), lambda l, j: (l, 0, j)),
                  pl.BlockSpec((1, 1, tn), lambda l, j: (l, 0, j))],
        out_specs=pl.BlockSpec((1, rows, tn), lambda l, j: (l, 0, j)),
        out_shape=jax.ShapeDtypeStruct((depth, rows, n), F32),
        compiler_params=_cparams("arbitrary", "arbitrary"),
        name="ada_mod",
    )(cp, w_ada, b_ada.reshape(depth, 1, n))
    return out[:, :bsz]


def _in_kernel(x_ref, g_ref, sc_ref, sh_ref, w_ref, o_ref, *, cn):
    h = _rms(x_ref[0], g_ref[...]) * (1.0 + sc_ref[0]) + sh_ref[0]
    hb = h.astype(BF16)
    for j in range(w_ref.shape[1] // cn):
        o_ref[0, :, j * cn:(j + 1) * cn] = _dot(hb, w_ref[:, j * cn:(j + 1) * cn]).astype(BF16)


def _in_proj(x, g, scale, shift, w):
    bsz, s, d = x.shape
    nc = w.shape[1]
    tm = min(512, s)
    cn = nc // 3 if (nc // 3) % LANES == 0 else nc
    return pl.pallas_call(
        functools.partial(_in_kernel, cn=cn),
        grid=(bsz, s // tm),
        in_specs=[pl.BlockSpec((1, tm, d), lambda b, i: (b, i, 0)),
                  pl.BlockSpec((1, d), lambda b, i: (0, 0)),
                  pl.BlockSpec((1, 1, d), lambda b, i: (b, 0, 0)),
                  pl.BlockSpec((1, 1, d), lambda b, i: (b, 0, 0)),
                  pl.BlockSpec((d, nc), lambda b, i: (0, 0))],
        out_specs=pl.BlockSpec((1, tm, nc), lambda b, i: (b, i, 0)),
        out_shape=jax.ShapeDtypeStruct((bsz, s, nc), BF16),
        compiler_params=_cparams("arbitrary", "arbitrary"),
        name="in_proj",
    )(x, g.reshape(1, d), scale.reshape(bsz, 1, d), shift.reshape(bsz, 1, d), w)


CB_SBQ, CB_SBK, CB_SBV, CB_NSAQ = 0, 4, 8, 12
CB_KCMP, CB_VCMP = 16, 17
CB_KSLC, CB_VSLC, CB_KWIN, CB_VWIN = 18, 20, 22, 24
CB_GATE = 26
N_CB = 27


def _arrange_w_in(w_in):
    d = w_in.shape[0]
    w_sb = N_SB * DH
    off_nsa_q = 3 * w_sb
    off_kv = off_nsa_q + N_NSA * DH
    off_gate = off_kv + 3 * 2 * G_NSA * DH
    scale = DH ** -0.5

    def kv(br, kvi, g):
        lo = off_kv + ((br * 2 + kvi) * G_NSA + g) * DH
        return w_in[:, lo:lo + DH]

    cols = [w_in[:, 0:w_sb] * scale, w_in[:, w_sb:2 * w_sb], w_in[:, 2 * w_sb:3 * w_sb],
            w_in[:, off_nsa_q:off_kv] * scale,
            kv(0, 0, 0), kv(0, 0, 1), kv(0, 1, 0), kv(0, 1, 1)]
    for br in (1, 2):
        for kvi in (0, 1):
            for g in range(G_NSA):
                cols += [kv(br, kvi, g), kv(br, kvi, g)]
    n_gate = 3 * N_NSA
    cols += [w_in[:, off_gate:off_gate + n_gate], jnp.zeros((d, LANES - n_gate), w_in.dtype)]
    out = jnp.concatenate(cols, axis=1).astype(BF16)
    assert out.shape[1] == N_CB * LANES
    return out


def _sb_kernel(q_ref, k_ref, v_ref, o_ref, *, t):
    qi = pl.program_id(2)
    q = q_ref[0]
    lane = lax.broadcasted_iota(I32, (t, LANES), 1)
    row = lax.broadcasted_iota(I32, (t, t), 0)
    col = lax.broadcasted_iota(I32, (t, t), 1)
    upper = jnp.where(row > col, 1.0, 0.0).astype(BF16)
    upper2 = jnp.concatenate([upper, upper], axis=0)
    ones2 = jnp.ones((2 * t, LANES), BF16)
    causal = jnp.concatenate([col < row, col < row], axis=0)
    rep = t // LANES

    def logits(q2, k, diag):
        z = _nt(q2, k)
        lk = -(jnp.maximum(z, 0.0) + jnp.log(1.0 + jnp.exp(-jnp.abs(z))))
        if diag:
            lk = jnp.where(causal, lk, 0.0)
        hi = lk.astype(BF16)
        lo = (lk - hi.astype(F32)).astype(BF16)
        return z + lk, jnp.concatenate([hi, lo], axis=1)

    def weigh(ls, later, tot, v, carry, acc, diag):
        cb = carry if rep == 1 else jnp.concatenate([carry] * rep, axis=1)
        w = jnp.exp(ls + later + cb)
        if diag:
            w = jnp.where(causal, w, 0.0)
        r = _dot(w.astype(BF16), v)
        return carry + tot, acc + jnp.where(lane < DH, r[:t], r[t:])

    n_blk = q.shape[1] // LANES
    zq = jnp.zeros((t, LANES), BF16)
    q2s = []
    for c in range(n_blk):
        qc = q[:, c * LANES:(c + 1) * LANES]
        q2s.append(jnp.concatenate([jnp.where(lane < DH, qc, zq), jnp.where(lane < DH, zq, qc)], axis=0))

    def step(kt, st, diag):
        k = k_ref[0, pl.ds(kt * t, t), :]
        v = v_ref[0, pl.ds(kt * t, t), :]
        sls = [slice(c * LANES, (c + 1) * LANES) for c in range(n_blk)]
        parts = [logits(q2s[c], k[:, sls[c]], diag) for c in range(n_blk)]
        hl = jnp.concatenate([p[1] for p in parts], axis=0)
        later = _dot(hl, upper2)
        tot = _dot(hl, ones2)
        out, mx = [], None
        for c in range(n_blk):
            rows = slice(c * 2 * t, (c + 1) * 2 * t)
            carry, acc = weigh(parts[c][0], later[rows], tot[rows], v[:, sls[c]],
                               st[2 * c], st[2 * c + 1], diag)
            out += [carry, acc]
            mx = jnp.max(carry) if mx is None else jnp.maximum(mx, jnp.max(carry))
        return mx, tuple(out)

    st0 = (jnp.zeros((2 * t, LANES), F32), jnp.zeros((t, LANES), F32)) * n_blk
    mx, st = step(qi, st0, True)

    def cond(s):
        return jnp.logical_and(s[0] >= 0, s[1] > SB_EXIT)

    def body(s):
        mx, st = step(s[0], s[2], False)
        return s[0] - 1, mx, st

    _, _, st = lax.while_loop(cond, body, (qi - 1, mx, st))
    o_ref[0] = jnp.concatenate([st[2 * c + 1] for c in range(n_blk)], axis=1)


def _sb_attention(proj, t=256, pairs=4):
    bsz, s, _ = proj.shape
    t = min(t, s)
    npair = N_SB // 2
    w = pairs * LANES
    return pl.pallas_call(
        functools.partial(_sb_kernel, t=t),
        grid=(bsz, npair // pairs, s // t),
        in_specs=[pl.BlockSpec((1, t, w), lambda b, j, i: (b, i, CB_SBQ // pairs + j)),
                  pl.BlockSpec((1, s, w), lambda b, j, i: (b, 0, CB_SBK // pairs + j)),
                  pl.BlockSpec((1, s, w), lambda b, j, i: (b, 0, CB_SBV // pairs + j))],
        out_specs=pl.BlockSpec((1, t, w), lambda b, j, i: (b, i, j)),
        out_shape=jax.ShapeDtypeStruct((bsz, s, N_SB * DH), F32),
        compiler_params=_cparams("arbitrary", "arbitrary", "arbitrary"),
        name="sb_attn",
    )(proj, proj, proj)


def _gelu_tanh(x):
    return 0.5 * x * (1.0 + jnp.tanh(math.sqrt(2.0 / math.pi) * (x + 0.044715 * (x * x * x))))


def _cmp_kernel(xa_ref, xb_ref, pos_ref, w1_ref, w2_ref, o_ref):
    half = w1_ref.shape[1] // 2
    w1a = w1_ref[0, :half, :]
    w1b = w1_ref[0, half:, :]
    pos = pos_ref[0]
    bias = _dot(pos[:, :half], w1a) + _dot(pos[:, half:], w1b)
    hid = _dot(xa_ref[0, 0], w1a) + _dot(xb_ref[0, 0], w1b) + bias[0:1, :]
    o_ref[0, 0] = _dot(_gelu_tanh(hid).astype(BF16), w2_ref[0])


def _compress(proj, pos_k, w1_k, w2_k, pos_v, w1_v, w2_v):
    bsz, s, _ = proj.shape
    n16 = s // CMP_STRIDE
    raw = proj[:, :, CB_KCMP * LANES:(CB_VCMP + 1) * LANES]
    x16 = raw.reshape(bsz, s, 4, DH).transpose(0, 2, 1, 3).reshape(bsz, 4, n16, CMP_STRIDE * DH)
    x16b = jnp.concatenate([x16[:, :, 1:], jnp.zeros_like(x16[:, :, :1])], axis=2)
    hidden = w1_k.shape[1]
    w1 = jnp.stack([w1_k, w1_v]).astype(BF16)
    w2 = jnp.stack([jnp.concatenate([w2_k, w2_k], 1), jnp.concatenate([w2_v, w2_v], 1)]).astype(BF16)
    pos = jnp.stack([pos_k.reshape(1, -1), pos_v.reshape(1, -1)])
    pos = jnp.concatenate([pos, jnp.zeros((2, 7, pos.shape[2]), pos.dtype)], axis=1).astype(BF16)
    k16 = CMP_STRIDE * DH
    return pl.pallas_call(
        _cmp_kernel,
        grid=(bsz, 4),
        in_specs=[pl.BlockSpec((1, 1, n16, k16), lambda b, j: (b, j, 0, 0)),
                  pl.BlockSpec((1, 1, n16, k16), lambda b, j: (b, j, 0, 0)),
                  pl.BlockSpec((1, 8, 2 * k16), lambda b, j: (j // 2, 0, 0)),
                  pl.BlockSpec((1, 2 * k16, hidden), lambda b, j: (j // 2, 0, 0)),
                  pl.BlockSpec((1, hidden, LANES), lambda b, j: (j // 2, 0, 0))],
        out_specs=pl.BlockSpec((1, 1, n16, LANES), lambda b, j: (b, j, 0, 0)),
        out_shape=jax.ShapeDtypeStruct((bsz, 4, n16, LANES), F32),
        compiler_params=_cparams("arbitrary", "arbitrary"),
        name="nsa_compress",
    )(x16, x16b, pos, w1, w2)


def _bucket_table():
    n = np.arange(REL_MAX_DIST + 1)
    exact = REL_BUCKETS // 2
    val = (np.log(np.maximum(n, 1).astype(np.float32) / np.float32(exact)).astype(np.float32)
           / np.float32(math.log(REL_MAX_DIST / exact)) * np.float32(REL_BUCKETS - exact))
    large = np.minimum(exact + val.astype(np.int32), REL_BUCKETS - 1)
    return np.where(n < exact, n, large).astype(np.int32)


def _bias_tiles(rel_table):
    bucket = _bucket_table()
    far_row = rel_table[int(bucket[REL_MAX_DIST])].astype(F32)
    r = np.arange(TQ)[:, None]

    def tile(dist, valid, base):
        onehot = jax.nn.one_hot(bucket[np.clip(dist, 0, REL_MAX_DIST)], REL_BUCKETS, dtype=F32)
        t = jnp.einsum('rwb,bh->hrw', onehot, rel_table.astype(F32),
                       precision=lax.Precision.HIGHEST)
        t = jnp.where(valid[None], t - base, NEG)
        return t.reshape(G_NSA, Z_NSA * TQ, dist.shape[1])

    w = np.arange(LANES)[None, :]
    d_c = r - CMP_STRIDE * (w - KC_FRONT) - (CMP_BLOCK - 1)
    w2 = np.arange(NEAR_BACK + TQ)[None, :]
    d_s = r + NEAR_BACK - w2
    w5 = np.arange(WINDOW + TQ)[None, :]
    d_w = r + WINDOW - w5
    far = jnp.broadcast_to(far_row[:, None, None], (N_NSA, TQ, LANES))
    far = far.reshape(G_NSA, Z_NSA * TQ, LANES)
    return (tile(d_c, d_c >= 0, far_row[:, None, None]), tile(d_s, d_s >= 0, 0.0),
            tile(d_w, (d_w >= 0) & (d_w < WINDOW), 0.0), far)


def _gate_expand():
    e = np.zeros((G_NSA, LANES, 3 * Z_NSA * DH), np.float32)
    for g in range(G_NSA):
        for z in range(Z_NSA):
            for br in range(3):
                e[g, (g * Z_NSA + z) * 3 + br, br * Z_NSA * DH + z * DH: br * Z_NSA * DH + (z + 1) * DH] = 1.0
    return e


def _overlap_padded(n16, kcp, n_slc):
    i = np.arange(kcp)[:, None] - KC_FRONT
    j = np.arange(LANES)[None, :]
    n_cmp = n16 - 1
    ok = (i >= 0) & (i < n_cmp) & (j < n_slc)
    ov = (i * CMP_STRIDE < j * SLC_BLOCK + SLC_BLOCK) & (i * CMP_STRIDE + CMP_BLOCK > j * SLC_BLOCK)
    return (ok & ov).astype(np.float32)


def _block_onehot(s):
    return (np.arange(s)[:, None] // SLC_BLOCK == np.arange(LANES)[None, :]).astype(np.float32)


def _nsa_kernel(q_ref, kcb_ref, vcb_ref, kcf_ref, vcf_ref, ks_ref, vs_ref, kw_ref, vw_ref, et_ref,
                gate_ref, ovb_ref, ovf_ref, bcn_ref, bsn_ref, bw_ref, cf_ref, eg_ref, o_ref,
                sb0_ref, sb1_ref, *, n_slc, n_round):
    blk = pl.program_id(2)
    c0 = blk * TQ
    r4 = Z_NSA * TQ
    q = q_ref[0]
    lane = lax.broadcasted_iota(I32, (TQ, LANES), 1)
    lo_half = lane < DH
    zq = jnp.zeros((TQ, LANES), BF16)
    qa, qb = q[:, :LANES], q[:, LANES:]
    qs = jnp.concatenate([jnp.where(lo_half, qa, zq), jnp.where(lo_half, zq, qa),
                          jnp.where(lo_half, qb, zq), jnp.where(lo_half, zq, qb)], axis=0)
    ones_k = jnp.ones((SEL_FAR_TILE, LANES), BF16)
    ones_w = jnp.ones((WINDOW + TQ, LANES), BF16)

    kcp = kcb_ref.shape[2]
    prow = lax.broadcasted_iota(I32, (1, kcp), 1)
    near0 = pl.multiple_of(blk * (TQ // CMP_STRIDE), 8)
    far_row = jnp.where((prow >= KC_FRONT) & (prow < near0), 0.0, NEG)
    lane1 = lax.broadcasted_iota(I32, (1, LANES), 1)
    near_row = jnp.where(lane1 + near0 >= KC_FRONT, 0.0, NEG)
    kcn = kcf_ref[0, 0, pl.ds(near0, LANES), :].astype(BF16)
    vcn = vcf_ref[0, 0, pl.ds(near0, LANES), :].astype(BF16)
    s_far = _nt(qs, kcb_ref[0, 0]) + far_row
    s_near = _nt(qs, kcn) + bcn_ref[0] + near_row
    m_c = jnp.maximum(jnp.max(s_far, axis=1, keepdims=True), jnp.max(s_near, axis=1, keepdims=True))
    m_c = jnp.maximum(m_c, M_INIT)
    p_far = jnp.exp(s_far - m_c)
    p_near = jnp.exp(s_near - m_c)
    den = jnp.sum(p_far, axis=1, keepdims=True) + jnp.sum(p_near, axis=1, keepdims=True)
    inv_c = 1.0 / jnp.maximum(den, 1e-30)
    o_c = (_dot(p_far.astype(BF16), vcb_ref[0, 0]) + _dot(p_near.astype(BF16), vcn)) * inv_c

    pn_far = p_far * inv_c
    pn_near = p_near * inv_c
    pz_far = pn_far[0:TQ] + pn_far[TQ:2 * TQ] + pn_far[2 * TQ:3 * TQ] + pn_far[3 * TQ:]
    pz_near = pn_near[0:TQ] + pn_near[TQ:2 * TQ] + pn_near[2 * TQ:3 * TQ] + pn_near[3 * TQ:]
    ovn = ovf_ref[pl.ds(near0, LANES), :].astype(BF16)
    imp = _dot_hl(pz_far, ovb_ref[...]) + _dot_hl(pz_near, ovn)
    tpos = c0 + lax.broadcasted_iota(I32, (TQ, LANES), 0)
    cur = lax.shift_right_logical(tpos, 6)
    valid = (lane * SLC_BLOCK <= tpos) & (lane < n_slc)
    forced = valid & ((lane == 0) | (lane == cur) | (lane == cur - 1))
    score = jnp.where(valid & jnp.logical_not(forced), imp, -1.0)
    score = jnp.where(lane < n_slc, score, -2.0)

    kws, vws = [], []
    for w in range((WINDOW + TQ) // LANES):
        st = pl.multiple_of(jnp.maximum(c0 - WINDOW + w * LANES, 0), LANES)
        kws.append(kw_ref[0, pl.ds(st, LANES), :])
        vws.append(vw_ref[0, pl.ds(st, LANES), :])
    wl = lax.broadcasted_iota(I32, (1, WINDOW + TQ), 1)
    pos_row = jnp.where(c0 - WINDOW + wl >= 0, 0.0, NEG)
    s = _nt(qs, jnp.concatenate(kws, axis=0)) + bw_ref[0] + pos_row
    p = jnp.exp(s - jnp.max(s, axis=1, keepdims=True)).astype(BF16)
    r_w = _dot(p, jnp.concatenate([jnp.concatenate(vws, axis=0), ones_w], axis=1))
    o_w = r_w[:, :LANES] / r_w[:, LANES:]
    gl = _dot(gate_ref[0], eg_ref[0])
    sig = 1.0 / (1.0 + jnp.exp(-gl))
    wide = Z_NSA * DH

    ones_b = jnp.ones((LANES, LANES), BF16)

    def count(mask):
        return _dot(jnp.where(mask, 1.0, 0.0).astype(BF16), ones_b)

    rest, taken, thr = score, jnp.zeros((TQ, LANES), F32), jnp.zeros((TQ, LANES), F32)
    for _ in range(n_round):
        m = jnp.max(rest, axis=1, keepdims=True)
        eq = rest == m
        thr = jnp.where(taken < n_round, m, thr)
        taken = taken + count(eq)
        rest = jnp.where(eq, -jnp.inf, rest)
    above = score > thr
    at_thr = score == thr
    need = n_round - count(above)
    brow = lax.broadcasted_iota(I32, (LANES, LANES), 0)
    bcol = lax.broadcasted_iota(I32, (LANES, LANES), 1)
    before = _dot(jnp.where(at_thr, 1.0, 0.0).astype(BF16),
                  jnp.where(brow < bcol, 1.0, 0.0).astype(BF16))
    sel = ((above | (at_thr & (before < need))) & valid) | forced


    sel_all = jnp.where(sel, 0.0, NEG).astype(BF16)
    far_blocks = blk * (TQ // SLC_BLOCK) - NEAR_BACK // SLC_BLOCK
    sel_far = jnp.where(sel & (lane < far_blocks), 0.0, NEG).astype(BF16)
    q_far = jnp.concatenate([qs, jnp.concatenate([sel_far] * Z_NSA, axis=0)], axis=1)
    q_near = jnp.concatenate([qs, jnp.concatenate([sel_all] * Z_NSA, axis=0)], axis=1)
    tk = SEL_FAR_TILE
    last_tile = ks_ref.shape[1] // tk - 1

    def qk(kt):
        k0 = pl.multiple_of(kt * tk, tk)
        return _nt(q_far, jnp.concatenate([ks_ref[0, pl.ds(k0, tk), :], et_ref[pl.ds(k0, tk), :]], axis=1))

    def fold(s, v, st):
        m, acc = st
        mn = jnp.maximum(m, jnp.max(s, axis=1, keepdims=True))
        p = jnp.exp(s - mn).astype(BF16)
        return mn, jnp.exp(m - mn) * acc + _dot(p, v)

    def v_tile(kt):
        k0 = pl.multiple_of(kt * tk, tk)
        return jnp.concatenate([vs_ref[0, pl.ds(k0, tk), :], ones_k], axis=1)

    n_pair = lax.shift_right_logical(jnp.maximum(c0 - NEAR_BACK, 0) + (2 * tk - 1), int(math.log2(2 * tk)))
    sb0_ref[...] = qk(0)

    def far_step(i, st):
        sb1_ref[...] = qk(2 * i + 1)
        st = fold(sb0_ref[...], v_tile(2 * i), st)
        sb0_ref[...] = qk(jnp.minimum(2 * i + 2, last_tile))
        return fold(sb1_ref[...], v_tile(2 * i + 1), st)

    def far_step2(i, st):
        return far_step(2 * i + 1, far_step(2 * i, st))

    m0 = jnp.full((r4, 1), M_INIT, F32)
    n_quad = lax.shift_right_logical(n_pair, 1)
    st = lax.fori_loop(0, n_quad, far_step2, (m0, jnp.zeros((r4, 2 * LANES), F32)))
    m_s, acc_s = lax.fori_loop(2 * n_quad, n_pair, far_step, st)
    m_s = m_s + cf_ref[0][:, 0:1]
    n_near = NEAR_BACK + TQ
    p0 = pl.multiple_of(jnp.maximum(c0 - NEAR_BACK, 0), NEAR_BACK)
    d0 = pl.multiple_of(c0, TQ)
    kn = jnp.concatenate([ks_ref[0, pl.ds(p0, NEAR_BACK), :], ks_ref[0, pl.ds(d0, TQ), :]], axis=0)
    vn = jnp.concatenate([vs_ref[0, pl.ds(p0, NEAR_BACK), :], vs_ref[0, pl.ds(d0, TQ), :]], axis=0)
    nrow = lax.broadcasted_iota(I32, (n_near, LANES), 0)
    nlane = lax.broadcasted_iota(I32, (n_near, LANES), 1)
    e_n = jnp.where(lax.shift_right_logical(nrow, 6) + far_blocks == nlane, 1.0, 0.0).astype(BF16)
    lane2 = lax.broadcasted_iota(I32, (1, n_near), 1)
    prev_row = jnp.where((lane2 < NEAR_BACK) & (blk == 0), NEG, 0.0)
    s = _nt(q_near, jnp.concatenate([kn, e_n], axis=1)) + bsn_ref[0] + prev_row
    _, acc_s = fold(s, jnp.concatenate([vn, ones_k[:n_near]], axis=1), (m_s, acc_s))
    o_s = acc_s[:, :LANES] / acc_s[:, LANES:]

    def heads(o):
        return jnp.concatenate([jnp.where(lo_half, o[0:TQ], o[TQ:2 * TQ]),
                                jnp.where(lo_half, o[2 * TQ:3 * TQ], o[3 * TQ:])], axis=1)

    o_ref[0] = (sig[:, 0:wide] * heads(o_c) + sig[:, wide:2 * wide] * heads(o_s)
                + sig[:, 2 * wide:] * heads(o_w))


def _nsa_attention(proj, cmp_out, rel_table):
    bsz, s, _ = proj.shape
    assert s % (2 * SEL_FAR_TILE) == 0 and s >= WINDOW + TQ
    n16 = s // CMP_STRIDE
    n_slc = s // SLC_BLOCK
    assert N_SELECT <= n_slc <= LANES
    nq = s // TQ
    kcp = -(-(max(n16 + KC_FRONT, (TQ // CMP_STRIDE) * (nq - 1) + LANES)) // LANES) * LANES
    pad = ((0, 0), (0, 0), (KC_FRONT, kcp - n16 - KC_FRONT), (0, 0))
    cf = jnp.pad(cmp_out, pad)
    cb = cf.astype(BF16)
    ov = _overlap_padded(n16, kcp, n_slc)
    bcn, bsn, bw, far = _bias_tiles(rel_table)
    eg = jnp.asarray(_gate_expand(), BF16)
    r4 = Z_NSA * TQ
    kv_spec = lambda cb0: pl.BlockSpec((1, s, LANES), lambda b, g, i, cb0=cb0: (b, 0, cb0 + g))
    cmp_spec = lambda j0: pl.BlockSpec((1, 1, kcp, LANES), lambda b, g, i, j0=j0: (b, j0 + g, 0, 0))
    tile_spec = lambda w: pl.BlockSpec((1, r4, w), lambda b, g, i: (g, 0, 0))
    const2 = lambda n: pl.BlockSpec((n, LANES), lambda b, g, i: (0, 0))
    return pl.pallas_call(
        functools.partial(_nsa_kernel, n_slc=n_slc, n_round=N_SELECT - 3),
        grid=(bsz, G_NSA, nq),
        in_specs=[pl.BlockSpec((1, TQ, 2 * LANES), lambda b, g, i: (b, i, CB_NSAQ // 2 + g)),
                  cmp_spec(0), cmp_spec(2), cmp_spec(0), cmp_spec(2),
                  kv_spec(CB_KSLC), kv_spec(CB_VSLC), kv_spec(CB_KWIN), kv_spec(CB_VWIN),
                  const2(s),
                  pl.BlockSpec((1, TQ, LANES), lambda b, g, i: (b, i, CB_GATE)),
                  const2(kcp), const2(kcp),
                  tile_spec(LANES), tile_spec(NEAR_BACK + TQ), tile_spec(WINDOW + TQ), tile_spec(LANES),
                  pl.BlockSpec((1, LANES, 3 * Z_NSA * DH), lambda b, g, i: (g, 0, 0))],
        out_specs=pl.BlockSpec((1, TQ, Z_NSA * DH), lambda b, g, i: (b, i, g)),
        out_shape=jax.ShapeDtypeStruct((bsz, s, N_NSA * DH), F32),
        scratch_shapes=[pltpu.VMEM((r4, SEL_FAR_TILE), F32), pltpu.VMEM((r4, SEL_FAR_TILE), F32)],
        compiler_params=_cparams("arbitrary", "arbitrary", "arbitrary"),
        name="nsa_attn",
    )(proj, cb, cb, cf, cf, proj, proj, proj, proj, jnp.asarray(_block_onehot(s), BF16), proj,
      jnp.asarray(ov, BF16), jnp.asarray(ov, F32), bcn, bsn, bw, far, eg)


def _out_kernel(osb_ref, onsa_ref, x_ref, gsb_ref, gnsa_ref, w_ref, gpost_ref, gate_ref,
                gpre_ref, sc_ref, sh_ref, xo_ref, h_ref):
    half = osb_ref.shape[2]
    a = _rms(osb_ref[0], gsb_ref[...]).astype(BF16)
    b = _rms(onsa_ref[0], gnsa_ref[...]).astype(BF16)
    m = _dot(a, w_ref[:half, :]) + _dot(b, w_ref[half:, :])
    x = x_ref[0] + gate_ref[0] * _rms(m, gpost_ref[...])
    xo_ref[0] = x
    h_ref[0] = (_rms(x, gpre_ref[...]) * (1.0 + sc_ref[0]) + sh_ref[0]).astype(BF16)


def _out_proj(o_sb, o_nsa, x, g_sb, g_nsa, w_out, g_post, gate_m, g_pre_ffn, scale_f, shift_f):
    bsz, s, d = x.shape
    half = o_sb.shape[2]
    tm = min(512, s)
    row = lambda n: pl.BlockSpec((1, n), lambda b, i: (0, 0))
    mod = pl.BlockSpec((1, 1, d), lambda b, i: (b, 0, 0))
    act = lambda n: pl.BlockSpec((1, tm, n), lambda b, i: (b, i, 0))
    return pl.pallas_call(
        _out_kernel,
        grid=(bsz, s // tm),
        in_specs=[act(half), act(half), act(d), row(half), row(half),
                  pl.BlockSpec((2 * half, d), lambda b, i: (0, 0)), row(d), mod, row(d), mod, mod],
        out_specs=[act(d), act(d)],
        out_shape=[jax.ShapeDtypeStruct((bsz, s, d), F32), jax.ShapeDtypeStruct((bsz, s, d), BF16)],
        compiler_params=_cparams("arbitrary", "arbitrary"),
        name="out_proj",
    )(o_sb, o_nsa, x, g_sb.reshape(1, half), g_nsa.reshape(1, half), w_out.astype(BF16),
      g_post.reshape(1, d), gate_m.reshape(bsz, 1, d), g_pre_ffn.reshape(1, d),
      scale_f.reshape(bsz, 1, d), shift_f.reshape(bsz, 1, d))


def _silu(x):
    return x * (1.0 / (1.0 + jnp.exp(-x)))


def _ffn_kernel(h_ref, x_ref, wg_ref, wu_ref, wd_ref, gpost_ref, gate_ref, o_ref):
    h = h_ref[0]
    a = (_silu(_dot(h, wg_ref[...])) * _dot(h, wu_ref[...])).astype(BF16)
    o_ref[0] = x_ref[0] + gate_ref[0] * _rms(_dot(a, wd_ref[...]), gpost_ref[...])


def _dense_ffn(h, x, w_gate, w_up, w_down, g_post, gate_f):
    bsz, s, d = x.shape
    ff = w_gate.shape[1]
    tm = min(512, s)
    act = pl.BlockSpec((1, tm, d), lambda b, i: (b, i, 0))
    once = pl.Buffered(1)
    return pl.pallas_call(
        _ffn_kernel,
        grid=(bsz, s // tm),
        in_specs=[act, act,
                  pl.BlockSpec((d, ff), lambda b, i: (0, 0), pipeline_mode=once),
                  pl.BlockSpec((d, ff), lambda b, i: (0, 0), pipeline_mode=once),
                  pl.BlockSpec((ff, d), lambda b, i: (0, 0), pipeline_mode=once),
                  pl.BlockSpec((1, d), lambda b, i: (0, 0)),
                  pl.BlockSpec((1, 1, d), lambda b, i: (b, 0, 0))],
        out_specs=act,
        out_shape=jax.ShapeDtypeStruct((bsz, s, d), F32),
        compiler_params=_cparams("arbitrary", "arbitrary"),
        name="dense_ffn",
    )(h, x, w_gate.astype(BF16), w_up.astype(BF16), w_down.astype(BF16),
      g_post.reshape(1, d), gate_f.reshape(bsz, 1, d))


MOE_RC = 128
MOE_GC = 256
MOE_SC = 1024


def _moe_rows(tile):
    return -(-(2 * tile + N_EXPERTS * MOE_RC) // MOE_SC) * MOE_SC


def _dispatch_kernel(x_ref, gpre_ref, sc_ref, sh_ref, wr_ref, br_ref,
                     xg_ref, wrow_ref, d_ref, off_ref, nch_ref):
    tile = x_ref.shape[1]
    rows = xg_ref.shape[1]
    h = _rms(x_ref[0], gpre_ref[...]) * (1.0 + sc_ref[0]) + sh_ref[0]
    logits = _dot_f32(h, wr_ref[...]) + br_ref[...]
    lane = lax.broadcasted_iota(I32, (tile, LANES), 1)
    lanef = lane.astype(F32)
    e = jnp.exp(logits - jnp.max(logits, axis=1, keepdims=True))
    probs = e / jnp.sum(e, axis=1, keepdims=True)
    probs = jnp.where(lane < N_EXPERTS, probs, -1.0)
    m1 = jnp.max(probs, axis=1, keepdims=True)
    i1 = jnp.min(jnp.where(probs == m1, lanef, 1e9), axis=1, keepdims=True)
    rest = jnp.where(lanef == i1, -1.0, probs)
    m2 = jnp.max(rest, axis=1, keepdims=True)
    i2 = jnp.min(jnp.where(rest == m2, lanef, 1e9), axis=1, keepdims=True)
    tot = m1 + m2
    hit1 = lanef == i1
    hit2 = lanef == i2
    mask = hit1 | hit2
    ch = 256
    r = lax.broadcasted_iota(I32, (ch, ch), 0)
    c = lax.broadcasted_iota(I32, (ch, ch), 1)
    lower = jnp.where(c < r, 1.0, 0.0).astype(BF16)
    ones = jnp.ones((8, ch), BF16)
    count = jnp.zeros((1, LANES), F32)
    ranks = []
    for k in range(tile // ch):
        mk = jnp.where(mask[k * ch:(k + 1) * ch], 1.0, 0.0).astype(BF16)
        ranks.append(_dot(lower, mk) + count)
        count = count + _dot(ones, mk)[0:1]
    rank = jnp.concatenate(ranks, axis=0)
    shift = int(math.log2(MOE_RC))
    nch = lax.shift_right_logical(count.astype(I32) + (MOE_RC - 1), shift)
    nch8 = jnp.broadcast_to(nch.astype(F32), (8, LANES))
    lr = lax.broadcasted_iota(I32, (LANES, LANES), 0)
    lc = lax.broadcasted_iota(I32, (LANES, LANES), 1)
    offc = _dot(nch8.astype(BF16), jnp.where(lr < lc, 1.0, 0.0).astype(BF16))
    row = offc[0:1] * float(MOE_RC) + rank
    d1 = jnp.sum(jnp.where(hit1, row, 0.0), axis=1, keepdims=True)
    d2 = jnp.sum(jnp.where(hit2, row, 0.0), axis=1, keepdims=True)
    dd = jnp.where(lane == 0, d1, jnp.where(lane == 1, d2, -1.0))
    d_ref[0] = dd.astype(I32)
    off_ref[0] = offc.astype(I32)
    nch_ref[0] = jnp.broadcast_to(nch, (8, LANES))

    ddt = jnp.transpose(dd)
    d1t = ddt[0:1].astype(I32)
    d2t = ddt[1:2].astype(I32)

    wt = jnp.transpose(jnp.where(lane == 0, m1 / tot, jnp.where(lane == 1, m2 / tot, 0.0)))
    w1t = wt[0:1]
    w2t = wt[1:2]
    hb = h.astype(BF16)
    rid = lax.broadcasted_iota(I32, (MOE_GC, tile), 0)
    for k in range(rows // MOE_GC):
        hit1 = rid + k * MOE_GC == d1t
        hit2 = rid + k * MOE_GC == d2t
        onehot = jnp.where(hit1 | hit2, 1.0, 0.0).astype(BF16)
        xg_ref[0, k * MOE_GC:(k + 1) * MOE_GC, :] = _dot(onehot, hb).astype(BF16)
        wrow = jnp.sum(jnp.where(hit1, w1t, jnp.where(hit2, w2t, 0.0)), axis=1, keepdims=True)
        wrow_ref[0, k * MOE_GC:(k + 1) * MOE_GC, :] = jnp.broadcast_to(wrow, (MOE_GC, LANES))


def _dispatch(x, g_pre, scale_f, shift_f, w_router, b_router, tile):
    bsz, s, d = x.shape
    nt = s // tile
    rows = _moe_rows(tile)
    wr = jnp.zeros((d, LANES), F32).at[:, :N_EXPERTS].set(w_router.astype(F32))
    br = jnp.full((1, LANES), NEG, F32).at[0, :N_EXPERTS].set(b_router.astype(F32))
    mod = pl.BlockSpec((1, 1, d), lambda b, i: (b, 0, 0))
    per_tile = lambda r, n: pl.BlockSpec((1, r, n), lambda b, i: (b * nt + i, 0, 0))
    xg, wrow, dest, offc, nch = pl.pallas_call(
        _dispatch_kernel,
        grid=(bsz, nt),
        in_specs=[pl.BlockSpec((1, tile, d), lambda b, i: (b, i, 0)),
                  pl.BlockSpec((1, d), lambda b, i: (0, 0)), mod, mod,
                  pl.BlockSpec((d, LANES), lambda b, i: (0, 0)),
                  pl.BlockSpec((1, LANES), lambda b, i: (0, 0))],
        out_specs=[per_tile(rows, d), per_tile(rows, LANES), per_tile(tile, LANES),
                   per_tile(8, LANES), per_tile(8, LANES)],
        out_shape=[jax.ShapeDtypeStruct((bsz * nt, rows, d), BF16),
                   jax.ShapeDtypeStruct((bsz * nt, rows, LANES), F32),
                   jax.ShapeDtypeStruct((bsz * nt, tile, LANES), I32),
                   jax.ShapeDtypeStruct((bsz * nt, 8, LANES), I32),
                   jax.ShapeDtypeStruct((bsz * nt, 8, LANES), I32)],
        compiler_params=_cparams("arbitrary", "arbitrary"),
        name="moe_dispatch",
    )(x, g_pre.reshape(1, d), scale_f.reshape(bsz, 1, d), shift_f.reshape(bsz, 1, d), wr, br)
    return xg, wrow, dest, offc[:, 0, :N_EXPERTS], nch[:, 0, :N_EXPERTS]


def _slot_order(offc, nch, slots_per_tile):
    ends = offc + nch
    c = jnp.arange(slots_per_tile, dtype=I32)[None, :, None]
    expert = jnp.sum((c >= ends[:, None, :]).astype(I32), axis=-1).reshape(-1)
    n = expert.shape[0]
    order = jnp.argsort(expert * n + jnp.arange(n, dtype=I32)).astype(I32)
    exp_sorted = expert[order]
    used = (exp_sorted < N_EXPERTS).astype(I32)
    return order, jnp.minimum(exp_sorted, N_EXPERTS - 1).astype(I32), used


def _slots_kernel(slot_ref, exp_ref, used_ref, x_ref, wrow_ref, wg_ref, wu_ref, wd_ref, y_ref):
    p = pl.program_id(0)

    @pl.when(used_ref[p] == 1)
    def _():
        x = x_ref[0]
        a = (_silu(_dot(x, wg_ref[0])) * _dot(x, wu_ref[0])).astype(BF16)
        y_ref[0] = (_dot(a, wd_ref[0]) * wrow_ref[0][:, 0:1]).astype(BF16)

    @pl.when(used_ref[p] == 0)
    def _():
        y_ref[0] = jnp.zeros(y_ref.shape[1:], BF16)


def _expert_slots(xg, wrow, order, exp_sorted, used, w_gate, w_up, w_down):
    ntile, rows, d = xg.shape
    nslot = ntile * rows // MOE_RC
    ff = w_gate.shape[2]
    once = pl.Buffered(1)
    slot = lambda n: pl.BlockSpec((1, MOE_RC, n), lambda p, s, e, u: (s[p], 0, 0))
    grid_spec = pltpu.PrefetchScalarGridSpec(
        num_scalar_prefetch=3,
        grid=(nslot,),
        in_specs=[slot(d), slot(LANES),
                  pl.BlockSpec((1, d, ff), lambda p, s, e, u: (e[p], 0, 0), pipeline_mode=once),
                  pl.BlockSpec((1, d, ff), lambda p, s, e, u: (e[p], 0, 0), pipeline_mode=once),
                  pl.BlockSpec((1, ff, d), lambda p, s, e, u: (e[p], 0, 0), pipeline_mode=once)],
        out_specs=slot(d))
    y = pl.pallas_call(
        _slots_kernel,
        grid_spec=grid_spec,
        out_shape=jax.ShapeDtypeStruct((nslot, MOE_RC, d), BF16),
        input_output_aliases={3: 0},
        compiler_params=_cparams("arbitrary"),
        name="moe_slots",
    )(order, exp_sorted, used, xg.reshape(nslot, MOE_RC, d), wrow.reshape(nslot, MOE_RC, LANES),
      w_gate.astype(BF16), w_up.astype(BF16), w_down.astype(BF16))
    return y.reshape(ntile, rows, d)


def _combine_kernel(y_ref, d_ref, x_ref, gpost_ref, gate_ref, o_ref):
    tile = x_ref.shape[1]
    rows = y_ref.shape[1]
    dd = d_ref[0]
    d1c = dd[:, 0:1]
    d2c = dd[:, 1:2]
    cid = lax.broadcasted_iota(I32, (tile, MOE_SC), 1)
    z = None
    for k in range(rows // MOE_SC):
        pt = jnp.where((cid + k * MOE_SC == d1c) | (cid + k * MOE_SC == d2c), 1.0, 0.0).astype(BF16)
        part = _dot(pt, y_ref[0, k * MOE_SC:(k + 1) * MOE_SC, :])
        z = part if z is None else z + part
    o_ref[0] = x_ref[0] + gate_ref[0] * _rms(z, gpost_ref[...])


def _combine(y, dest, x, g_post, gate_f):
    bsz, s, d = x.shape
    ntile, rows, _ = y.shape
    tile = dest.shape[1]
    nt = s // tile
    per_tile = lambda r, n: pl.BlockSpec((1, r, n), lambda i: (i, 0, 0))
    out = pl.pallas_call(
        _combine_kernel,
        grid=(ntile,),
        in_specs=[per_tile(rows, d), per_tile(tile, LANES), per_tile(tile, d),
                  pl.BlockSpec((1, d), lambda i: (0, 0)),
                  pl.BlockSpec((1, 1, d), lambda i: (i // nt, 0, 0))],
        out_specs=per_tile(tile, d),
        out_shape=jax.ShapeDtypeStruct((ntile, tile, d), F32),
        compiler_params=_cparams("arbitrary"),
        name="moe_combine",
    )(y, dest, x.reshape(ntile, tile, d), g_post.reshape(1, d), gate_f.reshape(bsz, 1, d))
    return out.reshape(bsz, s, d)


def _mixer(x, mod, layer, rel_table, g_pre_mix, w_in, cmp_params):
    shift_m, scale_m = mod[:, 0], mod[:, 1]
    proj = _in_proj(x, g_pre_mix, scale_m, shift_m, _arrange_w_in(w_in))
    o_sb = _sb_attention(proj)
    o_nsa = _nsa_attention(proj, _compress(proj, *cmp_params), rel_table)
    return o_sb, o_nsa


def kernel(x, c, rel_table, w_ada, b_ada, g_pre_mix, g_post_mix, g_pre_ffn, g_post_ffn, w_in, w_out, g_sb, g_nsa, cmp_pos_k, cmp_w1_k, cmp_w2_k, cmp_pos_v, cmp_w1_v, cmp_w2_v, ffn_w_gate, ffn_w_up, ffn_w_down, moe_w_router, moe_b_router, moe_w_gate, moe_w_up, moe_w_down):
    bsz, s, d = x.shape
    depth = w_in.shape[0]
    mods = _ada(c, w_ada, b_ada).reshape(depth, bsz, 6, d)
    moe_tile = min(1024, s)
    for layer in range(depth):
        mod = mods[layer]
        cmp_params = (cmp_pos_k[layer], cmp_w1_k[layer], cmp_w2_k[layer],
                      cmp_pos_v[layer], cmp_w1_v[layer], cmp_w2_v[layer])
        o_sb, o_nsa = _mixer(x, mod, layer, rel_table, g_pre_mix[layer], w_in[layer], cmp_params)
        x, h = _out_proj(o_sb, o_nsa, x, g_sb[layer], g_nsa[layer], w_out[layer], g_post_mix[layer],
                         mod[:, 2], g_pre_ffn[layer], mod[:, 4], mod[:, 3])
        i = layer // 2
        if layer % 2 == 0:
            x = _dense_ffn(h, x, ffn_w_gate[i], ffn_w_up[i], ffn_w_down[i], g_post_ffn[layer], mod[:, 5])
        else:
            xg, wrow, dest, offc, nch = _dispatch(x, g_pre_ffn[layer], mod[:, 4], mod[:, 3],
                                                  moe_w_router[i], moe_b_router[i], moe_tile)
            order = _slot_order(offc, nch, xg.shape[1] // MOE_RC)
            y = _expert_slots(xg, wrow, *order, moe_w_gate[i], moe_w_up[i], moe_w_down[i])
            x = _combine(y, dest, x, g_post_ffn[layer], mod[:, 5])
    return x
```

```python
import functools
import math

import numpy as np
import jax
import jax.numpy as jnp
from jax import lax
from jax.experimental import pallas as pl
from jax.experimental.pallas import tpu as pltpu

F32 = jnp.float32
BF16 = jnp.bfloat16
I32 = jnp.int32

LANES = 128
DH = 64
N_SB = 8
N_NSA = 8
G_NSA = 2
Z_NSA = 4
CMP_BLOCK = 32
CMP_STRIDE = 16
SLC_BLOCK = 64
N_SELECT = 16
WINDOW = 512
REL_BUCKETS = 32
REL_MAX_DIST = 128
N_EXPERTS = 8
EPS = 1e-6
FORCED = 1e4
NEG = -1e30
M_INIT = -1e29
SB_EXIT = -104.5
VMEM_LIMIT = 56 * 1024 * 1024

TQ = 256
NEAR_BACK = REL_MAX_DIST
KC_FRONT = 16
SEL_FAR_TILE = 512


def _cparams(*sem):
    return pltpu.CompilerParams(dimension_semantics=sem, vmem_limit_bytes=VMEM_LIMIT)


def _nt(a, b):
    return lax.dot_general(a, b, (((1,), (1,)), ((), ())), preferred_element_type=F32)


def _dot(a, b):
    return jnp.dot(a, b, preferred_element_type=F32)


def _split3(a):
    hi = a.astype(BF16)
    r = a - hi.astype(F32)
    mid = r.astype(BF16)
    lo = (r - mid.astype(F32)).astype(BF16)
    return hi, mid, lo


def _dot_hl(a, b):
    hi = a.astype(BF16)
    lo = (a - hi.astype(F32)).astype(BF16)
    return _dot(hi, b) + _dot(lo, b)


def _dot_f32(a, b):
    ah, am, al = _split3(a)
    bh, bm, bl = _split3(b)
    return (_dot(ah, bh) + (_dot(ah, bm) + _dot(am, bh))
            + (_dot(ah, bl) + _dot(am, bm) + _dot(al, bh)))


def _rms(x, g):
    return x * lax.rsqrt(jnp.mean(x * x, axis=-1, keepdims=True) + EPS) * g


def _ada_kernel(c_ref, w_ref, b_ref, o_ref):
    c = c_ref[...]
    ca = c * (1.0 / (1.0 + jnp.exp(-c)))
    o_ref[0] = _dot_f32(ca, w_ref[0]) + b_ref[0]


def _ada(c, w_ada, b_ada):
    depth, d, n = w_ada.shape
    bsz = c.shape[0]
    rows = 8
    tn = 1536
    cp = jnp.zeros((rows, d), F32).at[:bsz].set(c)
    out = pl.pallas_call(
        _ada_kernel,
        grid=(depth, n // tn),
        in_specs=[pl.BlockSpec((rows, d), lambda l, j: (0, 0)),
                  pl.BlockSpec((1, d, tn), lambda l, j: (l, 0, j)),
                  pl.BlockSpec((1, 1, tn), lambda l, j: (l, 0, j))],
        out_specs=pl.BlockSpec((1, rows, tn), lambda l, j: (l, 0, j)),
        out_shape=jax.ShapeDtypeStruct((depth, rows, n), F32),
        compiler_params=_cparams("arbitrary", "arbitrary"),
        name="ada_mod",
    )(cp, w_ada, b_ada.reshape(depth, 1, n))
    return out[:, :bsz]


def _in_kernel(x_ref, g_ref, sc_ref, sh_ref, w_ref, o_ref, *, cn):
    h = _rms(x_ref[0], g_ref[...]) * (1.0 + sc_ref[0]) + sh_ref[0]
    hb = h.astype(BF16)
    for j in range(w_ref.shape[1] // cn):
        o_ref[0, :, j * cn:(j + 1) * cn] = _dot(hb, w_ref[:, j * cn:(j + 1) * cn]).astype(BF16)


def _in_proj(x, g, scale, shift, w):
    bsz, s, d = x.shape
    nc = w.shape[1]
    tm = min(512, s)
    cn = nc // 3 if (nc // 3) % LANES == 0 else nc
    return pl.pallas_call(
        functools.partial(_in_kernel, cn=cn),
        grid=(bsz, s // tm),
        in_specs=[pl.BlockSpec((1, tm, d), lambda b, i: (b, i, 0)),
                  pl.BlockSpec((1, d), lambda b, i: (0, 0)),
                  pl.BlockSpec((1, 1, d), lambda b, i: (b, 0, 0)),
                  pl.BlockSpec((1, 1, d), lambda b, i: (b, 0, 0)),
                  pl.BlockSpec((d, nc), lambda b, i: (0, 0))],
        out_specs=pl.BlockSpec((1, tm, nc), lambda b, i: (b, i, 0)),
        out_shape=jax.ShapeDtypeStruct((bsz, s, nc), BF16),
        compiler_params=_cparams("arbitrary", "arbitrary"),
        name="in_proj",
    )(x, g.reshape(1, d), scale.reshape(bsz, 1, d), shift.reshape(bsz, 1, d), w)


CB_SBQ, CB_SBK, CB_SBV, CB_NSAQ = 0, 4, 8, 12
CB_KCMP, CB_VCMP = 16, 17
CB_KSLC, CB_VSLC, CB_KWIN, CB_VWIN = 18, 20, 22, 24
CB_GATE = 26
N_CB = 27


def _arrange_w_in(w_in):
    d = w_in.shape[0]
    w_sb = N_SB * DH
    off_nsa_q = 3 * w_sb
    off_kv = off_nsa_q + N_NSA * DH
    off_gate = off_kv + 3 * 2 * G_NSA * DH
    scale = DH ** -0.5

    def kv(br, kvi, g):
        lo = off_kv + ((br * 2 + kvi) * G_NSA + g) * DH
        return w_in[:, lo:lo + DH]

    cols = [w_in[:, 0:w_sb] * scale, w_in[:, w_sb:2 * w_sb], w_in[:, 2 * w_sb:3 * w_sb],
            w_in[:, off_nsa_q:off_kv] * scale,
            kv(0, 0, 0), kv(0, 0, 1), kv(0, 1, 0), kv(0, 1, 1)]
    for br in (1, 2):
        for kvi in (0, 1):
            for g in range(G_NSA):
                cols += [kv(br, kvi, g), kv(br, kvi, g)]
    n_gate = 3 * N_NSA
    cols += [w_in[:, off_gate:off_gate + n_gate], jnp.zeros((d, LANES - n_gate), w_in.dtype)]
    out = jnp.concatenate(cols, axis=1).astype(BF16)
    assert out.shape[1] == N_CB * LANES
    return out


def _sb_kernel(q_ref, k_ref, v_ref, o_ref, *, t):
    qi = pl.program_id(2)
    q = q_ref[0]
    lane = lax.broadcasted_iota(I32, (t, LANES), 1)
    row = lax.broadcasted_iota(I32, (t, t), 0)
    col = lax.broadcasted_iota(I32, (t, t), 1)
    upper = jnp.where(row > col, 1.0, 0.0).astype(BF16)
    upper2 = jnp.concatenate([upper, upper], axis=0)
    ones2 = jnp.ones((2 * t, LANES), BF16)
    causal = jnp.concatenate([col < row, col < row], axis=0)
    rep = t // LANES

    def logits(q2, k, diag):
        z = _nt(q2, k)
        lk = -(jnp.maximum(z, 0.0) + jnp.log(1.0 + jnp.exp(-jnp.abs(z))))
        if diag:
            lk = jnp.where(causal, lk, 0.0)
        hi = lk.astype(BF16)
        lo = (lk - hi.astype(F32)).astype(BF16)
        return z + lk, jnp.concatenate([hi, lo], axis=1)

    def weigh(ls, later, tot, v, carry, acc, diag):
        cb = carry if rep == 1 else jnp.concatenate([carry] * rep, axis=1)
        w = jnp.exp(ls + later + cb)
        if diag:
            w = jnp.where(causal, w, 0.0)
        r = _dot(w.astype(BF16), v)
        return carry + tot, acc + jnp.where(lane < DH, r[:t], r[t:])

    n_blk = q.shape[1] // LANES
    zq = jnp.zeros((t, LANES), BF16)
    q2s = []
    for c in range(n_blk):
        qc = q[:, c * LANES:(c + 1) * LANES]
        q2s.append(jnp.concatenate([jnp.where(lane < DH, qc, zq), jnp.where(lane < DH, zq, qc)], axis=0))

    def step(kt, st, diag):
        k = k_ref[0, pl.ds(kt * t, t), :]
        v = v_ref[0, pl.ds(kt * t, t), :]
        sls = [slice(c * LANES, (c + 1) * LANES) for c in range(n_blk)]
        parts = [logits(q2s[c], k[:, sls[c]], diag) for c in range(n_blk)]
        hl = jnp.concatenate([p[1] for p in parts], axis=0)
        later = _dot(hl, upper2)
        tot = _dot(hl, ones2)
        out, mx = [], None
        for c in range(n_blk):
            rows = slice(c * 2 * t, (c + 1) * 2 * t)
            carry, acc = weigh(parts[c][0], later[rows], tot[rows], v[:, sls[c]],
                               st[2 * c], st[2 * c + 1], diag)
            out += [carry, acc]
            mx = jnp.max(carry) if mx is None else jnp.maximum(mx, jnp.max(carry))
        return mx, tuple(out)

    st0 = (jnp.zeros((2 * t, LANES), F32), jnp.zeros((t, LANES), F32)) * n_blk
    mx, st = step(qi, st0, True)

    def cond(s):
        return jnp.logical_and(s[0] >= 0, s[1] > SB_EXIT)

    def body(s):
        mx, st = step(s[0], s[2], False)
        return s[0] - 1, mx, st

    _, _, st = lax.while_loop(cond, body, (qi - 1, mx, st))
    o_ref[0] = jnp.concatenate([st[2 * c + 1] for c in range(n_blk)], axis=1)


def _sb_attention(proj, t=256, pairs=4):
    bsz, s, _ = proj.shape
    t = min(t, s)
    npair = N_SB // 2
    w = pairs * LANES
    return pl.pallas_call(
        functools.partial(_sb_kernel, t=t),
        grid=(bsz, npair // pairs, s // t),
        in_specs=[pl.BlockSpec((1, t, w), lambda b, j, i: (b, i, CB_SBQ // pairs + j)),
                  pl.BlockSpec((1, s, w), lambda b, j, i: (b, 0, CB_SBK // pairs + j)),
                  pl.BlockSpec((1, s, w), lambda b, j, i: (b, 0, CB_SBV // pairs + j))],
        out_specs=pl.BlockSpec((1, t, w), lambda b, j, i: (b, i, j)),
        out_shape=jax.ShapeDtypeStruct((bsz, s, N_SB * DH), F32),
        compiler_params=_cparams("arbitrary", "arbitrary", "arbitrary"),
        name="sb_attn",
    )(proj, proj, proj)


def _gelu_tanh(x):
    return 0.5 * x * (1.0 + jnp.tanh(math.sqrt(2.0 / math.pi) * (x + 0.044715 * (x * x * x))))


def _cmp_kernel(xa_ref, xb_ref, pos_ref, w1_ref, w2_ref, o_ref):
    half = w1_ref.shape[1] // 2
    w1a = w1_ref[0, :half, :]
    w1b = w1_ref[0, half:, :]
    pos = pos_ref[0]
    bias = _dot(pos[:, :half], w1a) + _dot(pos[:, half:], w1b)
    hid = _dot(xa_ref[0, 0], w1a) + _dot(xb_ref[0, 0], w1b) + bias[0:1, :]
    o_ref[0, 0] = _dot(_gelu_tanh(hid).astype(BF16), w2_ref[0])


def _compress(proj, pos_k, w1_k, w2_k, pos_v, w1_v, w2_v):
    bsz, s, _ = proj.shape
    n16 = s // CMP_STRIDE
    raw = proj[:, :, CB_KCMP * LANES:(CB_VCMP + 1) * LANES]
    x16 = raw.reshape(bsz, s, 4, DH).transpose(0, 2, 1, 3).reshape(bsz, 4, n16, CMP_STRIDE * DH)
    x16b = jnp.concatenate([x16[:, :, 1:], jnp.zeros_like(x16[:, :, :1])], axis=2)
    hidden = w1_k.shape[1]
    w1 = jnp.stack([w1_k, w1_v]).astype(BF16)
    w2 = jnp.stack([jnp.concatenate([w2_k, w2_k], 1), jnp.concatenate([w2_v, w2_v], 1)]).astype(BF16)
    pos = jnp.stack([pos_k.reshape(1, -1), pos_v.reshape(1, -1)])
    pos = jnp.concatenate([pos, jnp.zeros((2, 7, pos.shape[2]), pos.dtype)], axis=1).astype(BF16)
    k16 = CMP_STRIDE * DH
    return pl.pallas_call(
        _cmp_kernel,
        grid=(bsz, 4),
        in_specs=[pl.BlockSpec((1, 1, n16, k16), lambda b, j: (b, j, 0, 0)),
                  pl.BlockSpec((1, 1, n16, k16), lambda b, j: (b, j, 0, 0)),
                  pl.BlockSpec((1, 8, 2 * k16), lambda b, j: (j // 2, 0, 0)),
                  pl.BlockSpec((1, 2 * k16, hidden), lambda b, j: (j // 2, 0, 0)),
                  pl.BlockSpec((1, hidden, LANES), lambda b, j: (j // 2, 0, 0))],
        out_specs=pl.BlockSpec((1, 1, n16, LANES), lambda b, j: (b, j, 0, 0)),
        out_shape=jax.ShapeDtypeStruct((bsz, 4, n16, LANES), F32),
        compiler_params=_cparams("arbitrary", "arbitrary"),
        name="nsa_compress",
    )(x16, x16b, pos, w1, w2)


def _bucket_table():
    n = np.arange(REL_MAX_DIST + 1)
    exact = REL_BUCKETS // 2
    val = (np.log(np.maximum(n, 1).astype(np.float32) / np.float32(exact)).astype(np.float32)
           / np.float32(math.log(REL_MAX_DIST / exact)) * np.float32(REL_BUCKETS - exact))
    large = np.minimum(exact + val.astype(np.int32), REL_BUCKETS - 1)
    return np.where(n < exact, n, large).astype(np.int32)


def _bias_tiles(rel_table):
    bucket = _bucket_table()
    far_row = rel_table[int(bucket[REL_MAX_DIST])].astype(F32)
    r = np.arange(TQ)[:, None]

    def tile(dist, valid, base):
        onehot = jax.nn.one_hot(bucket[np.clip(dist, 0, REL_MAX_DIST)], REL_BUCKETS, dtype=F32)
        t = jnp.einsum('rwb,bh->hrw', onehot, rel_table.astype(F32),
                       precision=lax.Precision.HIGHEST)
        t = jnp.where(valid[None], t - base, NEG)
        return t.reshape(G_NSA, Z_NSA * TQ, dist.shape[1])

    w = np.arange(LANES)[None, :]
    d_c = r - CMP_STRIDE * (w - KC_FRONT) - (CMP_BLOCK - 1)
    w2 = np.arange(NEAR_BACK + TQ)[None, :]
    d_s = r + NEAR_BACK - w2
    w5 = np.arange(WINDOW + TQ)[None, :]
    d_w = r + WINDOW - w5
    far = jnp.broadcast_to(far_row[:, None, None], (N_NSA, TQ, LANES))
    far = far.reshape(G_NSA, Z_NSA * TQ, LANES)
    return (tile(d_c, d_c >= 0, far_row[:, None, None]), tile(d_s, d_s >= 0, 0.0),
            tile(d_w, (d_w >= 0) & (d_w < WINDOW), 0.0), far)


def _gate_expand():
    e = np.zeros((G_NSA, LANES, 3 * Z_NSA * DH), np.float32)
    for g in range(G_NSA):
        for z in range(Z_NSA):
            for br in range(3):
                e[g, (g * Z_NSA + z) * 3 + br, br * Z_NSA * DH + z * DH: br * Z_NSA * DH + (z + 1) * DH] = 1.0
    return e


def _overlap_padded(n16, kcp, n_slc):
    i = np.arange(kcp)[:, None] - KC_FRONT
    j = np.arange(LANES)[None, :]
    n_cmp = n16 - 1
    ok = (i >= 0) & (i < n_cmp) & (j < n_slc)
    ov = (i * CMP_STRIDE < j * SLC_BLOCK + SLC_BLOCK) & (i * CMP_STRIDE + CMP_BLOCK > j * SLC_BLOCK)
    return (ok & ov).astype(np.float32)


def _block_onehot(s):
    return (np.arange(s)[:, None] // SLC_BLOCK == np.arange(LANES)[None, :]).astype(np.float32)


def _nsa_kernel(q_ref, kcb_ref, vcb_ref, kcf_ref, vcf_ref, ks_ref, vs_ref, kw_ref, vw_ref, et_ref,
                gate_ref, ovb_ref, ovf_ref, bcn_ref, bsn_ref, bw_ref, cf_ref, eg_ref, o_ref,
                sb0_ref, sb1_ref, *, n_slc, n_round):
    blk = pl.program_id(2)
    c0 = blk * TQ
    r4 = Z_NSA * TQ
    q = q_ref[0]
    lane = lax.broadcasted_iota(I32, (TQ, LANES), 1)
    lo_half = lane < DH
    zq = jnp.zeros((TQ, LANES), BF16)
    qa, qb = q[:, :LANES], q[:, LANES:]
    qs = jnp.concatenate([jnp.where(lo_half, qa, zq), jnp.where(lo_half, zq, qa),
                          jnp.where(lo_half, qb, zq), jnp.where(lo_half, zq, qb)], axis=0)
    ones_k = jnp.ones((SEL_FAR_TILE, LANES), BF16)
    ones_w = jnp.ones((WINDOW + TQ, LANES), BF16)

    kcp = kcb_ref.shape[2]
    prow = lax.broadcasted_iota(I32, (1, kcp), 1)
    near0 = pl.multiple_of(blk * (TQ // CMP_STRIDE), 8)
    far_row = jnp.where((prow >= KC_FRONT) & (prow < near0), 0.0, NEG)
    lane1 = lax.broadcasted_iota(I32, (1, LANES), 1)
    near_row = jnp.where(lane1 + near0 >= KC_FRONT, 0.0, NEG)
    kcn = kcf_ref[0, 0, pl.ds(near0, LANES), :].astype(BF16)
    vcn = vcf_ref[0, 0, pl.ds(near0, LANES), :].astype(BF16)
    s_far = _nt(qs, kcb_ref[0, 0]) + far_row
    s_near = _nt(qs, kcn) + bcn_ref[0] + near_row
    m_c = jnp.maximum(jnp.max(s_far, axis=1, keepdims=True), jnp.max(s_near, axis=1, keepdims=True))
    m_c = jnp.maximum(m_c, M_INIT)
    p_far = jnp.exp(s_far - m_c)
    p_near = jnp.exp(s_near - m_c)
    den = jnp.sum(p_far, axis=1, keepdims=True) + jnp.sum(p_near, axis=1, keepdims=True)
    inv_c = 1.0 / jnp.maximum(den, 1e-30)
    o_c = (_dot(p_far.astype(BF16), vcb_ref[0, 0]) + _dot(p_near.astype(BF16), vcn)) * inv_c

    pn_far = p_far * inv_c
    pn_near = p_near * inv_c
    pz_far = pn_far[0:TQ] + pn_far[TQ:2 * TQ] + pn_far[2 * TQ:3 * TQ] + pn_far[3 * TQ:]
    pz_near = pn_near[0:TQ] + pn_near[TQ:2 * TQ] + pn_near[2 * TQ:3 * TQ] + pn_near[3 * TQ:]
    ovn = ovf_ref[pl.ds(near0, LANES), :].astype(BF16)
    imp = _dot_hl(pz_far, ovb_ref[...]) + _dot_hl(pz_near, ovn)
    tpos = c0 + lax.broadcasted_iota(I32, (TQ, LANES), 0)
    cur = lax.shift_right_logical(tpos, 6)
    valid = (lane * SLC_BLOCK <= tpos) & (lane < n_slc)
    forced = valid & ((lane == 0) | (lane == cur) | (lane == cur - 1))
    score = jnp.where(valid & jnp.logical_not(forced), imp, -1.0)
    score = jnp.where(lane < n_slc, score, -2.0)

    kws, vws = [], []
    for w in range((WINDOW + TQ) // LANES):
        st = pl.multiple_of(jnp.maximum(c0 - WINDOW + w * LANES, 0), LANES)
        kws.append(kw_ref[0, pl.ds(st, LANES), :])
        vws.append(vw_ref[0, pl.ds(st, LANES), :])
    wl = lax.broadcasted_iota(I32, (1, WINDOW + TQ), 1)
    pos_row = jnp.where(c0 - WINDOW + wl >= 0, 0.0, NEG)
    s = _nt(qs, jnp.concatenate(kws, axis=0)) + bw_ref[0] + pos_row
    p = jnp.exp(s - jnp.max(s, axis=1, keepdims=True)).astype(BF16)
    r_w = _dot(p, jnp.concatenate([jnp.concatenate(vws, axis=0), ones_w], axis=1))
    o_w = r_w[:, :LANES] / r_w[:, LANES:]
    gl = _dot(gate_ref[0], eg_ref[0])
    sig = 1.0 / (1.0 + jnp.exp(-gl))
    wide = Z_NSA * DH

    ones_b = jnp.ones((LANES, LANES), BF16)

    def count(mask):
        return _dot(jnp.where(mask, 1.0, 0.0).astype(BF16), ones_b)

    rest, taken, thr = score, jnp.zeros((TQ, LANES), F32), jnp.zeros((TQ, LANES), F32)
    for _ in range(n_round):
        m = jnp.max(rest, axis=1, keepdims=True)
        eq = rest == m
        thr = jnp.where(taken < n_round, m, thr)
        taken = taken + count(eq)
        rest = jnp.where(eq, -jnp.inf, rest)
    above = score > thr
    at_thr = score == thr
    need = n_round - count(above)
    brow = lax.broadcasted_iota(I32, (LANES, LANES), 0)
    bcol = lax.broadcasted_iota(I32, (LANES, LANES), 1)
    before = _dot(jnp.where(at_thr, 1.0, 0.0).astype(BF16),
                  jnp.where(brow < bcol, 1.0, 0.0).astype(BF16))
    sel = ((above | (at_thr & (before < need))) & valid) | forced


    sel_all = jnp.where(sel, 0.0, NEG).astype(BF16)
    far_blocks = blk * (TQ // SLC_BLOCK) - NEAR_BACK // SLC_BLOCK
    sel_far = jnp.where(sel & (lane < far_blocks), 0.0, NEG).astype(BF16)
    q_far = jnp.concatenate([qs, jnp.concatenate([sel_far] * Z_NSA, axis=0)], axis=1)
    q_near = jnp.concatenate([qs, jnp.concatenate([sel_all] * Z_NSA, axis=0)], axis=1)
    tk = SEL_FAR_TILE
    last_tile = ks_ref.shape[1] // tk - 1

    def qk(kt):
        k0 = pl.multiple_of(kt * tk, tk)
        return _nt(q_far, jnp.concatenate([ks_ref[0, pl.ds(k0, tk), :], et_ref[pl.ds(k0, tk), :]], axis=1))

    def fold(s, v, st):
        m, acc = st
        mn = jnp.maximum(m, jnp.max(s, axis=1, keepdims=True))
        p = jnp.exp(s - mn).astype(BF16)
        return mn, jnp.exp(m - mn) * acc + _dot(p, v)

    def v_tile(kt):
        k0 = pl.multiple_of(kt * tk, tk)
        return jnp.concatenate([vs_ref[0, pl.ds(k0, tk), :], ones_k], axis=1)

    n_pair = lax.shift_right_logical(jnp.maximum(c0 - NEAR_BACK, 0) + (2 * tk - 1), int(math.log2(2 * tk)))
    sb0_ref[...] = qk(0)

    n_near = NEAR_BACK + TQ
    p0 = pl.multiple_of(jnp.maximum(c0 - NEAR_BACK, 0), NEAR_BACK)
    d0 = pl.multiple_of(c0, TQ)
    kn = jnp.concatenate([ks_ref[0, pl.ds(p0, NEAR_BACK), :], ks_ref[0, pl.ds(d0, TQ), :]], axis=0)
    nrow = lax.broadcasted_iota(I32, (n_near, LANES), 0)
    nlane = lax.broadcasted_iota(I32, (n_near, LANES), 1)
    e_n = jnp.where(lax.shift_right_logical(nrow, 6) + far_blocks == nlane, 1.0, 0.0).astype(BF16)
    lane2 = lax.broadcasted_iota(I32, (1, n_near), 1)
    prev_row = jnp.where((lane2 < NEAR_BACK) & (blk == 0), NEG, 0.0)
    s_near = _nt(q_near, jnp.concatenate([kn, e_n], axis=1)) + bsn_ref[0] + prev_row

    def far_step(i, st):
        sb1_ref[...] = qk(2 * i + 1)
        st = fold(sb0_ref[...], v_tile(2 * i), st)
        sb0_ref[...] = qk(jnp.minimum(2 * i + 2, last_tile))
        return fold(sb1_ref[...], v_tile(2 * i + 1), st)

    def far_step2(i, st):
        return far_step(2 * i + 1, far_step(2 * i, st))

    m0 = jnp.full((r4, 1), M_INIT, F32)
    n_quad = lax.shift_right_logical(n_pair, 1)
    st = lax.fori_loop(0, n_quad, far_step2, (m0, jnp.zeros((r4, 2 * LANES), F32)))
    m_s, acc_s = lax.fori_loop(2 * n_quad, n_pair, far_step, st)
    m_s = m_s + cf_ref[0][:, 0:1]
    vn = jnp.concatenate([vs_ref[0, pl.ds(p0, NEAR_BACK), :], vs_ref[0, pl.ds(d0, TQ), :]], axis=0)
    _, acc_s = fold(s_near, jnp.concatenate([vn, ones_k[:n_near]], axis=1), (m_s, acc_s))
    o_s = acc_s[:, :LANES] / acc_s[:, LANES:]

    def heads(o):
        return jnp.concatenate([jnp.where(lo_half, o[0:TQ], o[TQ:2 * TQ]),
                                jnp.where(lo_half, o[2 * TQ:3 * TQ], o[3 * TQ:])], axis=1)

    o_ref[0] = (sig[:, 0:wide] * heads(o_c) + sig[:, wide:2 * wide] * heads(o_s)
                + sig[:, 2 * wide:] * heads(o_w))


def _nsa_attention(proj, cmp_out, rel_table):
    bsz, s, _ = proj.shape
    assert s % (2 * SEL_FAR_TILE) == 0 and s >= WINDOW + TQ
    n16 = s // CMP_STRIDE
    n_slc = s // SLC_BLOCK
    assert N_SELECT <= n_slc <= LANES
    nq = s // TQ
    kcp = -(-(max(n16 + KC_FRONT, (TQ // CMP_STRIDE) * (nq - 1) + LANES)) // LANES) * LANES
    pad = ((0, 0), (0, 0), (KC_FRONT, kcp - n16 - KC_FRONT), (0, 0))
    cf = jnp.pad(cmp_out, pad)
    cb = cf.astype(BF16)
    ov = _overlap_padded(n16, kcp, n_slc)
    bcn, bsn, bw, far = _bias_tiles(rel_table)
    eg = jnp.asarray(_gate_expand(), BF16)
    r4 = Z_NSA * TQ
    kv_spec = lambda cb0: pl.BlockSpec((1, s, LANES), lambda b, g, i, cb0=cb0: (b, 0, cb0 + g))
    cmp_spec = lambda j0: pl.BlockSpec((1, 1, kcp, LANES), lambda b, g, i, j0=j0: (b, j0 + g, 0, 0))
    tile_spec = lambda w: pl.BlockSpec((1, r4, w), lambda b, g, i: (g, 0, 0))
    const2 = lambda n: pl.BlockSpec((n, LANES), lambda b, g, i: (0, 0))
    return pl.pallas_call(
        functools.partial(_nsa_kernel, n_slc=n_slc, n_round=N_SELECT - 3),
        grid=(bsz, G_NSA, nq),
        in_specs=[pl.BlockSpec((1, TQ, 2 * LANES), lambda b, g, i: (b, i, CB_NSAQ // 2 + g)),
                  cmp_spec(0), cmp_spec(2), cmp_spec(0), cmp_spec(2),
                  kv_spec(CB_KSLC), kv_spec(CB_VSLC), kv_spec(CB_KWIN), kv_spec(CB_VWIN),
                  const2(s),
                  pl.BlockSpec((1, TQ, LANES), lambda b, g, i: (b, i, CB_GATE)),
                  const2(kcp), const2(kcp),
                  tile_spec(LANES), tile_spec(NEAR_BACK + TQ), tile_spec(WINDOW + TQ), tile_spec(LANES),
                  pl.BlockSpec((1, LANES, 3 * Z_NSA * DH), lambda b, g, i: (g, 0, 0))],
        out_specs=pl.BlockSpec((1, TQ, Z_NSA * DH), lambda b, g, i: (b, i, g)),
        out_shape=jax.ShapeDtypeStruct((bsz, s, N_NSA * DH), F32),
        scratch_shapes=[pltpu.VMEM((r4, SEL_FAR_TILE), F32), pltpu.VMEM((r4, SEL_FAR_TILE), F32)],
        compiler_params=_cparams("arbitrary", "arbitrary", "arbitrary"),
        name="nsa_attn",
    )(proj, cb, cb, cf, cf, proj, proj, proj, proj, jnp.asarray(_block_onehot(s), BF16), proj,
      jnp.asarray(ov, BF16), jnp.asarray(ov, F32), bcn, bsn, bw, far, eg)


def _out_kernel(osb_ref, onsa_ref, x_ref, gsb_ref, gnsa_ref, w_ref, gpost_ref, gate_ref,
                gpre_ref, sc_ref, sh_ref, xo_ref, h_ref):
    half = osb_ref.shape[2]
    a = _rms(osb_ref[0], gsb_ref[...]).astype(BF16)
    b = _rms(onsa_ref[0], gnsa_ref[...]).astype(BF16)
    m = _dot(a, w_ref[:half, :]) + _dot(b, w_ref[half:, :])
    x = x_ref[0] + gate_ref[0] * _rms(m, gpost_ref[...])
    xo_ref[0] = x
    h_ref[0] = (_rms(x, gpre_ref[...]) * (1.0 + sc_ref[0]) + sh_ref[0]).astype(BF16)


def _out_proj(o_sb, o_nsa, x, g_sb, g_nsa, w_out, g_post, gate_m, g_pre_ffn, scale_f, shift_f):
    bsz, s, d = x.shape
    half = o_sb.shape[2]
    tm = min(512, s)
    row = lambda n: pl.BlockSpec((1, n), lambda b, i: (0, 0))
    mod = pl.BlockSpec((1, 1, d), lambda b, i: (b, 0, 0))
    act = lambda n: pl.BlockSpec((1, tm, n), lambda b, i: (b, i, 0))
    return pl.pallas_call(
        _out_kernel,
        grid=(bsz, s // tm),
        in_specs=[act(half), act(half), act(d), row(half), row(half),
                  pl.BlockSpec((2 * half, d), lambda b, i: (0, 0)), row(d), mod, row(d), mod, mod],
        out_specs=[act(d), act(d)],
        out_shape=[jax.ShapeDtypeStruct((bsz, s, d), F32), jax.ShapeDtypeStruct((bsz, s, d), BF16)],
        compiler_params=_cparams("arbitrary", "arbitrary"),
        name="out_proj",
    )(o_sb, o_nsa, x, g_sb.reshape(1, half), g_nsa.reshape(1, half), w_out.astype(BF16),
      g_post.reshape(1, d), gate_m.reshape(bsz, 1, d), g_pre_ffn.reshape(1, d),
      scale_f.reshape(bsz, 1, d), shift_f.reshape(bsz, 1, d))


def _silu(x):
    return x * (1.0 / (1.0 + jnp.exp(-x)))


def _ffn_kernel(h_ref, x_ref, wg_ref, wu_ref, wd_ref, gpost_ref, gate_ref, o_ref):
    h = h_ref[0]
    a = (_silu(_dot(h, wg_ref[...])) * _dot(h, wu_ref[...])).astype(BF16)
    o_ref[0] = x_ref[0] + gate_ref[0] * _rms(_dot(a, wd_ref[...]), gpost_ref[...])


def _dense_ffn(h, x, w_gate, w_up, w_down, g_post, gate_f):
    bsz, s, d = x.shape
    ff = w_gate.shape[1]
    tm = min(512, s)
    act = pl.BlockSpec((1, tm, d), lambda b, i: (b, i, 0))
    once = pl.Buffered(1)
    return pl.pallas_call(
        _ffn_kernel,
        grid=(bsz, s // tm),
        in_specs=[act, act,
                  pl.BlockSpec((d, ff), lambda b, i: (0, 0), pipeline_mode=once),
                  pl.BlockSpec((d, ff), lambda b, i: (0, 0), pipeline_mode=once),
                  pl.BlockSpec((ff, d), lambda b, i: (0, 0), pipeline_mode=once),
                  pl.BlockSpec((1, d), lambda b, i: (0, 0)),
                  pl.BlockSpec((1, 1, d), lambda b, i: (b, 0, 0))],
        out_specs=act,
        out_shape=jax.ShapeDtypeStruct((bsz, s, d), F32),
        compiler_params=_cparams("arbitrary", "arbitrary"),
        name="dense_ffn",
    )(h, x, w_gate.astype(BF16), w_up.astype(BF16), w_down.astype(BF16),
      g_post.reshape(1, d), gate_f.reshape(bsz, 1, d))


MOE_RC = 128
MOE_GC = 256
MOE_SC = 1024


def _moe_rows(tile):
    return -(-(2 * tile + N_EXPERTS * MOE_RC) // MOE_SC) * MOE_SC


def _dispatch_kernel(x_ref, gpre_ref, sc_ref, sh_ref, wr_ref, br_ref,
                     xg_ref, wrow_ref, d_ref, off_ref, nch_ref):
    tile = x_ref.shape[1]
    rows = xg_ref.shape[1]
    h = _rms(x_ref[0], gpre_ref[...]) * (1.0 + sc_ref[0]) + sh_ref[0]
    logits = _dot_f32(h, wr_ref[...]) + br_ref[...]
    lane = lax.broadcasted_iota(I32, (tile, LANES), 1)
    lanef = lane.astype(F32)
    e = jnp.exp(logits - jnp.max(logits, axis=1, keepdims=True))
    probs = e / jnp.sum(e, axis=1, keepdims=True)
    probs = jnp.where(lane < N_EXPERTS, probs, -1.0)
    m1 = jnp.max(probs, axis=1, keepdims=True)
    i1 = jnp.min(jnp.where(probs == m1, lanef, 1e9), axis=1, keepdims=True)
    rest = jnp.where(lanef == i1, -1.0, probs)
    m2 = jnp.max(rest, axis=1, keepdims=True)
    i2 = jnp.min(jnp.where(rest == m2, lanef, 1e9), axis=1, keepdims=True)
    tot = m1 + m2
    hit1 = lanef == i1
    hit2 = lanef == i2
    mask = hit1 | hit2
    ch = 256
    r = lax.broadcasted_iota(I32, (ch, ch), 0)
    c = lax.broadcasted_iota(I32, (ch, ch), 1)
    lower = jnp.where(c < r, 1.0, 0.0).astype(BF16)
    ones = jnp.ones((8, ch), BF16)
    count = jnp.zeros((1, LANES), F32)
    ranks = []
    for k in range(tile // ch):
        mk = jnp.where(mask[k * ch:(k + 1) * ch], 1.0, 0.0).astype(BF16)
        ranks.append(_dot(lower, mk) + count)
        count = count + _dot(ones, mk)[0:1]
    rank = jnp.concatenate(ranks, axis=0)
    shift = int(math.log2(MOE_RC))
    nch = lax.shift_right_logical(count.astype(I32) + (MOE_RC - 1), shift)
    nch8 = jnp.broadcast_to(nch.astype(F32), (8, LANES))
    lr = lax.broadcasted_iota(I32, (LANES, LANES), 0)
    lc = lax.broadcasted_iota(I32, (LANES, LANES), 1)
    offc = _dot(nch8.astype(BF16), jnp.where(lr < lc, 1.0, 0.0).astype(BF16))
    row = offc[0:1] * float(MOE_RC) + rank
    d1 = jnp.sum(jnp.where(hit1, row, 0.0), axis=1, keepdims=True)
    d2 = jnp.sum(jnp.where(hit2, row, 0.0), axis=1, keepdims=True)
    dd = jnp.where(lane == 0, d1, jnp.where(lane == 1, d2, -1.0))
    d_ref[0] = dd.astype(I32)
    off_ref[0] = offc.astype(I32)
    nch_ref[0] = jnp.broadcast_to(nch, (8, LANES))

    ddt = jnp.transpose(dd)
    d1t = ddt[0:1].astype(I32)
    d2t = ddt[1:2].astype(I32)

    wt = jnp.transpose(jnp.where(lane == 0, m1 / tot, jnp.where(lane == 1, m2 / tot, 0.0)))
    w1t = wt[0:1]
    w2t = wt[1:2]
    hb = h.astype(BF16)
    rid = lax.broadcasted_iota(I32, (MOE_GC, tile), 0)
    for k in range(rows // MOE_GC):
        hit1 = rid + k * MOE_GC == d1t
        hit2 = rid + k * MOE_GC == d2t
        onehot = jnp.where(hit1 | hit2, 1.0, 0.0).astype(BF16)
        xg_ref[0, k * MOE_GC:(k + 1) * MOE_GC, :] = _dot(onehot, hb).astype(BF16)
        wrow = jnp.sum(jnp.where(hit1, w1t, jnp.where(hit2, w2t, 0.0)), axis=1, keepdims=True)
        wrow_ref[0, k * MOE_GC:(k + 1) * MOE_GC, :] = jnp.broadcast_to(wrow, (MOE_GC, LANES))


def _dispatch(x, g_pre, scale_f, shift_f, w_router, b_router, tile):
    bsz, s, d = x.shape
    nt = s // tile
    rows = _moe_rows(tile)
    wr = jnp.zeros((d, LANES), F32).at[:, :N_EXPERTS].set(w_router.astype(F32))
    br = jnp.full((1, LANES), NEG, F32).at[0, :N_EXPERTS].set(b_router.astype(F32))
    mod = pl.BlockSpec((1, 1, d), lambda b, i: (b, 0, 0))
    per_tile = lambda r, n: pl.BlockSpec((1, r, n), lambda b, i: (b * nt + i, 0, 0))
    xg, wrow, dest, offc, nch = pl.pallas_call(
        _dispatch_kernel,
        grid=(bsz, nt),
        in_specs=[pl.BlockSpec((1, tile, d), lambda b, i: (b, i, 0)),
                  pl.BlockSpec((1, d), lambda b, i: (0, 0)), mod, mod,
                  pl.BlockSpec((d, LANES), lambda b, i: (0, 0)),
                  pl.BlockSpec((1, LANES), lambda b, i: (0, 0))],
        out_specs=[per_tile(rows, d), per_tile(rows, LANES), per_tile(tile, LANES),
                   per_tile(8, LANES), per_tile(8, LANES)],
        out_shape=[jax.ShapeDtypeStruct((bsz * nt, rows, d), BF16),
                   jax.ShapeDtypeStruct((bsz * nt, rows, LANES), F32),
                   jax.ShapeDtypeStruct((bsz * nt, tile, LANES), I32),
                   jax.ShapeDtypeStruct((bsz * nt, 8, LANES), I32),
                   jax.ShapeDtypeStruct((bsz * nt, 8, LANES), I32)],
        compiler_params=_cparams("arbitrary", "arbitrary"),
        name="moe_dispatch",
    )(x, g_pre.reshape(1, d), scale_f.reshape(bsz, 1, d), shift_f.reshape(bsz, 1, d), wr, br)
    return xg, wrow, dest, offc[:, 0, :N_EXPERTS], nch[:, 0, :N_EXPERTS]


def _slot_order(offc, nch, slots_per_tile):
    ends = offc + nch
    c = jnp.arange(slots_per_tile, dtype=I32)[None, :, None]
    expert = jnp.sum((c >= ends[:, None, :]).astype(I32), axis=-1).reshape(-1)
    n = expert.shape[0]
    order = jnp.argsort(expert * n + jnp.arange(n, dtype=I32)).astype(I32)
    exp_sorted = expert[order]
    used = (exp_sorted < N_EXPERTS).astype(I32)
    return order, jnp.minimum(exp_sorted, N_EXPERTS - 1).astype(I32), used


def _slots_kernel(slot_ref, exp_ref, used_ref, x_ref, wrow_ref, wg_ref, wu_ref, wd_ref, y_ref):
    p = pl.program_id(0)

    @pl.when(used_ref[p] == 1)
    def _():
        x = x_ref[0]
        a = (_silu(_dot(x, wg_ref[0])) * _dot(x, wu_ref[0])).astype(BF16)
        y_ref[0] = (_dot(a, wd_ref[0]) * wrow_ref[0][:, 0:1]).astype(BF16)

    @pl.when(used_ref[p] == 0)
    def _():
        y_ref[0] = jnp.zeros(y_ref.shape[1:], BF16)


def _expert_slots(xg, wrow, order, exp_sorted, used, w_gate, w_up, w_down):
    ntile, rows, d = xg.shape
    nslot = ntile * rows // MOE_RC
    ff = w_gate.shape[2]
    once = pl.Buffered(1)
    slot = lambda n: pl.BlockSpec((1, MOE_RC, n), lambda p, s, e, u: (s[p], 0, 0))
    grid_spec = pltpu.PrefetchScalarGridSpec(
        num_scalar_prefetch=3,
        grid=(nslot,),
        in_specs=[slot(d), slot(LANES),
                  pl.BlockSpec((1, d, ff), lambda p, s, e, u: (e[p], 0, 0), pipeline_mode=once),
                  pl.BlockSpec((1, d, ff), lambda p, s, e, u: (e[p], 0, 0), pipeline_mode=once),
                  pl.BlockSpec((1, ff, d), lambda p, s, e, u: (e[p], 0, 0), pipeline_mode=once)],
        out_specs=slot(d))
    y = pl.pallas_call(
        _slots_kernel,
        grid_spec=grid_spec,
        out_shape=jax.ShapeDtypeStruct((nslot, MOE_RC, d), BF16),
        input_output_aliases={3: 0},
        compiler_params=_cparams("arbitrary"),
        name="moe_slots",
    )(order, exp_sorted, used, xg.reshape(nslot, MOE_RC, d), wrow.reshape(nslot, MOE_RC, LANES),
      w_gate.astype(BF16), w_up.astype(BF16), w_down.astype(BF16))
    return y.reshape(ntile, rows, d)


def _combine_kernel(y_ref, d_ref, x_ref, gpost_ref, gate_ref, o_ref):
    tile = x_ref.shape[1]
    rows = y_ref.shape[1]
    dd = d_ref[0]
    d1c = dd[:, 0:1]
    d2c = dd[:, 1:2]
    cid = lax.broadcasted_iota(I32, (tile, MOE_SC), 1)
    z = None
    for k in range(rows // MOE_SC):
        pt = jnp.where((cid + k * MOE_SC == d1c) | (cid + k * MOE_SC == d2c), 1.0, 0.0).astype(BF16)
        part = _dot(pt, y_ref[0, k * MOE_SC:(k + 1) * MOE_SC, :])
        z = part if z is None else z + part
    o_ref[0] = x_ref[0] + gate_ref[0] * _rms(z, gpost_ref[...])


def _combine(y, dest, x, g_post, gate_f):
    bsz, s, d = x.shape
    ntile, rows, _ = y.shape
    tile = dest.shape[1]
    nt = s // tile
    per_tile = lambda r, n: pl.BlockSpec((1, r, n), lambda i: (i, 0, 0))
    out = pl.pallas_call(
        _combine_kernel,
        grid=(ntile,),
        in_specs=[per_tile(rows, d), per_tile(tile, LANES), per_tile(tile, d),
                  pl.BlockSpec((1, d), lambda i: (0, 0)),
                  pl.BlockSpec((1, 1, d), lambda i: (i // nt, 0, 0))],
        out_specs=per_tile(tile, d),
        out_shape=jax.ShapeDtypeStruct((ntile, tile, d), F32),
        compiler_params=_cparams("arbitrary"),
        name="moe_combine",
    )(y, dest, x.reshape(ntile, tile, d), g_post.reshape(1, d), gate_f.reshape(bsz, 1, d))
    return out.reshape(bsz, s, d)


def _mixer(x, mod, layer, rel_table, g_pre_mix, w_in, cmp_params):
    shift_m, scale_m = mod[:, 0], mod[:, 1]
    proj = _in_proj(x, g_pre_mix, scale_m, shift_m, _arrange_w_in(w_in))
    o_sb = _sb_attention(proj)
    o_nsa = _nsa_attention(proj, _compress(proj, *cmp_params), rel_table)
    return o_sb, o_nsa


def kernel(x, c, rel_table, w_ada, b_ada, g_pre_mix, g_post_mix, g_pre_ffn, g_post_ffn, w_in, w_out, g_sb, g_nsa, cmp_pos_k, cmp_w1_k, cmp_w2_k, cmp_pos_v, cmp_w1_v, cmp_w2_v, ffn_w_gate, ffn_w_up, ffn_w_down, moe_w_router, moe_b_router, moe_w_gate, moe_w_up, moe_w_down):
    bsz, s, d = x.shape
    depth = w_in.shape[0]
    mods = _ada(c, w_ada, b_ada).reshape(depth, bsz, 6, d)
    moe_tile = min(1024, s)
    for layer in range(depth):
        mod = mods[layer]
        cmp_params = (cmp_pos_k[layer], cmp_w1_k[layer], cmp_w2_k[layer],
                      cmp_pos_v[layer], cmp_w1_v[layer], cmp_w2_v[layer])
        o_sb, o_nsa = _mixer(x, mod, layer, rel_table, g_pre_mix[layer], w_in[layer], cmp_params)
        x, h = _out_proj(o_sb, o_nsa, x, g_sb[layer], g_nsa[layer], w_out[layer], g_post_mix[layer],
                         mod[:, 2], g_pre_ffn[layer], mod[:, 4], mod[:, 3])
        i = layer // 2
        if layer % 2 == 0:
            x = _dense_ffn(h, x, ffn_w_gate[i], ffn_w_up[i], ffn_w_down[i], g_post_ffn[layer], mod[:, 5])
        else:
            xg, wrow, dest, offc, nch = _dispatch(x, g_pre_ffn[layer], mod[:, 4], mod[:, 3],
                                                  moe_w_router[i], moe_b_router[i], moe_tile)
            order = _slot_order(offc, nch, xg.shape[1] // MOE_RC)
            y = _expert_slots(xg, wrow, *order, moe_w_gate[i], moe_w_up[i], moe_w_down[i])
            x = _combine(y, dest, x, g_post_ffn[layer], mod[:, 5])
    return x
```

```python
import functools
import math

import numpy as np
import jax
import jax.numpy as jnp
from jax import lax
from jax.experimental import pallas as pl
from jax.experimental.pallas import tpu as pltpu

F32 = jnp.float32
BF16 = jnp.bfloat16
I32 = jnp.int32

LANES = 128
DH = 64
N_SB = 8
N_NSA = 8
G_NSA = 2
Z_NSA = 4
CMP_BLOCK = 32
CMP_STRIDE = 16
SLC_BLOCK = 64
N_SELECT = 16
WINDOW = 512
REL_BUCKETS = 32
REL_MAX_DIST = 128
N_EXPERTS = 8
EPS = 1e-6
FORCED = 1e4
NEG = -1e30
M_INIT = -1e29
SB_EXIT = -104.5
VMEM_LIMIT = 56 * 1024 * 1024

TQ = 256
NEAR_BACK = REL_MAX_DIST
KC_FRONT = 16
SEL_FAR_TILE = 512


def _cparams(*sem):
    return pltpu.CompilerParams(dimension_semantics=sem, vmem_limit_bytes=VMEM_LIMIT)


def _nt(a, b):
    return lax.dot_general(a, b, (((1,), (1,)), ((), ())), preferred_element_type=F32)


def _dot(a, b):
    return jnp.dot(a, b, preferred_element_type=F32)


def _split3(a):
    hi = a.astype(BF16)
    r = a - hi.astype(F32)
    mid = r.astype(BF16)
    lo = (r - mid.astype(F32)).astype(BF16)
    return hi, mid, lo


def _dot_hl(a, b):
    hi = a.astype(BF16)
    lo = (a - hi.astype(F32)).astype(BF16)
    return _dot(hi, b) + _dot(lo, b)


def _dot_f32(a, b):
    ah, am, al = _split3(a)
    bh, bm, bl = _split3(b)
    return (_dot(ah, bh) + (_dot(ah, bm) + _dot(am, bh))
            + (_dot(ah, bl) + _dot(am, bm) + _dot(al, bh)))


def _rms(x, g):
    return x * lax.rsqrt(jnp.mean(x * x, axis=-1, keepdims=True) + EPS) * g


def _ada_kernel(c_ref, w_ref, b_ref, o_ref):
    c = c_ref[...]
    ca = c * (1.0 / (1.0 + jnp.exp(-c)))
    o_ref[0] = _dot_f32(ca, w_ref[0]) + b_ref[0]


def _ada(c, w_ada, b_ada):
    depth, d, n = w_ada.shape
    bsz = c.shape[0]
    rows = 8
    tn = 1536
    cp = jnp.zeros((rows, d), F32).at[:bsz].set(c)
    out = pl.pallas_call(
        _ada_kernel,
        grid=(depth, n // tn),
        in_specs=[pl.BlockSpec((rows, d), lambda l, j: (0, 0)),
                  pl.BlockSpec((1, d, tn), lambda l, j: (l, 0, j)),
                  pl.BlockSpec((1, 1, tn), lambda l, j: (l, 0, j))],
        out_specs=pl.BlockSpec((1, rows, tn), lambda l, j: (l, 0, j)),
        out_shape=jax.ShapeDtypeStruct((depth, rows, n), F32),
        compiler_params=_cparams("arbitrary", "arbitrary"),
        name="ada_mod",
    )(cp, w_ada, b_ada.reshape(depth, 1, n))
    return out[:, :bsz]


def _in_kernel(x_ref, g_ref, sc_ref, sh_ref, w_ref, o_ref, *, cn):
    h = _rms(x_ref[0], g_ref[...]) * (1.0 + sc_ref[0]) + sh_ref[0]
    hb = h.astype(BF16)
    for j in range(w_ref.shape[1] // cn):
        o_ref[0, :, j * cn:(j + 1) * cn] = _dot(hb, w_ref[:, j * cn:(j + 1) * cn]).astype(BF16)


def _in_proj(x, g, scale, shift, w):
    bsz, s, d = x.shape
    nc = w.shape[1]
    tm = min(512, s)
    cn = nc // 3 if (nc // 3) % LANES == 0 else nc
    return pl.pallas_call(
        functools.partial(_in_kernel, cn=cn),
        grid=(bsz, s // tm),
        in_specs=[pl.BlockSpec((1, tm, d), lambda b, i: (b, i, 0)),
                  pl.BlockSpec((1, d), lambda b, i: (0, 0)),
                  pl.BlockSpec((1, 1, d), lambda b, i: (b, 0, 0)),
                  pl.BlockSpec((1, 1, d), lambda b, i: (b, 0, 0)),
                  pl.BlockSpec((d, nc), lambda b, i: (0, 0))],
        out_specs=pl.BlockSpec((1, tm, nc), lambda b, i: (b, i, 0)),
        out_shape=jax.ShapeDtypeStruct((bsz, s, nc), BF16),
        compiler_params=_cparams("arbitrary", "arbitrary"),
        name="in_proj",
    )(x, g.reshape(1, d), scale.reshape(bsz, 1, d), shift.reshape(bsz, 1, d), w)


CB_SBQ, CB_SBK, CB_SBV, CB_NSAQ = 0, 4, 8, 12
CB_KCMP, CB_VCMP = 16, 17
CB_KSLC, CB_VSLC, CB_KWIN, CB_VWIN = 18, 20, 22, 24
CB_GATE = 26
N_CB = 27


def _arrange_w_in(w_in):
    d = w_in.shape[0]
    w_sb = N_SB * DH
    off_nsa_q = 3 * w_sb
    off_kv = off_nsa_q + N_NSA * DH
    off_gate = off_kv + 3 * 2 * G_NSA * DH
    scale = DH ** -0.5

    def kv(br, kvi, g):
        lo = off_kv + ((br * 2 + kvi) * G_NSA + g) * DH
        return w_in[:, lo:lo + DH]

    cols = [w_in[:, 0:w_sb] * scale, w_in[:, w_sb:2 * w_sb], w_in[:, 2 * w_sb:3 * w_sb],
            w_in[:, off_nsa_q:off_kv] * scale,
            kv(0, 0, 0), kv(0, 0, 1), kv(0, 1, 0), kv(0, 1, 1)]
    for br in (1, 2):
        for kvi in (0, 1):
            for g in range(G_NSA):
                cols += [kv(br, kvi, g), kv(br, kvi, g)]
    n_gate = 3 * N_NSA
    cols += [w_in[:, off_gate:off_gate + n_gate], jnp.zeros((d, LANES - n_gate), w_in.dtype)]
    out = jnp.concatenate(cols, axis=1).astype(BF16)
    assert out.shape[1] == N_CB * LANES
    return out


def _sb_kernel(q_ref, k_ref, v_ref, o_ref, *, t):
    qi = pl.program_id(2)
    q = q_ref[0]
    lane = lax.broadcasted_iota(I32, (t, LANES), 1)
    row = lax.broadcasted_iota(I32, (t, t), 0)
    col = lax.broadcasted_iota(I32, (t, t), 1)
    upper = jnp.where(row > col, 1.0, 0.0).astype(BF16)
    upper2 = jnp.concatenate([upper, upper], axis=0)
    ones2 = jnp.ones((2 * t, LANES), BF16)
    causal = jnp.concatenate([col < row, col < row], axis=0)
    rep = t // LANES

    def logits(q2, k, diag):
        z = _nt(q2, k)
        lk = -(jnp.maximum(z, 0.0) + jnp.log(1.0 + jnp.exp(-jnp.abs(z))))
        if diag:
            lk = jnp.where(causal, lk, 0.0)
        hi = lk.astype(BF16)
        lo = (lk - hi.astype(F32)).astype(BF16)
        return z + lk, jnp.concatenate([hi, lo], axis=1)

    def weigh(ls, later, tot, v, carry, acc, diag):
        cb = carry if rep == 1 else jnp.concatenate([carry] * rep, axis=1)
        w = jnp.exp(ls + later + cb)
        if diag:
            w = jnp.where(causal, w, 0.0)
        r = _dot(w.astype(BF16), v)
        return carry + tot, acc + jnp.where(lane < DH, r[:t], r[t:])

    n_blk = q.shape[1] // LANES
    zq = jnp.zeros((t, LANES), BF16)
    q2s = []
    for c in range(n_blk):
        qc = q[:, c * LANES:(c + 1) * LANES]
        q2s.append(jnp.concatenate([jnp.where(lane < DH, qc, zq), jnp.where(lane < DH, zq, qc)], axis=0))

    def step(kt, st, diag):
        k = k_ref[0, pl.ds(kt * t, t), :]
        v = v_ref[0, pl.ds(kt * t, t), :]
        sls = [slice(c * LANES, (c + 1) * LANES) for c in range(n_blk)]
        parts = [logits(q2s[c], k[:, sls[c]], diag) for c in range(n_blk)]
        hl = jnp.concatenate([p[1] for p in parts], axis=0)
        later = _dot(hl, upper2)
        tot = _dot(hl, ones2)
        out, mx = [], None
        for c in range(n_blk):
            rows = slice(c * 2 * t, (c + 1) * 2 * t)
            carry, acc = weigh(parts[c][0], later[rows], tot[rows], v[:, sls[c]],
                               st[2 * c], st[2 * c + 1], diag)
            out += [carry, acc]
            mx = jnp.max(carry) if mx is None else jnp.maximum(mx, jnp.max(carry))
        return mx, tuple(out)

    st0 = (jnp.zeros((2 * t, LANES), F32), jnp.zeros((t, LANES), F32)) * n_blk
    _, st = step(qi, st0, True)
    dead = jnp.where(qi >= 1, 0.0, NEG)
    st = tuple(a + dead if i % 2 == 0 else a for i, a in enumerate(st))
    mx, st = step(jnp.maximum(qi - 1, 0), st, False)

    def cond(s):
        return jnp.logical_and(s[0] >= 0, s[1] > SB_EXIT)

    def body(s):
        mx, st = step(s[0], s[2], False)
        return s[0] - 1, mx, st

    _, _, st = lax.while_loop(cond, body, (qi - 2, mx, st))
    o_ref[0] = jnp.concatenate([st[2 * c + 1] for c in range(n_blk)], axis=1)


def _sb_attention(proj, t=256, pairs=4):
    bsz, s, _ = proj.shape
    t = min(t, s)
    npair = N_SB // 2
    w = pairs * LANES
    return pl.pallas_call(
        functools.partial(_sb_kernel, t=t),
        grid=(bsz, npair // pairs, s // t),
        in_specs=[pl.BlockSpec((1, t, w), lambda b, j, i: (b, i, CB_SBQ // pairs + j)),
                  pl.BlockSpec((1, s, w), lambda b, j, i: (b, 0, CB_SBK // pairs + j)),
                  pl.BlockSpec((1, s, w), lambda b, j, i: (b, 0, CB_SBV // pairs + j))],
        out_specs=pl.BlockSpec((1, t, w), lambda b, j, i: (b, i, j)),
        out_shape=jax.ShapeDtypeStruct((bsz, s, N_SB * DH), F32),
        compiler_params=_cparams("arbitrary", "arbitrary", "arbitrary"),
        name="sb_attn",
    )(proj, proj, proj)


def _gelu_tanh(x):
    return 0.5 * x * (1.0 + jnp.tanh(math.sqrt(2.0 / math.pi) * (x + 0.044715 * (x * x * x))))


def _cmp_kernel(xa_ref, xb_ref, pos_ref, w1_ref, w2_ref, o_ref):
    half = w1_ref.shape[1] // 2
    w1a = w1_ref[0, :half, :]
    w1b = w1_ref[0, half:, :]
    pos = pos_ref[0]
    bias = _dot(pos[:, :half], w1a) + _dot(pos[:, half:], w1b)
    hid = _dot(xa_ref[0, 0], w1a) + _dot(xb_ref[0, 0], w1b) + bias[0:1, :]
    o_ref[0, 0] = _dot(_gelu_tanh(hid).astype(BF16), w2_ref[0])


def _compress(proj, pos_k, w1_k, w2_k, pos_v, w1_v, w2_v):
    bsz, s, _ = proj.shape
    n16 = s // CMP_STRIDE
    raw = proj[:, :, CB_KCMP * LANES:(CB_VCMP + 1) * LANES]
    x16 = raw.reshape(bsz, s, 4, DH).transpose(0, 2, 1, 3).reshape(bsz, 4, n16, CMP_STRIDE * DH)
    x16b = jnp.concatenate([x16[:, :, 1:], jnp.zeros_like(x16[:, :, :1])], axis=2)
    hidden = w1_k.shape[1]
    w1 = jnp.stack([w1_k, w1_v]).astype(BF16)
    w2 = jnp.stack([jnp.concatenate([w2_k, w2_k], 1), jnp.concatenate([w2_v, w2_v], 1)]).astype(BF16)
    pos = jnp.stack([pos_k.reshape(1, -1), pos_v.reshape(1, -1)])
    pos = jnp.concatenate([pos, jnp.zeros((2, 7, pos.shape[2]), pos.dtype)], axis=1).astype(BF16)
    k16 = CMP_STRIDE * DH
    return pl.pallas_call(
        _cmp_kernel,
        grid=(bsz, 4),
        in_specs=[pl.BlockSpec((1, 1, n16, k16), lambda b, j: (b, j, 0, 0)),
                  pl.BlockSpec((1, 1, n16, k16), lambda b, j: (b, j, 0, 0)),
                  pl.BlockSpec((1, 8, 2 * k16), lambda b, j: (j // 2, 0, 0)),
                  pl.BlockSpec((1, 2 * k16, hidden), lambda b, j: (j // 2, 0, 0)),
                  pl.BlockSpec((1, hidden, LANES), lambda b, j: (j // 2, 0, 0))],
        out_specs=pl.BlockSpec((1, 1, n16, LANES), lambda b, j: (b, j, 0, 0)),
        out_shape=jax.ShapeDtypeStruct((bsz, 4, n16, LANES), F32),
        compiler_params=_cparams("arbitrary", "arbitrary"),
        name="nsa_compress",
    )(x16, x16b, pos, w1, w2)


def _bucket_table():
    n = np.arange(REL_MAX_DIST + 1)
    exact = REL_BUCKETS // 2
    val = (np.log(np.maximum(n, 1).astype(np.float32) / np.float32(exact)).astype(np.float32)
           / np.float32(math.log(REL_MAX_DIST / exact)) * np.float32(REL_BUCKETS - exact))
    large = np.minimum(exact + val.astype(np.int32), REL_BUCKETS - 1)
    return np.where(n < exact, n, large).astype(np.int32)


def _bias_tiles(rel_table):
    bucket = _bucket_table()
    far_row = rel_table[int(bucket[REL_MAX_DIST])].astype(F32)
    r = np.arange(TQ)[:, None]

    def tile(dist, valid, base):
        onehot = jax.nn.one_hot(bucket[np.clip(dist, 0, REL_MAX_DIST)], REL_BUCKETS, dtype=F32)
        t = jnp.einsum('rwb,bh->hrw', onehot, rel_table.astype(F32),
                       precision=lax.Precision.HIGHEST)
        t = jnp.where(valid[None], t - base, NEG)
        return t.reshape(G_NSA, Z_NSA * TQ, dist.shape[1])

    w = np.arange(LANES)[None, :]
    d_c = r - CMP_STRIDE * (w - KC_FRONT) - (CMP_BLOCK - 1)
    w2 = np.arange(NEAR_BACK + TQ)[None, :]
    d_s = r + NEAR_BACK - w2
    w5 = np.arange(WINDOW + TQ)[None, :]
    d_w = r + WINDOW - w5
    far = jnp.broadcast_to(far_row[:, None, None], (N_NSA, TQ, LANES))
    far = far.reshape(G_NSA, Z_NSA * TQ, LANES)
    return (tile(d_c, d_c >= 0, far_row[:, None, None]), tile(d_s, d_s >= 0, 0.0),
            tile(d_w, (d_w >= 0) & (d_w < WINDOW), 0.0), far)


def _gate_expand():
    e = np.zeros((G_NSA, LANES, 3 * Z_NSA * DH), np.float32)
    for g in range(G_NSA):
        for z in range(Z_NSA):
            for br in range(3):
                e[g, (g * Z_NSA + z) * 3 + br, br * Z_NSA * DH + z * DH: br * Z_NSA * DH + (z + 1) * DH] = 1.0
    return e


def _overlap_padded(n16, kcp, n_slc):
    i = np.arange(kcp)[:, None] - KC_FRONT
    j = np.arange(LANES)[None, :]
    n_cmp = n16 - 1
    ok = (i >= 0) & (i < n_cmp) & (j < n_slc)
    ov = (i * CMP_STRIDE < j * SLC_BLOCK + SLC_BLOCK) & (i * CMP_STRIDE + CMP_BLOCK > j * SLC_BLOCK)
    return (ok & ov).astype(np.float32)


def _block_onehot(s):
    return (np.arange(s)[:, None] // SLC_BLOCK == np.arange(LANES)[None, :]).astype(np.float32)


def _nsa_kernel(q_ref, kcb_ref, vcb_ref, kcf_ref, vcf_ref, ks_ref, vs_ref, kw_ref, vw_ref, et_ref,
                gate_ref, ovb_ref, ovf_ref, bcn_ref, bsn_ref, bw_ref, cf_ref, eg_ref, o_ref,
                sb0_ref, sb1_ref, *, n_slc, n_round):
    blk = pl.program_id(2)
    c0 = blk * TQ
    r4 = Z_NSA * TQ
    q = q_ref[0]
    lane = lax.broadcasted_iota(I32, (TQ, LANES), 1)
    lo_half = lane < DH
    zq = jnp.zeros((TQ, LANES), BF16)
    qa, qb = q[:, :LANES], q[:, LANES:]
    qs = jnp.concatenate([jnp.where(lo_half, qa, zq), jnp.where(lo_half, zq, qa),
                          jnp.where(lo_half, qb, zq), jnp.where(lo_half, zq, qb)], axis=0)
    ones_k = jnp.ones((SEL_FAR_TILE, LANES), BF16)
    ones_w = jnp.ones((WINDOW + TQ, LANES), BF16)

    kcp = kcb_ref.shape[2]
    prow = lax.broadcasted_iota(I32, (1, kcp), 1)
    near0 = pl.multiple_of(blk * (TQ // CMP_STRIDE), 8)
    far_row = jnp.where((prow >= KC_FRONT) & (prow < near0), 0.0, NEG)
    lane1 = lax.broadcasted_iota(I32, (1, LANES), 1)
    near_row = jnp.where(lane1 + near0 >= KC_FRONT, 0.0, NEG)
    kcn = kcf_ref[0, 0, pl.ds(near0, LANES), :].astype(BF16)
    vcn = vcf_ref[0, 0, pl.ds(near0, LANES), :].astype(BF16)
    s_far = _nt(qs, kcb_ref[0, 0]) + far_row
    s_near = _nt(qs, kcn) + bcn_ref[0] + near_row
    m_c = jnp.maximum(jnp.max(s_far, axis=1, keepdims=True), jnp.max(s_near, axis=1, keepdims=True))
    m_c = jnp.maximum(m_c, M_INIT)
    p_far = jnp.exp(s_far - m_c)
    p_near = jnp.exp(s_near - m_c)
    den = jnp.sum(p_far, axis=1, keepdims=True) + jnp.sum(p_near, axis=1, keepdims=True)
    inv_c = 1.0 / jnp.maximum(den, 1e-30)
    o_c = (_dot(p_far.astype(BF16), vcb_ref[0, 0]) + _dot(p_near.astype(BF16), vcn)) * inv_c

    pn_far = p_far * inv_c
    pn_near = p_near * inv_c
    pz_far = pn_far[0:TQ] + pn_far[TQ:2 * TQ] + pn_far[2 * TQ:3 * TQ] + pn_far[3 * TQ:]
    pz_near = pn_near[0:TQ] + pn_near[TQ:2 * TQ] + pn_near[2 * TQ:3 * TQ] + pn_near[3 * TQ:]
    ovn = ovf_ref[pl.ds(near0, LANES), :].astype(BF16)
    imp = _dot_hl(pz_far, ovb_ref[...]) + _dot_hl(pz_near, ovn)
    tpos = c0 + lax.broadcasted_iota(I32, (TQ, LANES), 0)
    cur = lax.shift_right_logical(tpos, 6)
    valid = (lane * SLC_BLOCK <= tpos) & (lane < n_slc)
    forced = valid & ((lane == 0) | (lane == cur) | (lane == cur - 1))
    score = jnp.where(valid & jnp.logical_not(forced), imp, -1.0)
    score = jnp.where(lane < n_slc, score, -2.0)

    kws, vws = [], []
    for w in range((WINDOW + TQ) // LANES):
        st = pl.multiple_of(jnp.maximum(c0 - WINDOW + w * LANES, 0), LANES)
        kws.append(kw_ref[0, pl.ds(st, LANES), :])
        vws.append(vw_ref[0, pl.ds(st, LANES), :])
    wl = lax.broadcasted_iota(I32, (1, WINDOW + TQ), 1)
    pos_row = jnp.where(c0 - WINDOW + wl >= 0, 0.0, NEG)
    s = _nt(qs, jnp.concatenate(kws, axis=0)) + bw_ref[0] + pos_row
    p = jnp.exp(s - jnp.max(s, axis=1, keepdims=True)).astype(BF16)
    r_w = _dot(p, jnp.concatenate([jnp.concatenate(vws, axis=0), ones_w], axis=1))
    o_w = r_w[:, :LANES] / r_w[:, LANES:]
    gl = _dot(gate_ref[0], eg_ref[0])
    sig = 1.0 / (1.0 + jnp.exp(-gl))
    wide = Z_NSA * DH

    ones_b = jnp.ones((LANES, LANES), BF16)

    def count(mask):
        return _dot(jnp.where(mask, 1.0, 0.0).astype(BF16), ones_b)

    rest, taken, thr = score, jnp.zeros((TQ, LANES), F32), jnp.zeros((TQ, LANES), F32)
    for _ in range(n_round):
        m = jnp.max(rest, axis=1, keepdims=True)
        eq = rest == m
        thr = jnp.where(taken < n_round, m, thr)
        taken = taken + count(eq)
        rest = jnp.where(eq, -jnp.inf, rest)
    above = score > thr
    at_thr = score == thr
    need = n_round - count(above)
    brow = lax.broadcasted_iota(I32, (LANES, LANES), 0)
    bcol = lax.broadcasted_iota(I32, (LANES, LANES), 1)
    before = _dot(jnp.where(at_thr, 1.0, 0.0).astype(BF16),
                  jnp.where(brow < bcol, 1.0, 0.0).astype(BF16))
    sel = ((above | (at_thr & (before < need))) & valid) | forced


    sel_all = jnp.where(sel, 0.0, NEG).astype(BF16)
    far_blocks = blk * (TQ // SLC_BLOCK) - NEAR_BACK // SLC_BLOCK
    sel_far = jnp.where(sel & (lane < far_blocks), 0.0, NEG).astype(BF16)
    q_far = jnp.concatenate([qs, jnp.concatenate([sel_far] * Z_NSA, axis=0)], axis=1)
    q_near = jnp.concatenate([qs, jnp.concatenate([sel_all] * Z_NSA, axis=0)], axis=1)
    tk = SEL_FAR_TILE
    last_tile = ks_ref.shape[1] // tk - 1

    def qk(kt):
        k0 = pl.multiple_of(kt * tk, tk)
        return _nt(q_far, jnp.concatenate([ks_ref[0, pl.ds(k0, tk), :], et_ref[pl.ds(k0, tk), :]], axis=1))

    def fold(s, v, st):
        m, acc = st
        mn = jnp.maximum(m, jnp.max(s, axis=1, keepdims=True))
        p = jnp.exp(s - mn).astype(BF16)
        return mn, jnp.exp(m - mn) * acc + _dot(p, v)

    def v_tile(kt):
        k0 = pl.multiple_of(kt * tk, tk)
        return jnp.concatenate([vs_ref[0, pl.ds(k0, tk), :], ones_k], axis=1)

    n_pair = lax.shift_right_logical(jnp.maximum(c0 - NEAR_BACK, 0) + (2 * tk - 1), int(math.log2(2 * tk)))
    sb0_ref[...] = qk(0)

    n_near = NEAR_BACK + TQ
    p0 = pl.multiple_of(jnp.maximum(c0 - NEAR_BACK, 0), NEAR_BACK)
    d0 = pl.multiple_of(c0, TQ)
    kn = jnp.concatenate([ks_ref[0, pl.ds(p0, NEAR_BACK), :], ks_ref[0, pl.ds(d0, TQ), :]], axis=0)
    nrow = lax.broadcasted_iota(I32, (n_near, LANES), 0)
    nlane = lax.broadcasted_iota(I32, (n_near, LANES), 1)
    e_n = jnp.where(lax.shift_right_logical(nrow, 6) + far_blocks == nlane, 1.0, 0.0).astype(BF16)
    lane2 = lax.broadcasted_iota(I32, (1, n_near), 1)
    prev_row = jnp.where((lane2 < NEAR_BACK) & (blk == 0), NEG, 0.0)
    s_near = _nt(q_near, jnp.concatenate([kn, e_n], axis=1)) + bsn_ref[0] + prev_row

    def far_step(i, st):
        sb1_ref[...] = qk(2 * i + 1)
        st = fold(sb0_ref[...], v_tile(2 * i), st)
        sb0_ref[...] = qk(jnp.minimum(2 * i + 2, last_tile))
        return fold(sb1_ref[...], v_tile(2 * i + 1), st)

    def far_step2(i, st):
        return far_step(2 * i + 1, far_step(2 * i, st))

    m0 = jnp.full((r4, 1), M_INIT, F32)
    n_quad = lax.shift_right_logical(n_pair, 1)
    st = lax.fori_loop(0, n_quad, far_step2, (m0, jnp.zeros((r4, 2 * LANES), F32)))
    m_s, acc_s = lax.fori_loop(2 * n_quad, n_pair, far_step, st)
    m_s = m_s + cf_ref[0][:, 0:1]
    vn = jnp.concatenate([vs_ref[0, pl.ds(p0, NEAR_BACK), :], vs_ref[0, pl.ds(d0, TQ), :]], axis=0)
    _, acc_s = fold(s_near, jnp.concatenate([vn, ones_k[:n_near]], axis=1), (m_s, acc_s))
    o_s = acc_s[:, :LANES] / acc_s[:, LANES:]

    def heads(o):
        return jnp.concatenate([jnp.where(lo_half, o[0:TQ], o[TQ:2 * TQ]),
                                jnp.where(lo_half, o[2 * TQ:3 * TQ], o[3 * TQ:])], axis=1)

    o_ref[0] = (sig[:, 0:wide] * heads(o_c) + sig[:, wide:2 * wide] * heads(o_s)
                + sig[:, 2 * wide:] * heads(o_w))


def _nsa_attention(proj, cmp_out, rel_table):
    bsz, s, _ = proj.shape
    assert s % (2 * SEL_FAR_TILE) == 0 and s >= WINDOW + TQ
    n16 = s // CMP_STRIDE
    n_slc = s // SLC_BLOCK
    assert N_SELECT <= n_slc <= LANES
    nq = s // TQ
    kcp = -(-(max(n16 + KC_FRONT, (TQ // CMP_STRIDE) * (nq - 1) + LANES)) // LANES) * LANES
    pad = ((0, 0), (0, 0), (KC_FRONT, kcp - n16 - KC_FRONT), (0, 0))
    cf = jnp.pad(cmp_out, pad)
    cb = cf.astype(BF16)
    ov = _overlap_padded(n16, kcp, n_slc)
    bcn, bsn, bw, far = _bias_tiles(rel_table)
    eg = jnp.asarray(_gate_expand(), BF16)
    r4 = Z_NSA * TQ
    kv_spec = lambda cb0: pl.BlockSpec((1, s, LANES), lambda b, g, i, cb0=cb0: (b, 0, cb0 + g))
    cmp_spec = lambda j0: pl.BlockSpec((1, 1, kcp, LANES), lambda b, g, i, j0=j0: (b, j0 + g, 0, 0))
    tile_spec = lambda w: pl.BlockSpec((1, r4, w), lambda b, g, i: (g, 0, 0))
    const2 = lambda n: pl.BlockSpec((n, LANES), lambda b, g, i: (0, 0))
    return pl.pallas_call(
        functools.partial(_nsa_kernel, n_slc=n_slc, n_round=N_SELECT - 3),
        grid=(bsz, G_NSA, nq),
        in_specs=[pl.BlockSpec((1, TQ, 2 * LANES), lambda b, g, i: (b, i, CB_NSAQ // 2 + g)),
                  cmp_spec(0), cmp_spec(2), cmp_spec(0), cmp_spec(2),
                  kv_spec(CB_KSLC), kv_spec(CB_VSLC), kv_spec(CB_KWIN), kv_spec(CB_VWIN),
                  const2(s),
                  pl.BlockSpec((1, TQ, LANES), lambda b, g, i: (b, i, CB_GATE)),
                  const2(kcp), const2(kcp),
                  tile_spec(LANES), tile_spec(NEAR_BACK + TQ), tile_spec(WINDOW + TQ), tile_spec(LANES),
                  pl.BlockSpec((1, LANES, 3 * Z_NSA * DH), lambda b, g, i: (g, 0, 0))],
        out_specs=pl.BlockSpec((1, TQ, Z_NSA * DH), lambda b, g, i: (b, i, g)),
        out_shape=jax.ShapeDtypeStruct((bsz, s, N_NSA * DH), F32),
        scratch_shapes=[pltpu.VMEM((r4, SEL_FAR_TILE), F32), pltpu.VMEM((r4, SEL_FAR_TILE), F32)],
        compiler_params=_cparams("arbitrary", "arbitrary", "arbitrary"),
        name="nsa_attn",
    )(proj, cb, cb, cf, cf, proj, proj, proj, proj, jnp.asarray(_block_onehot(s), BF16), proj,
      jnp.asarray(ov, BF16), jnp.asarray(ov, F32), bcn, bsn, bw, far, eg)


def _out_kernel(osb_ref, onsa_ref, x_ref, gsb_ref, gnsa_ref, w_ref, gpost_ref, gate_ref,
                gpre_ref, sc_ref, sh_ref, xo_ref, h_ref):
    half = osb_ref.shape[2]
    a = _rms(osb_ref[0], gsb_ref[...]).astype(BF16)
    b = _rms(onsa_ref[0], gnsa_ref[...]).astype(BF16)
    m = _dot(a, w_ref[:half, :]) + _dot(b, w_ref[half:, :])
    x = x_ref[0] + gate_ref[0] * _rms(m, gpost_ref[...])
    xo_ref[0] = x
    h_ref[0] = (_rms(x, gpre_ref[...]) * (1.0 + sc_ref[0]) + sh_ref[0]).astype(BF16)


def _out_proj(o_sb, o_nsa, x, g_sb, g_nsa, w_out, g_post, gate_m, g_pre_ffn, scale_f, shift_f):
    bsz, s, d = x.shape
    half = o_sb.shape[2]
    tm = min(512, s)
    row = lambda n: pl.BlockSpec((1, n), lambda b, i: (0, 0))
    mod = pl.BlockSpec((1, 1, d), lambda b, i: (b, 0, 0))
    act = lambda n: pl.BlockSpec((1, tm, n), lambda b, i: (b, i, 0))
    return pl.pallas_call(
        _out_kernel,
        grid=(bsz, s // tm),
        in_specs=[act(half), act(half), act(d), row(half), row(half),
                  pl.BlockSpec((2 * half, d), lambda b, i: (0, 0)), row(d), mod, row(d), mod, mod],
        out_specs=[act(d), act(d)],
        out_shape=[jax.ShapeDtypeStruct((bsz, s, d), F32), jax.ShapeDtypeStruct((bsz, s, d), BF16)],
        compiler_params=_cparams("arbitrary", "arbitrary"),
        name="out_proj",
    )(o_sb, o_nsa, x, g_sb.reshape(1, half), g_nsa.reshape(1, half), w_out.astype(BF16),
      g_post.reshape(1, d), gate_m.reshape(bsz, 1, d), g_pre_ffn.reshape(1, d),
      scale_f.reshape(bsz, 1, d), shift_f.reshape(bsz, 1, d))


def _silu(x):
    return x * (1.0 / (1.0 + jnp.exp(-x)))


def _ffn_kernel(h_ref, x_ref, wg_ref, wu_ref, wd_ref, gpost_ref, gate_ref, o_ref):
    h = h_ref[0]
    a = (_silu(_dot(h, wg_ref[...])) * _dot(h, wu_ref[...])).astype(BF16)
    o_ref[0] = x_ref[0] + gate_ref[0] * _rms(_dot(a, wd_ref[...]), gpost_ref[...])


def _dense_ffn(h, x, w_gate, w_up, w_down, g_post, gate_f):
    bsz, s, d = x.shape
    ff = w_gate.shape[1]
    tm = min(512, s)
    act = pl.BlockSpec((1, tm, d), lambda b, i: (b, i, 0))
    once = pl.Buffered(1)
    return pl.pallas_call(
        _ffn_kernel,
        grid=(bsz, s // tm),
        in_specs=[act, act,
                  pl.BlockSpec((d, ff), lambda b, i: (0, 0), pipeline_mode=once),
                  pl.BlockSpec((d, ff), lambda b, i: (0, 0), pipeline_mode=once),
                  pl.BlockSpec((ff, d), lambda b, i: (0, 0), pipeline_mode=once),
                  pl.BlockSpec((1, d), lambda b, i: (0, 0)),
                  pl.BlockSpec((1, 1, d), lambda b, i: (b, 0, 0))],
        out_specs=act,
        out_shape=jax.ShapeDtypeStruct((bsz, s, d), F32),
        compiler_params=_cparams("arbitrary", "arbitrary"),
        name="dense_ffn",
    )(h, x, w_gate.astype(BF16), w_up.astype(BF16), w_down.astype(BF16),
      g_post.reshape(1, d), gate_f.reshape(bsz, 1, d))


MOE_RC = 128
MOE_GC = 256
MOE_SC = 1024


def _moe_rows(tile):
    return -(-(2 * tile + N_EXPERTS * MOE_RC) // MOE_SC) * MOE_SC


def _dispatch_kernel(x_ref, gpre_ref, sc_ref, sh_ref, wr_ref, br_ref,
                     xg_ref, wrow_ref, d_ref, off_ref, nch_ref):
    tile = x_ref.shape[1]
    rows = xg_ref.shape[1]
    h = _rms(x_ref[0], gpre_ref[...]) * (1.0 + sc_ref[0]) + sh_ref[0]
    logits = _dot_f32(h, wr_ref[...]) + br_ref[...]
    lane = lax.broadcasted_iota(I32, (tile, LANES), 1)
    lanef = lane.astype(F32)
    e = jnp.exp(logits - jnp.max(logits, axis=1, keepdims=True))
    probs = e / jnp.sum(e, axis=1, keepdims=True)
    probs = jnp.where(lane < N_EXPERTS, probs, -1.0)
    m1 = jnp.max(probs, axis=1, keepdims=True)
    i1 = jnp.min(jnp.where(probs == m1, lanef, 1e9), axis=1, keepdims=True)
    rest = jnp.where(lanef == i1, -1.0, probs)
    m2 = jnp.max(rest, axis=1, keepdims=True)
    i2 = jnp.min(jnp.where(rest == m2, lanef, 1e9), axis=1, keepdims=True)
    tot = m1 + m2
    hit1 = lanef == i1
    hit2 = lanef == i2
    mask = hit1 | hit2
    ch = 256
    r = lax.broadcasted_iota(I32, (ch, ch), 0)
    c = lax.broadcasted_iota(I32, (ch, ch), 1)
    lower = jnp.where(c < r, 1.0, 0.0).astype(BF16)
    ones = jnp.ones((8, ch), BF16)
    count = jnp.zeros((1, LANES), F32)
    ranks = []
    for k in range(tile // ch):
        mk = jnp.where(mask[k * ch:(k + 1) * ch], 1.0, 0.0).astype(BF16)
        ranks.append(_dot(lower, mk) + count)
        count = count + _dot(ones, mk)[0:1]
    rank = jnp.concatenate(ranks, axis=0)
    shift = int(math.log2(MOE_RC))
    nch = lax.shift_right_logical(count.astype(I32) + (MOE_RC - 1), shift)
    nch8 = jnp.broadcast_to(nch.astype(F32), (8, LANES))
    lr = lax.broadcasted_iota(I32, (LANES, LANES), 0)
    lc = lax.broadcasted_iota(I32, (LANES, LANES), 1)
    offc = _dot(nch8.astype(BF16), jnp.where(lr < lc, 1.0, 0.0).astype(BF16))
    row = offc[0:1] * float(MOE_RC) + rank
    d1 = jnp.sum(jnp.where(hit1, row, 0.0), axis=1, keepdims=True)
    d2 = jnp.sum(jnp.where(hit2, row, 0.0), axis=1, keepdims=True)
    dd = jnp.where(lane == 0, d1, jnp.where(lane == 1, d2, -1.0))
    d_ref[0] = dd.astype(I32)
    off_ref[0] = offc.astype(I32)
    nch_ref[0] = jnp.broadcast_to(nch, (8, LANES))

    ddt = jnp.transpose(dd)
    d1t = ddt[0:1].astype(I32)
    d2t = ddt[1:2].astype(I32)

    wt = jnp.transpose(jnp.where(lane == 0, m1 / tot, jnp.where(lane == 1, m2 / tot, 0.0)))
    w1t = wt[0:1]
    w2t = wt[1:2]
    hb = h.astype(BF16)
    rid = lax.broadcasted_iota(I32, (MOE_GC, tile), 0)
    for k in range(rows // MOE_GC):
        hit1 = rid + k * MOE_GC == d1t
        hit2 = rid + k * MOE_GC == d2t
        onehot = jnp.where(hit1 | hit2, 1.0, 0.0).astype(BF16)
        xg_ref[0, k * MOE_GC:(k + 1) * MOE_GC, :] = _dot(onehot, hb).astype(BF16)
        wrow = jnp.sum(jnp.where(hit1, w1t, jnp.where(hit2, w2t, 0.0)), axis=1, keepdims=True)
        wrow_ref[0, k * MOE_GC:(k + 1) * MOE_GC, :] = jnp.broadcast_to(wrow, (MOE_GC, LANES))


def _dispatch(x, g_pre, scale_f, shift_f, w_router, b_router, tile):
    bsz, s, d = x.shape
    nt = s // tile
    rows = _moe_rows(tile)
    wr = jnp.zeros((d, LANES), F32).at[:, :N_EXPERTS].set(w_router.astype(F32))
    br = jnp.full((1, LANES), NEG, F32).at[0, :N_EXPERTS].set(b_router.astype(F32))
    mod = pl.BlockSpec((1, 1, d), lambda b, i: (b, 0, 0))
    per_tile = lambda r, n: pl.BlockSpec((1, r, n), lambda b, i: (b * nt + i, 0, 0))
    xg, wrow, dest, offc, nch = pl.pallas_call(
        _dispatch_kernel,
        grid=(bsz, nt),
        in_specs=[pl.BlockSpec((1, tile, d), lambda b, i: (b, i, 0)),
                  pl.BlockSpec((1, d), lambda b, i: (0, 0)), mod, mod,
                  pl.BlockSpec((d, LANES), lambda b, i: (0, 0)),
                  pl.BlockSpec((1, LANES), lambda b, i: (0, 0))],
        out_specs=[per_tile(rows, d), per_tile(rows, LANES), per_tile(tile, LANES),
                   per_tile(8, LANES), per_tile(8, LANES)],
        out_shape=[jax.ShapeDtypeStruct((bsz * nt, rows, d), BF16),
                   jax.ShapeDtypeStruct((bsz * nt, rows, LANES), F32),
                   jax.ShapeDtypeStruct((bsz * nt, tile, LANES), I32),
                   jax.ShapeDtypeStruct((bsz * nt, 8, LANES), I32),
                   jax.ShapeDtypeStruct((bsz * nt, 8, LANES), I32)],
        compiler_params=_cparams("arbitrary", "arbitrary"),
        name="moe_dispatch",
    )(x, g_pre.reshape(1, d), scale_f.reshape(bsz, 1, d), shift_f.reshape(bsz, 1, d), wr, br)
    return xg, wrow, dest, offc[:, 0, :N_EXPERTS], nch[:, 0, :N_EXPERTS]


def _slot_order(offc, nch, slots_per_tile):
    ends = offc + nch
    c = jnp.arange(slots_per_tile, dtype=I32)[None, :, None]
    expert = jnp.sum((c >= ends[:, None, :]).astype(I32), axis=-1).reshape(-1)
    n = expert.shape[0]
    order = jnp.argsort(expert * n + jnp.arange(n, dtype=I32)).astype(I32)
    exp_sorted = expert[order]
    used = (exp_sorted < N_EXPERTS).astype(I32)
    return order, jnp.minimum(exp_sorted, N_EXPERTS - 1).astype(I32), used


def _slots_kernel(slot_ref, exp_ref, used_ref, x_ref, wrow_ref, wg_ref, wu_ref, wd_ref, y_ref):
    p = pl.program_id(0)

    @pl.when(used_ref[p] == 1)
    def _():
        x = x_ref[0]
        a = (_silu(_dot(x, wg_ref[0])) * _dot(x, wu_ref[0])).astype(BF16)
        y_ref[0] = (_dot(a, wd_ref[0]) * wrow_ref[0][:, 0:1]).astype(BF16)

    @pl.when(used_ref[p] == 0)
    def _():
        y_ref[0] = jnp.zeros(y_ref.shape[1:], BF16)


def _expert_slots(xg, wrow, order, exp_sorted, used, w_gate, w_up, w_down):
    ntile, rows, d = xg.shape
    nslot = ntile * rows // MOE_RC
    ff = w_gate.shape[2]
    once = pl.Buffered(1)
    slot = lambda n: pl.BlockSpec((1, MOE_RC, n), lambda p, s, e, u: (s[p], 0, 0))
    grid_spec = pltpu.PrefetchScalarGridSpec(
        num_scalar_prefetch=3,
        grid=(nslot,),
        in_specs=[slot(d), slot(LANES),
                  pl.BlockSpec((1, d, ff), lambda p, s, e, u: (e[p], 0, 0), pipeline_mode=once),
                  pl.BlockSpec((1, d, ff), lambda p, s, e, u: (e[p], 0, 0), pipeline_mode=once),
                  pl.BlockSpec((1, ff, d), lambda p, s, e, u: (e[p], 0, 0), pipeline_mode=once)],
        out_specs=slot(d))
    y = pl.pallas_call(
        _slots_kernel,
        grid_spec=grid_spec,
        out_shape=jax.ShapeDtypeStruct((nslot, MOE_RC, d), BF16),
        input_output_aliases={3: 0},
        compiler_params=_cparams("arbitrary"),
        name="moe_slots",
    )(order, exp_sorted, used, xg.reshape(nslot, MOE_RC, d), wrow.reshape(nslot, MOE_RC, LANES),
      w_gate.astype(BF16), w_up.astype(BF16), w_down.astype(BF16))
    return y.reshape(ntile, rows, d)


def _combine_kernel(y_ref, d_ref, x_ref, gpost_ref, gate_ref, o_ref):
    tile = x_ref.shape[1]
    rows = y_ref.shape[1]
    dd = d_ref[0]
    d1c = dd[:, 0:1]
    d2c = dd[:, 1:2]
    cid = lax.broadcasted_iota(I32, (tile, MOE_SC), 1)
    z = None
    for k in range(rows // MOE_SC):
        pt = jnp.where((cid + k * MOE_SC == d1c) | (cid + k * MOE_SC == d2c), 1.0, 0.0).astype(BF16)
        part = _dot(pt, y_ref[0, k * MOE_SC:(k + 1) * MOE_SC, :])
        z = part if z is None else z + part
    o_ref[0] = x_ref[0] + gate_ref[0] * _rms(z, gpost_ref[...])


def _combine(y, dest, x, g_post, gate_f):
    bsz, s, d = x.shape
    ntile, rows, _ = y.shape
    tile = dest.shape[1]
    nt = s // tile
    per_tile = lambda r, n: pl.BlockSpec((1, r, n), lambda i: (i, 0, 0))
    out = pl.pallas_call(
        _combine_kernel,
        grid=(ntile,),
        in_specs=[per_tile(rows, d), per_tile(tile, LANES), per_tile(tile, d),
                  pl.BlockSpec((1, d), lambda i: (0, 0)),
                  pl.BlockSpec((1, 1, d), lambda i: (i // nt, 0, 0))],
        out_specs=per_tile(tile, d),
        out_shape=jax.ShapeDtypeStruct((ntile, tile, d), F32),
        compiler_params=_cparams("arbitrary"),
        name="moe_combine",
    )(y, dest, x.reshape(ntile, tile, d), g_post.reshape(1, d), gate_f.reshape(bsz, 1, d))
    return out.reshape(bsz, s, d)


def _mixer(x, mod, layer, rel_table, g_pre_mix, w_in, cmp_params):
    shift_m, scale_m = mod[:, 0], mod[:, 1]
    proj = _in_proj(x, g_pre_mix, scale_m, shift_m, _arrange_w_in(w_in))
    o_sb = _sb_attention(proj)
    o_nsa = _nsa_attention(proj, _compress(proj, *cmp_params), rel_table)
    return o_sb, o_nsa


def kernel(x, c, rel_table, w_ada, b_ada, g_pre_mix, g_post_mix, g_pre_ffn, g_post_ffn, w_in, w_out, g_sb, g_nsa, cmp_pos_k, cmp_w1_k, cmp_w2_k, cmp_pos_v, cmp_w1_v, cmp_w2_v, ffn_w_gate, ffn_w_up, ffn_w_down, moe_w_router, moe_b_router, moe_w_gate, moe_w_up, moe_w_down):
    bsz, s, d = x.shape
    depth = w_in.shape[0]
    mods = _ada(c, w_ada, b_ada).reshape(depth, bsz, 6, d)
    moe_tile = min(1024, s)
    for layer in range(depth):
        mod = mods[layer]
        cmp_params = (cmp_pos_k[layer], cmp_w1_k[layer], cmp_w2_k[layer],
                      cmp_pos_v[layer], cmp_w1_v[layer], cmp_w2_v[layer])
        o_sb, o_nsa = _mixer(x, mod, layer, rel_table, g_pre_mix[layer], w_in[layer], cmp_params)
        x, h = _out_proj(o_sb, o_nsa, x, g_sb[layer], g_nsa[layer], w_out[layer], g_post_mix[layer],
                         mod[:, 2], g_pre_ffn[layer], mod[:, 4], mod[:, 3])
        i = layer // 2
        if layer % 2 == 0:
            x = _dense_ffn(h, x, ffn_w_gate[i], ffn_w_up[i], ffn_w_down[i], g_post_ffn[layer], mod[:, 5])
        else:
            xg, wrow, dest, offc, nch = _dispatch(x, g_pre_ffn[layer], mod[:, 4], mod[:, 3],
                                                  moe_w_router[i], moe_b_router[i], moe_tile)
            order = _slot_order(offc, nch, xg.shape[1] // MOE_RC)
            y = _expert_slots(xg, wrow, *order, moe_w_gate[i], moe_w_up[i], moe_w_down[i])
            x = _combine(y, dest, x, g_post_ffn[layer], mod[:, 5])
    return x
```
